```python
import jax, jax.numpy as jnp
from jax import lax
import numpy as np

D_MODEL = 2048
BATCH = 1
SEQ = 8192
DEPTH = 1

CHUNK = 64
D_MIX = D_MODEL
D_CONV = D_MIX // 2
D_ATT = D_MIX - D_CONV
HEAD_DIM = 64
N_HEADS = D_ATT // HEAD_DIM
LEFT_CHUNKS = 8
BAND = (LEFT_CHUNKS + 1) * CHUNK
REL_CLIP = 128
N_REL = 2 * REL_CLIP + 1
CONV_WIDTH = 31
N_GROUPS = 8
EXPERTS_PER_GROUP = 8
N_EXPERTS = N_GROUPS * EXPERTS_PER_GROUP
TOP_K = 2
D_EXPERT = D_MODEL // 4
BLOCK_ROWS = 128
D_IN = 2 * D_CONV + 3 * D_ATT
RMS_EPS = 1e-6
LN_EPS = 1e-5
NEG_INF = -1e30

kernel_name = "hybrid_conv_chunkattn_hiermoe_block"


def rmsnorm(x, g):
    xf = x.astype(jnp.float32)
    y = xf * lax.rsqrt(jnp.mean(xf * xf, axis=-1, keepdims=True) + RMS_EPS)
    return (y * g.astype(jnp.float32)).astype(x.dtype)


def conv_module(a, gate, w_dw, b_dw, ln_g, ln_b):
    u = a * jax.nn.sigmoid(gate)
    u = lax.conv_general_dilated(
        u, w_dw.astype(u.dtype), window_strides=(1,),
        padding=((CONV_WIDTH - 1, 0),),
        dimension_numbers=("NWC", "WIO", "NWC"),
        feature_group_count=D_CONV) + b_dw.astype(u.dtype)
    uf = u.astype(jnp.float32)
    mu = jnp.mean(uf, axis=-1, keepdims=True)
    var = jnp.mean(jnp.square(uf - mu), axis=-1, keepdims=True)
    un = (uf - mu) * lax.rsqrt(var + LN_EPS) * ln_g.astype(jnp.float32) + ln_b.astype(jnp.float32)
    return jax.nn.silu(un).astype(a.dtype)


def chunk_band(t, nc):
    b = t.shape[0]
    t = jnp.pad(t, ((0, 0), (LEFT_CHUNKS * CHUNK, 0), (0, 0), (0, 0)))
    t = t.reshape(b, nc + LEFT_CHUNKS, CHUNK, N_HEADS, HEAD_DIM)
    return jnp.concatenate([t[:, i:i + nc] for i in range(LEFT_CHUNKS + 1)], axis=2)


def chunk_attention(q, k, v, rel_table):
    b, s, _ = q.shape
    nc = s // CHUNK
    qc = q.reshape(b, nc, CHUNK, N_HEADS, HEAD_DIM)
    kb = chunk_band(k.reshape(b, s, N_HEADS, HEAD_DIM), nc)
    vb = chunk_band(v.reshape(b, s, N_HEADS, HEAD_DIM), nc)
    scores = jnp.einsum("bcqhd,bckhd->bchqk", qc, kb,
                        preferred_element_type=jnp.float32) * (HEAD_DIM ** -0.5)
    qi = jnp.arange(CHUNK)[:, None]
    kj = jnp.arange(BAND)[None, :]
    rel = jnp.clip(LEFT_CHUNKS * CHUNK + qi - kj, -REL_CLIP, REL_CLIP) + REL_CLIP
    bias = rel_table.astype(jnp.float32)[:, rel]
    kpos = (jnp.arange(nc)[:, None] * CHUNK - LEFT_CHUNKS * CHUNK
            + jnp.arange(BAND)[None, :])
    valid = (kpos >= 0)[None, :, None, None, :]
    scores = jnp.where(valid, scores + bias[None, None], NEG_INF)
    p = jax.nn.softmax(scores, axis=-1).astype(v.dtype)
    o = jnp.einsum("bchqk,bckhd->bcqhd", p, vb)
    return o.reshape(b, s, D_ATT)


def hier_moe(h, w_group, b_group, w_expert_router, b_expert_router, w_gate, w_up, w_down):
    b, s, d = h.shape
    n = b * s
    t = h.reshape(n, d)
    g_logits = jnp.matmul(t, w_group, preferred_element_type=jnp.float32) + b_group.astype(jnp.float32)
    g_prob = jax.nn.softmax(g_logits, axis=-1)
    grp = jnp.argmax(g_logits, axis=-1)
    p_grp = jnp.take_along_axis(g_prob, grp[:, None], axis=1)
    e_logits = (jnp.matmul(t, w_expert_router, preferred_element_type=jnp.float32)
                + b_expert_router.astype(jnp.float32)).reshape(n, N_GROUPS, EXPERTS_PER_GROUP)
    e_logits_g = jnp.take_along_axis(e_logits, grp[:, None, None], axis=1)[:, 0]
    top_v, top_i = lax.top_k(e_logits_g, TOP_K)
    gates = jax.nn.softmax(top_v, axis=-1) * p_grp
    expert = grp[:, None] * EXPERTS_PER_GROUP + top_i

    m = n * TOP_K
    e_flat = expert.reshape(m)
    tok = jnp.repeat(jnp.arange(n, dtype=jnp.int32), TOP_K)
    g_flat = gates.reshape(m)
    order = jnp.argsort(e_flat)
    e_s, tok_s, g_s = e_flat[order], tok[order], g_flat[order]
    counts = jax.ops.segment_sum(jnp.ones((m,), jnp.int32), e_flat, num_segments=N_EXPERTS)
    starts = jnp.cumsum(counts) - counts
    pcounts = (counts + BLOCK_ROWS - 1) // BLOCK_ROWS * BLOCK_ROWS
    pends = jnp.cumsum(pcounts)
    pstarts = pends - pcounts
    dest = pstarts[e_s] + jnp.arange(m, dtype=jnp.int32) - starts[e_s]
    n_blocks = -(-m // BLOCK_ROWS) + N_EXPERTS
    rows = n_blocks * BLOCK_ROWS
    row_tok = jnp.zeros((rows,), jnp.int32).at[dest].set(tok_s)
    row_gate = jnp.zeros((rows,), jnp.float32).at[dest].set(g_s)
    block_exp = jnp.minimum(
        jnp.searchsorted(pends, jnp.arange(n_blocks, dtype=jnp.int32) * BLOCK_ROWS, side="right"),
        N_EXPERTS - 1)
    xb = t[row_tok].reshape(n_blocks, BLOCK_ROWS, d)

    def expert_block(args):
        xe, e = args
        a = xe @ w_gate[e]
        u = xe @ w_up[e]
        return (jax.nn.silu(a) * u) @ w_down[e]

    yb = lax.map(expert_block, (xb, block_exp))
    y = yb.reshape(rows, d).astype(jnp.float32) * row_gate[:, None]
    out = jax.ops.segment_sum(y, row_tok, num_segments=n)
    return out.reshape(b, s, d).astype(h.dtype)


def setup_inputs(seed: int = 0) -> dict:
    key = jax.random.key(seed)
    ks = jax.random.split(key, 20)
    f32 = jnp.float32
    nrm = lambda k, shape, scale: jax.random.normal(k, shape, f32) * scale
    return {
        "x": nrm(ks[0], (BATCH, SEQ, D_MODEL), 1.0),
        "norm1_g": 1.0 + nrm(ks[1], (DEPTH, D_MODEL), 0.02),
        "w_in": nrm(ks[2], (DEPTH, D_MODEL, D_IN), D_MODEL ** -0.5),
        "w_dw": nrm(ks[3], (DEPTH, CONV_WIDTH, 1, D_CONV), CONV_WIDTH ** -0.5),
        "b_dw": nrm(ks[4], (DEPTH, D_CONV), 0.01),
        "ln_g": 1.0 + nrm(ks[5], (DEPTH, D_CONV), 0.02),
        "ln_b": nrm(ks[6], (DEPTH, D_CONV), 0.01),
        "rel_bias": nrm(ks[7], (DEPTH, N_HEADS, N_REL), 0.5),
        "w_out": nrm(ks[8], (DEPTH, D_MIX, D_MODEL), D_MIX ** -0.5),
        "norm2_g": 1.0 + nrm(ks[9], (DEPTH, D_MODEL), 0.02),
        "w_group": nrm(ks[10], (DEPTH, D_MODEL, N_GROUPS), D_MODEL ** -0.5),
        "b_group": nrm(ks[11], (DEPTH, N_GROUPS), 0.01),
        "w_expert_router": nrm(ks[12], (DEPTH, D_MODEL, N_EXPERTS), D_MODEL ** -0.5),
        "b_expert_router": nrm(ks[13], (DEPTH, N_EXPERTS), 0.01),
        "w_gate": nrm(ks[14], (DEPTH, N_EXPERTS, D_MODEL, D_EXPERT), D_MODEL ** -0.5),
        "w_up": nrm(ks[15], (DEPTH, N_EXPERTS, D_MODEL, D_EXPERT), D_MODEL ** -0.5),
        "w_down": nrm(ks[16], (DEPTH, N_EXPERTS, D_EXPERT, D_MODEL), D_EXPERT ** -0.5),
        "final_g": 1.0 + nrm(ks[17], (D_MODEL,), 0.02),
    }


def reference(x, norm1_g, w_in, w_dw, b_dw, ln_g, ln_b, rel_bias, w_out, norm2_g,
              w_group, b_group, w_expert_router, b_expert_router, w_gate, w_up, w_down,
              final_g):
    for l in range(DEPTH):
        h = rmsnorm(x, norm1_g[l])
        p = jnp.matmul(h, w_in[l])
        c_val = p[..., :D_CONV]
        c_gate = p[..., D_CONV:2 * D_CONV]
        q = p[..., 2 * D_CONV:2 * D_CONV + D_ATT]
        k = p[..., 2 * D_CONV + D_ATT:2 * D_CONV + 2 * D_ATT]
        v = p[..., 2 * D_CONV + 2 * D_ATT:]
        conv_out = conv_module(c_val, c_gate, w_dw[l], b_dw[l], ln_g[l], ln_b[l])
        attn_out = chunk_attention(q, k, v, rel_bias[l])
        mixed = jnp.concatenate([conv_out, attn_out.astype(conv_out.dtype)], axis=-1)
        x = x + jnp.matmul(mixed, w_out[l]).astype(x.dtype)
        h2 = rmsnorm(x, norm2_g[l])
        x = x + hier_moe(h2, w_group[l], b_group[l], w_expert_router[l], b_expert_router[l],
                         w_gate[l], w_up[l], w_down[l]).astype(x.dtype)
    return rmsnorm(x, final_g)
```

```python
import functools

import jax
import jax.numpy as jnp
from jax import lax
from jax.experimental import pallas as pl
from jax.experimental.pallas import tpu as pltpu

F32 = jnp.float32
BF16 = jnp.bfloat16
I32 = jnp.int32

CHUNK = 64
HEAD_DIM = 64
LEFT_CHUNKS = 8
REL_CLIP = 128
CONV_WIDTH = 31
N_GROUPS = 8
EXPERTS_PER_GROUP = 8
N_EXPERTS = N_GROUPS * EXPERTS_PER_GROUP
TOP_K = 2
BLOCK_ROWS = 128
RMS_EPS = 1e-6
LN_EPS = 1e-5
NEG_INF = -1e30

VMEM_LIMIT_BYTES = 56 * 1024 * 1024
LANES = 128
PAIR = 2 * CHUNK
BAND_PAIR = (LEFT_CHUNKS + 2) * CHUNK
FRONT_PAD = LEFT_CHUNKS * CHUNK
CONV_HALO = 32


def _cparams(*sem):
    return pltpu.CompilerParams(dimension_semantics=sem, vmem_limit_bytes=VMEM_LIMIT_BYTES)


def _inproj_body(x_ref, g_ref, w_ref, cv_ref, cg_ref, q_ref, k_ref, v_ref):
    x = x_ref[...]
    ms = jnp.mean(x * x, axis=-1, keepdims=True)
    h = (x * lax.rsqrt(ms + RMS_EPS) * g_ref[...]).astype(BF16)
    dc = cv_ref.shape[1]
    da = q_ref.shape[1]

    def proj(lo, width):
        return jnp.dot(h, w_ref[:, lo:lo + width], preferred_element_type=F32)

    cv_ref[...] = proj(0, dc)
    cg_ref[...] = proj(dc, dc)
    q_ref[...] = (proj(2 * dc, da) * (HEAD_DIM ** -0.5)).astype(BF16)
    k_ref[...] = proj(2 * dc + da, da).astype(BF16)
    v_ref[...] = proj(2 * dc + 2 * da, da).astype(BF16)


def _in_proj(x, g, w_bf16, d_conv, d_att, tm=256):
    s, d = x.shape
    d_in = w_bf16.shape[1]
    row = lambda i: (i, 0)
    const = lambda i: (0, 0)
    return pl.pallas_call(
        _inproj_body,
        grid=(s // tm,),
        in_specs=[
            pl.BlockSpec((tm, d), row),
            pl.BlockSpec((1, d), const),
            pl.BlockSpec((d, d_in), const, pipeline_mode=pl.Buffered(1)),
        ],
        out_specs=[
            pl.BlockSpec((tm, d_conv), row),
            pl.BlockSpec((tm, d_conv), row),
            pl.BlockSpec((tm, d_att), row),
            pl.BlockSpec((tm, d_att), row),
            pl.BlockSpec((tm, d_att), row),
        ],
        out_shape=[
            jax.ShapeDtypeStruct((s, d_conv), F32),
            jax.ShapeDtypeStruct((s, d_conv), F32),
            jax.ShapeDtypeStruct((s, d_att), BF16),
            jax.ShapeDtypeStruct((s, d_att), BF16),
            jax.ShapeDtypeStruct((s, d_att), BF16),
        ],
        compiler_params=_cparams("arbitrary"),
        name="in_proj",
    )(x, g, w_bf16)


def _sigmoid(x):
    return 1.0 / (1.0 + jnp.exp(-x))


def _conv_body(cv_ref, cg_ref, cvh_ref, cgh_ref, w_ref, b_ref, lg_ref, lb_ref, o_ref, u_ref,
               *, rows_per_chunk):
    i = pl.program_id(0)
    t = cv_ref.shape[0]
    halo = cvh_ref[...] * _sigmoid(cgh_ref[...])
    u_ref[0:CONV_HALO, :] = jnp.where(i > 0, halo, 0.0)
    u_ref[CONV_HALO:, :] = cv_ref[...] * _sigmoid(cg_ref[...])
    r = rows_per_chunk
    first_tap = CONV_HALO - (CONV_WIDTH - 1)
    for c in range(t // r):
        acc = jnp.broadcast_to(b_ref[...], (r, b_ref.shape[1]))
        for w in range(CONV_WIDTH):
            lo = c * r + first_tap + w
            acc = acc + w_ref[w:w + 1, :] * u_ref[lo:lo + r, :]
        mu = jnp.mean(acc, axis=-1, keepdims=True)
        cen = acc - mu
        var = jnp.mean(cen * cen, axis=-1, keepdims=True)
        un = cen * lax.rsqrt(var + LN_EPS) * lg_ref[...] + lb_ref[...]
        o_ref[c * r:(c + 1) * r, :] = (un * _sigmoid(un)).astype(o_ref.dtype)


def _conv_module(cv, cg, w_dw, b_dw, ln_g, ln_b, t=128, rows_per_chunk=16):
    s, dc = cv.shape
    halo_blocks = t // CONV_HALO
    row = lambda i: (i, 0)
    prev = lambda i: (jnp.maximum(i * halo_blocks - 1, 0), 0)
    const = lambda i: (0, 0)
    return pl.pallas_call(
        functools.partial(_conv_body, rows_per_chunk=rows_per_chunk),
        grid=(s // t,),
        in_specs=[
            pl.BlockSpec((t, dc), row),
            pl.BlockSpec((t, dc), row),
            pl.BlockSpec((CONV_HALO, dc), prev),
            pl.BlockSpec((CONV_HALO, dc), prev),
            pl.BlockSpec((CONV_WIDTH, dc), const),
            pl.BlockSpec((1, dc), const),
            pl.BlockSpec((1, dc), const),
            pl.BlockSpec((1, dc), const),
        ],
        out_specs=pl.BlockSpec((t, dc), row),
        out_shape=jax.ShapeDtypeStruct((s, dc), BF16),
        scratch_shapes=[pltpu.VMEM((t + CONV_HALO, dc), F32)],
        compiler_params=_cparams("arbitrary"),
        name="conv_module",
    )(cv, cg, cv, cg, w_dw, b_dw, ln_g, ln_b)


def _attn_body(q_ref, k_ref, v_ref, bias_ref, o_ref):
    p = pl.program_id(1)
    q2 = q_ref[...]
    start = pl.multiple_of(p * PAIR, PAIR)
    kb = k_ref[pl.ds(start, BAND_PAIR), :]
    vb = v_ref[pl.ds(start, BAND_PAIR), :]
    lane = lax.broadcasted_iota(I32, q2.shape, 1)
    col = lax.broadcasted_iota(I32, (PAIR, BAND_PAIR), 1)
    valid = col >= FRONT_PAD - p * PAIR
    outs = []
    for h in range(2):
        in_head = (lane >= h * HEAD_DIM) & (lane < (h + 1) * HEAD_DIM)
        qh = jnp.where(in_head, q2, jnp.zeros_like(q2))
        sc = lax.dot_general(qh, kb, (((1,), (1,)), ((), ())), preferred_element_type=F32)
        sc = jnp.where(valid, sc + bias_ref[h], NEG_INF)
        m = jnp.max(sc, axis=-1, keepdims=True)
        e = jnp.exp(sc - m)
        den = jnp.sum(e, axis=-1, keepdims=True)
        pr = (e / den).astype(BF16)
        outs.append(jnp.dot(pr, vb, preferred_element_type=F32))
    o_ref[...] = jnp.where(lane < HEAD_DIM, outs[0], outs[1]).astype(o_ref.dtype)


def _pair_bias(rel_table):
    qi = jnp.arange(PAIR)[:, None]
    kj = jnp.arange(BAND_PAIR)[None, :]
    rel = jnp.clip(LEFT_CHUNKS * CHUNK + qi - kj, -REL_CLIP, REL_CLIP) + REL_CLIP
    band_lo = (qi // CHUNK) * CHUNK
    in_band = (kj >= band_lo) & (kj < band_lo + (LEFT_CHUNKS + 1) * CHUNK)
    return jnp.where(in_band[None], rel_table.astype(F32)[:, rel], NEG_INF)


def _attention(q, k_pad, v_pad, bias):
    s, da = q.shape
    sp = k_pad.shape[0]
    n_hp = da // (2 * HEAD_DIM)
    return pl.pallas_call(
        _attn_body,
        grid=(n_hp, s // PAIR),
        in_specs=[
            pl.BlockSpec((PAIR, 2 * HEAD_DIM), lambda hp, p: (p, hp)),
            pl.BlockSpec((sp, 2 * HEAD_DIM), lambda hp, p: (0, hp)),
            pl.BlockSpec((sp, 2 * HEAD_DIM), lambda hp, p: (0, hp)),
            pl.BlockSpec((2, PAIR, BAND_PAIR), lambda hp, p: (hp, 0, 0)),
        ],
        out_specs=pl.BlockSpec((PAIR, 2 * HEAD_DIM), lambda hp, p: (p, hp)),
        out_shape=jax.ShapeDtypeStruct((s, da), BF16),
        compiler_params=_cparams("arbitrary", "arbitrary"),
        name="chunk_attention",
    )(q, k_pad, v_pad, bias)


def _first_index_of_max(vals, sub, n):
    vmax = jnp.max(vals, axis=0, keepdims=True)
    idx = jnp.min(jnp.where(vals == vmax, sub, n), axis=0, keepdims=True)
    return vmax, idx


def _outproj_body(x_ref, co_ref, ao_ref, w_ref, g2_ref, wr_ref, br_ref,
                  x2_ref, h2_ref, eid_ref, gcol_ref):
    dc = co_ref.shape[1]
    y = jnp.dot(co_ref[...], w_ref[0:dc, :], preferred_element_type=F32)
    y = y + jnp.dot(ao_ref[...], w_ref[dc:, :], preferred_element_type=F32)
    x2 = x_ref[...] + y
    ms = jnp.mean(x2 * x2, axis=-1, keepdims=True)
    h2 = x2 * lax.rsqrt(ms + RMS_EPS) * g2_ref[...]
    x2_ref[...] = x2
    h2_ref[...] = h2

    lt = lax.dot_general(wr_ref[...], h2, (((1,), (1,)), ((), ())),
                         precision=lax.Precision.HIGHEST, preferred_element_type=F32)
    lt = lt + br_ref[...]
    tm = lt.shape[1]
    sub = lax.broadcasted_iota(I32, (N_GROUPS, tm), 0)
    g = lt[0:N_GROUPS, :]
    gmax, grp = _first_index_of_max(g, sub, N_GROUPS)
    p_grp = 1.0 / jnp.sum(jnp.exp(g - gmax), axis=0, keepdims=True)
    sel = jnp.zeros((EXPERTS_PER_GROUP, tm), F32)
    for gi in range(N_GROUPS):
        lo = N_GROUPS + gi * EXPERTS_PER_GROUP
        sel = jnp.where(grp == gi, lt[lo:lo + EXPERTS_PER_GROUP, :], sel)
    v1, i1 = _first_index_of_max(sel, sub, EXPERTS_PER_GROUP)
    rest = jnp.where(sub == i1, -jnp.inf, sel)
    v2, i2 = _first_index_of_max(rest, sub, EXPERTS_PER_GROUP)
    t = jnp.exp(v2 - v1)
    gate1 = p_grp / (1.0 + t)
    gate2 = p_grp * t / (1.0 + t)
    eid_ref[0:1, :] = grp * EXPERTS_PER_GROUP + i1
    eid_ref[1:2, :] = grp * EXPERTS_PER_GROUP + i2
    row = lax.broadcasted_iota(I32, (LANES, tm), 0)
    gl = jnp.where(row == 0, gate1, jnp.where(row == 1, gate2, 0.0))
    gcol_ref[...] = gl.T


def _out_proj_router(x, conv_out, attn_out, w_bf16, g2, wr_t, br_col, tm=256):
    s, d = x.shape
    dc = conv_out.shape[1]
    da = attn_out.shape[1]
    row = lambda i: (i, 0)
    const = lambda i: (0, 0)
    return pl.pallas_call(
        _outproj_body,
        grid=(s // tm,),
        in_specs=[
            pl.BlockSpec((tm, d), row),
            pl.BlockSpec((tm, dc), row),
            pl.BlockSpec((tm, da), row),
            pl.BlockSpec((dc + da, d), const, pipeline_mode=pl.Buffered(1)),
            pl.BlockSpec((1, d), const),
            pl.BlockSpec((LANES, d), const),
            pl.BlockSpec((LANES, 1), const),
        ],
        out_specs=[
            pl.BlockSpec((tm, d), row),
            pl.BlockSpec((tm, d), row),
            pl.BlockSpec((TOP_K, tm), lambda i: (0, i)),
            pl.BlockSpec((tm, LANES), row),
        ],
        out_shape=[
            jax.ShapeDtypeStruct((s, d), F32),
            jax.ShapeDtypeStruct((s, d), F32),
            jax.ShapeDtypeStruct((TOP_K, s), I32),
            jax.ShapeDtypeStruct((s, LANES), F32),
        ],
        compiler_params=_cparams("arbitrary"),
        name="out_proj_router",
    )(x, conv_out, attn_out, w_bf16, g2, wr_t, br_col)


META_LANES = 256


def _dispatch_body(eid_ref, dest_ref, meta_ref, rank_ref):
    s = eid_ref.shape[1]
    w = META_LANES
    sub = lax.broadcasted_iota(I32, (N_EXPERTS, w), 0)
    lane = lax.broadcasted_iota(I32, (N_EXPERTS, w), 1)
    ri = lax.broadcasted_iota(I32, (w, w), 0)
    ci = lax.broadcasted_iota(I32, (w, w), 1)
    upper = (ri < ci).astype(BF16)

    def onehots(b):
        off = pl.multiple_of(b * w, w)
        e0 = eid_ref[0:1, pl.ds(off, w)]
        e1 = eid_ref[1:2, pl.ds(off, w)]
        return off, sub == e0, sub == e1

    def rank_block(b, carry):
        off, o0, o1 = onehots(b)
        both = jnp.where(o0 | o1, 1.0, 0.0)
        pre = jnp.dot(both.astype(BF16), upper, preferred_element_type=F32) + carry
        rank_ref[0:1, pl.ds(off, w)] = jnp.sum(jnp.where(o0, pre, 0.0), axis=0, keepdims=True)
        rank_ref[1:2, pl.ds(off, w)] = jnp.sum(jnp.where(o1, pre, 0.0), axis=0, keepdims=True)
        return carry + jnp.sum(both, axis=1, keepdims=True)

    counts = lax.fori_loop(0, s // w, rank_block, jnp.zeros((N_EXPERTS, w), F32))

    nblk = ((counts.astype(I32) + (BLOCK_ROWS - 1)) // BLOCK_ROWS).astype(F32)
    er = lax.broadcasted_iota(I32, (N_EXPERTS, N_EXPERTS), 0)
    ec = lax.broadcasted_iota(I32, (N_EXPERTS, N_EXPERTS), 1)
    lower_incl = (ec <= er).astype(BF16)
    pend_blk = jnp.dot(lower_incl, nblk.astype(BF16), preferred_element_type=F32)
    pstart = (pend_blk - nblk) * float(BLOCK_ROWS)

    done = jnp.where(pend_blk.astype(I32) <= lane, 1, 0)
    block_exp = jnp.minimum(jnp.sum(done, axis=0, keepdims=True), N_EXPERTS - 1)
    used = pend_blk[N_EXPERTS - 1:N_EXPERTS, :].astype(I32)
    mrow = lax.broadcasted_iota(I32, (8, w), 0)
    meta_ref[...] = jnp.where(mrow == 0, block_exp, jnp.where(mrow == 1, used, 0))

    def dest_block(b, _):
        off, o0, o1 = onehots(b)
        d0 = rank_ref[0:1, pl.ds(off, w)] + jnp.sum(jnp.where(o0, pstart, 0.0), axis=0, keepdims=True)
        d1 = rank_ref[1:2, pl.ds(off, w)] + jnp.sum(jnp.where(o1, pstart, 0.0), axis=0, keepdims=True)
        dest_ref[0:1, pl.ds(off, w)] = d0.astype(I32)
        dest_ref[1:2, pl.ds(off, w)] = d1.astype(I32)
        return 0

    lax.fori_loop(0, s // w, dest_block, 0)


def _dispatch(eids):
    s = eids.shape[1]
    return pl.pallas_call(
        _dispatch_body,
        out_shape=[
            jax.ShapeDtypeStruct((TOP_K, s), I32),
            jax.ShapeDtypeStruct((8, META_LANES), I32),
        ],
        scratch_shapes=[pltpu.VMEM((TOP_K, s), F32)],
        compiler_params=pltpu.CompilerParams(vmem_limit_bytes=VMEM_LIMIT_BYTES),
        name="dispatch",
    )(eids)


def _invert_body(dest_ref, rowtok_ref):
    rows = rowtok_ref.shape[0]
    s = dest_ref.shape[1]

    def clear(r, _):
        rowtok_ref[r] = 0
        return 0

    lax.fori_loop(0, rows, clear, 0)

    def put(t, _):
        rowtok_ref[dest_ref[0, t]] = t
        rowtok_ref[dest_ref[1, t]] = t
        return 0

    lax.fori_loop(0, s, put, 0)


def _invert(dest, rows):
    return pl.pallas_call(
        _invert_body,
        in_specs=[pl.BlockSpec(memory_space=pltpu.SMEM)],
        out_specs=pl.BlockSpec(memory_space=pltpu.SMEM),
        out_shape=jax.ShapeDtypeStruct((rows,), I32),
        name="invert_dest",
    )(dest)


def _row_gather_start(src_hbm, dst, sem, rowtok_ref, base, n_rows):
    def one(r, _):
        tok = rowtok_ref[base + r]
        pltpu.make_async_copy(src_hbm.at[pl.ds(tok, 1)], dst.at[pl.ds(r, 1)], sem).start()
        return 0

    lax.fori_loop(0, n_rows, one, 0)


def _row_gather_wait(src_hbm, dst, sem):
    pltpu.make_async_copy(src_hbm.at[pl.ds(0, dst.shape[0])], dst, sem).wait()


def _experts_body(meta_ref, rowtok_ref, h2_hbm, wg_ref, wu_ref, wd_ref, ys_ref, xbuf, sem):
    b = pl.program_id(0)
    used = meta_ref[1, 0]
    slot = b % 2

    def start(blk, sl):
        _row_gather_start(h2_hbm, xbuf.at[sl], sem.at[sl], rowtok_ref, blk * BLOCK_ROWS, BLOCK_ROWS)

    @pl.when((b == 0) & (used > 0))
    def _():
        start(0, 0)

    @pl.when(b + 1 < used)
    def _():
        start(b + 1, 1 - slot)

    @pl.when(b < used)
    def _():
        _row_gather_wait(h2_hbm, xbuf.at[slot], sem.at[slot])
        x = xbuf[slot].astype(BF16)
        a = jnp.dot(x, wg_ref[...].astype(BF16), preferred_element_type=F32)
        u = jnp.dot(x, wu_ref[...].astype(BF16), preferred_element_type=F32)
        act = (a * _sigmoid(a) * u).astype(BF16)
        ys_ref[...] = jnp.dot(act, wd_ref[...].astype(BF16), preferred_element_type=F32)

    @pl.when(b >= used)
    def _():
        ys_ref[...] = jnp.zeros_like(ys_ref)


def _experts(meta, rowtok, h2, w_gate, w_up, w_down):
    s, d = h2.shape
    n_e, _, de = w_gate.shape
    rows = rowtok.shape[0]
    n_blocks = rows // BLOCK_ROWS
    wsel = lambda b, meta_ref, rt_ref: (meta_ref[0, b], 0, 0)
    grid_spec = pltpu.PrefetchScalarGridSpec(
        num_scalar_prefetch=2,
        grid=(n_blocks,),
        in_specs=[
            pl.BlockSpec(memory_space=pl.ANY),
            pl.BlockSpec((None, d, de), wsel),
            pl.BlockSpec((None, d, de), wsel),
            pl.BlockSpec((None, de, d), wsel),
        ],
        out_specs=pl.BlockSpec((BLOCK_ROWS, d), lambda b, meta_ref, rt_ref: (b, 0)),
        scratch_shapes=[
            pltpu.VMEM((2, BLOCK_ROWS, d), F32),
            pltpu.SemaphoreType.DMA((2,)),
        ],
    )
    return pl.pallas_call(
        _experts_body,
        grid_spec=grid_spec,
        out_shape=jax.ShapeDtypeStruct((rows, d), F32),
        compiler_params=_cparams("arbitrary"),
        name="expert_mlp",
    )(meta, rowtok, h2, w_gate, w_up, w_down)


def _combine_body(dest_ref, x2_ref, gcol_ref, fg_ref, ys_hbm, o_ref, ybuf, sem, *, tm):
    i = pl.program_id(0)
    n = pl.num_programs(0)
    s = n * tm
    slot = i % 2

    def start(tile, sl):
        for k in range(TOP_K):
            _row_gather_start(ys_hbm, ybuf.at[sl, k], sem.at[sl, k], dest_ref, k * s + tile * tm, tm)

    @pl.when(i == 0)
    def _():
        start(0, 0)

    @pl.when(i + 1 < n)
    def _():
        start(i + 1, 1 - slot)

    for k in range(TOP_K):
        _row_gather_wait(ys_hbm, ybuf.at[slot, k], sem.at[slot, k])
    g = gcol_ref[...]
    moe = g[:, 0:1] * ybuf[slot, 0] + g[:, 1:2] * ybuf[slot, 1]
    x3 = x2_ref[...] + moe
    ms = jnp.mean(x3 * x3, axis=-1, keepdims=True)
    o_ref[...] = x3 * lax.rsqrt(ms + RMS_EPS) * fg_ref[...]


def _combine(dest_flat, x2, gcol, final_g, ys, tm=128):
    s, d = x2.shape
    row = lambda i, dest_ref: (i, 0)
    const = lambda i, dest_ref: (0, 0)
    grid_spec = pltpu.PrefetchScalarGridSpec(
        num_scalar_prefetch=1,
        grid=(s // tm,),
        in_specs=[
            pl.BlockSpec((tm, d), row),
            pl.BlockSpec((tm, LANES), row),
            pl.BlockSpec((1, d), const),
            pl.BlockSpec(memory_space=pl.ANY),
        ],
        out_specs=pl.BlockSpec((tm, d), row),
        scratch_shapes=[
            pltpu.VMEM((2, TOP_K, tm, d), F32),
            pltpu.SemaphoreType.DMA((2, TOP_K)),
        ],
    )
    return pl.pallas_call(
        functools.partial(_combine_body, tm=tm),
        grid_spec=grid_spec,
        out_shape=jax.ShapeDtypeStruct((s, d), F32),
        compiler_params=_cparams("arbitrary"),
        name="combine_norm",
    )(dest_flat, x2, gcol, final_g, ys)


def _layer(x, norm1_g, w_in, w_dw, b_dw, ln_g, ln_b, rel_bias, w_out, norm2_g,
           w_group, b_group, w_expert_router, b_expert_router, w_gate, w_up, w_down, out_g):
    s, d = x.shape
    dc = w_dw.shape[-1]
    da = (w_in.shape[1] - 2 * dc) // 3
    row2 = lambda a: a.reshape(1, -1)

    cv, cg, q, k, v = _in_proj(x, row2(norm1_g), w_in.astype(BF16), dc, da)
    conv_out = _conv_module(cv, cg, w_dw.reshape(CONV_WIDTH, dc), row2(b_dw), row2(ln_g), row2(ln_b))
    front = ((FRONT_PAD, 0), (0, 0))
    attn_out = _attention(q, jnp.pad(k, front), jnp.pad(v, front), _pair_bias(rel_bias))

    n_r = N_GROUPS + N_EXPERTS
    wr_t = jnp.zeros((LANES, d), F32).at[:n_r].set(jnp.concatenate([w_group, w_expert_router], axis=1).T)
    br_col = jnp.zeros((LANES, 1), F32).at[:n_r, 0].set(jnp.concatenate([b_group, b_expert_router]))
    x2, h2, eids, gcol = _out_proj_router(x, conv_out, attn_out, w_out.astype(BF16), row2(norm2_g),
                                          wr_t, br_col)

    n_blocks = (s * TOP_K) // BLOCK_ROWS + N_EXPERTS
    assert n_blocks <= META_LANES
    dest, meta = _dispatch(eids)
    rowtok = _invert(dest, n_blocks * BLOCK_ROWS)
    ys = _experts(meta, rowtok, h2, w_gate, w_up, w_down)
    return _combine(dest.reshape(-1), x2, gcol, row2(out_g), ys)


def kernel(x, norm1_g, w_in, w_dw, b_dw, ln_g, ln_b, rel_bias, w_out, norm2_g, w_group, b_group,
           w_expert_router, b_expert_router, w_gate, w_up, w_down, final_g):
    depth = norm1_g.shape[0]
    assert depth == 1, "the final norm is fused into the last layer's combine step"
    outs = []
    for bi in range(x.shape[0]):
        outs.append(_layer(x[bi], norm1_g[0], w_in[0], w_dw[0], b_dw[0], ln_g[0], ln_b[0], rel_bias[0],
                           w_out[0], norm2_g[0], w_group[0], b_group[0], w_expert_router[0],
                           b_expert_router[0], w_gate[0], w_up[0], w_down[0], final_g))
    return jnp.stack(outs, axis=0)
```

```python
import functools

import jax
import jax.numpy as jnp
from jax import lax
from jax.experimental import pallas as pl
from jax.experimental.pallas import tpu as pltpu

F32 = jnp.float32
BF16 = jnp.bfloat16
I32 = jnp.int32

CHUNK = 64
HEAD_DIM = 64
LEFT_CHUNKS = 8
REL_CLIP = 128
CONV_WIDTH = 31
N_GROUPS = 8
EXPERTS_PER_GROUP = 8
N_EXPERTS = N_GROUPS * EXPERTS_PER_GROUP
TOP_K = 2
BLOCK_ROWS = 128
RMS_EPS = 1e-6
LN_EPS = 1e-5
NEG_INF = -1e30

VMEM_LIMIT_BYTES = 56 * 1024 * 1024
LANES = 128
SUBLANES = 8
PAIR = 2 * CHUNK
BAND_PAIR = (LEFT_CHUNKS + 2) * CHUNK
FRONT_PAD = LEFT_CHUNKS * CHUNK
CONV_HALO = 32


def _cparams(*sem):
    return pltpu.CompilerParams(dimension_semantics=sem, vmem_limit_bytes=VMEM_LIMIT_BYTES)


def _inproj_body(x_ref, g_ref, w_ref, cv_ref, cg_ref, q_ref, k_ref, v_ref):
    x = x_ref[...]
    ms = jnp.mean(x * x, axis=-1, keepdims=True)
    h = (x * lax.rsqrt(ms + RMS_EPS) * g_ref[...]).astype(BF16)
    dc = cv_ref.shape[1]
    da = q_ref.shape[1]

    def proj(lo, width):
        return jnp.dot(h, w_ref[:, lo:lo + width], preferred_element_type=F32)

    cv_ref[...] = proj(0, dc)
    cg_ref[...] = proj(dc, dc)
    q_ref[...] = (proj(2 * dc, da) * (HEAD_DIM ** -0.5)).astype(BF16)
    k_ref[...] = proj(2 * dc + da, da).astype(BF16)
    v_ref[...] = proj(2 * dc + 2 * da, da).astype(BF16)


def _in_proj(x, g, w_bf16, d_conv, d_att, tm=256):
    s, d = x.shape
    d_in = w_bf16.shape[1]
    row = lambda i: (i, 0)
    const = lambda i: (0, 0)
    return pl.pallas_call(
        _inproj_body,
        grid=(s // tm,),
        in_specs=[
            pl.BlockSpec((tm, d), row),
            pl.BlockSpec((1, d), const),
            pl.BlockSpec((d, d_in), const, pipeline_mode=pl.Buffered(1)),
        ],
        out_specs=[
            pl.BlockSpec((tm, d_conv), row),
            pl.BlockSpec((tm, d_conv), row),
            pl.BlockSpec((tm, d_att), row),
            pl.BlockSpec((tm, d_att), row),
            pl.BlockSpec((tm, d_att), row),
        ],
        out_shape=[
            jax.ShapeDtypeStruct((s, d_conv), F32),
            jax.ShapeDtypeStruct((s, d_conv), F32),
            jax.ShapeDtypeStruct((s, d_att), BF16),
            jax.ShapeDtypeStruct((s, d_att), BF16),
            jax.ShapeDtypeStruct((s, d_att), BF16),
        ],
        compiler_params=_cparams("arbitrary"),
        name="in_proj",
    )(x, g, w_bf16)


def _sigmoid(x):
    return 1.0 / (1.0 + jnp.exp(-x))


def _conv_body(cv_ref, cg_ref, cvh_ref, cgh_ref, w_ref, b_ref, lg_ref, lb_ref, o_ref, u_ref, us_ref,
               *, rows_per_chunk):
    i = pl.program_id(0)
    t = cv_ref.shape[0]
    halo = cvh_ref[...] * _sigmoid(cgh_ref[...])
    u_ref[0:CONV_HALO, :] = jnp.where(i > 0, halo, 0.0)
    u_ref[CONV_HALO:, :] = cv_ref[...] * _sigmoid(cg_ref[...])
    n_shift_rows = us_ref.shape[1]
    for sft in range(1, SUBLANES):
        us_ref[sft - 1] = u_ref[sft:sft + n_shift_rows, :]
    r = rows_per_chunk
    first_tap = CONV_HALO - (CONV_WIDTH - 1)
    for c in range(t // r):
        acc = jnp.broadcast_to(b_ref[...], (r, b_ref.shape[1]))
        for w in range(CONV_WIDTH):
            off = first_tap + w
            sft = off % SUBLANES
            lo = c * r + off - sft
            src = u_ref[lo:lo + r, :] if sft == 0 else us_ref[sft - 1, lo:lo + r, :]
            acc = acc + w_ref[w:w + 1, :] * src
        mu = jnp.mean(acc, axis=-1, keepdims=True)
        cen = acc - mu
        var = jnp.mean(cen * cen, axis=-1, keepdims=True)
        un = cen * lax.rsqrt(var + LN_EPS) * lg_ref[...] + lb_ref[...]
        o_ref[c * r:(c + 1) * r, :] = (un * _sigmoid(un)).astype(o_ref.dtype)


def _conv_module(cv, cg, w_dw, b_dw, ln_g, ln_b, t=256, rows_per_chunk=16):
    s, dc = cv.shape
    halo_blocks = t // CONV_HALO
    row = lambda i: (i, 0)
    prev = lambda i: (jnp.maximum(i * halo_blocks - 1, 0), 0)
    const = lambda i: (0, 0)
    return pl.pallas_call(
        functools.partial(_conv_body, rows_per_chunk=rows_per_chunk),
        grid=(s // t,),
        in_specs=[
            pl.BlockSpec((t, dc), row),
            pl.BlockSpec((t, dc), row),
            pl.BlockSpec((CONV_HALO, dc), prev),
            pl.BlockSpec((CONV_HALO, dc), prev),
            pl.BlockSpec((CONV_WIDTH, dc), const),
            pl.BlockSpec((1, dc), const),
            pl.BlockSpec((1, dc), const),
            pl.BlockSpec((1, dc), const),
        ],
        out_specs=pl.BlockSpec((t, dc), row),
        out_shape=jax.ShapeDtypeStruct((s, dc), BF16),
        scratch_shapes=[
            pltpu.VMEM((t + CONV_HALO, dc), F32),
            pltpu.VMEM((SUBLANES - 1, t + CONV_HALO - SUBLANES, dc), F32),
        ],
        compiler_params=_cparams("arbitrary"),
        name="conv_module",
    )(cv, cg, cv, cg, w_dw, b_dw, ln_g, ln_b)


def _attn_pair(q2, kb, vb, bias_ref, col_lo):
    n_keys = kb.shape[0]
    lane = lax.broadcasted_iota(I32, q2.shape, 1)
    outs = []
    for h in range(2):
        in_head = (lane >= h * HEAD_DIM) & (lane < (h + 1) * HEAD_DIM)
        qh = jnp.where(in_head, q2, jnp.zeros_like(q2))
        sc = lax.dot_general(qh, kb, (((1,), (1,)), ((), ())), preferred_element_type=F32)
        sc = sc + bias_ref[h, :, col_lo:col_lo + n_keys]
        m = jnp.max(sc, axis=-1, keepdims=True)
        e = jnp.exp(sc - m)
        den = jnp.sum(e, axis=-1, keepdims=True)
        o = jnp.dot(e.astype(BF16), vb, preferred_element_type=F32)
        outs.append(o * (1.0 / den))
    return jnp.where(lane < HEAD_DIM, outs[0], outs[1])


def _attn_body(q_ref, k_ref, v_ref, g_ref, o_ref, bias_ref, *, pairs_per_step):
    st = pl.program_id(1)
    npp = pairs_per_step

    @pl.when(st == 0)
    def _():
        qi = lax.broadcasted_iota(I32, (PAIR, BAND_PAIR), 0)
        kj = lax.broadcasted_iota(I32, (PAIR, BAND_PAIR), 1)
        band_lo = jnp.where(qi >= CHUNK, CHUNK, 0)
        in_band = (kj >= band_lo) & (kj < band_lo + (LEFT_CHUNKS + 1) * CHUNK)
        for h in range(2):
            t = jnp.broadcast_to(g_ref[h], (PAIR, BAND_PAIR))
            shift = 1
            while shift < PAIR:
                t = jnp.where((qi & shift) != 0, pltpu.roll(t, shift, 1), t)
                shift *= 2
            bias_ref[h] = jnp.where(in_band, t, NEG_INF)

    def run(first_step):
        for u in range(npp):
            q2 = q_ref[u * PAIR:(u + 1) * PAIR, :]
            if first_step:
                n_keys = min((u + 1) * PAIR, BAND_PAIR)
                lo = max((u + 1) * PAIR - BAND_PAIR, 0)
                kb = k_ref[lo:lo + n_keys, :]
                vb = v_ref[lo:lo + n_keys, :]
                col_lo = BAND_PAIR - n_keys
            else:
                start = pl.multiple_of((st * npp + u) * PAIR - FRONT_PAD, PAIR)
                kb = k_ref[pl.ds(start, BAND_PAIR), :]
                vb = v_ref[pl.ds(start, BAND_PAIR), :]
                col_lo = 0
            o = _attn_pair(q2, kb, vb, bias_ref, col_lo)
            o_ref[u * PAIR:(u + 1) * PAIR, :] = o.astype(o_ref.dtype)

    @pl.when(st == 0)
    def _():
        run(True)

    @pl.when(st > 0)
    def _():
        run(False)


def _distance_rows(rel_table):
    t = rel_table.astype(F32)
    far = t[:, 2 * REL_CLIP:]
    n_far = LEFT_CHUNKS * CHUNK - REL_CLIP + 1
    near = jnp.flip(t[:, REL_CLIP - CHUNK + 1:2 * REL_CLIP], axis=1)
    tail = BAND_PAIR - n_far - near.shape[1]
    g = jnp.concatenate([jnp.repeat(far, n_far, axis=1), near, jnp.repeat(far, tail, axis=1)], axis=1)
    return g[:, None, :]


def _attention(q, k, v, g_rows, pairs_per_step=4):
    s, da = q.shape
    n_hp = da // (2 * HEAD_DIM)
    tq = pairs_per_step * PAIR
    assert pairs_per_step * PAIR >= FRONT_PAD, "first step must cover every clamped band"
    return pl.pallas_call(
        functools.partial(_attn_body, pairs_per_step=pairs_per_step),
        grid=(n_hp, s // tq),
        in_specs=[
            pl.BlockSpec((tq, 2 * HEAD_DIM), lambda hp, st: (st, hp)),
            pl.BlockSpec((s, 2 * HEAD_DIM), lambda hp, st: (0, hp)),
            pl.BlockSpec((s, 2 * HEAD_DIM), lambda hp, st: (0, hp)),
            pl.BlockSpec((2, 1, BAND_PAIR), lambda hp, st: (hp, 0, 0)),
        ],
        out_specs=pl.BlockSpec((tq, 2 * HEAD_DIM), lambda hp, st: (st, hp)),
        out_shape=jax.ShapeDtypeStruct((s, da), BF16),
        scratch_shapes=[pltpu.VMEM((2, PAIR, BAND_PAIR), F32)],
        compiler_params=_cparams("arbitrary", "arbitrary"),
        name="chunk_attention",
    )(q, k, v, g_rows)


def _first_index_of_max(vals, sub, n):
    vmax = jnp.max(vals, axis=0, keepdims=True)
    idx = jnp.min(jnp.where(vals == vmax, sub, n), axis=0, keepdims=True)
    return vmax, idx


def _outproj_body(x_ref, co_ref, ao_ref, w_ref, g2_ref, wr_ref, br_ref,
                  x2_ref, h2_ref, eid_ref, gcol_ref):
    dc = co_ref.shape[1]
    y = jnp.dot(co_ref[...], w_ref[0:dc, :], preferred_element_type=F32)
    y = y + jnp.dot(ao_ref[...], w_ref[dc:, :], preferred_element_type=F32)
    x2 = x_ref[...] + y
    ms = jnp.mean(x2 * x2, axis=-1, keepdims=True)
    h2 = x2 * lax.rsqrt(ms + RMS_EPS) * g2_ref[...]
    x2_ref[...] = x2
    h2_ref[...] = h2

    lt = lax.dot_general(wr_ref[...], h2, (((1,), (1,)), ((), ())),
                         precision=lax.Precision.HIGHEST, preferred_element_type=F32)
    lt = lt + br_ref[...]
    tm = lt.shape[1]
    sub = lax.broadcasted_iota(I32, (N_GROUPS, tm), 0)
    g = lt[0:N_GROUPS, :]
    gmax, grp = _first_index_of_max(g, sub, N_GROUPS)
    p_grp = 1.0 / jnp.sum(jnp.exp(g - gmax), axis=0, keepdims=True)
    sel = jnp.zeros((EXPERTS_PER_GROUP, tm), F32)
    for gi in range(N_GROUPS):
        lo = N_GROUPS + gi * EXPERTS_PER_GROUP
        sel = jnp.where(grp == gi, lt[lo:lo + EXPERTS_PER_GROUP, :], sel)
    v1, i1 = _first_index_of_max(sel, sub, EXPERTS_PER_GROUP)
    rest = jnp.where(sub == i1, -jnp.inf, sel)
    v2, i2 = _first_index_of_max(rest, sub, EXPERTS_PER_GROUP)
    t = jnp.exp(v2 - v1)
    gate1 = p_grp / (1.0 + t)
    gate2 = p_grp * t / (1.0 + t)
    eid_ref[0:1, :] = grp * EXPERTS_PER_GROUP + i1
    eid_ref[1:2, :] = grp * EXPERTS_PER_GROUP + i2
    row = lax.broadcasted_iota(I32, (LANES, tm), 0)
    gl = jnp.where(row == 0, gate1, jnp.where(row == 1, gate2, 0.0))
    gcol_ref[...] = gl.T


def _out_proj_router(x, conv_out, attn_out, w_bf16, g2, wr_t, br_col, tm=256):
    s, d = x.shape
    dc = conv_out.shape[1]
    da = attn_out.shape[1]
    row = lambda i: (i, 0)
    const = lambda i: (0, 0)
    return pl.pallas_call(
        _outproj_body,
        grid=(s // tm,),
        in_specs=[
            pl.BlockSpec((tm, d), row),
            pl.BlockSpec((tm, dc), row),
            pl.BlockSpec((tm, da), row),
            pl.BlockSpec((dc + da, d), const, pipeline_mode=pl.Buffered(1)),
            pl.BlockSpec((1, d), const),
            pl.BlockSpec((LANES, d), const),
            pl.BlockSpec((LANES, 1), const),
        ],
        out_specs=[
            pl.BlockSpec((tm, d), row),
            pl.BlockSpec((tm, d), row),
            pl.BlockSpec((TOP_K, tm), lambda i: (0, i)),
            pl.BlockSpec((tm, LANES), row),
        ],
        out_shape=[
            jax.ShapeDtypeStruct((s, d), F32),
            jax.ShapeDtypeStruct((s, d), F32),
            jax.ShapeDtypeStruct((TOP_K, s), I32),
            jax.ShapeDtypeStruct((s, LANES), F32),
        ],
        compiler_params=_cparams("arbitrary"),
        name="out_proj_router",
    )(x, conv_out, attn_out, w_bf16, g2, wr_t, br_col)


META_LANES = 256
META_FIRST_BLK, META_N_BLK, META_COUNT, META_WEIGHT_EXP, META_USED_BLKS = range(5)


def _dispatch_body(eid_ref, dest_ref, meta_ref, rank_ref):
    s = eid_ref.shape[1]
    w = META_LANES
    sub = lax.broadcasted_iota(I32, (N_EXPERTS, w), 0)
    lane = lax.broadcasted_iota(I32, (N_EXPERTS, w), 1)
    ri = lax.broadcasted_iota(I32, (w, w), 0)
    ci = lax.broadcasted_iota(I32, (w, w), 1)
    upper = (ri < ci).astype(BF16)

    def onehots(b):
        off = pl.multiple_of(b * w, w)
        e0 = eid_ref[0:1, pl.ds(off, w)]
        e1 = eid_ref[1:2, pl.ds(off, w)]
        return off, sub == e0, sub == e1

    def rank_block(b, carry):
        off, o0, o1 = onehots(b)
        both = jnp.where(o0 | o1, 1.0, 0.0)
        pre = jnp.dot(both.astype(BF16), upper, preferred_element_type=F32) + carry
        rank_ref[0:1, pl.ds(off, w)] = jnp.sum(jnp.where(o0, pre, 0.0), axis=0, keepdims=True)
        rank_ref[1:2, pl.ds(off, w)] = jnp.sum(jnp.where(o1, pre, 0.0), axis=0, keepdims=True)
        return carry + jnp.sum(both, axis=1, keepdims=True)

    counts = lax.fori_loop(0, s // w, rank_block, jnp.zeros((N_EXPERTS, w), F32))

    nblk = ((counts.astype(I32) + (BLOCK_ROWS - 1)) // BLOCK_ROWS).astype(F32)
    er = lax.broadcasted_iota(I32, (N_EXPERTS, N_EXPERTS), 0)
    ec = lax.broadcasted_iota(I32, (N_EXPERTS, N_EXPERTS), 1)
    lower_incl = (ec <= er).astype(BF16)
    pend_blk = jnp.dot(lower_incl, nblk.astype(BF16), preferred_element_type=F32)
    pstart = (pend_blk - nblk) * float(BLOCK_ROWS)

    def on_lanes(v):
        return jnp.sum(jnp.where(sub == lane, v, 0), axis=0, keepdims=True)

    nblk_i = nblk.astype(I32)
    first_blk = on_lanes((pend_blk - nblk).astype(I32))
    n_blk = on_lanes(nblk_i)
    count = on_lanes(counts.astype(I32))
    has_rows = nblk_i > 0
    prev_used = jnp.max(jnp.where(has_rows & (sub <= lane), sub, -1), axis=0, keepdims=True)
    first_used = jnp.min(jnp.where(has_rows, sub, N_EXPERTS), axis=0, keepdims=True)
    weight_exp = jnp.where(prev_used >= 0, prev_used, jnp.minimum(first_used, N_EXPERTS - 1))
    used = pend_blk[N_EXPERTS - 1:N_EXPERTS, :].astype(I32)
    mrow = lax.broadcasted_iota(I32, (8, w), 0)
    meta = jnp.where(mrow == META_FIRST_BLK, first_blk, 0)
    meta = jnp.where(mrow == META_N_BLK, n_blk, meta)
    meta = jnp.where(mrow == META_COUNT, count, meta)
    meta = jnp.where(mrow == META_WEIGHT_EXP, weight_exp, meta)
    meta_ref[...] = jnp.where(mrow == META_USED_BLKS, used, meta)

    def dest_block(b, _):
        off, o0, o1 = onehots(b)
        d0 = rank_ref[0:1, pl.ds(off, w)] + jnp.sum(jnp.where(o0, pstart, 0.0), axis=0, keepdims=True)
        d1 = rank_ref[1:2, pl.ds(off, w)] + jnp.sum(jnp.where(o1, pstart, 0.0), axis=0, keepdims=True)
        dest_ref[0:1, pl.ds(off, w)] = d0.astype(I32)
        dest_ref[1:2, pl.ds(off, w)] = d1.astype(I32)
        return 0

    lax.fori_loop(0, s // w, dest_block, 0)


def _dispatch(eids):
    s = eids.shape[1]
    return pl.pallas_call(
        _dispatch_body,
        out_shape=[
            jax.ShapeDtypeStruct((TOP_K, s), I32),
            jax.ShapeDtypeStruct((8, META_LANES), I32),
        ],
        scratch_shapes=[pltpu.VMEM((TOP_K, s), F32)],
        compiler_params=pltpu.CompilerParams(vmem_limit_bytes=VMEM_LIMIT_BYTES),
        name="dispatch",
    )(eids)


def _invert_body(dest_ref, meta_ref, rowtok_ref, blkcnt_ref):
    s = dest_ref.shape[1]
    n_blocks = rowtok_ref.shape[0] // BLOCK_ROWS
    used = meta_ref[META_USED_BLKS, 0]

    def per_expert(e, _):
        b0 = meta_ref[META_FIRST_BLK, e]
        nb = meta_ref[META_N_BLK, e]
        cnt = meta_ref[META_COUNT, e]

        def per_block(j, _):
            left = (cnt - j * BLOCK_ROWS + SUBLANES - 1) // SUBLANES * SUBLANES
            blkcnt_ref[b0 + j] = jnp.minimum(left, BLOCK_ROWS)
            return 0

        lax.fori_loop(0, nb, per_block, 0)

        def pad_row(r, _):
            rowtok_ref[r] = 0
            return 0

        lax.fori_loop(b0 * BLOCK_ROWS + cnt, (b0 + nb) * BLOCK_ROWS, pad_row, 0)
        return 0

    lax.fori_loop(0, N_EXPERTS, per_expert, 0)

    def unused_block(b, _):
        blkcnt_ref[b] = 0
        return 0

    lax.fori_loop(used, blkcnt_ref.shape[0], unused_block, 0)

    def unused_row(r, _):
        rowtok_ref[r] = 0
        return 0

    lax.fori_loop(used * BLOCK_ROWS, n_blocks * BLOCK_ROWS, unused_row, 0)

    def put(t, _):
        rowtok_ref[dest_ref[0, t]] = t
        rowtok_ref[dest_ref[1, t]] = t
        return 0

    lax.fori_loop(0, s, put, 0, unroll=8)


def _invert(dest, meta, rows):
    smem = pl.BlockSpec(memory_space=pltpu.SMEM)
    return pl.pallas_call(
        _invert_body,
        in_specs=[smem, smem],
        out_specs=[smem, smem],
        out_shape=[
            jax.ShapeDtypeStruct((rows,), I32),
            jax.ShapeDtypeStruct((META_LANES,), I32),
        ],
        name="invert_dest",
    )(dest, meta)


def _row_gather_start(src_hbm, dst, sem, rowtok_ref, base, n_rows):
    def one(r, _):
        tok = rowtok_ref[base + r]
        pltpu.make_async_copy(src_hbm.at[pl.ds(tok, 1)], dst.at[pl.ds(r, 1)], sem).start()
        return 0

    lax.fori_loop(0, n_rows, one, 0)


def _row_gather_wait(src_hbm, dst, sem, n_rows):
    if not isinstance(n_rows, int):
        n_rows = pl.multiple_of(n_rows, SUBLANES)
    pltpu.make_async_copy(src_hbm.at[pl.ds(0, n_rows)], dst.at[pl.ds(0, n_rows)], sem).wait()


def _experts_body(meta_ref, rowtok_ref, blkcnt_ref, h2_hbm, wg_ref, wu_ref, wd_ref, ys_hbm,
                  xbuf, ybuf, zbuf, wgb, wub, wdb, gsem, ysem, zsem):
    e = pl.program_id(0)
    b0 = meta_ref[META_FIRST_BLK, e]
    nb = meta_ref[META_N_BLK, e]
    used = meta_ref[META_USED_BLKS, 0]

    def gather_start(b):
        sl = b % 2
        _row_gather_start(h2_hbm, xbuf.at[sl], gsem.at[sl], rowtok_ref, b * BLOCK_ROWS, blkcnt_ref[b])

    def ys_copy(b):
        sl = b % 2
        rows = pl.ds(pl.multiple_of(b * BLOCK_ROWS, BLOCK_ROWS), BLOCK_ROWS)
        return pltpu.make_async_copy(ybuf.at[sl], ys_hbm.at[rows], ysem.at[sl])

    n_blocks = ys_hbm.shape[0] // BLOCK_ROWS

    def zero_copy(b):
        rows = pl.ds(pl.multiple_of(b * BLOCK_ROWS, BLOCK_ROWS), BLOCK_ROWS)
        return pltpu.make_async_copy(zbuf, ys_hbm.at[rows], zsem)

    @pl.when(e == 0)
    def _():
        xbuf[...] = jnp.zeros_like(xbuf)
        gather_start(0)
        zbuf[...] = jnp.zeros_like(zbuf)
        lax.fori_loop(used, n_blocks, lambda b, _: (zero_copy(b).start(), 0)[1], 0)

    @pl.when(nb > 0)
    def _():
        wgb[...] = wg_ref[...].astype(BF16)
        wub[...] = wu_ref[...].astype(BF16)
        wdb[...] = wd_ref[...].astype(BF16)

        def block(j, _):
            b = b0 + j
            sl = b % 2

            @pl.when(b + 1 < used)
            def _():
                gather_start(b + 1)

            _row_gather_wait(h2_hbm, xbuf.at[sl], gsem.at[sl], blkcnt_ref[b])

            @pl.when(b >= 2)
            def _():
                ys_copy(b - 2).wait()

            x = xbuf[sl].astype(BF16)
            a = jnp.dot(x, wgb[...], preferred_element_type=F32)
            u = jnp.dot(x, wub[...], preferred_element_type=F32)
            act = (a * _sigmoid(a) * u).astype(BF16)
            ybuf[sl] = jnp.dot(act, wdb[...], preferred_element_type=F32)
            ys_copy(b).start()
            return 0

        lax.fori_loop(0, nb, block, 0)

    @pl.when(e == pl.num_programs(0) - 1)
    def _():
        for back in (1, 2):
            @pl.when(used >= back)
            def _():
                ys_copy(used - back).wait()
        lax.fori_loop(used, n_blocks, lambda b, _: (zero_copy(b).wait(), 0)[1], 0)


def _experts(meta, rowtok, blkcnt, h2, w_gate, w_up, w_down):
    s, d = h2.shape
    n_e, _, de = w_gate.shape
    rows = rowtok.shape[0]
    wsel = lambda e, meta_ref, rt_ref, bc_ref: (meta_ref[META_WEIGHT_EXP, e], 0, 0)
    grid_spec = pltpu.PrefetchScalarGridSpec(
        num_scalar_prefetch=3,
        grid=(n_e,),
        in_specs=[
            pl.BlockSpec(memory_space=pl.ANY),
            pl.BlockSpec((None, d, de), wsel),
            pl.BlockSpec((None, d, de), wsel),
            pl.BlockSpec((None, de, d), wsel),
        ],
        out_specs=pl.BlockSpec(memory_space=pl.ANY),
        scratch_shapes=[
            pltpu.VMEM((2, BLOCK_ROWS, d), F32),
            pltpu.VMEM((2, BLOCK_ROWS, d), F32),
            pltpu.VMEM((BLOCK_ROWS, d), F32),
            pltpu.VMEM((d, de), BF16),
            pltpu.VMEM((d, de), BF16),
            pltpu.VMEM((de, d), BF16),
            pltpu.SemaphoreType.DMA((2,)),
            pltpu.SemaphoreType.DMA((2,)),
            pltpu.SemaphoreType.DMA(()),
        ],
    )
    return pl.pallas_call(
        _experts_body,
        grid_spec=grid_spec,
        out_shape=jax.ShapeDtypeStruct((rows, d), F32),
        compiler_params=_cparams("arbitrary"),
        name="expert_mlp",
    )(meta, rowtok, blkcnt, h2, w_gate, w_up, w_down)


def _combine_body(dest_ref, x2_ref, gcol_ref, fg_ref, ys_hbm, o_ref, ybuf, sem, *, tm):
    i = pl.program_id(0)
    n = pl.num_programs(0)
    s = n * tm
    slot = i % 2

    def start(tile, sl):
        for k in range(TOP_K):
            _row_gather_start(ys_hbm, ybuf.at[sl, k], sem.at[sl, k], dest_ref, k * s + tile * tm, tm)

    @pl.when(i == 0)
    def _():
        start(0, 0)

    @pl.when(i + 1 < n)
    def _():
        start(i + 1, 1 - slot)

    for k in range(TOP_K):
        _row_gather_wait(ys_hbm, ybuf.at[slot, k], sem.at[slot, k], tm)
    g = gcol_ref[...]
    moe = g[:, 0:1] * ybuf[slot, 0] + g[:, 1:2] * ybuf[slot, 1]
    x3 = x2_ref[...] + moe
    ms = jnp.mean(x3 * x3, axis=-1, keepdims=True)
    o_ref[...] = x3 * lax.rsqrt(ms + RMS_EPS) * fg_ref[...]


def _combine(dest_flat, x2, gcol, final_g, ys, tm=128):
    s, d = x2.shape
    row = lambda i, dest_ref: (i, 0)
    const = lambda i, dest_ref: (0, 0)
    grid_spec = pltpu.PrefetchScalarGridSpec(
        num_scalar_prefetch=1,
        grid=(s // tm,),
        in_specs=[
            pl.BlockSpec((tm, d), row),
            pl.BlockSpec((tm, LANES), row),
            pl.BlockSpec((1, d), const),
            pl.BlockSpec(memory_space=pl.ANY),
        ],
        out_specs=pl.BlockSpec((tm, d), row),
        scratch_shapes=[
            pltpu.VMEM((2, TOP_K, tm, d), F32),
            pltpu.SemaphoreType.DMA((2, TOP_K)),
        ],
    )
    return pl.pallas_call(
        functools.partial(_combine_body, tm=tm),
        grid_spec=grid_spec,
        out_shape=jax.ShapeDtypeStruct((s, d), F32),
        compiler_params=_cparams("arbitrary"),
        name="combine_norm",
    )(dest_flat, x2, gcol, final_g, ys)


def _layer(x, norm1_g, w_in, w_dw, b_dw, ln_g, ln_b, rel_bias, w_out, norm2_g,
           w_group, b_group, w_expert_router, b_expert_router, w_gate, w_up, w_down, out_g):
    s, d = x.shape
    dc = w_dw.shape[-1]
    da = (w_in.shape[1] - 2 * dc) // 3
    row2 = lambda a: a.reshape(1, -1)

    cv, cg, q, k, v = _in_proj(x, row2(norm1_g), w_in.astype(BF16), dc, da)
    conv_out = _conv_module(cv, cg, w_dw.reshape(CONV_WIDTH, dc), row2(b_dw), row2(ln_g), row2(ln_b))
    attn_out = _attention(q, k, v, _distance_rows(rel_bias))

    n_r = N_GROUPS + N_EXPERTS
    wr_t = jnp.zeros((LANES, d), F32).at[:n_r].set(jnp.concatenate([w_group, w_expert_router], axis=1).T)
    br_col = jnp.zeros((LANES, 1), F32).at[:n_r, 0].set(jnp.concatenate([b_group, b_expert_router]))
    x2, h2, eids, gcol = _out_proj_router(x, conv_out, attn_out, w_out.astype(BF16), row2(norm2_g),
                                          wr_t, br_col)

    n_blocks = (s * TOP_K) // BLOCK_ROWS + N_EXPERTS
    assert n_blocks <= META_LANES
    dest, meta = _dispatch(eids)
    rowtok, blkcnt = _invert(dest, meta, n_blocks * BLOCK_ROWS)
    ys = _experts(meta, rowtok, blkcnt, h2, w_gate, w_up, w_down)
    return _combine(dest.reshape(-1), x2, gcol, row2(out_g), ys)


def kernel(x, norm1_g, w_in, w_dw, b_dw, ln_g, ln_b, rel_bias, w_out, norm2_g, w_group, b_group,
           w_expert_router, b_expert_router, w_gate, w_up, w_down, final_g):
    depth = norm1_g.shape[0]
    assert depth == 1, "the final norm is fused into the last layer's combine step"
    batch, s, d = x.shape
    assert batch == 1, "one sequence per call"
    out = _layer(x.reshape(s, d), norm1_g[0], w_in[0], w_dw[0], b_dw[0], ln_g[0], ln_b[0], rel_bias[0],
                 w_out[0], norm2_g[0], w_group[0], b_group[0], w_expert_router[0],
                 b_expert_router[0], w_gate[0], w_up[0], w_down[0], final_g)
    return out.reshape(batch, s, d)
```

```python
import functools

import jax
import jax.numpy as jnp
from jax import lax
from jax.experimental import pallas as pl
from jax.experimental.pallas import tpu as pltpu

F32 = jnp.float32
BF16 = jnp.bfloat16
I32 = jnp.int32

CHUNK = 64
HEAD_DIM = 64
LEFT_CHUNKS = 8
REL_CLIP = 128
CONV_WIDTH = 31
N_GROUPS = 8
EXPERTS_PER_GROUP = 8
N_EXPERTS = N_GROUPS * EXPERTS_PER_GROUP
TOP_K = 2
BLOCK_ROWS = 128
RMS_EPS = 1e-6
LN_EPS = 1e-5
NEG_INF = -1e30

VMEM_LIMIT_BYTES = 56 * 1024 * 1024
LANES = 128
SUBLANES = 8
PAIR = 2 * CHUNK
BAND_PAIR = (LEFT_CHUNKS + 2) * CHUNK
FRONT_PAD = LEFT_CHUNKS * CHUNK
CONV_HALO = 32


def _cparams(*sem):
    return pltpu.CompilerParams(dimension_semantics=sem, vmem_limit_bytes=VMEM_LIMIT_BYTES)


def _inproj_body(x_ref, g_ref, w_ref, cv_ref, cg_ref, q_ref, k_ref, v_ref):
    x = x_ref[...]
    ms = jnp.mean(x * x, axis=-1, keepdims=True)
    h = (x * lax.rsqrt(ms + RMS_EPS) * g_ref[...]).astype(BF16)
    dc = cv_ref.shape[1]
    da = q_ref.shape[1]

    def proj(lo, width):
        return jnp.dot(h, w_ref[:, lo:lo + width], preferred_element_type=F32)

    cv_ref[...] = proj(0, dc)
    cg_ref[...] = proj(dc, dc)
    q_ref[...] = (proj(2 * dc, da) * (HEAD_DIM ** -0.5)).astype(BF16)
    k_ref[...] = proj(2 * dc + da, da).astype(BF16)
    v_ref[...] = proj(2 * dc + 2 * da, da).astype(BF16)


def _in_proj(x, g, w_bf16, d_conv, d_att, tm=256):
    s, d = x.shape
    d_in = w_bf16.shape[1]
    row = lambda i: (i, 0)
    const = lambda i: (0, 0)
    return pl.pallas_call(
        _inproj_body,
        grid=(s // tm,),
        in_specs=[
            pl.BlockSpec((tm, d), row),
            pl.BlockSpec((1, d), const),
            pl.BlockSpec((d, d_in), const, pipeline_mode=pl.Buffered(1)),
        ],
        out_specs=[
            pl.BlockSpec((tm, d_conv), row),
            pl.BlockSpec((tm, d_conv), row),
            pl.BlockSpec((tm, d_att), row),
            pl.BlockSpec((tm, d_att), row),
            pl.BlockSpec((tm, d_att), row),
        ],
        out_shape=[
            jax.ShapeDtypeStruct((s, d_conv), F32),
            jax.ShapeDtypeStruct((s, d_conv), F32),
            jax.ShapeDtypeStruct((s, d_att), BF16),
            jax.ShapeDtypeStruct((s, d_att), BF16),
            jax.ShapeDtypeStruct((s, d_att), BF16),
        ],
        compiler_params=_cparams("arbitrary"),
        name="in_proj",
    )(x, g, w_bf16)


def _sigmoid(x):
    return 1.0 / (1.0 + jnp.exp(-x))


def _conv_body(cv_ref, cg_ref, cvh_ref, cgh_ref, w_ref, b_ref, lg_ref, lb_ref, o_ref,
               u_ref, us_ref, wb_ref, conv_ref, *, rows_per_chunk, groups_per_pass):
    i = pl.program_id(0)
    t, dc = cv_ref.shape

    @pl.when(i == 0)
    def _():
        for w in range(CONV_WIDTH):
            wb_ref[w] = jnp.broadcast_to(w_ref[w:w + 1, :], (SUBLANES, dc))

    halo = cvh_ref[...] * _sigmoid(cgh_ref[...])
    u_ref[0:CONV_HALO, :] = jnp.where(i > 0, halo, 0.0)
    u_ref[CONV_HALO:, :] = cv_ref[...] * _sigmoid(cg_ref[...])
    n_shift_rows = us_ref.shape[1]
    for sft in range(1, SUBLANES):
        us_ref[sft - 1] = u_ref[sft:sft + n_shift_rows, :]
    first_tap = CONV_HALO - (CONV_WIDTH - 1)

    def lane_tile(lt, _):
        lanes = pl.ds(pl.multiple_of(lt * LANES, LANES), LANES)
        wv = [wb_ref[w, :, lanes] for w in range(CONV_WIDTH)]
        bias = jnp.broadcast_to(b_ref[:, lanes], (SUBLANES, LANES))
        for g0 in range(0, t // SUBLANES, groups_per_pass):
            accs = [bias] * groups_per_pass
            for w in range(CONV_WIDTH):
                off = first_tap + w
                sft = off % SUBLANES
                for g in range(groups_per_pass):
                    lo = (g0 + g) * SUBLANES + off - sft
                    if sft == 0:
                        src = u_ref[lo:lo + SUBLANES, lanes]
                    else:
                        src = us_ref[sft - 1, lo:lo + SUBLANES, lanes]
                    accs[g] = accs[g] + wv[w] * src
            for g in range(groups_per_pass):
                lo = (g0 + g) * SUBLANES
                conv_ref[lo:lo + SUBLANES, lanes] = accs[g]
        return 0

    lax.fori_loop(0, dc // LANES, lane_tile, 0)

    r = rows_per_chunk
    for c in range(t // r):
        acc = conv_ref[c * r:(c + 1) * r, :]
        mu = jnp.mean(acc, axis=-1, keepdims=True)
        cen = acc - mu
        var = jnp.mean(cen * cen, axis=-1, keepdims=True)
        un = cen * lax.rsqrt(var + LN_EPS) * lg_ref[...] + lb_ref[...]
        o_ref[c * r:(c + 1) * r, :] = (un * _sigmoid(un)).astype(o_ref.dtype)


def _conv_module(cv, cg, w_dw, b_dw, ln_g, ln_b, t=256, rows_per_chunk=16, groups_per_pass=4):
    s, dc = cv.shape
    halo_blocks = t // CONV_HALO
    row = lambda i: (i, 0)
    prev = lambda i: (jnp.maximum(i * halo_blocks - 1, 0), 0)
    const = lambda i: (0, 0)
    return pl.pallas_call(
        functools.partial(_conv_body, rows_per_chunk=rows_per_chunk, groups_per_pass=groups_per_pass),
        grid=(s // t,),
        in_specs=[
            pl.BlockSpec((t, dc), row),
            pl.BlockSpec((t, dc), row),
            pl.BlockSpec((CONV_HALO, dc), prev),
            pl.BlockSpec((CONV_HALO, dc), prev),
            pl.BlockSpec((CONV_WIDTH, dc), const),
            pl.BlockSpec((1, dc), const),
            pl.BlockSpec((1, dc), const),
            pl.BlockSpec((1, dc), const),
        ],
        out_specs=pl.BlockSpec((t, dc), row),
        out_shape=jax.ShapeDtypeStruct((s, dc), BF16),
        scratch_shapes=[
            pltpu.VMEM((t + CONV_HALO, dc), F32),
            pltpu.VMEM((SUBLANES - 1, t + CONV_HALO - SUBLANES, dc), F32),
            pltpu.VMEM((CONV_WIDTH, SUBLANES, dc), F32),
            pltpu.VMEM((t, dc), F32),
        ],
        compiler_params=_cparams("arbitrary"),
        name="conv_module",
    )(cv, cg, cv, cg, w_dw, b_dw, ln_g, ln_b)


def _attn_pair(q2, kb, vb, bias_ref, col_lo):
    n_keys = kb.shape[0]
    lane = lax.broadcasted_iota(I32, q2.shape, 1)
    outs = []
    for h in range(2):
        in_head = (lane >= h * HEAD_DIM) & (lane < (h + 1) * HEAD_DIM)
        qh = jnp.where(in_head, q2, jnp.zeros_like(q2))
        sc = lax.dot_general(qh, kb, (((1,), (1,)), ((), ())), preferred_element_type=F32)
        sc = sc + bias_ref[h, :, col_lo:col_lo + n_keys]
        m = jnp.max(sc, axis=-1, keepdims=True)
        e = jnp.exp(sc - m)
        den = jnp.sum(e, axis=-1, keepdims=True)
        o = jnp.dot(e.astype(BF16), vb, preferred_element_type=F32)
        outs.append(o * (1.0 / den))
    return jnp.where(lane < HEAD_DIM, outs[0], outs[1])


def _attn_body(q_ref, k_ref, v_ref, g_ref, o_ref, bias_ref, *, pairs_per_step):
    st = pl.program_id(1)
    npp = pairs_per_step

    @pl.when(st == 0)
    def _():
        qi = lax.broadcasted_iota(I32, (PAIR, BAND_PAIR), 0)
        kj = lax.broadcasted_iota(I32, (PAIR, BAND_PAIR), 1)
        band_lo = jnp.where(qi >= CHUNK, CHUNK, 0)
        in_band = (kj >= band_lo) & (kj < band_lo + (LEFT_CHUNKS + 1) * CHUNK)
        for h in range(2):
            t = jnp.broadcast_to(g_ref[h], (PAIR, BAND_PAIR))
            shift = 1
            while shift < PAIR:
                t = jnp.where((qi & shift) != 0, pltpu.roll(t, shift, 1), t)
                shift *= 2
            bias_ref[h] = jnp.where(in_band, t, NEG_INF)

    def run(first_step):
        for u in range(npp):
            q2 = q_ref[u * PAIR:(u + 1) * PAIR, :]
            if first_step:
                n_keys = min((u + 1) * PAIR, BAND_PAIR)
                lo = max((u + 1) * PAIR - BAND_PAIR, 0)
                kb = k_ref[lo:lo + n_keys, :]
                vb = v_ref[lo:lo + n_keys, :]
                col_lo = BAND_PAIR - n_keys
            else:
                start = pl.multiple_of((st * npp + u) * PAIR - FRONT_PAD, PAIR)
                kb = k_ref[pl.ds(start, BAND_PAIR), :]
                vb = v_ref[pl.ds(start, BAND_PAIR), :]
                col_lo = 0
            o = _attn_pair(q2, kb, vb, bias_ref, col_lo)
            o_ref[u * PAIR:(u + 1) * PAIR, :] = o.astype(o_ref.dtype)

    @pl.when(st == 0)
    def _():
        run(True)

    @pl.when(st > 0)
    def _():
        run(False)


def _distance_rows(rel_table):
    t = rel_table.astype(F32)
    far = t[:, 2 * REL_CLIP:]
    n_far = LEFT_CHUNKS * CHUNK - REL_CLIP + 1
    near = jnp.flip(t[:, REL_CLIP - CHUNK + 1:2 * REL_CLIP], axis=1)
    tail = BAND_PAIR - n_far - near.shape[1]
    g = jnp.concatenate([jnp.repeat(far, n_far, axis=1), near, jnp.repeat(far, tail, axis=1)], axis=1)
    return g[:, None, :]


def _attention(q, k, v, g_rows, pairs_per_step=4):
    s, da = q.shape
    n_hp = da // (2 * HEAD_DIM)
    tq = pairs_per_step * PAIR
    assert pairs_per_step * PAIR >= FRONT_PAD, "first step must cover every clamped band"
    return pl.pallas_call(
        functools.partial(_attn_body, pairs_per_step=pairs_per_step),
        grid=(n_hp, s // tq),
        in_specs=[
            pl.BlockSpec((tq, 2 * HEAD_DIM), lambda hp, st: (st, hp)),
            pl.BlockSpec((s, 2 * HEAD_DIM), lambda hp, st: (0, hp)),
            pl.BlockSpec((s, 2 * HEAD_DIM), lambda hp, st: (0, hp)),
            pl.BlockSpec((2, 1, BAND_PAIR), lambda hp, st: (hp, 0, 0)),
        ],
        out_specs=pl.BlockSpec((tq, 2 * HEAD_DIM), lambda hp, st: (st, hp)),
        out_shape=jax.ShapeDtypeStruct((s, da), BF16),
        scratch_shapes=[pltpu.VMEM((2, PAIR, BAND_PAIR), F32)],
        compiler_params=_cparams("arbitrary", "arbitrary"),
        name="chunk_attention",
    )(q, k, v, g_rows)


def _first_index_of_max(vals, sub, n):
    vmax = jnp.max(vals, axis=0, keepdims=True)
    idx = jnp.min(jnp.where(vals == vmax, sub, n), axis=0, keepdims=True)
    return vmax, idx


def _outproj_body(x_ref, co_ref, ao_ref, w_ref, g2_ref, wr_ref, br_ref,
                  x2_ref, h2_ref, eid_ref, gcol_ref):
    dc = co_ref.shape[1]
    y = jnp.dot(co_ref[...], w_ref[0:dc, :], preferred_element_type=F32)
    y = y + jnp.dot(ao_ref[...], w_ref[dc:, :], preferred_element_type=F32)
    x2 = x_ref[...] + y
    ms = jnp.mean(x2 * x2, axis=-1, keepdims=True)
    h2 = x2 * lax.rsqrt(ms + RMS_EPS) * g2_ref[...]
    x2_ref[...] = x2
    h2_ref[...] = h2

    hi = h2.astype(BF16)
    lo = (h2 - hi.astype(F32)).astype(BF16)
    pa = jnp.dot(hi, wr_ref[...], preferred_element_type=F32)
    pb = jnp.dot(lo, wr_ref[...], preferred_element_type=F32)
    logits = pa[:, :LANES] + pa[:, LANES:] + pb[:, :LANES]
    lt = logits.T + br_ref[...]
    tm = lt.shape[1]
    sub = lax.broadcasted_iota(I32, (N_GROUPS, tm), 0)
    g = lt[0:N_GROUPS, :]
    gmax, grp = _first_index_of_max(g, sub, N_GROUPS)
    p_grp = 1.0 / jnp.sum(jnp.exp(g - gmax), axis=0, keepdims=True)
    sel = jnp.zeros((EXPERTS_PER_GROUP, tm), F32)
    for gi in range(N_GROUPS):
        lo = N_GROUPS + gi * EXPERTS_PER_GROUP
        sel = jnp.where(grp == gi, lt[lo:lo + EXPERTS_PER_GROUP, :], sel)
    v1, i1 = _first_index_of_max(sel, sub, EXPERTS_PER_GROUP)
    rest = jnp.where(sub == i1, -jnp.inf, sel)
    v2, i2 = _first_index_of_max(rest, sub, EXPERTS_PER_GROUP)
    t = jnp.exp(v2 - v1)
    gate1 = p_grp / (1.0 + t)
    gate2 = p_grp * t / (1.0 + t)
    eid_ref[0:1, :] = grp * EXPERTS_PER_GROUP + i1
    eid_ref[1:2, :] = grp * EXPERTS_PER_GROUP + i2
    row = lax.broadcasted_iota(I32, (LANES, tm), 0)
    gl = jnp.where(row == 0, gate1, jnp.where(row == 1, gate2, 0.0))
    gcol_ref[...] = gl.T


def _out_proj_router(x, conv_out, attn_out, w_bf16, g2, wr_cat, br_col, tm=256):
    s, d = x.shape
    dc = conv_out.shape[1]
    da = attn_out.shape[1]
    row = lambda i: (i, 0)
    const = lambda i: (0, 0)
    return pl.pallas_call(
        _outproj_body,
        grid=(s // tm,),
        in_specs=[
            pl.BlockSpec((tm, d), row),
            pl.BlockSpec((tm, dc), row),
            pl.BlockSpec((tm, da), row),
            pl.BlockSpec((dc + da, d), const, pipeline_mode=pl.Buffered(1)),
            pl.BlockSpec((1, d), const),
            pl.BlockSpec((d, 2 * LANES), const),
            pl.BlockSpec((LANES, 1), const),
        ],
        out_specs=[
            pl.BlockSpec((tm, d), row),
            pl.BlockSpec((tm, d), row),
            pl.BlockSpec((TOP_K, tm), lambda i: (0, i)),
            pl.BlockSpec((tm, LANES), row),
        ],
        out_shape=[
            jax.ShapeDtypeStruct((s, d), F32),
            jax.ShapeDtypeStruct((s, d), F32),
            jax.ShapeDtypeStruct((TOP_K, s), I32),
            jax.ShapeDtypeStruct((s, LANES), F32),
        ],
        compiler_params=_cparams("arbitrary"),
        name="out_proj_router",
    )(x, conv_out, attn_out, w_bf16, g2, wr_cat, br_col)


META_LANES = 256
META_FIRST_BLK, META_N_BLK, META_COUNT, META_WEIGHT_EXP, META_USED_BLKS = range(5)


def _dispatch_body(eid_ref, dest_ref, meta_ref, rank_ref):
    s = eid_ref.shape[1]
    w = META_LANES
    sub = lax.broadcasted_iota(I32, (N_EXPERTS, w), 0)
    lane = lax.broadcasted_iota(I32, (N_EXPERTS, w), 1)
    ri = lax.broadcasted_iota(I32, (w, w), 0)
    ci = lax.broadcasted_iota(I32, (w, w), 1)
    upper = (ri < ci).astype(BF16)

    def onehots(b):
        off = pl.multiple_of(b * w, w)
        e0 = eid_ref[0:1, pl.ds(off, w)]
        e1 = eid_ref[1:2, pl.ds(off, w)]
        return off, sub == e0, sub == e1

    def rank_block(b, carry):
        off, o0, o1 = onehots(b)
        both = jnp.where(o0 | o1, 1.0, 0.0)
        pre = jnp.dot(both.astype(BF16), upper, preferred_element_type=F32) + carry
        rank_ref[0:1, pl.ds(off, w)] = jnp.sum(jnp.where(o0, pre, 0.0), axis=0, keepdims=True)
        rank_ref[1:2, pl.ds(off, w)] = jnp.sum(jnp.where(o1, pre, 0.0), axis=0, keepdims=True)
        return carry + jnp.sum(both, axis=1, keepdims=True)

    counts = lax.fori_loop(0, s // w, rank_block, jnp.zeros((N_EXPERTS, w), F32))

    nblk = ((counts.astype(I32) + (BLOCK_ROWS - 1)) // BLOCK_ROWS).astype(F32)
    er = lax.broadcasted_iota(I32, (N_EXPERTS, N_EXPERTS), 0)
    ec = lax.broadcasted_iota(I32, (N_EXPERTS, N_EXPERTS), 1)
    lower_incl = (ec <= er).astype(BF16)
    pend_blk = jnp.dot(lower_incl, nblk.astype(BF16), preferred_element_type=F32)
    pstart = (pend_blk - nblk) * float(BLOCK_ROWS)

    def on_lanes(v):
        return jnp.sum(jnp.where(sub == lane, v, 0), axis=0, keepdims=True)

    nblk_i = nblk.astype(I32)
    first_blk = on_lanes((pend_blk - nblk).astype(I32))
    n_blk = on_lanes(nblk_i)
    count = on_lanes(counts.astype(I32))
    has_rows = nblk_i > 0
    prev_used = jnp.max(jnp.where(has_rows & (sub <= lane), sub, -1), axis=0, keepdims=True)
    first_used = jnp.min(jnp.where(has_rows, sub, N_EXPERTS), axis=0, keepdims=True)
    weight_exp = jnp.where(prev_used >= 0, prev_used, jnp.minimum(first_used, N_EXPERTS - 1))
    used = pend_blk[N_EXPERTS - 1:N_EXPERTS, :].astype(I32)
    mrow = lax.broadcasted_iota(I32, (8, w), 0)
    meta = jnp.where(mrow == META_FIRST_BLK, first_blk, 0)
    meta = jnp.where(mrow == META_N_BLK, n_blk, meta)
    meta = jnp.where(mrow == META_COUNT, count, meta)
    meta = jnp.where(mrow == META_WEIGHT_EXP, weight_exp, meta)
    meta_ref[...] = jnp.where(mrow == META_USED_BLKS, used, meta)

    def dest_block(b, _):
        off, o0, o1 = onehots(b)
        d0 = rank_ref[0:1, pl.ds(off, w)] + jnp.sum(jnp.where(o0, pstart, 0.0), axis=0, keepdims=True)
        d1 = rank_ref[1:2, pl.ds(off, w)] + jnp.sum(jnp.where(o1, pstart, 0.0), axis=0, keepdims=True)
        dest_ref[0:1, pl.ds(off, w)] = d0.astype(I32)
        dest_ref[1:2, pl.ds(off, w)] = d1.astype(I32)
        return 0

    lax.fori_loop(0, s // w, dest_block, 0)


def _dispatch(eids):
    s = eids.shape[1]
    return pl.pallas_call(
        _dispatch_body,
        out_shape=[
            jax.ShapeDtypeStruct((TOP_K, s), I32),
            jax.ShapeDtypeStruct((8, META_LANES), I32),
        ],
        scratch_shapes=[pltpu.VMEM((TOP_K, s), F32)],
        compiler_params=pltpu.CompilerParams(vmem_limit_bytes=VMEM_LIMIT_BYTES),
        name="dispatch",
    )(eids)


def _invert_body(dest_ref, meta_ref, rowtok_ref):
    s = dest_ref.shape[1]
    n_blocks = rowtok_ref.shape[0] // BLOCK_ROWS
    used = meta_ref[META_USED_BLKS, 0]

    def per_expert(e, _):
        b0 = meta_ref[META_FIRST_BLK, e]
        nb = meta_ref[META_N_BLK, e]
        cnt = meta_ref[META_COUNT, e]

        def pad_row(r, _):
            rowtok_ref[r] = 0
            return 0

        lax.fori_loop(b0 * BLOCK_ROWS + cnt, (b0 + nb) * BLOCK_ROWS, pad_row, 0)
        return 0

    lax.fori_loop(0, N_EXPERTS, per_expert, 0)

    def unused_row(r, _):
        rowtok_ref[r] = 0
        return 0

    lax.fori_loop(used * BLOCK_ROWS, n_blocks * BLOCK_ROWS, unused_row, 0)

    def put(t, _):
        rowtok_ref[dest_ref[0, t]] = t
        rowtok_ref[dest_ref[1, t]] = t
        return 0

    lax.fori_loop(0, s, put, 0, unroll=8)


def _invert(dest, meta, rows):
    smem = pl.BlockSpec(memory_space=pltpu.SMEM)
    return pl.pallas_call(
        _invert_body,
        in_specs=[smem, smem],
        out_specs=smem,
        out_shape=jax.ShapeDtypeStruct((rows,), I32),
        name="invert_dest",
    )(dest, meta)


def _row_gather_wait(src_hbm, dst, sem, n_rows):
    pltpu.make_async_copy(src_hbm.at[pl.ds(0, n_rows)], dst.at[pl.ds(0, n_rows)], sem).wait()


def _experts_body(meta_ref, rowtok_ref, h2_hbm, wg_ref, wu_ref, wd_ref, ys_hbm,
                  xbuf, ybuf, zbuf, gsem, ysem, zsem):
    e = pl.program_id(0)
    b0 = meta_ref[META_FIRST_BLK, e]
    nb = meta_ref[META_N_BLK, e]
    used = meta_ref[META_USED_BLKS, 0]
    n_blocks = ys_hbm.shape[0] // BLOCK_ROWS

    def gather_start(b, sl):
        src_blk = jnp.minimum(b, n_blocks - 1)
        for r in range(BLOCK_ROWS):
            tok = rowtok_ref[src_blk * BLOCK_ROWS + r]
            pltpu.make_async_copy(h2_hbm.at[pl.ds(tok, 1)], xbuf.at[sl, pl.ds(r, 1)],
                                  gsem.at[sl]).start(priority=1)

    def gather_wait(sl):
        _row_gather_wait(h2_hbm, xbuf.at[sl], gsem.at[sl], BLOCK_ROWS)

    def ys_copy(b):
        sl = b % 2
        rows = pl.ds(pl.multiple_of(b * BLOCK_ROWS, BLOCK_ROWS), BLOCK_ROWS)
        return pltpu.make_async_copy(ybuf.at[sl], ys_hbm.at[rows], ysem.at[sl])

    def zero_copy(b):
        rows = pl.ds(pl.multiple_of(b * BLOCK_ROWS, BLOCK_ROWS), BLOCK_ROWS)
        return pltpu.make_async_copy(zbuf, ys_hbm.at[rows], zsem)

    @pl.when(e == 0)
    def _():
        gather_start(0, 0)
        zbuf[...] = jnp.zeros_like(zbuf)
        lax.fori_loop(used, n_blocks, lambda b, _: (zero_copy(b).start(), 0)[1], 0)

    @pl.when(nb > 0)
    def _():
        def block(j, _):
            b = b0 + j
            sl = b % 2
            gather_start(b + 1, 1 - sl)
            gather_wait(sl)

            @pl.when(b >= 2)
            def _():
                ys_copy(b - 2).wait()

            x = xbuf[sl].astype(BF16)
            a = jnp.dot(x, wg_ref[...].astype(BF16), preferred_element_type=F32)
            u = jnp.dot(x, wu_ref[...].astype(BF16), preferred_element_type=F32)
            act = (a * _sigmoid(a) * u).astype(BF16)
            ybuf[sl] = jnp.dot(act, wd_ref[...].astype(BF16), preferred_element_type=F32)
            ys_copy(b).start()
            return 0

        lax.fori_loop(0, nb, block, 0)

    @pl.when(e == pl.num_programs(0) - 1)
    def _():
        gather_wait(used % 2)
        for back in (1, 2):
            @pl.when(used >= back)
            def _():
                ys_copy(used - back).wait()
        lax.fori_loop(used, n_blocks, lambda b, _: (zero_copy(b).wait(), 0)[1], 0)


def _experts(meta, rowtok, h2, w_gate, w_up, w_down):
    s, d = h2.shape
    n_e, _, de = w_gate.shape
    rows = rowtok.shape[0]
    wsel = lambda e, meta_ref, rt_ref: (meta_ref[META_WEIGHT_EXP, e], 0, 0)
    grid_spec = pltpu.PrefetchScalarGridSpec(
        num_scalar_prefetch=2,
        grid=(n_e,),
        in_specs=[
            pl.BlockSpec(memory_space=pl.ANY),
            pl.BlockSpec((None, d, de), wsel),
            pl.BlockSpec((None, d, de), wsel),
            pl.BlockSpec((None, de, d), wsel),
        ],
        out_specs=pl.BlockSpec(memory_space=pl.ANY),
        scratch_shapes=[
            pltpu.VMEM((2, BLOCK_ROWS, d), F32),
            pltpu.VMEM((2, BLOCK_ROWS, d), F32),
            pltpu.VMEM((BLOCK_ROWS, d), F32),
            pltpu.SemaphoreType.DMA((2,)),
            pltpu.SemaphoreType.DMA((2,)),
            pltpu.SemaphoreType.DMA(()),
        ],
    )
    return pl.pallas_call(
        _experts_body,
        grid_spec=grid_spec,
        out_shape=jax.ShapeDtypeStruct((rows, d), F32),
        compiler_params=_cparams("arbitrary"),
        name="expert_mlp",
    )(meta, rowtok, h2, w_gate, w_up, w_down)


def _combine_body(dest_ref, x2_ref, gcol_ref, fg_ref, ys_hbm, o_ref, ybuf, sem, *, tm):
    i = pl.program_id(0)
    n = pl.num_programs(0)
    s = n * tm
    slot = i % 2

    def start(tile, sl):
        for k in range(TOP_K):
            for r in range(tm):
                row = dest_ref[k * s + tile * tm + r]
                pltpu.make_async_copy(ys_hbm.at[pl.ds(row, 1)], ybuf.at[sl, k, pl.ds(r, 1)],
                                      sem.at[sl, k]).start(priority=r % 2)

    def wait(sl):
        for k in range(TOP_K):
            _row_gather_wait(ys_hbm, ybuf.at[sl, k], sem.at[sl, k], tm)

    @pl.when(i == 0)
    def _():
        start(0, 0)

    start(jnp.minimum(i + 1, n - 1), 1 - slot)
    wait(slot)
    g = gcol_ref[...]
    moe = g[:, 0:1] * ybuf[slot, 0] + g[:, 1:2] * ybuf[slot, 1]
    x3 = x2_ref[...] + moe
    ms = jnp.mean(x3 * x3, axis=-1, keepdims=True)
    o_ref[...] = x3 * lax.rsqrt(ms + RMS_EPS) * fg_ref[...]

    @pl.when(i == n - 1)
    def _():
        wait(1 - slot)


def _combine(dest_flat, x2, gcol, final_g, ys, tm=128):
    s, d = x2.shape
    row = lambda i, dest_ref: (i, 0)
    const = lambda i, dest_ref: (0, 0)
    grid_spec = pltpu.PrefetchScalarGridSpec(
        num_scalar_prefetch=1,
        grid=(s // tm,),
        in_specs=[
            pl.BlockSpec((tm, d), row),
            pl.BlockSpec((tm, LANES), row),
            pl.BlockSpec((1, d), const),
            pl.BlockSpec(memory_space=pl.ANY),
        ],
        out_specs=pl.BlockSpec((tm, d), row),
        scratch_shapes=[
            pltpu.VMEM((2, TOP_K, tm, d), F32),
            pltpu.SemaphoreType.DMA((2, TOP_K)),
        ],
    )
    return pl.pallas_call(
        functools.partial(_combine_body, tm=tm),
        grid_spec=grid_spec,
        out_shape=jax.ShapeDtypeStruct((s, d), F32),
        compiler_params=_cparams("arbitrary"),
        name="combine_norm",
    )(dest_flat, x2, gcol, final_g, ys)


def _layer(x, norm1_g, w_in, w_dw, b_dw, ln_g, ln_b, rel_bias, w_out, norm2_g,
           w_group, b_group, w_expert_router, b_expert_router, w_gate, w_up, w_down, out_g):
    s, d = x.shape
    dc = w_dw.shape[-1]
    da = (w_in.shape[1] - 2 * dc) // 3
    row2 = lambda a: a.reshape(1, -1)

    cv, cg, q, k, v = _in_proj(x, row2(norm1_g), w_in.astype(BF16), dc, da)
    conv_out = _conv_module(cv, cg, w_dw.reshape(CONV_WIDTH, dc), row2(b_dw), row2(ln_g), row2(ln_b))
    attn_out = _attention(q, k, v, _distance_rows(rel_bias))

    n_r = N_GROUPS + N_EXPERTS
    wr = jnp.pad(jnp.concatenate([w_group, w_expert_router], axis=1), ((0, 0), (0, LANES - n_r)))
    wr_hi = wr.astype(BF16)
    wr_lo = (wr - wr_hi.astype(F32)).astype(BF16)
    wr_cat = jnp.concatenate([wr_hi, wr_lo], axis=1)
    br_col = jnp.zeros((LANES, 1), F32).at[:n_r, 0].set(jnp.concatenate([b_group, b_expert_router]))
    x2, h2, eids, gcol = _out_proj_router(x, conv_out, attn_out, w_out.astype(BF16), row2(norm2_g),
                                          wr_cat, br_col)

    n_blocks = (s * TOP_K) // BLOCK_ROWS + N_EXPERTS
    assert n_blocks <= META_LANES
    dest, meta = _dispatch(eids)
    rowtok = _invert(dest, meta, n_blocks * BLOCK_ROWS)
    ys = _experts(meta, rowtok, h2, w_gate, w_up, w_down)
    return _combine(dest.reshape(-1), x2, gcol, row2(out_g), ys)


def kernel(x, norm1_g, w_in, w_dw, b_dw, ln_g, ln_b, rel_bias, w_out, norm2_g, w_group, b_group,
           w_expert_router, b_expert_router, w_gate, w_up, w_down, final_g):
    depth = norm1_g.shape[0]
    assert depth == 1, "the final norm is fused into the last layer's combine step"
    batch, s, d = x.shape
    assert batch == 1, "one sequence per call"
    out = _layer(x.reshape(s, d), norm1_g[0], w_in[0], w_dw[0], b_dw[0], ln_g[0], ln_b[0], rel_bias[0],
                 w_out[0], norm2_g[0], w_group[0], b_group[0], w_expert_router[0],
                 b_expert_router[0], w_gate[0], w_up[0], w_down[0], final_g)
    return out.reshape(batch, s, d)
```

```python
import functools

import jax
import jax.numpy as jnp
from jax import lax
from jax.experimental import pallas as pl
from jax.experimental.pallas import tpu as pltpu

F32 = jnp.float32
BF16 = jnp.bfloat16
I32 = jnp.int32

CHUNK = 64
HEAD_DIM = 64
LEFT_CHUNKS = 8
REL_CLIP = 128
CONV_WIDTH = 31
N_GROUPS = 8
EXPERTS_PER_GROUP = 8
N_EXPERTS = N_GROUPS * EXPERTS_PER_GROUP
TOP_K = 2
BLOCK_ROWS = 128
RMS_EPS = 1e-6
LN_EPS = 1e-5
NEG_INF = -1e30

VMEM_LIMIT_BYTES = 56 * 1024 * 1024
LANES = 128
SUBLANES = 8
PAIR = 2 * CHUNK
BAND_PAIR = (LEFT_CHUNKS + 2) * CHUNK
FRONT_PAD = LEFT_CHUNKS * CHUNK
CONV_HALO = 32


def _cparams(*sem):
    return pltpu.CompilerParams(dimension_semantics=sem, vmem_limit_bytes=VMEM_LIMIT_BYTES)


def _inproj_body(x_ref, g_ref, w_ref, cv_ref, cg_ref, q_ref, k_ref, v_ref):
    x = x_ref[...]
    ms = jnp.mean(x * x, axis=-1, keepdims=True)
    h = (x * lax.rsqrt(ms + RMS_EPS) * g_ref[...]).astype(BF16)
    dc = cv_ref.shape[1]
    da = q_ref.shape[1]

    def proj(lo, width):
        return jnp.dot(h, w_ref[:, lo:lo + width], preferred_element_type=F32)

    cv_ref[...] = proj(0, dc)
    cg_ref[...] = proj(dc, dc)
    q_ref[...] = (proj(2 * dc, da) * (HEAD_DIM ** -0.5)).astype(BF16)
    k_ref[...] = proj(2 * dc + da, da).astype(BF16)
    v_ref[...] = proj(2 * dc + 2 * da, da).astype(BF16)


def _in_proj(x, g, w_bf16, d_conv, d_att, tm=256):
    s, d = x.shape
    d_in = w_bf16.shape[1]
    row = lambda i: (i, 0)
    const = lambda i: (0, 0)
    return pl.pallas_call(
        _inproj_body,
        grid=(s // tm,),
        in_specs=[
            pl.BlockSpec((tm, d), row),
            pl.BlockSpec((1, d), const),
            pl.BlockSpec((d, d_in), const, pipeline_mode=pl.Buffered(1)),
        ],
        out_specs=[
            pl.BlockSpec((tm, d_conv), row),
            pl.BlockSpec((tm, d_conv), row),
            pl.BlockSpec((tm, d_att), row),
            pl.BlockSpec((tm, d_att), row),
            pl.BlockSpec((tm, d_att), row),
        ],
        out_shape=[
            jax.ShapeDtypeStruct((s, d_conv), F32),
            jax.ShapeDtypeStruct((s, d_conv), F32),
            jax.ShapeDtypeStruct((s, d_att), BF16),
            jax.ShapeDtypeStruct((s, d_att), BF16),
            jax.ShapeDtypeStruct((s, d_att), BF16),
        ],
        compiler_params=_cparams("arbitrary"),
        name="in_proj",
    )(x, g, w_bf16)


def _sigmoid(x):
    return 1.0 / (1.0 + jnp.exp(-x))


def _conv_body(cv_ref, cg_ref, cvh_ref, cgh_ref, w_ref, b_ref, lg_ref, lb_ref, o_ref,
               u_ref, us_ref, wb_ref, conv_ref, *, rows_per_chunk, groups_per_pass):
    i = pl.program_id(0)
    t, dc = cv_ref.shape

    @pl.when(i == 0)
    def _():
        for w in range(CONV_WIDTH):
            wb_ref[w] = jnp.broadcast_to(w_ref[w:w + 1, :], (SUBLANES, dc))

    halo = cvh_ref[...] * _sigmoid(cgh_ref[...])
    u_ref[0:CONV_HALO, :] = jnp.where(i > 0, halo, 0.0)
    u_ref[CONV_HALO:, :] = cv_ref[...] * _sigmoid(cg_ref[...])
    n_shift_rows = us_ref.shape[1]
    for sft in range(1, SUBLANES):
        us_ref[sft - 1] = u_ref[sft:sft + n_shift_rows, :]
    first_tap = CONV_HALO - (CONV_WIDTH - 1)

    def lane_tile(lt, _):
        lanes = pl.ds(pl.multiple_of(lt * LANES, LANES), LANES)
        wv = [wb_ref[w, :, lanes] for w in range(CONV_WIDTH)]
        bias = jnp.broadcast_to(b_ref[:, lanes], (SUBLANES, LANES))
        for g0 in range(0, t // SUBLANES, groups_per_pass):
            accs = [bias] * groups_per_pass
            for w in range(CONV_WIDTH):
                off = first_tap + w
                sft = off % SUBLANES
                for g in range(groups_per_pass):
                    lo = (g0 + g) * SUBLANES + off - sft
                    if sft == 0:
                        src = u_ref[lo:lo + SUBLANES, lanes]
                    else:
                        src = us_ref[sft - 1, lo:lo + SUBLANES, lanes]
                    accs[g] = accs[g] + wv[w] * src
            for g in range(groups_per_pass):
                lo = (g0 + g) * SUBLANES
                conv_ref[lo:lo + SUBLANES, lanes] = accs[g]
        return 0

    lax.fori_loop(0, dc // LANES, lane_tile, 0)

    r = rows_per_chunk
    for c in range(t // r):
        acc = conv_ref[c * r:(c + 1) * r, :]
        mu = jnp.mean(acc, axis=-1, keepdims=True)
        cen = acc - mu
        var = jnp.mean(cen * cen, axis=-1, keepdims=True)
        un = cen * lax.rsqrt(var + LN_EPS) * lg_ref[...] + lb_ref[...]
        o_ref[c * r:(c + 1) * r, :] = (un * _sigmoid(un)).astype(o_ref.dtype)


def _conv_module(cv, cg, w_dw, b_dw, ln_g, ln_b, t=256, rows_per_chunk=16, groups_per_pass=4):
    s, dc = cv.shape
    halo_blocks = t // CONV_HALO
    row = lambda i: (i, 0)
    prev = lambda i: (jnp.maximum(i * halo_blocks - 1, 0), 0)
    const = lambda i: (0, 0)
    return pl.pallas_call(
        functools.partial(_conv_body, rows_per_chunk=rows_per_chunk, groups_per_pass=groups_per_pass),
        grid=(s // t,),
        in_specs=[
            pl.BlockSpec((t, dc), row),
            pl.BlockSpec((t, dc), row),
            pl.BlockSpec((CONV_HALO, dc), prev),
            pl.BlockSpec((CONV_HALO, dc), prev),
            pl.BlockSpec((CONV_WIDTH, dc), const),
            pl.BlockSpec((1, dc), const),
            pl.BlockSpec((1, dc), const),
            pl.BlockSpec((1, dc), const),
        ],
        out_specs=pl.BlockSpec((t, dc), row),
        out_shape=jax.ShapeDtypeStruct((s, dc), BF16),
        scratch_shapes=[
            pltpu.VMEM((t + CONV_HALO, dc), F32),
            pltpu.VMEM((SUBLANES - 1, t + CONV_HALO - SUBLANES, dc), F32),
            pltpu.VMEM((CONV_WIDTH, SUBLANES, dc), F32),
            pltpu.VMEM((t, dc), F32),
        ],
        compiler_params=_cparams("arbitrary"),
        name="conv_module",
    )(cv, cg, cv, cg, w_dw, b_dw, ln_g, ln_b)


def _attn_pair(q2, kb, vb, bias_ref, col_lo):
    n_keys = kb.shape[0]
    lane = lax.broadcasted_iota(I32, q2.shape, 1)
    outs = []
    for h in range(2):
        in_head = (lane >= h * HEAD_DIM) & (lane < (h + 1) * HEAD_DIM)
        qh = jnp.where(in_head, q2, jnp.zeros_like(q2))
        sc = lax.dot_general(qh, kb, (((1,), (1,)), ((), ())), preferred_element_type=F32)
        sc = sc + bias_ref[h, :, col_lo:col_lo + n_keys]
        m = jnp.max(sc, axis=-1, keepdims=True)
        e = jnp.exp(sc - m)
        den = jnp.sum(e, axis=-1, keepdims=True)
        o = jnp.dot(e.astype(BF16), vb, preferred_element_type=F32)
        outs.append(o * (1.0 / den))
    return jnp.where(lane < HEAD_DIM, outs[0], outs[1])


def _attn_body(q_ref, k_ref, v_ref, g_ref, o_ref, bias_ref, *, pairs_per_step):
    st = pl.program_id(1)
    npp = pairs_per_step

    @pl.when(st == 0)
    def _():
        qi = lax.broadcasted_iota(I32, (PAIR, BAND_PAIR), 0)
        kj = lax.broadcasted_iota(I32, (PAIR, BAND_PAIR), 1)
        band_lo = jnp.where(qi >= CHUNK, CHUNK, 0)
        in_band = (kj >= band_lo) & (kj < band_lo + (LEFT_CHUNKS + 1) * CHUNK)
        for h in range(2):
            t = jnp.broadcast_to(g_ref[h], (PAIR, BAND_PAIR))
            shift = 1
            while shift < PAIR:
                t = jnp.where((qi & shift) != 0, pltpu.roll(t, shift, 1), t)
                shift *= 2
            bias_ref[h] = jnp.where(in_band, t, NEG_INF)

    def run(first_step):
        for u in range(npp):
            q2 = q_ref[u * PAIR:(u + 1) * PAIR, :]
            if first_step:
                n_keys = min((u + 1) * PAIR, BAND_PAIR)
                lo = max((u + 1) * PAIR - BAND_PAIR, 0)
                kb = k_ref[lo:lo + n_keys, :]
                vb = v_ref[lo:lo + n_keys, :]
                col_lo = BAND_PAIR - n_keys
            else:
                start = pl.multiple_of((st * npp + u) * PAIR - FRONT_PAD, PAIR)
                kb = k_ref[pl.ds(start, BAND_PAIR), :]
                vb = v_ref[pl.ds(start, BAND_PAIR), :]
                col_lo = 0
            o = _attn_pair(q2, kb, vb, bias_ref, col_lo)
            o_ref[u * PAIR:(u + 1) * PAIR, :] = o.astype(o_ref.dtype)

    @pl.when(st == 0)
    def _():
        run(True)

    @pl.when(st > 0)
    def _():
        run(False)


def _distance_rows(rel_table):
    t = rel_table.astype(F32)
    far = t[:, 2 * REL_CLIP:]
    n_far = LEFT_CHUNKS * CHUNK - REL_CLIP + 1
    near = jnp.flip(t[:, REL_CLIP - CHUNK + 1:2 * REL_CLIP], axis=1)
    tail = BAND_PAIR - n_far - near.shape[1]
    g = jnp.concatenate([jnp.repeat(far, n_far, axis=1), near, jnp.repeat(far, tail, axis=1)], axis=1)
    return g[:, None, :]


def _attention(q, k, v, g_rows, pairs_per_step=4):
    s, da = q.shape
    n_hp = da // (2 * HEAD_DIM)
    tq = pairs_per_step * PAIR
    assert pairs_per_step * PAIR >= FRONT_PAD, "first step must cover every clamped band"
    return pl.pallas_call(
        functools.partial(_attn_body, pairs_per_step=pairs_per_step),
        grid=(n_hp, s // tq),
        in_specs=[
            pl.BlockSpec((tq, 2 * HEAD_DIM), lambda hp, st: (st, hp)),
            pl.BlockSpec((s, 2 * HEAD_DIM), lambda hp, st: (0, hp)),
            pl.BlockSpec((s, 2 * HEAD_DIM), lambda hp, st: (0, hp)),
            pl.BlockSpec((2, 1, BAND_PAIR), lambda hp, st: (hp, 0, 0)),
        ],
        out_specs=pl.BlockSpec((tq, 2 * HEAD_DIM), lambda hp, st: (st, hp)),
        out_shape=jax.ShapeDtypeStruct((s, da), BF16),
        scratch_shapes=[pltpu.VMEM((2, PAIR, BAND_PAIR), F32)],
        compiler_params=_cparams("arbitrary", "arbitrary"),
        name="chunk_attention",
    )(q, k, v, g_rows)


def _first_index_of_max(vals, sub, n):
    vmax = jnp.max(vals, axis=0, keepdims=True)
    idx = jnp.min(jnp.where(vals == vmax, sub, n), axis=0, keepdims=True)
    return vmax, idx


def _outproj_body(x_ref, co_ref, ao_ref, w_ref, g2_ref, wr_ref, br_ref,
                  x2_ref, h2_ref, eid_ref, gcol_ref):
    dc = co_ref.shape[1]
    y = jnp.dot(co_ref[...], w_ref[0:dc, :], preferred_element_type=F32)
    y = y + jnp.dot(ao_ref[...], w_ref[dc:, :], preferred_element_type=F32)
    x2 = x_ref[...] + y
    ms = jnp.mean(x2 * x2, axis=-1, keepdims=True)
    h2 = x2 * lax.rsqrt(ms + RMS_EPS) * g2_ref[...]
    x2_ref[...] = x2
    h2_ref[...] = h2

    hi = h2.astype(BF16)
    lo = (h2 - hi.astype(F32)).astype(BF16)
    pa = jnp.dot(hi, wr_ref[...], preferred_element_type=F32)
    pb = jnp.dot(lo, wr_ref[...], preferred_element_type=F32)
    logits = pa[:, :LANES] + pa[:, LANES:] + pb[:, :LANES]
    lt = logits.T + br_ref[...]
    tm = lt.shape[1]
    sub = lax.broadcasted_iota(I32, (N_GROUPS, tm), 0)
    g = lt[0:N_GROUPS, :]
    gmax, grp = _first_index_of_max(g, sub, N_GROUPS)
    p_grp = 1.0 / jnp.sum(jnp.exp(g - gmax), axis=0, keepdims=True)
    sel = jnp.zeros((EXPERTS_PER_GROUP, tm), F32)
    for gi in range(N_GROUPS):
        lo = N_GROUPS + gi * EXPERTS_PER_GROUP
        sel = jnp.where(grp == gi, lt[lo:lo + EXPERTS_PER_GROUP, :], sel)
    v1, i1 = _first_index_of_max(sel, sub, EXPERTS_PER_GROUP)
    rest = jnp.where(sub == i1, -jnp.inf, sel)
    v2, i2 = _first_index_of_max(rest, sub, EXPERTS_PER_GROUP)
    t = jnp.exp(v2 - v1)
    gate1 = p_grp / (1.0 + t)
    gate2 = p_grp * t / (1.0 + t)
    eid_ref[0:1, :] = grp * EXPERTS_PER_GROUP + i1
    eid_ref[1:2, :] = grp * EXPERTS_PER_GROUP + i2
    row = lax.broadcasted_iota(I32, (LANES, tm), 0)
    gl = jnp.where(row == 0, gate1, jnp.where(row == 1, gate2, 0.0))
    gcol_ref[...] = gl.T


def _out_proj_router(x, conv_out, attn_out, w_bf16, g2, wr_cat, br_col, tm=256):
    s, d = x.shape
    dc = conv_out.shape[1]
    da = attn_out.shape[1]
    row = lambda i: (i, 0)
    const = lambda i: (0, 0)
    return pl.pallas_call(
        _outproj_body,
        grid=(s // tm,),
        in_specs=[
            pl.BlockSpec((tm, d), row),
            pl.BlockSpec((tm, dc), row),
            pl.BlockSpec((tm, da), row),
            pl.BlockSpec((dc + da, d), const, pipeline_mode=pl.Buffered(1)),
            pl.BlockSpec((1, d), const),
            pl.BlockSpec((d, 2 * LANES), const),
            pl.BlockSpec((LANES, 1), const),
        ],
        out_specs=[
            pl.BlockSpec((tm, d), row),
            pl.BlockSpec((tm, d), row),
            pl.BlockSpec((TOP_K, tm), lambda i: (0, i)),
            pl.BlockSpec((tm, LANES), row),
        ],
        out_shape=[
            jax.ShapeDtypeStruct((s, d), F32),
            jax.ShapeDtypeStruct((s, d), F32),
            jax.ShapeDtypeStruct((TOP_K, s), I32),
            jax.ShapeDtypeStruct((s, LANES), F32),
        ],
        compiler_params=_cparams("arbitrary"),
        name="out_proj_router",
    )(x, conv_out, attn_out, w_bf16, g2, wr_cat, br_col)


META_LANES = 256
(META_FIRST_BLK, META_N_BLK, META_COUNT, META_NEXT_EXP, META_ORDINAL,
 META_USED_BLKS, META_FIRST_USED) = range(7)


def _dispatch_body(eid_ref, dest_ref, meta_ref, rank_ref):
    s = eid_ref.shape[1]
    w = META_LANES
    sub = lax.broadcasted_iota(I32, (N_EXPERTS, w), 0)
    lane = lax.broadcasted_iota(I32, (N_EXPERTS, w), 1)
    ri = lax.broadcasted_iota(I32, (w, w), 0)
    ci = lax.broadcasted_iota(I32, (w, w), 1)
    upper = (ri < ci).astype(BF16)

    def onehots(b):
        off = pl.multiple_of(b * w, w)
        e0 = eid_ref[0:1, pl.ds(off, w)]
        e1 = eid_ref[1:2, pl.ds(off, w)]
        return off, sub == e0, sub == e1

    def rank_block(b, carry):
        off, o0, o1 = onehots(b)
        both = jnp.where(o0 | o1, 1.0, 0.0)
        pre = jnp.dot(both.astype(BF16), upper, preferred_element_type=F32) + carry
        rank_ref[0:1, pl.ds(off, w)] = jnp.sum(jnp.where(o0, pre, 0.0), axis=0, keepdims=True)
        rank_ref[1:2, pl.ds(off, w)] = jnp.sum(jnp.where(o1, pre, 0.0), axis=0, keepdims=True)
        return carry + jnp.sum(both, axis=1, keepdims=True)

    counts = lax.fori_loop(0, s // w, rank_block, jnp.zeros((N_EXPERTS, w), F32))

    nblk = ((counts.astype(I32) + (BLOCK_ROWS - 1)) // BLOCK_ROWS).astype(F32)
    er = lax.broadcasted_iota(I32, (N_EXPERTS, N_EXPERTS), 0)
    ec = lax.broadcasted_iota(I32, (N_EXPERTS, N_EXPERTS), 1)
    lower_incl = (ec <= er).astype(BF16)
    pend_blk = jnp.dot(lower_incl, nblk.astype(BF16), preferred_element_type=F32)
    pstart = (pend_blk - nblk) * float(BLOCK_ROWS)

    def on_lanes(v):
        return jnp.sum(jnp.where(sub == lane, v, 0), axis=0, keepdims=True)

    nblk_i = nblk.astype(I32)
    first_blk = on_lanes((pend_blk - nblk).astype(I32))
    n_blk = on_lanes(nblk_i)
    count = on_lanes(counts.astype(I32))
    has_rows = nblk_i > 0
    next_exp = jnp.min(jnp.where(has_rows & (sub > lane), sub, N_EXPERTS), axis=0, keepdims=True)
    next_exp = jnp.where(next_exp < N_EXPERTS, next_exp, -1)
    ordinal = jnp.sum(jnp.where(has_rows & (sub < lane), 1, 0), axis=0, keepdims=True)
    first_used = jnp.min(jnp.where(has_rows, sub, N_EXPERTS - 1), axis=0, keepdims=True)
    used = pend_blk[N_EXPERTS - 1:N_EXPERTS, :].astype(I32)
    mrow = lax.broadcasted_iota(I32, (8, w), 0)
    meta = jnp.where(mrow == META_FIRST_BLK, first_blk, 0)
    meta = jnp.where(mrow == META_N_BLK, n_blk, meta)
    meta = jnp.where(mrow == META_COUNT, count, meta)
    meta = jnp.where(mrow == META_NEXT_EXP, next_exp, meta)
    meta = jnp.where(mrow == META_ORDINAL, ordinal, meta)
    meta = jnp.where(mrow == META_FIRST_USED, first_used, meta)
    meta_ref[...] = jnp.where(mrow == META_USED_BLKS, used, meta)

    def dest_block(b, _):
        off, o0, o1 = onehots(b)
        d0 = rank_ref[0:1, pl.ds(off, w)] + jnp.sum(jnp.where(o0, pstart, 0.0), axis=0, keepdims=True)
        d1 = rank_ref[1:2, pl.ds(off, w)] + jnp.sum(jnp.where(o1, pstart, 0.0), axis=0, keepdims=True)
        dest_ref[0:1, pl.ds(off, w)] = d0.astype(I32)
        dest_ref[1:2, pl.ds(off, w)] = d1.astype(I32)
        return 0

    lax.fori_loop(0, s // w, dest_block, 0)


def _dispatch(eids):
    s = eids.shape[1]
    return pl.pallas_call(
        _dispatch_body,
        out_shape=[
            jax.ShapeDtypeStruct((TOP_K, s), I32),
            jax.ShapeDtypeStruct((8, META_LANES), I32),
        ],
        scratch_shapes=[pltpu.VMEM((TOP_K, s), F32)],
        compiler_params=pltpu.CompilerParams(vmem_limit_bytes=VMEM_LIMIT_BYTES),
        name="dispatch",
    )(eids)


def _invert_body(dest_ref, meta_ref, rowtok_ref):
    s = dest_ref.shape[1]
    n_blocks = rowtok_ref.shape[0] // BLOCK_ROWS
    used = meta_ref[META_USED_BLKS, 0]

    def per_expert(e, _):
        b0 = meta_ref[META_FIRST_BLK, e]
        nb = meta_ref[META_N_BLK, e]
        cnt = meta_ref[META_COUNT, e]

        def pad_row(r, _):
            rowtok_ref[r] = 0
            return 0

        lax.fori_loop(b0 * BLOCK_ROWS + cnt, (b0 + nb) * BLOCK_ROWS, pad_row, 0)
        return 0

    lax.fori_loop(0, N_EXPERTS, per_expert, 0)

    def unused_row(r, _):
        rowtok_ref[r] = 0
        return 0

    lax.fori_loop(used * BLOCK_ROWS, n_blocks * BLOCK_ROWS, unused_row, 0)

    def put(t, _):
        rowtok_ref[dest_ref[0, t]] = t
        rowtok_ref[dest_ref[1, t]] = t
        return 0

    lax.fori_loop(0, s, put, 0, unroll=8)


def _invert(dest, meta, rows):
    smem = pl.BlockSpec(memory_space=pltpu.SMEM)
    return pl.pallas_call(
        _invert_body,
        in_specs=[smem, smem],
        out_specs=smem,
        out_shape=jax.ShapeDtypeStruct((rows,), I32),
        name="invert_dest",
    )(dest, meta)


def _row_gather_wait(src_hbm, dst, sem, n_rows):
    pltpu.make_async_copy(src_hbm.at[pl.ds(0, n_rows)], dst.at[pl.ds(0, n_rows)], sem).wait()


BIG_DMA_PRIORITY = 1


def _experts_body(meta_ref, rowtok_ref, h2_hbm, wg_hbm, wu_hbm, wd_hbm, ys_hbm,
                  xbuf, ybuf, zbuf, wgbuf, wubuf, wdbuf, gsem, ysem, zsem, wsem):
    used = meta_ref[META_USED_BLKS, 0]
    n_blocks = ys_hbm.shape[0] // BLOCK_ROWS
    n_experts = wg_hbm.shape[0]

    def gather_start(b, sl):
        src_blk = jnp.minimum(b, n_blocks - 1)
        for r in range(BLOCK_ROWS):
            tok = rowtok_ref[src_blk * BLOCK_ROWS + r]
            pltpu.make_async_copy(h2_hbm.at[pl.ds(tok, 1)], xbuf.at[sl, pl.ds(r, 1)],
                                  gsem.at[sl]).start()

    def gather_wait(sl):
        _row_gather_wait(h2_hbm, xbuf.at[sl], gsem.at[sl], BLOCK_ROWS)

    def ys_copy(b):
        sl = b % 2
        rows = pl.ds(pl.multiple_of(b * BLOCK_ROWS, BLOCK_ROWS), BLOCK_ROWS)
        return pltpu.make_async_copy(ybuf.at[sl], ys_hbm.at[rows], ysem.at[sl])

    def zero_copy(b):
        rows = pl.ds(pl.multiple_of(b * BLOCK_ROWS, BLOCK_ROWS), BLOCK_ROWS)
        return pltpu.make_async_copy(zbuf, ys_hbm.at[rows], zsem)

    def weight_copies(e, ws):
        return (pltpu.make_async_copy(wg_hbm.at[e], wgbuf.at[ws], wsem.at[ws, 0]),
                pltpu.make_async_copy(wu_hbm.at[e], wubuf.at[ws], wsem.at[ws, 1]),
                pltpu.make_async_copy(wd_hbm.at[e], wdbuf.at[ws], wsem.at[ws, 2]))

    for c in weight_copies(meta_ref[META_FIRST_USED, 0], 0):
        c.start(priority=BIG_DMA_PRIORITY)
    gather_start(0, 0)
    zbuf[...] = jnp.zeros_like(zbuf)
    lax.fori_loop(used, n_blocks, lambda b, _: (zero_copy(b).start(priority=BIG_DMA_PRIORITY), 0)[1], 0)

    def expert(e, _):
        nb = meta_ref[META_N_BLK, e]

        @pl.when(nb > 0)
        def _():
            b0 = meta_ref[META_FIRST_BLK, e]
            ws = meta_ref[META_ORDINAL, e] % 2
            nxt = meta_ref[META_NEXT_EXP, e]

            @pl.when(nxt >= 0)
            def _():
                for c in weight_copies(nxt, 1 - ws):
                    c.start(priority=BIG_DMA_PRIORITY)

            for c in weight_copies(e, ws):
                c.wait()

            def block(j, _):
                b = b0 + j
                sl = b % 2
                gather_start(b + 1, 1 - sl)
                gather_wait(sl)

                @pl.when(b >= 2)
                def _():
                    ys_copy(b - 2).wait()

                x = xbuf[sl].astype(BF16)
                a = jnp.dot(x, wgbuf[ws].astype(BF16), preferred_element_type=F32)
                u = jnp.dot(x, wubuf[ws].astype(BF16), preferred_element_type=F32)
                act = (a * _sigmoid(a) * u).astype(BF16)
                ybuf[sl] = jnp.dot(act, wdbuf[ws].astype(BF16), preferred_element_type=F32)
                ys_copy(b).start(priority=BIG_DMA_PRIORITY)
                return 0

            lax.fori_loop(0, nb, block, 0)

        return 0

    lax.fori_loop(0, n_experts, expert, 0)

    gather_wait(used % 2)
    for back in (1, 2):
        @pl.when(used >= back)
        def _():
            ys_copy(used - back).wait()
    lax.fori_loop(used, n_blocks, lambda b, _: (zero_copy(b).wait(), 0)[1], 0)


def _experts(meta, rowtok, h2, w_gate, w_up, w_down):
    s, d = h2.shape
    n_e, _, de = w_gate.shape
    rows = rowtok.shape[0]
    smem = pl.BlockSpec(memory_space=pltpu.SMEM)
    hbm = pl.BlockSpec(memory_space=pl.ANY)
    return pl.pallas_call(
        _experts_body,
        in_specs=[smem, smem, hbm, hbm, hbm, hbm],
        out_specs=hbm,
        out_shape=jax.ShapeDtypeStruct((rows, d), F32),
        scratch_shapes=[
            pltpu.VMEM((2, BLOCK_ROWS, d), F32),
            pltpu.VMEM((2, BLOCK_ROWS, d), F32),
            pltpu.VMEM((BLOCK_ROWS, d), F32),
            pltpu.VMEM((2, d, de), F32),
            pltpu.VMEM((2, d, de), F32),
            pltpu.VMEM((2, de, d), F32),
            pltpu.SemaphoreType.DMA((2,)),
            pltpu.SemaphoreType.DMA((2,)),
            pltpu.SemaphoreType.DMA(()),
            pltpu.SemaphoreType.DMA((2, 3)),
        ],
        compiler_params=pltpu.CompilerParams(vmem_limit_bytes=VMEM_LIMIT_BYTES),
        name="expert_mlp",
    )(meta, rowtok, h2, w_gate, w_up, w_down)


def _combine_body(dest_ref, x2_ref, gcol_ref, fg_ref, ys_hbm, o_ref, ybuf, sem, *, tm):
    i = pl.program_id(0)
    n = pl.num_programs(0)
    s = n * tm
    slot = i % 2

    def start(tile, sl):
        for k in range(TOP_K):
            for r in range(tm):
                row = dest_ref[k * s + tile * tm + r]
                pltpu.make_async_copy(ys_hbm.at[pl.ds(row, 1)], ybuf.at[sl, k, pl.ds(r, 1)],
                                      sem.at[sl, k]).start(priority=r % 2)

    def wait(sl):
        for k in range(TOP_K):
            _row_gather_wait(ys_hbm, ybuf.at[sl, k], sem.at[sl, k], tm)

    @pl.when(i == 0)
    def _():
        start(0, 0)

    start(jnp.minimum(i + 1, n - 1), 1 - slot)
    wait(slot)
    g = gcol_ref[...]
    moe = g[:, 0:1] * ybuf[slot, 0] + g[:, 1:2] * ybuf[slot, 1]
    x3 = x2_ref[...] + moe
    ms = jnp.mean(x3 * x3, axis=-1, keepdims=True)
    o_ref[...] = x3 * lax.rsqrt(ms + RMS_EPS) * fg_ref[...]

    @pl.when(i == n - 1)
    def _():
        wait(1 - slot)


def _combine(dest_flat, x2, gcol, final_g, ys, tm=128):
    s, d = x2.shape
    row = lambda i, dest_ref: (i, 0)
    const = lambda i, dest_ref: (0, 0)
    grid_spec = pltpu.PrefetchScalarGridSpec(
        num_scalar_prefetch=1,
        grid=(s // tm,),
        in_specs=[
            pl.BlockSpec((tm, d), row),
            pl.BlockSpec((tm, LANES), row),
            pl.BlockSpec((1, d), const),
            pl.BlockSpec(memory_space=pl.ANY),
        ],
        out_specs=pl.BlockSpec((tm, d), row),
        scratch_shapes=[
            pltpu.VMEM((2, TOP_K, tm, d), F32),
            pltpu.SemaphoreType.DMA((2, TOP_K)),
        ],
    )
    return pl.pallas_call(
        functools.partial(_combine_body, tm=tm),
        grid_spec=grid_spec,
        out_shape=jax.ShapeDtypeStruct((s, d), F32),
        compiler_params=_cparams("arbitrary"),
        name="combine_norm",
    )(dest_flat, x2, gcol, final_g, ys)


def _layer(x, norm1_g, w_in, w_dw, b_dw, ln_g, ln_b, rel_bias, w_out, norm2_g,
           w_group, b_group, w_expert_router, b_expert_router, w_gate, w_up, w_down, out_g):
    s, d = x.shape
    dc = w_dw.shape[-1]
    da = (w_in.shape[1] - 2 * dc) // 3
    row2 = lambda a: a.reshape(1, -1)

    cv, cg, q, k, v = _in_proj(x, row2(norm1_g), w_in.astype(BF16), dc, da)
    conv_out = _conv_module(cv, cg, w_dw.reshape(CONV_WIDTH, dc), row2(b_dw), row2(ln_g), row2(ln_b))
    attn_out = _attention(q, k, v, _distance_rows(rel_bias))

    n_r = N_GROUPS + N_EXPERTS
    wr = jnp.pad(jnp.concatenate([w_group, w_expert_router], axis=1), ((0, 0), (0, LANES - n_r)))
    wr_hi = wr.astype(BF16)
    wr_lo = (wr - wr_hi.astype(F32)).astype(BF16)
    wr_cat = jnp.concatenate([wr_hi, wr_lo], axis=1)
    br_col = jnp.zeros((LANES, 1), F32).at[:n_r, 0].set(jnp.concatenate([b_group, b_expert_router]))
    x2, h2, eids, gcol = _out_proj_router(x, conv_out, attn_out, w_out.astype(BF16), row2(norm2_g),
                                          wr_cat, br_col)

    n_blocks = (s * TOP_K) // BLOCK_ROWS + N_EXPERTS
    assert n_blocks <= META_LANES
    dest, meta = _dispatch(eids)
    rowtok = _invert(dest, meta, n_blocks * BLOCK_ROWS)
    ys = _experts(meta, rowtok, h2, w_gate, w_up, w_down)
    return _combine(dest.reshape(-1), x2, gcol, row2(out_g), ys)


def kernel(x, norm1_g, w_in, w_dw, b_dw, ln_g, ln_b, rel_bias, w_out, norm2_g, w_group, b_group,
           w_expert_router, b_expert_router, w_gate, w_up, w_down, final_g):
    depth = norm1_g.shape[0]
    assert depth == 1, "the final norm is fused into the last layer's combine step"
    batch, s, d = x.shape
    assert batch == 1, "one sequence per call"
    out = _layer(x.reshape(s, d), norm1_g[0], w_in[0], w_dw[0], b_dw[0], ln_g[0], ln_b[0], rel_bias[0],
                 w_out[0], norm2_g[0], w_group[0], b_group[0], w_expert_router[0],
                 b_expert_router[0], w_gate[0], w_up[0], w_down[0], final_g)
    return out.reshape(batch, s, d)
```

```python
import functools

import jax
import jax.numpy as jnp
from jax import lax
from jax.experimental import pallas as pl
from jax.experimental.pallas import tpu as pltpu

F32 = jnp.float32
BF16 = jnp.bfloat16
I32 = jnp.int32

CHUNK = 64
HEAD_DIM = 64
LEFT_CHUNKS = 8
REL_CLIP = 128
CONV_WIDTH = 31
N_GROUPS = 8
EXPERTS_PER_GROUP = 8
N_EXPERTS = N_GROUPS * EXPERTS_PER_GROUP
TOP_K = 2
BLOCK_ROWS = 128
RMS_EPS = 1e-6
LN_EPS = 1e-5
NEG_INF = -1e30

VMEM_LIMIT_BYTES = 56 * 1024 * 1024
LANES = 128
SUBLANES = 8
PAIR = 2 * CHUNK
BAND_PAIR = (LEFT_CHUNKS + 2) * CHUNK
FRONT_PAD = LEFT_CHUNKS * CHUNK
CONV_HALO = 32


def _cparams(*sem):
    return pltpu.CompilerParams(dimension_semantics=sem, vmem_limit_bytes=VMEM_LIMIT_BYTES)


def _inproj_body(x_ref, g_ref, w_ref, cv_ref, cg_ref, q_ref, k_ref, v_ref):
    x = x_ref[...]
    ms = jnp.mean(x * x, axis=-1, keepdims=True)
    h = (x * lax.rsqrt(ms + RMS_EPS) * g_ref[...]).astype(BF16)
    dc = cv_ref.shape[1]
    da = q_ref.shape[1]

    def proj(lo, width):
        return jnp.dot(h, w_ref[:, lo:lo + width], preferred_element_type=F32)

    cv_ref[...] = proj(0, dc)
    cg_ref[...] = proj(dc, dc)
    q_ref[...] = (proj(2 * dc, da) * (HEAD_DIM ** -0.5)).astype(BF16)
    k_ref[...] = proj(2 * dc + da, da).astype(BF16)
    v_ref[...] = proj(2 * dc + 2 * da, da).astype(BF16)


def _in_proj(x, g, w_bf16, d_conv, d_att, tm=256):
    s, d = x.shape
    d_in = w_bf16.shape[1]
    row = lambda i: (i, 0)
    const = lambda i: (0, 0)
    return pl.pallas_call(
        _inproj_body,
        grid=(s // tm,),
        in_specs=[
            pl.BlockSpec((tm, d), row),
            pl.BlockSpec((1, d), const),
            pl.BlockSpec((d, d_in), const, pipeline_mode=pl.Buffered(1)),
        ],
        out_specs=[
            pl.BlockSpec((tm, d_conv), row),
            pl.BlockSpec((tm, d_conv), row),
            pl.BlockSpec((tm, d_att), row),
            pl.BlockSpec((tm, d_att), row),
            pl.BlockSpec((tm, d_att), row),
        ],
        out_shape=[
            jax.ShapeDtypeStruct((s, d_conv), F32),
            jax.ShapeDtypeStruct((s, d_conv), F32),
            jax.ShapeDtypeStruct((s, d_att), BF16),
            jax.ShapeDtypeStruct((s, d_att), BF16),
            jax.ShapeDtypeStruct((s, d_att), BF16),
        ],
        compiler_params=_cparams("arbitrary"),
        name="in_proj",
    )(x, g, w_bf16)


def _sigmoid(x):
    return 1.0 / (1.0 + jnp.exp(-x))


def _conv_body(cv_ref, cg_ref, cvh_ref, cgh_ref, w_ref, b_ref, lg_ref, lb_ref, o_ref,
               u_ref, us_ref, wb_ref, conv_ref, *, rows_per_chunk, groups_per_pass):
    i = pl.program_id(0)
    t, dc = cv_ref.shape

    @pl.when(i == 0)
    def _():
        for w in range(CONV_WIDTH):
            wb_ref[w] = jnp.broadcast_to(w_ref[w:w + 1, :], (SUBLANES, dc))

    halo = cvh_ref[...] * _sigmoid(cgh_ref[...])
    u_ref[0:CONV_HALO, :] = jnp.where(i > 0, halo, 0.0)
    u_ref[CONV_HALO:, :] = cv_ref[...] * _sigmoid(cg_ref[...])
    n_shift_rows = us_ref.shape[1]
    for sft in range(1, SUBLANES):
        us_ref[sft - 1] = u_ref[sft:sft + n_shift_rows, :]
    first_tap = CONV_HALO - (CONV_WIDTH - 1)

    def lane_tile(lt, _):
        lanes = pl.ds(pl.multiple_of(lt * LANES, LANES), LANES)
        wv = [wb_ref[w, :, lanes] for w in range(CONV_WIDTH)]
        bias = jnp.broadcast_to(b_ref[:, lanes], (SUBLANES, LANES))
        for g0 in range(0, t // SUBLANES, groups_per_pass):
            accs = [bias] * groups_per_pass
            for w in range(CONV_WIDTH):
                off = first_tap + w
                sft = off % SUBLANES
                for g in range(groups_per_pass):
                    lo = (g0 + g) * SUBLANES + off - sft
                    if sft == 0:
                        src = u_ref[lo:lo + SUBLANES, lanes]
                    else:
                        src = us_ref[sft - 1, lo:lo + SUBLANES, lanes]
                    accs[g] = accs[g] + wv[w] * src
            for g in range(groups_per_pass):
                lo = (g0 + g) * SUBLANES
                conv_ref[lo:lo + SUBLANES, lanes] = accs[g]
        return 0

    lax.fori_loop(0, dc // LANES, lane_tile, 0)

    r = rows_per_chunk
    for c in range(t // r):
        acc = conv_ref[c * r:(c + 1) * r, :]
        mu = jnp.mean(acc, axis=-1, keepdims=True)
        cen = acc - mu
        var = jnp.mean(cen * cen, axis=-1, keepdims=True)
        un = cen * lax.rsqrt(var + LN_EPS) * lg_ref[...] + lb_ref[...]
        o_ref[c * r:(c + 1) * r, :] = (un * _sigmoid(un)).astype(o_ref.dtype)


def _conv_module(cv, cg, w_dw, b_dw, ln_g, ln_b, t=256, rows_per_chunk=16, groups_per_pass=4):
    s, dc = cv.shape
    halo_blocks = t // CONV_HALO
    row = lambda i: (i, 0)
    prev = lambda i: (jnp.maximum(i * halo_blocks - 1, 0), 0)
    const = lambda i: (0, 0)
    return pl.pallas_call(
        functools.partial(_conv_body, rows_per_chunk=rows_per_chunk, groups_per_pass=groups_per_pass),
        grid=(s // t,),
        in_specs=[
            pl.BlockSpec((t, dc), row),
            pl.BlockSpec((t, dc), row),
            pl.BlockSpec((CONV_HALO, dc), prev),
            pl.BlockSpec((CONV_HALO, dc), prev),
            pl.BlockSpec((CONV_WIDTH, dc), const),
            pl.BlockSpec((1, dc), const),
            pl.BlockSpec((1, dc), const),
            pl.BlockSpec((1, dc), const),
        ],
        out_specs=pl.BlockSpec((t, dc), row),
        out_shape=jax.ShapeDtypeStruct((s, dc), BF16),
        scratch_shapes=[
            pltpu.VMEM((t + CONV_HALO, dc), F32),
            pltpu.VMEM((SUBLANES - 1, t + CONV_HALO - SUBLANES, dc), F32),
            pltpu.VMEM((CONV_WIDTH, SUBLANES, dc), F32),
            pltpu.VMEM((t, dc), F32),
        ],
        compiler_params=_cparams("arbitrary"),
        name="conv_module",
    )(cv, cg, cv, cg, w_dw, b_dw, ln_g, ln_b)


def _attn_pair(q2, kb, vb, bias_ref, col_lo):
    n_keys = kb.shape[0]
    lane = lax.broadcasted_iota(I32, q2.shape, 1)
    outs = []
    for h in range(2):
        in_head = (lane >= h * HEAD_DIM) & (lane < (h + 1) * HEAD_DIM)
        qh = jnp.where(in_head, q2, jnp.zeros_like(q2))
        sc = lax.dot_general(qh, kb, (((1,), (1,)), ((), ())), preferred_element_type=F32)
        sc = sc + bias_ref[h, :, col_lo:col_lo + n_keys]
        m = jnp.max(sc, axis=-1, keepdims=True)
        e = jnp.exp(sc - m)
        den = jnp.sum(e, axis=-1, keepdims=True)
        o = jnp.dot(e.astype(BF16), vb, preferred_element_type=F32)
        outs.append(o * (1.0 / den))
    return jnp.where(lane < HEAD_DIM, outs[0], outs[1])


def _attn_body(q_ref, k_ref, v_ref, g_ref, o_ref, bias_ref, *, pairs_per_step):
    st = pl.program_id(1)
    npp = pairs_per_step

    @pl.when(st == 0)
    def _():
        qi = lax.broadcasted_iota(I32, (PAIR, BAND_PAIR), 0)
        kj = lax.broadcasted_iota(I32, (PAIR, BAND_PAIR), 1)
        band_lo = jnp.where(qi >= CHUNK, CHUNK, 0)
        in_band = (kj >= band_lo) & (kj < band_lo + (LEFT_CHUNKS + 1) * CHUNK)
        for h in range(2):
            t = jnp.broadcast_to(g_ref[h], (PAIR, BAND_PAIR))
            shift = 1
            while shift < PAIR:
                t = jnp.where((qi & shift) != 0, pltpu.roll(t, shift, 1), t)
                shift *= 2
            bias_ref[h] = jnp.where(in_band, t, NEG_INF)

    def run(first_step):
        for u in range(npp):
            q2 = q_ref[u * PAIR:(u + 1) * PAIR, :]
            if first_step:
                n_keys = min((u + 1) * PAIR, BAND_PAIR)
                lo = max((u + 1) * PAIR - BAND_PAIR, 0)
                kb = k_ref[lo:lo + n_keys, :]
                vb = v_ref[lo:lo + n_keys, :]
                col_lo = BAND_PAIR - n_keys
            else:
                start = pl.multiple_of((st * npp + u) * PAIR - FRONT_PAD, PAIR)
                kb = k_ref[pl.ds(start, BAND_PAIR), :]
                vb = v_ref[pl.ds(start, BAND_PAIR), :]
                col_lo = 0
            o = _attn_pair(q2, kb, vb, bias_ref, col_lo)
            o_ref[u * PAIR:(u + 1) * PAIR, :] = o.astype(o_ref.dtype)

    @pl.when(st == 0)
    def _():
        run(True)

    @pl.when(st > 0)
    def _():
        run(False)


def _distance_rows(rel_table):
    t = rel_table.astype(F32)
    far = t[:, 2 * REL_CLIP:]
    n_far = LEFT_CHUNKS * CHUNK - REL_CLIP + 1
    near = jnp.flip(t[:, REL_CLIP - CHUNK + 1:2 * REL_CLIP], axis=1)
    tail = BAND_PAIR - n_far - near.shape[1]
    g = jnp.concatenate([jnp.repeat(far, n_far, axis=1), near, jnp.repeat(far, tail, axis=1)], axis=1)
    return g[:, None, :]


def _attention(q, k, v, g_rows, pairs_per_step=4):
    s, da = q.shape
    n_hp = da // (2 * HEAD_DIM)
    tq = pairs_per_step * PAIR
    assert pairs_per_step * PAIR >= FRONT_PAD, "first step must cover every clamped band"
    return pl.pallas_call(
        functools.partial(_attn_body, pairs_per_step=pairs_per_step),
        grid=(n_hp, s // tq),
        in_specs=[
            pl.BlockSpec((tq, 2 * HEAD_DIM), lambda hp, st: (st, hp)),
            pl.BlockSpec((s, 2 * HEAD_DIM), lambda hp, st: (0, hp)),
            pl.BlockSpec((s, 2 * HEAD_DIM), lambda hp, st: (0, hp)),
            pl.BlockSpec((2, 1, BAND_PAIR), lambda hp, st: (hp, 0, 0)),
        ],
        out_specs=pl.BlockSpec((tq, 2 * HEAD_DIM), lambda hp, st: (st, hp)),
        out_shape=jax.ShapeDtypeStruct((s, da), BF16),
        scratch_shapes=[pltpu.VMEM((2, PAIR, BAND_PAIR), F32)],
        compiler_params=_cparams("arbitrary", "arbitrary"),
        name="chunk_attention",
    )(q, k, v, g_rows)


def _store_slabs(ref, val):
    t, d = val.shape
    n = d // LANES
    for c in range(n):
        ref[pl.ds(c, t, stride=n), :] = val[:, c * LANES:(c + 1) * LANES]


def _load_slabs(ref, t, n):
    return jnp.concatenate([ref[pl.ds(c, t, stride=n), :] for c in range(n)], axis=1)


def _first_index_of_max(vals, sub, n):
    vmax = jnp.max(vals, axis=0, keepdims=True)
    idx = jnp.min(jnp.where(vals == vmax, sub, n), axis=0, keepdims=True)
    return vmax, idx


def _outproj_body(x_ref, co_ref, ao_ref, w_ref, g2_ref, wr_ref, br_ref,
                  x2_ref, h2_ref, eid_ref, gcol_ref):
    dc = co_ref.shape[1]
    y = jnp.dot(co_ref[...], w_ref[0:dc, :], preferred_element_type=F32)
    y = y + jnp.dot(ao_ref[...], w_ref[dc:, :], preferred_element_type=F32)
    x2 = x_ref[...] + y
    ms = jnp.mean(x2 * x2, axis=-1, keepdims=True)
    h2 = x2 * lax.rsqrt(ms + RMS_EPS) * g2_ref[...]
    x2_ref[...] = x2
    _store_slabs(h2_ref, h2)

    hi = h2.astype(BF16)
    lo = (h2 - hi.astype(F32)).astype(BF16)
    pa = jnp.dot(hi, wr_ref[...], preferred_element_type=F32)
    pb = jnp.dot(lo, wr_ref[...], preferred_element_type=F32)
    logits = pa[:, :LANES] + pa[:, LANES:] + pb[:, :LANES]
    lt = logits.T + br_ref[...]
    tm = lt.shape[1]
    sub = lax.broadcasted_iota(I32, (N_GROUPS, tm), 0)
    g = lt[0:N_GROUPS, :]
    gmax, grp = _first_index_of_max(g, sub, N_GROUPS)
    p_grp = 1.0 / jnp.sum(jnp.exp(g - gmax), axis=0, keepdims=True)
    sel = jnp.zeros((EXPERTS_PER_GROUP, tm), F32)
    for gi in range(N_GROUPS):
        lo = N_GROUPS + gi * EXPERTS_PER_GROUP
        sel = jnp.where(grp == gi, lt[lo:lo + EXPERTS_PER_GROUP, :], sel)
    v1, i1 = _first_index_of_max(sel, sub, EXPERTS_PER_GROUP)
    rest = jnp.where(sub == i1, -jnp.inf, sel)
    v2, i2 = _first_index_of_max(rest, sub, EXPERTS_PER_GROUP)
    t = jnp.exp(v2 - v1)
    gate1 = p_grp / (1.0 + t)
    gate2 = p_grp * t / (1.0 + t)
    eid_ref[0:1, :] = grp * EXPERTS_PER_GROUP + i1
    eid_ref[1:2, :] = grp * EXPERTS_PER_GROUP + i2
    row = lax.broadcasted_iota(I32, (LANES, tm), 0)
    gl = jnp.where(row == 0, gate1, jnp.where(row == 1, gate2, 0.0))
    gcol_ref[...] = gl.T


def _out_proj_router(x, conv_out, attn_out, w_bf16, g2, wr_cat, br_col, tm=256):
    s, d = x.shape
    dc = conv_out.shape[1]
    da = attn_out.shape[1]
    row = lambda i: (i, 0)
    const = lambda i: (0, 0)
    return pl.pallas_call(
        _outproj_body,
        grid=(s // tm,),
        in_specs=[
            pl.BlockSpec((tm, d), row),
            pl.BlockSpec((tm, dc), row),
            pl.BlockSpec((tm, da), row),
            pl.BlockSpec((dc + da, d), const, pipeline_mode=pl.Buffered(1)),
            pl.BlockSpec((1, d), const),
            pl.BlockSpec((d, 2 * LANES), const),
            pl.BlockSpec((LANES, 1), const),
        ],
        out_specs=[
            pl.BlockSpec((tm, d), row),
            pl.BlockSpec((tm * (d // LANES), LANES), row),
            pl.BlockSpec((TOP_K, tm), lambda i: (0, i)),
            pl.BlockSpec((tm, LANES), row),
        ],
        out_shape=[
            jax.ShapeDtypeStruct((s, d), F32),
            jax.ShapeDtypeStruct((s * (d // LANES), LANES), F32),
            jax.ShapeDtypeStruct((TOP_K, s), I32),
            jax.ShapeDtypeStruct((s, LANES), F32),
        ],
        compiler_params=_cparams("arbitrary"),
        name="out_proj_router",
    )(x, conv_out, attn_out, w_bf16, g2, wr_cat, br_col)


META_LANES = 256
(META_FIRST_BLK, META_N_BLK, META_COUNT, META_NEXT_EXP, META_ORDINAL,
 META_USED_BLKS, META_FIRST_USED) = range(7)


def _dispatch_body(eid_ref, dest_ref, meta_ref, rank_ref):
    s = eid_ref.shape[1]
    w = META_LANES
    sub = lax.broadcasted_iota(I32, (N_EXPERTS, w), 0)
    lane = lax.broadcasted_iota(I32, (N_EXPERTS, w), 1)
    ri = lax.broadcasted_iota(I32, (w, w), 0)
    ci = lax.broadcasted_iota(I32, (w, w), 1)
    upper = (ri < ci).astype(BF16)

    def onehots(b):
        off = pl.multiple_of(b * w, w)
        e0 = eid_ref[0:1, pl.ds(off, w)]
        e1 = eid_ref[1:2, pl.ds(off, w)]
        return off, sub == e0, sub == e1

    def rank_block(b, carry):
        off, o0, o1 = onehots(b)
        both = jnp.where(o0 | o1, 1.0, 0.0)
        pre = jnp.dot(both.astype(BF16), upper, preferred_element_type=F32) + carry
        rank_ref[0:1, pl.ds(off, w)] = jnp.sum(jnp.where(o0, pre, 0.0), axis=0, keepdims=True)
        rank_ref[1:2, pl.ds(off, w)] = jnp.sum(jnp.where(o1, pre, 0.0), axis=0, keepdims=True)
        return carry + jnp.sum(both, axis=1, keepdims=True)

    counts = lax.fori_loop(0, s // w, rank_block, jnp.zeros((N_EXPERTS, w), F32))

    nblk = ((counts.astype(I32) + (BLOCK_ROWS - 1)) // BLOCK_ROWS).astype(F32)
    er = lax.broadcasted_iota(I32, (N_EXPERTS, N_EXPERTS), 0)
    ec = lax.broadcasted_iota(I32, (N_EXPERTS, N_EXPERTS), 1)
    lower_incl = (ec <= er).astype(BF16)
    pend_blk = jnp.dot(lower_incl, nblk.astype(BF16), preferred_element_type=F32)
    pstart = (pend_blk - nblk) * float(BLOCK_ROWS)

    def on_lanes(v):
        return jnp.sum(jnp.where(sub == lane, v, 0), axis=0, keepdims=True)

    nblk_i = nblk.astype(I32)
    first_blk = on_lanes((pend_blk - nblk).astype(I32))
    n_blk = on_lanes(nblk_i)
    count = on_lanes(counts.astype(I32))
    has_rows = nblk_i > 0
    next_exp = jnp.min(jnp.where(has_rows & (sub > lane), sub, N_EXPERTS), axis=0, keepdims=True)
    next_exp = jnp.where(next_exp < N_EXPERTS, next_exp, -1)
    ordinal = jnp.sum(jnp.where(has_rows & (sub < lane), 1, 0), axis=0, keepdims=True)
    first_used = jnp.min(jnp.where(has_rows, sub, N_EXPERTS - 1), axis=0, keepdims=True)
    used = pend_blk[N_EXPERTS - 1:N_EXPERTS, :].astype(I32)
    mrow = lax.broadcasted_iota(I32, (8, w), 0)
    meta = jnp.where(mrow == META_FIRST_BLK, first_blk, 0)
    meta = jnp.where(mrow == META_N_BLK, n_blk, meta)
    meta = jnp.where(mrow == META_COUNT, count, meta)
    meta = jnp.where(mrow == META_NEXT_EXP, next_exp, meta)
    meta = jnp.where(mrow == META_ORDINAL, ordinal, meta)
    meta = jnp.where(mrow == META_FIRST_USED, first_used, meta)
    meta_ref[...] = jnp.where(mrow == META_USED_BLKS, used, meta)

    def dest_block(b, _):
        off, o0, o1 = onehots(b)
        d0 = rank_ref[0:1, pl.ds(off, w)] + jnp.sum(jnp.where(o0, pstart, 0.0), axis=0, keepdims=True)
        d1 = rank_ref[1:2, pl.ds(off, w)] + jnp.sum(jnp.where(o1, pstart, 0.0), axis=0, keepdims=True)
        dest_ref[0:1, pl.ds(off, w)] = d0.astype(I32)
        dest_ref[1:2, pl.ds(off, w)] = d1.astype(I32)
        return 0

    lax.fori_loop(0, s // w, dest_block, 0)


def _dispatch(eids):
    s = eids.shape[1]
    return pl.pallas_call(
        _dispatch_body,
        out_shape=[
            jax.ShapeDtypeStruct((TOP_K, s), I32),
            jax.ShapeDtypeStruct((8, META_LANES), I32),
        ],
        scratch_shapes=[pltpu.VMEM((TOP_K, s), F32)],
        compiler_params=pltpu.CompilerParams(vmem_limit_bytes=VMEM_LIMIT_BYTES),
        name="dispatch",
    )(eids)


def _invert_body(dest_ref, meta_ref, rowtok_ref):
    s = dest_ref.shape[1]
    n_blocks = rowtok_ref.shape[0] // BLOCK_ROWS
    used = meta_ref[META_USED_BLKS, 0]

    def per_expert(e, _):
        b0 = meta_ref[META_FIRST_BLK, e]
        nb = meta_ref[META_N_BLK, e]
        cnt = meta_ref[META_COUNT, e]

        def pad_row(r, _):
            rowtok_ref[r] = 0
            return 0

        lax.fori_loop(b0 * BLOCK_ROWS + cnt, (b0 + nb) * BLOCK_ROWS, pad_row, 0)
        return 0

    lax.fori_loop(0, N_EXPERTS, per_expert, 0)

    def unused_row(r, _):
        rowtok_ref[r] = 0
        return 0

    lax.fori_loop(used * BLOCK_ROWS, n_blocks * BLOCK_ROWS, unused_row, 0)

    def put(t, _):
        rowtok_ref[dest_ref[0, t]] = t
        rowtok_ref[dest_ref[1, t]] = t
        return 0

    lax.fori_loop(0, s, put, 0, unroll=8)


def _invert(dest, meta, rows):
    smem = pl.BlockSpec(memory_space=pltpu.SMEM)
    return pl.pallas_call(
        _invert_body,
        in_specs=[smem, smem],
        out_specs=smem,
        out_shape=jax.ShapeDtypeStruct((rows,), I32),
        name="invert_dest",
    )(dest, meta)


def _row_gather_wait(src_hbm, dst, sem, n_rows):
    pltpu.make_async_copy(src_hbm.at[pl.ds(0, n_rows)], dst.at[pl.ds(0, n_rows)], sem).wait()


BIG_DMA_PRIORITY = 1


def _experts_body(meta_ref, rowtok_ref, h2_hbm, wg_hbm, wu_hbm, wd_hbm, ys_hbm,
                  xbuf, ybuf, zbuf, wgbuf, wubuf, wdbuf, gsem, ysem, zsem, wsem):
    used = meta_ref[META_USED_BLKS, 0]
    n_experts, d, _ = wg_hbm.shape
    n_sl = d // LANES
    blk = BLOCK_ROWS * n_sl
    n_blocks = ys_hbm.shape[0] // blk

    def gather_start(b, sl):
        src_blk = jnp.minimum(b, n_blocks - 1)
        for r in range(BLOCK_ROWS):
            tok = rowtok_ref[src_blk * BLOCK_ROWS + r]
            src = h2_hbm.at[pl.ds(pl.multiple_of(tok * n_sl, n_sl), n_sl)]
            pltpu.make_async_copy(src, xbuf.at[sl, pl.ds(r * n_sl, n_sl)], gsem.at[sl]).start()

    def gather_wait(sl):
        _row_gather_wait(h2_hbm, xbuf.at[sl], gsem.at[sl], blk)

    def ys_copy(b):
        sl = b % 2
        rows = pl.ds(pl.multiple_of(b * blk, blk), blk)
        return pltpu.make_async_copy(ybuf.at[sl], ys_hbm.at[rows], ysem.at[sl])

    def zero_copy(b):
        rows = pl.ds(pl.multiple_of(b * blk, blk), blk)
        return pltpu.make_async_copy(zbuf, ys_hbm.at[rows], zsem)

    def weight_copies(e, ws):
        return (pltpu.make_async_copy(wg_hbm.at[e], wgbuf.at[ws], wsem.at[ws, 0]),
                pltpu.make_async_copy(wu_hbm.at[e], wubuf.at[ws], wsem.at[ws, 1]),
                pltpu.make_async_copy(wd_hbm.at[e], wdbuf.at[ws], wsem.at[ws, 2]))

    for c in weight_copies(meta_ref[META_FIRST_USED, 0], 0):
        c.start(priority=BIG_DMA_PRIORITY)
    gather_start(0, 0)
    zbuf[...] = jnp.zeros_like(zbuf)
    lax.fori_loop(used, n_blocks, lambda b, _: (zero_copy(b).start(priority=BIG_DMA_PRIORITY), 0)[1], 0)

    def expert(e, _):
        nb = meta_ref[META_N_BLK, e]

        @pl.when(nb > 0)
        def _():
            b0 = meta_ref[META_FIRST_BLK, e]
            ws = meta_ref[META_ORDINAL, e] % 2
            nxt = meta_ref[META_NEXT_EXP, e]

            @pl.when(nxt >= 0)
            def _():
                for c in weight_copies(nxt, 1 - ws):
                    c.start(priority=BIG_DMA_PRIORITY)

            for c in weight_copies(e, ws):
                c.wait()

            def block(j, _):
                b = b0 + j
                sl = b % 2
                gather_start(b + 1, 1 - sl)
                gather_wait(sl)

                @pl.when(b >= 2)
                def _():
                    ys_copy(b - 2).wait()

                x = _load_slabs(xbuf.at[sl], BLOCK_ROWS, n_sl).astype(BF16)
                a = jnp.dot(x, wgbuf[ws].astype(BF16), preferred_element_type=F32)
                u = jnp.dot(x, wubuf[ws].astype(BF16), preferred_element_type=F32)
                act = (a * _sigmoid(a) * u).astype(BF16)
                y = jnp.dot(act, wdbuf[ws].astype(BF16), preferred_element_type=F32)
                _store_slabs(ybuf.at[sl], y)
                ys_copy(b).start(priority=BIG_DMA_PRIORITY)
                return 0

            lax.fori_loop(0, nb, block, 0)

        return 0

    lax.fori_loop(0, n_experts, expert, 0)

    gather_wait(used % 2)
    for back in (1, 2):
        @pl.when(used >= back)
        def _():
            ys_copy(used - back).wait()
    lax.fori_loop(used, n_blocks, lambda b, _: (zero_copy(b).wait(), 0)[1], 0)


def _experts(meta, rowtok, h2_slabs, w_gate, w_up, w_down):
    n_e, d, de = w_gate.shape
    rows = rowtok.shape[0]
    n_sl = d // LANES
    blk = BLOCK_ROWS * n_sl
    smem = pl.BlockSpec(memory_space=pltpu.SMEM)
    hbm = pl.BlockSpec(memory_space=pl.ANY)
    return pl.pallas_call(
        _experts_body,
        in_specs=[smem, smem, hbm, hbm, hbm, hbm],
        out_specs=hbm,
        out_shape=jax.ShapeDtypeStruct((rows * n_sl, LANES), F32),
        scratch_shapes=[
            pltpu.VMEM((2, blk, LANES), F32),
            pltpu.VMEM((2, blk, LANES), F32),
            pltpu.VMEM((blk, LANES), F32),
            pltpu.VMEM((2, d, de), F32),
            pltpu.VMEM((2, d, de), F32),
            pltpu.VMEM((2, de, d), F32),
            pltpu.SemaphoreType.DMA((2,)),
            pltpu.SemaphoreType.DMA((2,)),
            pltpu.SemaphoreType.DMA(()),
            pltpu.SemaphoreType.DMA((2, 3)),
        ],
        compiler_params=pltpu.CompilerParams(vmem_limit_bytes=VMEM_LIMIT_BYTES),
        name="expert_mlp",
    )(meta, rowtok, h2_slabs, w_gate, w_up, w_down)


def _combine_body(dest_ref, x2_ref, gcol_ref, fg_ref, ys_hbm, o_ref, ybuf, sem, *, tm):
    i = pl.program_id(0)
    n = pl.num_programs(0)
    s = n * tm
    slot = i % 2
    n_sl = x2_ref.shape[1] // LANES

    def start(tile, sl):
        for k in range(TOP_K):
            for r in range(tm):
                row = dest_ref[k * s + tile * tm + r]
                src = ys_hbm.at[pl.ds(pl.multiple_of(row * n_sl, n_sl), n_sl)]
                pltpu.make_async_copy(src, ybuf.at[sl, k, pl.ds(r * n_sl, n_sl)],
                                      sem.at[sl, k]).start(priority=r % 2)

    def wait(sl):
        for k in range(TOP_K):
            _row_gather_wait(ys_hbm, ybuf.at[sl, k], sem.at[sl, k], tm * n_sl)

    @pl.when(i == 0)
    def _():
        start(0, 0)

    start(jnp.minimum(i + 1, n - 1), 1 - slot)
    wait(slot)
    g = gcol_ref[...]
    y0 = _load_slabs(ybuf.at[slot, 0], tm, n_sl)
    y1 = _load_slabs(ybuf.at[slot, 1], tm, n_sl)
    moe = g[:, 0:1] * y0 + g[:, 1:2] * y1
    x3 = x2_ref[...] + moe
    ms = jnp.mean(x3 * x3, axis=-1, keepdims=True)
    o_ref[...] = x3 * lax.rsqrt(ms + RMS_EPS) * fg_ref[...]

    @pl.when(i == n - 1)
    def _():
        wait(1 - slot)


def _combine(dest_flat, x2, gcol, final_g, ys, tm=128):
    s, d = x2.shape
    row = lambda i, dest_ref: (i, 0)
    const = lambda i, dest_ref: (0, 0)
    grid_spec = pltpu.PrefetchScalarGridSpec(
        num_scalar_prefetch=1,
        grid=(s // tm,),
        in_specs=[
            pl.BlockSpec((tm, d), row),
            pl.BlockSpec((tm, LANES), row),
            pl.BlockSpec((1, d), const),
            pl.BlockSpec(memory_space=pl.ANY),
        ],
        out_specs=pl.BlockSpec((tm, d), row),
        scratch_shapes=[
            pltpu.VMEM((2, TOP_K, tm * (d // LANES), LANES), F32),
            pltpu.SemaphoreType.DMA((2, TOP_K)),
        ],
    )
    return pl.pallas_call(
        functools.partial(_combine_body, tm=tm),
        grid_spec=grid_spec,
        out_shape=jax.ShapeDtypeStruct((s, d), F32),
        compiler_params=_cparams("arbitrary"),
        name="combine_norm",
    )(dest_flat, x2, gcol, final_g, ys)


def _layer(x, norm1_g, w_in, w_dw, b_dw, ln_g, ln_b, rel_bias, w_out, norm2_g,
           w_group, b_group, w_expert_router, b_expert_router, w_gate, w_up, w_down, out_g):
    s, d = x.shape
    dc = w_dw.shape[-1]
    da = (w_in.shape[1] - 2 * dc) // 3
    row2 = lambda a: a.reshape(1, -1)

    cv, cg, q, k, v = _in_proj(x, row2(norm1_g), w_in.astype(BF16), dc, da)
    conv_out = _conv_module(cv, cg, w_dw.reshape(CONV_WIDTH, dc), row2(b_dw), row2(ln_g), row2(ln_b))
    attn_out = _attention(q, k, v, _distance_rows(rel_bias))

    n_r = N_GROUPS + N_EXPERTS
    wr = jnp.pad(jnp.concatenate([w_group, w_expert_router], axis=1), ((0, 0), (0, LANES - n_r)))
    wr_hi = wr.astype(BF16)
    wr_lo = (wr - wr_hi.astype(F32)).astype(BF16)
    wr_cat = jnp.concatenate([wr_hi, wr_lo], axis=1)
    br_col = jnp.zeros((LANES, 1), F32).at[:n_r, 0].set(jnp.concatenate([b_group, b_expert_router]))
    x2, h2, eids, gcol = _out_proj_router(x, conv_out, attn_out, w_out.astype(BF16), row2(norm2_g),
                                          wr_cat, br_col)

    n_blocks = (s * TOP_K) // BLOCK_ROWS + N_EXPERTS
    assert n_blocks <= META_LANES
    dest, meta = _dispatch(eids)
    rowtok = _invert(dest, meta, n_blocks * BLOCK_ROWS)
    ys = _experts(meta, rowtok, h2, w_gate, w_up, w_down)
    return _combine(dest.reshape(-1), x2, gcol, row2(out_g), ys)


def kernel(x, norm1_g, w_in, w_dw, b_dw, ln_g, ln_b, rel_bias, w_out, norm2_g, w_group, b_group,
           w_expert_router, b_expert_router, w_gate, w_up, w_down, final_g):
    depth = norm1_g.shape[0]
    assert depth == 1, "the final norm is fused into the last layer's combine step"
    batch, s, d = x.shape
    assert batch == 1, "one sequence per call"
    out = _layer(x.reshape(s, d), norm1_g[0], w_in[0], w_dw[0], b_dw[0], ln_g[0], ln_b[0], rel_bias[0],
                 w_out[0], norm2_g[0], w_group[0], b_group[0], w_expert_router[0],
                 b_expert_router[0], w_gate[0], w_up[0], w_down[0], final_g)
    return out.reshape(batch, s, d)
```

```python
import functools

import jax
import jax.numpy as jnp
from jax import lax
from jax.experimental import pallas as pl
from jax.experimental.pallas import tpu as pltpu

F32 = jnp.float32
BF16 = jnp.bfloat16
I32 = jnp.int32

CHUNK = 64
HEAD_DIM = 64
LEFT_CHUNKS = 8
REL_CLIP = 128
CONV_WIDTH = 31
N_GROUPS = 8
EXPERTS_PER_GROUP = 8
N_EXPERTS = N_GROUPS * EXPERTS_PER_GROUP
TOP_K = 2
BLOCK_ROWS = 128
RMS_EPS = 1e-6
LN_EPS = 1e-5
NEG_INF = -1e30

VMEM_LIMIT_BYTES = 56 * 1024 * 1024
LANES = 128
SUBLANES = 8
PAIR = 2 * CHUNK
BAND_PAIR = (LEFT_CHUNKS + 2) * CHUNK
FRONT_PAD = LEFT_CHUNKS * CHUNK
CONV_HALO = 32


def _cparams(*sem):
    return pltpu.CompilerParams(dimension_semantics=sem, vmem_limit_bytes=VMEM_LIMIT_BYTES)


def _inproj_body(x_ref, g_ref, w_ref, cv_ref, cg_ref, q_ref, k_ref, v_ref):
    x = x_ref[...]
    ms = jnp.mean(x * x, axis=-1, keepdims=True)
    h = (x * lax.rsqrt(ms + RMS_EPS) * g_ref[...]).astype(BF16)
    dc = cv_ref.shape[1]
    da = q_ref.shape[1]

    def proj(lo, width):
        return jnp.dot(h, w_ref[:, lo:lo + width], preferred_element_type=F32)

    cv_ref[...] = proj(0, dc)
    cg_ref[...] = proj(dc, dc)
    q_ref[...] = (proj(2 * dc, da) * (HEAD_DIM ** -0.5)).astype(BF16)
    k_ref[...] = proj(2 * dc + da, da).astype(BF16)
    v_ref[...] = proj(2 * dc + 2 * da, da).astype(BF16)


def _in_proj(x, g, w_bf16, d_conv, d_att, tm=256):
    s, d = x.shape
    d_in = w_bf16.shape[1]
    row = lambda i: (i, 0)
    const = lambda i: (0, 0)
    return pl.pallas_call(
        _inproj_body,
        grid=(s // tm,),
        in_specs=[
            pl.BlockSpec((tm, d), row),
            pl.BlockSpec((1, d), const),
            pl.BlockSpec((d, d_in), const, pipeline_mode=pl.Buffered(1)),
        ],
        out_specs=[
            pl.BlockSpec((tm, d_conv), row),
            pl.BlockSpec((tm, d_conv), row),
            pl.BlockSpec((tm, d_att), row),
            pl.BlockSpec((tm, d_att), row),
            pl.BlockSpec((tm, d_att), row),
        ],
        out_shape=[
            jax.ShapeDtypeStruct((s, d_conv), F32),
            jax.ShapeDtypeStruct((s, d_conv), F32),
            jax.ShapeDtypeStruct((s, d_att), BF16),
            jax.ShapeDtypeStruct((s, d_att), BF16),
            jax.ShapeDtypeStruct((s, d_att), BF16),
        ],
        compiler_params=_cparams("arbitrary"),
        name="in_proj",
    )(x, g, w_bf16)


def _sigmoid(x):
    return 1.0 / (1.0 + jnp.exp(-x))


def _conv_body(cv_ref, cg_ref, cvh_ref, cgh_ref, w_ref, b_ref, lg_ref, lb_ref, o_ref,
               u_ref, us_ref, wb_ref, conv_ref, *, rows_per_chunk, groups_per_pass):
    i = pl.program_id(0)
    t, dc = cv_ref.shape

    @pl.when(i == 0)
    def _():
        for w in range(CONV_WIDTH):
            wb_ref[w] = jnp.broadcast_to(w_ref[w:w + 1, :], (SUBLANES, dc))

    halo = cvh_ref[...] * _sigmoid(cgh_ref[...])
    u_ref[0:CONV_HALO, :] = jnp.where(i > 0, halo, 0.0)
    u_ref[CONV_HALO:, :] = cv_ref[...] * _sigmoid(cg_ref[...])
    n_shift_rows = us_ref.shape[1]
    for sft in range(1, SUBLANES):
        us_ref[sft - 1] = u_ref[sft:sft + n_shift_rows, :]
    first_tap = CONV_HALO - (CONV_WIDTH - 1)

    def lane_tile(lt, _):
        lanes = pl.ds(pl.multiple_of(lt * LANES, LANES), LANES)
        wv = [wb_ref[w, :, lanes] for w in range(CONV_WIDTH)]
        bias = jnp.broadcast_to(b_ref[:, lanes], (SUBLANES, LANES))
        for g0 in range(0, t // SUBLANES, groups_per_pass):
            accs = [bias] * groups_per_pass
            for w in range(CONV_WIDTH):
                off = first_tap + w
                sft = off % SUBLANES
                for g in range(groups_per_pass):
                    lo = (g0 + g) * SUBLANES + off - sft
                    if sft == 0:
                        src = u_ref[lo:lo + SUBLANES, lanes]
                    else:
                        src = us_ref[sft - 1, lo:lo + SUBLANES, lanes]
                    accs[g] = accs[g] + wv[w] * src
            for g in range(groups_per_pass):
                lo = (g0 + g) * SUBLANES
                conv_ref[lo:lo + SUBLANES, lanes] = accs[g]
        return 0

    lax.fori_loop(0, dc // LANES, lane_tile, 0)

    r = rows_per_chunk
    for c in range(t // r):
        acc = conv_ref[c * r:(c + 1) * r, :]
        mu = jnp.mean(acc, axis=-1, keepdims=True)
        cen = acc - mu
        var = jnp.mean(cen * cen, axis=-1, keepdims=True)
        un = cen * lax.rsqrt(var + LN_EPS) * lg_ref[...] + lb_ref[...]
        o_ref[c * r:(c + 1) * r, :] = (un * _sigmoid(un)).astype(o_ref.dtype)


def _conv_module(cv, cg, w_dw, b_dw, ln_g, ln_b, t=256, rows_per_chunk=16, groups_per_pass=4):
    s, dc = cv.shape
    halo_blocks = t // CONV_HALO
    row = lambda i: (i, 0)
    prev = lambda i: (jnp.maximum(i * halo_blocks - 1, 0), 0)
    const = lambda i: (0, 0)
    return pl.pallas_call(
        functools.partial(_conv_body, rows_per_chunk=rows_per_chunk, groups_per_pass=groups_per_pass),
        grid=(s // t,),
        in_specs=[
            pl.BlockSpec((t, dc), row),
            pl.BlockSpec((t, dc), row),
            pl.BlockSpec((CONV_HALO, dc), prev),
            pl.BlockSpec((CONV_HALO, dc), prev),
            pl.BlockSpec((CONV_WIDTH, dc), const),
            pl.BlockSpec((1, dc), const),
            pl.BlockSpec((1, dc), const),
            pl.BlockSpec((1, dc), const),
        ],
        out_specs=pl.BlockSpec((t, dc), row),
        out_shape=jax.ShapeDtypeStruct((s, dc), BF16),
        scratch_shapes=[
            pltpu.VMEM((t + CONV_HALO, dc), F32),
            pltpu.VMEM((SUBLANES - 1, t + CONV_HALO - SUBLANES, dc), F32),
            pltpu.VMEM((CONV_WIDTH, SUBLANES, dc), F32),
            pltpu.VMEM((t, dc), F32),
        ],
        compiler_params=_cparams("arbitrary"),
        name="conv_module",
    )(cv, cg, cv, cg, w_dw, b_dw, ln_g, ln_b)


def _attn_pair(q2, kb, vb, bias_ref, col_lo):
    n_keys = kb.shape[0]
    lane = lax.broadcasted_iota(I32, q2.shape, 1)
    outs = []
    for h in range(2):
        in_head = (lane >= h * HEAD_DIM) & (lane < (h + 1) * HEAD_DIM)
        qh = jnp.where(in_head, q2, jnp.zeros_like(q2))
        sc = lax.dot_general(qh, kb, (((1,), (1,)), ((), ())), preferred_element_type=F32)
        sc = sc + bias_ref[h, :, col_lo:col_lo + n_keys]
        m = jnp.max(sc, axis=-1, keepdims=True)
        e = jnp.exp(sc - m)
        den = jnp.sum(e, axis=-1, keepdims=True)
        o = jnp.dot(e.astype(BF16), vb, preferred_element_type=F32)
        outs.append(o * (1.0 / den))
    return jnp.where(lane < HEAD_DIM, outs[0], outs[1])


def _attn_body(q_ref, k_ref, v_ref, g_ref, o_ref, bias_ref, *, pairs_per_step):
    st = pl.program_id(1)
    npp = pairs_per_step

    @pl.when(st == 0)
    def _():
        qi = lax.broadcasted_iota(I32, (PAIR, BAND_PAIR), 0)
        kj = lax.broadcasted_iota(I32, (PAIR, BAND_PAIR), 1)
        band_lo = jnp.where(qi >= CHUNK, CHUNK, 0)
        in_band = (kj >= band_lo) & (kj < band_lo + (LEFT_CHUNKS + 1) * CHUNK)
        for h in range(2):
            t = jnp.broadcast_to(g_ref[h], (PAIR, BAND_PAIR))
            shift = 1
            while shift < PAIR:
                t = jnp.where((qi & shift) != 0, pltpu.roll(t, shift, 1), t)
                shift *= 2
            bias_ref[h] = jnp.where(in_band, t, NEG_INF)

    def run(first_step):
        for u in range(npp):
            q2 = q_ref[u * PAIR:(u + 1) * PAIR, :]
            if first_step:
                n_keys = min((u + 1) * PAIR, BAND_PAIR)
                lo = max((u + 1) * PAIR - BAND_PAIR, 0)
                kb = k_ref[lo:lo + n_keys, :]
                vb = v_ref[lo:lo + n_keys, :]
                col_lo = BAND_PAIR - n_keys
            else:
                start = pl.multiple_of((st * npp + u) * PAIR - FRONT_PAD, PAIR)
                kb = k_ref[pl.ds(start, BAND_PAIR), :]
                vb = v_ref[pl.ds(start, BAND_PAIR), :]
                col_lo = 0
            o = _attn_pair(q2, kb, vb, bias_ref, col_lo)
            o_ref[u * PAIR:(u + 1) * PAIR, :] = o.astype(o_ref.dtype)

    @pl.when(st == 0)
    def _():
        run(True)

    @pl.when(st > 0)
    def _():
        run(False)


def _distance_rows(rel_table):
    t = rel_table.astype(F32)
    far = t[:, 2 * REL_CLIP:]
    n_far = LEFT_CHUNKS * CHUNK - REL_CLIP + 1
    near = jnp.flip(t[:, REL_CLIP - CHUNK + 1:2 * REL_CLIP], axis=1)
    tail = BAND_PAIR - n_far - near.shape[1]
    g = jnp.concatenate([jnp.repeat(far, n_far, axis=1), near, jnp.repeat(far, tail, axis=1)], axis=1)
    return g[:, None, :]


def _attention(q, k, v, g_rows, pairs_per_step=4):
    s, da = q.shape
    n_hp = da // (2 * HEAD_DIM)
    tq = pairs_per_step * PAIR
    assert pairs_per_step * PAIR >= FRONT_PAD, "first step must cover every clamped band"
    return pl.pallas_call(
        functools.partial(_attn_body, pairs_per_step=pairs_per_step),
        grid=(n_hp, s // tq),
        in_specs=[
            pl.BlockSpec((tq, 2 * HEAD_DIM), lambda hp, st: (st, hp)),
            pl.BlockSpec((s, 2 * HEAD_DIM), lambda hp, st: (0, hp)),
            pl.BlockSpec((s, 2 * HEAD_DIM), lambda hp, st: (0, hp)),
            pl.BlockSpec((2, 1, BAND_PAIR), lambda hp, st: (hp, 0, 0)),
        ],
        out_specs=pl.BlockSpec((tq, 2 * HEAD_DIM), lambda hp, st: (st, hp)),
        out_shape=jax.ShapeDtypeStruct((s, da), BF16),
        scratch_shapes=[pltpu.VMEM((2, PAIR, BAND_PAIR), F32)],
        compiler_params=_cparams("arbitrary", "arbitrary"),
        name="chunk_attention",
    )(q, k, v, g_rows)


def _slab_rows(d):
    return d // LANES


def _store_slabs(ref, val):
    t, d = val.shape
    n = _slab_rows(d)
    for c in range(n):
        ref[pl.ds(c, t, stride=n), :] = val[:, c * LANES:(c + 1) * LANES]


def _load_slabs(ref, t, n):
    return jnp.concatenate([ref[pl.ds(c, t, stride=n), :] for c in range(n)], axis=1)


def _first_index_of_max(vals, sub, n):
    vmax = jnp.max(vals, axis=0, keepdims=True)
    idx = jnp.min(jnp.where(vals == vmax, sub, n), axis=0, keepdims=True)
    return vmax, idx


def _outproj_body(x_ref, co_ref, ao_ref, w_ref, g2_ref, wr_ref, br_ref,
                  x2_ref, h2_ref, eid_ref, gcol_ref):
    dc = co_ref.shape[1]
    y = jnp.dot(co_ref[...], w_ref[0:dc, :], preferred_element_type=F32)
    y = y + jnp.dot(ao_ref[...], w_ref[dc:, :], preferred_element_type=F32)
    x2 = x_ref[...] + y
    ms = jnp.mean(x2 * x2, axis=-1, keepdims=True)
    h2 = x2 * lax.rsqrt(ms + RMS_EPS) * g2_ref[...]
    x2_ref[...] = x2
    _store_slabs(h2_ref, h2)

    hi = h2.astype(BF16)
    lo = (h2 - hi.astype(F32)).astype(BF16)
    pa = jnp.dot(hi, wr_ref[...], preferred_element_type=F32)
    pb = jnp.dot(lo, wr_ref[...], preferred_element_type=F32)
    logits = pa[:, :LANES] + pa[:, LANES:] + pb[:, :LANES]
    lt = logits.T + br_ref[...]
    tm = lt.shape[1]
    sub = lax.broadcasted_iota(I32, (N_GROUPS, tm), 0)
    g = lt[0:N_GROUPS, :]
    gmax, grp = _first_index_of_max(g, sub, N_GROUPS)
    p_grp = 1.0 / jnp.sum(jnp.exp(g - gmax), axis=0, keepdims=True)
    sel = jnp.zeros((EXPERTS_PER_GROUP, tm), F32)
    for gi in range(N_GROUPS):
        lo = N_GROUPS + gi * EXPERTS_PER_GROUP
        sel = jnp.where(grp == gi, lt[lo:lo + EXPERTS_PER_GROUP, :], sel)
    v1, i1 = _first_index_of_max(sel, sub, EXPERTS_PER_GROUP)
    rest = jnp.where(sub == i1, -jnp.inf, sel)
    v2, i2 = _first_index_of_max(rest, sub, EXPERTS_PER_GROUP)
    t = jnp.exp(v2 - v1)
    gate1 = p_grp / (1.0 + t)
    gate2 = p_grp * t / (1.0 + t)
    eid_ref[0:1, :] = grp * EXPERTS_PER_GROUP + i1
    eid_ref[1:2, :] = grp * EXPERTS_PER_GROUP + i2
    row = lax.broadcasted_iota(I32, (LANES, tm), 0)
    gl = jnp.where(row == 0, gate1, jnp.where(row == 1, gate2, 0.0))
    gcol_ref[...] = gl.T


def _out_proj_router(x, conv_out, attn_out, w_bf16, g2, wr_cat, br_col, tm=256):
    s, d = x.shape
    dc = conv_out.shape[1]
    da = attn_out.shape[1]
    row = lambda i: (i, 0)
    const = lambda i: (0, 0)
    return pl.pallas_call(
        _outproj_body,
        grid=(s // tm,),
        in_specs=[
            pl.BlockSpec((tm, d), row),
            pl.BlockSpec((tm, dc), row),
            pl.BlockSpec((tm, da), row),
            pl.BlockSpec((dc + da, d), const, pipeline_mode=pl.Buffered(1)),
            pl.BlockSpec((1, d), const),
            pl.BlockSpec((d, 2 * LANES), const),
            pl.BlockSpec((LANES, 1), const),
        ],
        out_specs=[
            pl.BlockSpec((tm, d), row),
            pl.BlockSpec((tm * _slab_rows(d), LANES), row),
            pl.BlockSpec((TOP_K, tm), lambda i: (0, i)),
            pl.BlockSpec((tm, LANES), row),
        ],
        out_shape=[
            jax.ShapeDtypeStruct((s, d), F32),
            jax.ShapeDtypeStruct((s * _slab_rows(d), LANES), F32),
            jax.ShapeDtypeStruct((TOP_K, s), I32),
            jax.ShapeDtypeStruct((s, LANES), F32),
        ],
        compiler_params=_cparams("arbitrary"),
        name="out_proj_router",
    )(x, conv_out, attn_out, w_bf16, g2, wr_cat, br_col)


META_LANES = 256
(META_FIRST_BLK, META_N_BLK, META_COUNT, META_NEXT_EXP, META_ORDINAL,
 META_USED_BLKS, META_FIRST_USED) = range(7)


def _dispatch_body(eid_ref, dest_ref, meta_ref, rank_ref):
    s = eid_ref.shape[1]
    w = META_LANES
    sub = lax.broadcasted_iota(I32, (N_EXPERTS, w), 0)
    lane = lax.broadcasted_iota(I32, (N_EXPERTS, w), 1)
    ri = lax.broadcasted_iota(I32, (w, w), 0)
    ci = lax.broadcasted_iota(I32, (w, w), 1)
    upper = (ri < ci).astype(BF16)

    def onehots(b):
        off = pl.multiple_of(b * w, w)
        e0 = eid_ref[0:1, pl.ds(off, w)]
        e1 = eid_ref[1:2, pl.ds(off, w)]
        return off, sub == e0, sub == e1

    def rank_block(b, carry):
        off, o0, o1 = onehots(b)
        both = jnp.where(o0 | o1, 1.0, 0.0)
        pre = jnp.dot(both.astype(BF16), upper, preferred_element_type=F32) + carry
        rank_ref[0:1, pl.ds(off, w)] = jnp.sum(jnp.where(o0, pre, 0.0), axis=0, keepdims=True)
        rank_ref[1:2, pl.ds(off, w)] = jnp.sum(jnp.where(o1, pre, 0.0), axis=0, keepdims=True)
        return carry + jnp.sum(both, axis=1, keepdims=True)

    counts = lax.fori_loop(0, s // w, rank_block, jnp.zeros((N_EXPERTS, w), F32))

    nblk = ((counts.astype(I32) + (BLOCK_ROWS - 1)) // BLOCK_ROWS).astype(F32)
    er = lax.broadcasted_iota(I32, (N_EXPERTS, N_EXPERTS), 0)
    ec = lax.broadcasted_iota(I32, (N_EXPERTS, N_EXPERTS), 1)
    lower_incl = (ec <= er).astype(BF16)
    pend_blk = jnp.dot(lower_incl, nblk.astype(BF16), preferred_element_type=F32)
    pstart = (pend_blk - nblk) * float(BLOCK_ROWS)

    def on_lanes(v):
        return jnp.sum(jnp.where(sub == lane, v, 0), axis=0, keepdims=True)

    nblk_i = nblk.astype(I32)
    first_blk = on_lanes((pend_blk - nblk).astype(I32))
    n_blk = on_lanes(nblk_i)
    count = on_lanes(counts.astype(I32))
    has_rows = nblk_i > 0
    next_exp = jnp.min(jnp.where(has_rows & (sub > lane), sub, N_EXPERTS), axis=0, keepdims=True)
    next_exp = jnp.where(next_exp < N_EXPERTS, next_exp, -1)
    ordinal = jnp.sum(jnp.where(has_rows & (sub < lane), 1, 0), axis=0, keepdims=True)
    first_used = jnp.min(jnp.where(has_rows, sub, N_EXPERTS - 1), axis=0, keepdims=True)
    used = pend_blk[N_EXPERTS - 1:N_EXPERTS, :].astype(I32)
    mrow = lax.broadcasted_iota(I32, (8, w), 0)
    meta = jnp.where(mrow == META_FIRST_BLK, first_blk, 0)
    meta = jnp.where(mrow == META_N_BLK, n_blk, meta)
    meta = jnp.where(mrow == META_COUNT, count, meta)
    meta = jnp.where(mrow == META_NEXT_EXP, next_exp, meta)
    meta = jnp.where(mrow == META_ORDINAL, ordinal, meta)
    meta = jnp.where(mrow == META_FIRST_USED, first_used, meta)
    meta_ref[...] = jnp.where(mrow == META_USED_BLKS, used, meta)

    def dest_block(b, _):
        off, o0, o1 = onehots(b)
        d0 = rank_ref[0:1, pl.ds(off, w)] + jnp.sum(jnp.where(o0, pstart, 0.0), axis=0, keepdims=True)
        d1 = rank_ref[1:2, pl.ds(off, w)] + jnp.sum(jnp.where(o1, pstart, 0.0), axis=0, keepdims=True)
        dest_ref[0:1, pl.ds(off, w)] = d0.astype(I32)
        dest_ref[1:2, pl.ds(off, w)] = d1.astype(I32)
        return 0

    lax.fori_loop(0, s // w, dest_block, 0)


def _dispatch(eids):
    s = eids.shape[1]
    return pl.pallas_call(
        _dispatch_body,
        out_shape=[
            jax.ShapeDtypeStruct((TOP_K, s), I32),
            jax.ShapeDtypeStruct((8, META_LANES), I32),
        ],
        scratch_shapes=[pltpu.VMEM((TOP_K, s), F32)],
        compiler_params=pltpu.CompilerParams(vmem_limit_bytes=VMEM_LIMIT_BYTES),
        name="dispatch",
    )(eids)


SCATTER_CHUNK = 256
PAD_BITS = tuple(reversed(range(BLOCK_ROWS.bit_length() - 1)))


def _scatter_body(dest_ref, meta_ref, h2_hbm, xs_hbm, zbuf, sem, zsem):
    s = dest_ref.shape[1]
    n_sl = h2_hbm.shape[0] // s
    blk = BLOCK_ROWS * n_sl
    n_blocks = xs_hbm.shape[0] // blk
    used = meta_ref[META_USED_BLKS, 0]

    def slab(row, n_tokens=1):
        return pl.ds(pl.multiple_of(row * n_sl, n_sl), n_tokens * n_sl)

    def zero_copy(row, n_tokens):
        return pltpu.make_async_copy(zbuf.at[pl.ds(0, n_tokens * n_sl)], xs_hbm.at[slab(row, n_tokens)], zsem)

    zbuf[...] = jnp.zeros_like(zbuf)

    def per_expert(e, issued):
        b0 = meta_ref[META_FIRST_BLK, e]
        nb = meta_ref[META_N_BLK, e]
        cnt = meta_ref[META_COUNT, e]
        pad = nb * BLOCK_ROWS - cnt
        row = b0 * BLOCK_ROWS + cnt
        out = []
        for bit, n_issued in zip(PAD_BITS, issued):
            size = 1 << bit
            take = pad & size

            @pl.when(take != 0)
            def _():
                zero_copy(row, size).start()

            row = row + take
            out.append(n_issued + take // size)
        return tuple(out)

    issued = lax.fori_loop(0, N_EXPERTS, per_expert, tuple(jnp.int32(0) for _ in PAD_BITS))
    lax.fori_loop(used, n_blocks, lambda b, _: (zero_copy(b * BLOCK_ROWS, BLOCK_ROWS).start(), 0)[1], 0)

    def token_copy(t, k):
        return pltpu.make_async_copy(h2_hbm.at[slab(t)], xs_hbm.at[slab(dest_ref[k, t])], sem.at[k])

    def issue_chunk(c):
        def one(i, _):
            for k in range(TOP_K):
                token_copy(c * SCATTER_CHUNK + i, k).start(priority=k)
            return 0

        lax.fori_loop(0, SCATTER_CHUNK, one, 0, unroll=8)

    def wait_chunk():
        for k in range(TOP_K):
            pltpu.make_async_copy(h2_hbm.at[slab(0, SCATTER_CHUNK)], xs_hbm.at[slab(0, SCATTER_CHUNK)],
                                  sem.at[k]).wait()

    issue_chunk(0)

    def steady(c, _):
        issue_chunk(c)
        wait_chunk()
        return 0

    lax.fori_loop(1, s // SCATTER_CHUNK, steady, 0)
    wait_chunk()

    for bit, n_issued in zip(PAD_BITS, issued):
        lax.fori_loop(0, n_issued, lambda i, _, n=1 << bit: (zero_copy(0, n).wait(), 0)[1], 0)
    lax.fori_loop(used, n_blocks, lambda b, _: (zero_copy(0, BLOCK_ROWS).wait(), 0)[1], 0)


def _scatter_rows(dest, meta, h2_slabs, rows):
    s = dest.shape[1]
    n_sl = h2_slabs.shape[0] // s
    assert s % SCATTER_CHUNK == 0
    smem = pl.BlockSpec(memory_space=pltpu.SMEM)
    hbm = pl.BlockSpec(memory_space=pl.ANY)
    return pl.pallas_call(
        _scatter_body,
        in_specs=[smem, smem, hbm],
        out_specs=hbm,
        out_shape=jax.ShapeDtypeStruct((rows * n_sl, LANES), h2_slabs.dtype),
        scratch_shapes=[
            pltpu.VMEM((BLOCK_ROWS * n_sl, LANES), h2_slabs.dtype),
            pltpu.SemaphoreType.DMA((TOP_K,)),
            pltpu.SemaphoreType.DMA(()),
        ],
        name="scatter_rows",
    )(dest, meta, h2_slabs)


def _row_gather_wait(src_hbm, dst, sem, n_rows):
    pltpu.make_async_copy(src_hbm.at[pl.ds(0, n_rows)], dst.at[pl.ds(0, n_rows)], sem).wait()


BIG_DMA_PRIORITY = 1


def _experts_body(meta_ref, xs_hbm, wg_hbm, wu_hbm, wd_hbm, ys_hbm,
                  xbuf, ybuf, zbuf, wgbuf, wubuf, wdbuf, gsem, ysem, zsem, wsem):
    used = meta_ref[META_USED_BLKS, 0]
    n_experts, d, _ = wg_hbm.shape
    n_sl = _slab_rows(d)
    blk = BLOCK_ROWS * n_sl
    n_blocks = ys_hbm.shape[0] // blk

    def xs_copy(b, sl):
        src_blk = jnp.minimum(b, n_blocks - 1)
        rows = pl.ds(pl.multiple_of(src_blk * blk, blk), blk)
        return pltpu.make_async_copy(xs_hbm.at[rows], xbuf.at[sl], gsem.at[sl])

    def gather_start(b, sl):
        xs_copy(b, sl).start()

    def gather_wait(sl):
        xs_copy(0, sl).wait()

    def ys_copy(b):
        sl = b % 2
        rows = pl.ds(pl.multiple_of(b * blk, blk), blk)
        return pltpu.make_async_copy(ybuf.at[sl], ys_hbm.at[rows], ysem.at[sl])

    def zero_copy(b):
        rows = pl.ds(pl.multiple_of(b * blk, blk), blk)
        return pltpu.make_async_copy(zbuf, ys_hbm.at[rows], zsem)

    def weight_copies(e, ws):
        return (pltpu.make_async_copy(wg_hbm.at[e], wgbuf.at[ws], wsem.at[ws, 0]),
                pltpu.make_async_copy(wu_hbm.at[e], wubuf.at[ws], wsem.at[ws, 1]),
                pltpu.make_async_copy(wd_hbm.at[e], wdbuf.at[ws], wsem.at[ws, 2]))

    for c in weight_copies(meta_ref[META_FIRST_USED, 0], 0):
        c.start(priority=BIG_DMA_PRIORITY)
    gather_start(0, 0)
    zbuf[...] = jnp.zeros_like(zbuf)
    lax.fori_loop(used, n_blocks, lambda b, _: (zero_copy(b).start(priority=BIG_DMA_PRIORITY), 0)[1], 0)

    def expert(e, _):
        nb = meta_ref[META_N_BLK, e]

        @pl.when(nb > 0)
        def _():
            b0 = meta_ref[META_FIRST_BLK, e]
            ws = meta_ref[META_ORDINAL, e] % 2
            nxt = meta_ref[META_NEXT_EXP, e]

            @pl.when(nxt >= 0)
            def _():
                for c in weight_copies(nxt, 1 - ws):
                    c.start(priority=BIG_DMA_PRIORITY)

            for c in weight_copies(e, ws):
                c.wait()

            def block(j, _):
                b = b0 + j
                sl = b % 2
                gather_start(b + 1, 1 - sl)
                gather_wait(sl)

                @pl.when(b >= 2)
                def _():
                    ys_copy(b - 2).wait()

                x = _load_slabs(xbuf.at[sl], BLOCK_ROWS, n_sl).astype(BF16)
                a = jnp.dot(x, wgbuf[ws].astype(BF16), preferred_element_type=F32)
                u = jnp.dot(x, wubuf[ws].astype(BF16), preferred_element_type=F32)
                act = (a * _sigmoid(a) * u).astype(BF16)
                y = jnp.dot(act, wdbuf[ws].astype(BF16), preferred_element_type=F32)
                _store_slabs(ybuf.at[sl], y)
                ys_copy(b).start(priority=BIG_DMA_PRIORITY)
                return 0

            lax.fori_loop(0, nb, block, 0)

        return 0

    lax.fori_loop(0, n_experts, expert, 0)

    gather_wait(used % 2)
    for back in (1, 2):
        @pl.when(used >= back)
        def _():
            ys_copy(used - back).wait()
    lax.fori_loop(used, n_blocks, lambda b, _: (zero_copy(b).wait(), 0)[1], 0)


def _experts(meta, xs_slabs, w_gate, w_up, w_down):
    n_e, d, de = w_gate.shape
    n_sl = _slab_rows(d)
    blk = BLOCK_ROWS * n_sl
    smem = pl.BlockSpec(memory_space=pltpu.SMEM)
    hbm = pl.BlockSpec(memory_space=pl.ANY)
    return pl.pallas_call(
        _experts_body,
        in_specs=[smem, hbm, hbm, hbm, hbm],
        out_specs=hbm,
        out_shape=jax.ShapeDtypeStruct(xs_slabs.shape, F32),
        scratch_shapes=[
            pltpu.VMEM((2, blk, LANES), F32),
            pltpu.VMEM((2, blk, LANES), F32),
            pltpu.VMEM((blk, LANES), F32),
            pltpu.VMEM((2, d, de), F32),
            pltpu.VMEM((2, d, de), F32),
            pltpu.VMEM((2, de, d), F32),
            pltpu.SemaphoreType.DMA((2,)),
            pltpu.SemaphoreType.DMA((2,)),
            pltpu.SemaphoreType.DMA(()),
            pltpu.SemaphoreType.DMA((2, 3)),
        ],
        compiler_params=pltpu.CompilerParams(vmem_limit_bytes=VMEM_LIMIT_BYTES),
        name="expert_mlp",
    )(meta, xs_slabs, w_gate, w_up, w_down)


def _combine_body(dest_ref, x2_ref, gcol_ref, fg_ref, ys_hbm, o_ref, ybuf, sem, *, tm):
    i = pl.program_id(0)
    n = pl.num_programs(0)
    s = n * tm
    slot = i % 2
    n_sl = _slab_rows(x2_ref.shape[1])

    def start(tile, sl):
        for k in range(TOP_K):
            for r in range(tm):
                row = dest_ref[k * s + tile * tm + r]
                src = ys_hbm.at[pl.ds(pl.multiple_of(row * n_sl, n_sl), n_sl)]
                pltpu.make_async_copy(src, ybuf.at[sl, k, pl.ds(r * n_sl, n_sl)],
                                      sem.at[sl, k]).start(priority=r % 2)

    def wait(sl):
        for k in range(TOP_K):
            _row_gather_wait(ys_hbm, ybuf.at[sl, k], sem.at[sl, k], tm * n_sl)

    @pl.when(i == 0)
    def _():
        start(0, 0)

    start(jnp.minimum(i + 1, n - 1), 1 - slot)
    wait(slot)
    g = gcol_ref[...]
    y0 = _load_slabs(ybuf.at[slot, 0], tm, n_sl)
    y1 = _load_slabs(ybuf.at[slot, 1], tm, n_sl)
    moe = g[:, 0:1] * y0 + g[:, 1:2] * y1
    x3 = x2_ref[...] + moe
    ms = jnp.mean(x3 * x3, axis=-1, keepdims=True)
    o_ref[...] = x3 * lax.rsqrt(ms + RMS_EPS) * fg_ref[...]

    @pl.when(i == n - 1)
    def _():
        wait(1 - slot)


def _combine(dest_flat, x2, gcol, final_g, ys, tm=128):
    s, d = x2.shape
    row = lambda i, dest_ref: (i, 0)
    const = lambda i, dest_ref: (0, 0)
    grid_spec = pltpu.PrefetchScalarGridSpec(
        num_scalar_prefetch=1,
        grid=(s // tm,),
        in_specs=[
            pl.BlockSpec((tm, d), row),
            pl.BlockSpec((tm, LANES), row),
            pl.BlockSpec((1, d), const),
            pl.BlockSpec(memory_space=pl.ANY),
        ],
        out_specs=pl.BlockSpec((tm, d), row),
        scratch_shapes=[
            pltpu.VMEM((2, TOP_K, tm * _slab_rows(d), LANES), F32),
            pltpu.SemaphoreType.DMA((2, TOP_K)),
        ],
    )
    return pl.pallas_call(
        functools.partial(_combine_body, tm=tm),
        grid_spec=grid_spec,
        out_shape=jax.ShapeDtypeStruct((s, d), F32),
        compiler_params=_cparams("arbitrary"),
        name="combine_norm",
    )(dest_flat, x2, gcol, final_g, ys)


def _layer(x, norm1_g, w_in, w_dw, b_dw, ln_g, ln_b, rel_bias, w_out, norm2_g,
           w_group, b_group, w_expert_router, b_expert_router, w_gate, w_up, w_down, out_g):
    s, d = x.shape
    dc = w_dw.shape[-1]
    da = (w_in.shape[1] - 2 * dc) // 3
    row2 = lambda a: a.reshape(1, -1)

    cv, cg, q, k, v = _in_proj(x, row2(norm1_g), w_in.astype(BF16), dc, da)
    conv_out = _conv_module(cv, cg, w_dw.reshape(CONV_WIDTH, dc), row2(b_dw), row2(ln_g), row2(ln_b))
    attn_out = _attention(q, k, v, _distance_rows(rel_bias))

    n_r = N_GROUPS + N_EXPERTS
    wr = jnp.pad(jnp.concatenate([w_group, w_expert_router], axis=1), ((0, 0), (0, LANES - n_r)))
    wr_hi = wr.astype(BF16)
    wr_lo = (wr - wr_hi.astype(F32)).astype(BF16)
    wr_cat = jnp.concatenate([wr_hi, wr_lo], axis=1)
    br_col = jnp.zeros((LANES, 1), F32).at[:n_r, 0].set(jnp.concatenate([b_group, b_expert_router]))
    x2, h2, eids, gcol = _out_proj_router(x, conv_out, attn_out, w_out.astype(BF16), row2(norm2_g),
                                          wr_cat, br_col)

    n_blocks = (s * TOP_K) // BLOCK_ROWS + N_EXPERTS
    assert n_blocks <= META_LANES
    dest, meta = _dispatch(eids)
    xs = _scatter_rows(dest, meta, h2, n_blocks * BLOCK_ROWS)
    ys = _experts(meta, xs, w_gate, w_up, w_down)
    return _combine(dest.reshape(-1), x2, gcol, row2(out_g), ys)


def kernel(x, norm1_g, w_in, w_dw, b_dw, ln_g, ln_b, rel_bias, w_out, norm2_g, w_group, b_group,
           w_expert_router, b_expert_router, w_gate, w_up, w_down, final_g):
    depth = norm1_g.shape[0]
    assert depth == 1, "the final norm is fused into the last layer's combine step"
    batch, s, d = x.shape
    assert batch == 1, "one sequence per call"
    out = _layer(x.reshape(s, d), norm1_g[0], w_in[0], w_dw[0], b_dw[0], ln_g[0], ln_b[0], rel_bias[0],
                 w_out[0], norm2_g[0], w_group[0], b_group[0], w_expert_router[0],
                 b_expert_router[0], w_gate[0], w_up[0], w_down[0], final_g)
    return out.reshape(batch, s, d)
```

```python
import functools

import jax
import jax.numpy as jnp
from jax import lax
from jax.experimental import pallas as pl
from jax.experimental.pallas import tpu as pltpu

F32 = jnp.float32
BF16 = jnp.bfloat16
I32 = jnp.int32

CHUNK = 64
HEAD_DIM = 64
LEFT_CHUNKS = 8
REL_CLIP = 128
CONV_WIDTH = 31
N_GROUPS = 8
EXPERTS_PER_GROUP = 8
N_EXPERTS = N_GROUPS * EXPERTS_PER_GROUP
TOP_K = 2
BLOCK_ROWS = 128
RMS_EPS = 1e-6
LN_EPS = 1e-5
NEG_INF = -1e30

VMEM_LIMIT_BYTES = 56 * 1024 * 1024
LANES = 128
SUBLANES = 8
PAIR = 2 * CHUNK
BAND_PAIR = (LEFT_CHUNKS + 2) * CHUNK
FRONT_PAD = LEFT_CHUNKS * CHUNK
CONV_HALO = 32


def _cparams(*sem):
    return pltpu.CompilerParams(dimension_semantics=sem, vmem_limit_bytes=VMEM_LIMIT_BYTES)


def _inproj_body(x_ref, g_ref, w_ref, cv_ref, cg_ref, q_ref, k_ref, v_ref):
    x = x_ref[...]
    ms = jnp.mean(x * x, axis=-1, keepdims=True)
    h = (x * lax.rsqrt(ms + RMS_EPS) * g_ref[...]).astype(BF16)
    dc = cv_ref.shape[1]
    da = q_ref.shape[1]

    def proj(lo, width):
        return jnp.dot(h, w_ref[:, lo:lo + width], preferred_element_type=F32)

    cv_ref[...] = proj(0, dc)
    cg_ref[...] = proj(dc, dc)
    q_ref[...] = (proj(2 * dc, da) * (HEAD_DIM ** -0.5)).astype(BF16)
    k_ref[...] = proj(2 * dc + da, da).astype(BF16)
    v_ref[...] = proj(2 * dc + 2 * da, da).astype(BF16)


def _in_proj(x, g, w_bf16, d_conv, d_att, tm=256):
    s, d = x.shape
    d_in = w_bf16.shape[1]
    row = lambda i: (i, 0)
    const = lambda i: (0, 0)
    return pl.pallas_call(
        _inproj_body,
        grid=(s // tm,),
        in_specs=[
            pl.BlockSpec((tm, d), row),
            pl.BlockSpec((1, d), const),
            pl.BlockSpec((d, d_in), const, pipeline_mode=pl.Buffered(1)),
        ],
        out_specs=[
            pl.BlockSpec((tm, d_conv), row),
            pl.BlockSpec((tm, d_conv), row),
            pl.BlockSpec((tm, d_att), row),
            pl.BlockSpec((tm, d_att), row),
            pl.BlockSpec((tm, d_att), row),
        ],
        out_shape=[
            jax.ShapeDtypeStruct((s, d_conv), F32),
            jax.ShapeDtypeStruct((s, d_conv), F32),
            jax.ShapeDtypeStruct((s, d_att), BF16),
            jax.ShapeDtypeStruct((s, d_att), BF16),
            jax.ShapeDtypeStruct((s, d_att), BF16),
        ],
        compiler_params=_cparams("arbitrary"),
        name="in_proj",
    )(x, g, w_bf16)


def _sigmoid(x):
    return 1.0 / (1.0 + jnp.exp(-x))


def _conv_body(cv_ref, cg_ref, cvh_ref, cgh_ref, w_ref, b_ref, lg_ref, lb_ref, o_ref,
               u_ref, us_ref, wb_ref, conv_ref, *, rows_per_chunk, groups_per_pass):
    i = pl.program_id(0)
    t, dc = cv_ref.shape

    @pl.when(i == 0)
    def _():
        for w in range(CONV_WIDTH):
            wb_ref[w] = jnp.broadcast_to(w_ref[w:w + 1, :], (SUBLANES, dc))

    halo = cvh_ref[...] * _sigmoid(cgh_ref[...])
    u_ref[0:CONV_HALO, :] = jnp.where(i > 0, halo, 0.0)
    u_ref[CONV_HALO:, :] = cv_ref[...] * _sigmoid(cg_ref[...])
    n_shift_rows = us_ref.shape[1]
    for sft in range(1, SUBLANES):
        us_ref[sft - 1] = u_ref[sft:sft + n_shift_rows, :]
    first_tap = CONV_HALO - (CONV_WIDTH - 1)

    def lane_tile(lt, _):
        lanes = pl.ds(pl.multiple_of(lt * LANES, LANES), LANES)
        wv = [wb_ref[w, :, lanes] for w in range(CONV_WIDTH)]
        bias = jnp.broadcast_to(b_ref[:, lanes], (SUBLANES, LANES))
        for g0 in range(0, t // SUBLANES, groups_per_pass):
            accs = [bias] * groups_per_pass
            for w in range(CONV_WIDTH):
                off = first_tap + w
                sft = off % SUBLANES
                for g in range(groups_per_pass):
                    lo = (g0 + g) * SUBLANES + off - sft
                    if sft == 0:
                        src = u_ref[lo:lo + SUBLANES, lanes]
                    else:
                        src = us_ref[sft - 1, lo:lo + SUBLANES, lanes]
                    accs[g] = accs[g] + wv[w] * src
            for g in range(groups_per_pass):
                lo = (g0 + g) * SUBLANES
                conv_ref[lo:lo + SUBLANES, lanes] = accs[g]
        return 0

    lax.fori_loop(0, dc // LANES, lane_tile, 0)

    r = rows_per_chunk
    for c in range(t // r):
        acc = conv_ref[c * r:(c + 1) * r, :]
        mu = jnp.mean(acc, axis=-1, keepdims=True)
        cen = acc - mu
        var = jnp.mean(cen * cen, axis=-1, keepdims=True)
        un = cen * lax.rsqrt(var + LN_EPS) * lg_ref[...] + lb_ref[...]
        o_ref[c * r:(c + 1) * r, :] = (un * _sigmoid(un)).astype(o_ref.dtype)


def _conv_module(cv, cg, w_dw, b_dw, ln_g, ln_b, t=256, rows_per_chunk=16, groups_per_pass=4):
    s, dc = cv.shape
    halo_blocks = t // CONV_HALO
    row = lambda i: (i, 0)
    prev = lambda i: (jnp.maximum(i * halo_blocks - 1, 0), 0)
    const = lambda i: (0, 0)
    return pl.pallas_call(
        functools.partial(_conv_body, rows_per_chunk=rows_per_chunk, groups_per_pass=groups_per_pass),
        grid=(s // t,),
        in_specs=[
            pl.BlockSpec((t, dc), row),
            pl.BlockSpec((t, dc), row),
            pl.BlockSpec((CONV_HALO, dc), prev),
            pl.BlockSpec((CONV_HALO, dc), prev),
            pl.BlockSpec((CONV_WIDTH, dc), const),
            pl.BlockSpec((1, dc), const),
            pl.BlockSpec((1, dc), const),
            pl.BlockSpec((1, dc), const),
        ],
        out_specs=pl.BlockSpec((t, dc), row),
        out_shape=jax.ShapeDtypeStruct((s, dc), BF16),
        scratch_shapes=[
            pltpu.VMEM((t + CONV_HALO, dc), F32),
            pltpu.VMEM((SUBLANES - 1, t + CONV_HALO - SUBLANES, dc), F32),
            pltpu.VMEM((CONV_WIDTH, SUBLANES, dc), F32),
            pltpu.VMEM((t, dc), F32),
        ],
        compiler_params=_cparams("arbitrary"),
        name="conv_module",
    )(cv, cg, cv, cg, w_dw, b_dw, ln_g, ln_b)


def _attn_pair(q2, kb, vb, bias_ref, col_lo):
    n_keys = kb.shape[0]
    lane = lax.broadcasted_iota(I32, q2.shape, 1)
    outs = []
    for h in range(2):
        in_head = (lane >= h * HEAD_DIM) & (lane < (h + 1) * HEAD_DIM)
        qh = jnp.where(in_head, q2, jnp.zeros_like(q2))
        sc = lax.dot_general(qh, kb, (((1,), (1,)), ((), ())), preferred_element_type=F32)
        sc = sc + bias_ref[h, :, col_lo:col_lo + n_keys]
        m = jnp.max(sc, axis=-1, keepdims=True)
        e = jnp.exp(sc - m)
        den = jnp.sum(e, axis=-1, keepdims=True)
        o = jnp.dot(e.astype(BF16), vb, preferred_element_type=F32)
        outs.append(o * (1.0 / den))
    return jnp.where(lane < HEAD_DIM, outs[0], outs[1])


def _attn_body(q_ref, k_ref, v_ref, g_ref, o_ref, bias_ref, *, pairs_per_step):
    st = pl.program_id(1)
    npp = pairs_per_step

    @pl.when(st == 0)
    def _():
        qi = lax.broadcasted_iota(I32, (PAIR, BAND_PAIR), 0)
        kj = lax.broadcasted_iota(I32, (PAIR, BAND_PAIR), 1)
        band_lo = jnp.where(qi >= CHUNK, CHUNK, 0)
        in_band = (kj >= band_lo) & (kj < band_lo + (LEFT_CHUNKS + 1) * CHUNK)
        for h in range(2):
            t = jnp.broadcast_to(g_ref[h], (PAIR, BAND_PAIR))
            shift = 1
            while shift < PAIR:
                t = jnp.where((qi & shift) != 0, pltpu.roll(t, shift, 1), t)
                shift *= 2
            bias_ref[h] = jnp.where(in_band, t, NEG_INF)

    def run(first_step):
        for u in range(npp):
            q2 = q_ref[u * PAIR:(u + 1) * PAIR, :]
            if first_step:
                n_keys = min((u + 1) * PAIR, BAND_PAIR)
                lo = max((u + 1) * PAIR - BAND_PAIR, 0)
                kb = k_ref[lo:lo + n_keys, :]
                vb = v_ref[lo:lo + n_keys, :]
                col_lo = BAND_PAIR - n_keys
            else:
                start = pl.multiple_of((st * npp + u) * PAIR - FRONT_PAD, PAIR)
                kb = k_ref[pl.ds(start, BAND_PAIR), :]
                vb = v_ref[pl.ds(start, BAND_PAIR), :]
                col_lo = 0
            o = _attn_pair(q2, kb, vb, bias_ref, col_lo)
            o_ref[u * PAIR:(u + 1) * PAIR, :] = o.astype(o_ref.dtype)

    @pl.when(st == 0)
    def _():
        run(True)

    @pl.when(st > 0)
    def _():
        run(False)


def _distance_rows(rel_table):
    t = rel_table.astype(F32)
    far = t[:, 2 * REL_CLIP:]
    n_far = LEFT_CHUNKS * CHUNK - REL_CLIP + 1
    near = jnp.flip(t[:, REL_CLIP - CHUNK + 1:2 * REL_CLIP], axis=1)
    tail = BAND_PAIR - n_far - near.shape[1]
    g = jnp.concatenate([jnp.repeat(far, n_far, axis=1), near, jnp.repeat(far, tail, axis=1)], axis=1)
    return g[:, None, :]


def _attention(q, k, v, g_rows, pairs_per_step=4):
    s, da = q.shape
    n_hp = da // (2 * HEAD_DIM)
    tq = pairs_per_step * PAIR
    assert pairs_per_step * PAIR >= FRONT_PAD, "first step must cover every clamped band"
    return pl.pallas_call(
        functools.partial(_attn_body, pairs_per_step=pairs_per_step),
        grid=(n_hp, s // tq),
        in_specs=[
            pl.BlockSpec((tq, 2 * HEAD_DIM), lambda hp, st: (st, hp)),
            pl.BlockSpec((s, 2 * HEAD_DIM), lambda hp, st: (0, hp)),
            pl.BlockSpec((s, 2 * HEAD_DIM), lambda hp, st: (0, hp)),
            pl.BlockSpec((2, 1, BAND_PAIR), lambda hp, st: (hp, 0, 0)),
        ],
        out_specs=pl.BlockSpec((tq, 2 * HEAD_DIM), lambda hp, st: (st, hp)),
        out_shape=jax.ShapeDtypeStruct((s, da), BF16),
        scratch_shapes=[pltpu.VMEM((2, PAIR, BAND_PAIR), F32)],
        compiler_params=_cparams("arbitrary", "arbitrary"),
        name="chunk_attention",
    )(q, k, v, g_rows)


def _slab_rows(d):
    return d // LANES


def _store_slabs(ref, val):
    t, d = val.shape
    n = _slab_rows(d)
    for c in range(n):
        ref[pl.ds(c, t, stride=n), :] = val[:, c * LANES:(c + 1) * LANES]


def _load_slabs(ref, t, n):
    return jnp.concatenate([ref[pl.ds(c, t, stride=n), :] for c in range(n)], axis=1)


def _first_index_of_max(vals, sub, n):
    vmax = jnp.max(vals, axis=0, keepdims=True)
    idx = jnp.min(jnp.where(vals == vmax, sub, n), axis=0, keepdims=True)
    return vmax, idx


def _outproj_body(x_ref, co_ref, ao_ref, w_ref, g2_ref, wr_ref, br_ref,
                  x2_ref, h2_ref, eid_ref, gcol_ref):
    dc = co_ref.shape[1]
    y = jnp.dot(co_ref[...], w_ref[0:dc, :], preferred_element_type=F32)
    y = y + jnp.dot(ao_ref[...], w_ref[dc:, :], preferred_element_type=F32)
    x2 = x_ref[...] + y
    ms = jnp.mean(x2 * x2, axis=-1, keepdims=True)
    h2 = x2 * lax.rsqrt(ms + RMS_EPS) * g2_ref[...]
    x2_ref[...] = x2
    _store_slabs(h2_ref, h2)

    hi = h2.astype(BF16)
    lo = (h2 - hi.astype(F32)).astype(BF16)
    pa = jnp.dot(hi, wr_ref[...], preferred_element_type=F32)
    pb = jnp.dot(lo, wr_ref[...], preferred_element_type=F32)
    logits = pa[:, :LANES] + pa[:, LANES:] + pb[:, :LANES]
    lt = logits.T + br_ref[...]
    tm = lt.shape[1]
    sub = lax.broadcasted_iota(I32, (N_GROUPS, tm), 0)
    g = lt[0:N_GROUPS, :]
    gmax, grp = _first_index_of_max(g, sub, N_GROUPS)
    p_grp = 1.0 / jnp.sum(jnp.exp(g - gmax), axis=0, keepdims=True)
    sel = jnp.zeros((EXPERTS_PER_GROUP, tm), F32)
    for gi in range(N_GROUPS):
        lo = N_GROUPS + gi * EXPERTS_PER_GROUP
        sel = jnp.where(grp == gi, lt[lo:lo + EXPERTS_PER_GROUP, :], sel)
    v1, i1 = _first_index_of_max(sel, sub, EXPERTS_PER_GROUP)
    rest = jnp.where(sub == i1, -jnp.inf, sel)
    v2, i2 = _first_index_of_max(rest, sub, EXPERTS_PER_GROUP)
    t = jnp.exp(v2 - v1)
    gate1 = p_grp / (1.0 + t)
    gate2 = p_grp * t / (1.0 + t)
    eid_ref[0:1, :] = grp * EXPERTS_PER_GROUP + i1
    eid_ref[1:2, :] = grp * EXPERTS_PER_GROUP + i2
    row = lax.broadcasted_iota(I32, (LANES, tm), 0)
    gl = jnp.where(row == 0, gate1, jnp.where(row == 1, gate2, 0.0))
    gcol_ref[...] = gl.T


def _out_proj_router(x, conv_out, attn_out, w_bf16, g2, wr_cat, br_col, tm=256):
    s, d = x.shape
    dc = conv_out.shape[1]
    da = attn_out.shape[1]
    row = lambda i: (i, 0)
    const = lambda i: (0, 0)
    return pl.pallas_call(
        _outproj_body,
        grid=(s // tm,),
        in_specs=[
            pl.BlockSpec((tm, d), row),
            pl.BlockSpec((tm, dc), row),
            pl.BlockSpec((tm, da), row),
            pl.BlockSpec((dc + da, d), const, pipeline_mode=pl.Buffered(1)),
            pl.BlockSpec((1, d), const),
            pl.BlockSpec((d, 2 * LANES), const),
            pl.BlockSpec((LANES, 1), const),
        ],
        out_specs=[
            pl.BlockSpec((tm, d), row),
            pl.BlockSpec((tm * _slab_rows(d), LANES), row),
            pl.BlockSpec((TOP_K, tm), lambda i: (0, i)),
            pl.BlockSpec((tm, LANES), row),
        ],
        out_shape=[
            jax.ShapeDtypeStruct((s, d), F32),
            jax.ShapeDtypeStruct((s * _slab_rows(d), LANES), F32),
            jax.ShapeDtypeStruct((TOP_K, s), I32),
            jax.ShapeDtypeStruct((s, LANES), F32),
        ],
        compiler_params=_cparams("arbitrary"),
        name="out_proj_router",
    )(x, conv_out, attn_out, w_bf16, g2, wr_cat, br_col)


META_LANES = 256
(META_FIRST_BLK, META_N_BLK, META_COUNT, META_NEXT_EXP, META_ORDINAL,
 META_USED_BLKS, META_FIRST_USED) = range(7)


def _dispatch_body(eid_ref, dest_ref, meta_ref, rank_ref):
    s = eid_ref.shape[1]
    w = META_LANES
    sub = lax.broadcasted_iota(I32, (N_EXPERTS, w), 0)
    lane = lax.broadcasted_iota(I32, (N_EXPERTS, w), 1)
    ri = lax.broadcasted_iota(I32, (w, w), 0)
    ci = lax.broadcasted_iota(I32, (w, w), 1)
    upper = (ri < ci).astype(BF16)

    def onehots(b):
        off = pl.multiple_of(b * w, w)
        e0 = eid_ref[0:1, pl.ds(off, w)]
        e1 = eid_ref[1:2, pl.ds(off, w)]
        return off, sub == e0, sub == e1

    def rank_block(b, carry):
        off, o0, o1 = onehots(b)
        both = jnp.where(o0 | o1, 1.0, 0.0)
        pre = jnp.dot(both.astype(BF16), upper, preferred_element_type=F32) + carry
        rank_ref[0:1, pl.ds(off, w)] = jnp.sum(jnp.where(o0, pre, 0.0), axis=0, keepdims=True)
        rank_ref[1:2, pl.ds(off, w)] = jnp.sum(jnp.where(o1, pre, 0.0), axis=0, keepdims=True)
        return carry + jnp.sum(both, axis=1, keepdims=True)

    counts = lax.fori_loop(0, s // w, rank_block, jnp.zeros((N_EXPERTS, w), F32))

    nblk = ((counts.astype(I32) + (BLOCK_ROWS - 1)) // BLOCK_ROWS).astype(F32)
    er = lax.broadcasted_iota(I32, (N_EXPERTS, N_EXPERTS), 0)
    ec = lax.broadcasted_iota(I32, (N_EXPERTS, N_EXPERTS), 1)
    lower_incl = (ec <= er).astype(BF16)
    pend_blk = jnp.dot(lower_incl, nblk.astype(BF16), preferred_element_type=F32)
    pstart = (pend_blk - nblk) * float(BLOCK_ROWS)

    def on_lanes(v):
        return jnp.sum(jnp.where(sub == lane, v, 0), axis=0, keepdims=True)

    nblk_i = nblk.astype(I32)
    first_blk = on_lanes((pend_blk - nblk).astype(I32))
    n_blk = on_lanes(nblk_i)
    count = on_lanes(counts.astype(I32))
    has_rows = nblk_i > 0
    next_exp = jnp.min(jnp.where(has_rows & (sub > lane), sub, N_EXPERTS), axis=0, keepdims=True)
    next_exp = jnp.where(next_exp < N_EXPERTS, next_exp, -1)
    ordinal = jnp.sum(jnp.where(has_rows & (sub < lane), 1, 0), axis=0, keepdims=True)
    first_used = jnp.min(jnp.where(has_rows, sub, N_EXPERTS - 1), axis=0, keepdims=True)
    used = pend_blk[N_EXPERTS - 1:N_EXPERTS, :].astype(I32)
    mrow = lax.broadcasted_iota(I32, (8, w), 0)
    meta = jnp.where(mrow == META_FIRST_BLK, first_blk, 0)
    meta = jnp.where(mrow == META_N_BLK, n_blk, meta)
    meta = jnp.where(mrow == META_COUNT, count, meta)
    meta = jnp.where(mrow == META_NEXT_EXP, next_exp, meta)
    meta = jnp.where(mrow == META_ORDINAL, ordinal, meta)
    meta = jnp.where(mrow == META_FIRST_USED, first_used, meta)
    meta_ref[...] = jnp.where(mrow == META_USED_BLKS, used, meta)

    def dest_block(b, _):
        off, o0, o1 = onehots(b)
        d0 = rank_ref[0:1, pl.ds(off, w)] + jnp.sum(jnp.where(o0, pstart, 0.0), axis=0, keepdims=True)
        d1 = rank_ref[1:2, pl.ds(off, w)] + jnp.sum(jnp.where(o1, pstart, 0.0), axis=0, keepdims=True)
        dest_ref[0:1, pl.ds(off, w)] = d0.astype(I32)
        dest_ref[1:2, pl.ds(off, w)] = d1.astype(I32)
        return 0

    lax.fori_loop(0, s // w, dest_block, 0)


def _dispatch(eids):
    s = eids.shape[1]
    return pl.pallas_call(
        _dispatch_body,
        out_shape=[
            jax.ShapeDtypeStruct((TOP_K, s), I32),
            jax.ShapeDtypeStruct((8, META_LANES), I32),
        ],
        scratch_shapes=[pltpu.VMEM((TOP_K, s), F32)],
        compiler_params=pltpu.CompilerParams(vmem_limit_bytes=VMEM_LIMIT_BYTES),
        name="dispatch",
    )(eids)


PAD_BITS = tuple(reversed(range(BLOCK_ROWS.bit_length() - 1)))


def _scatter_body(dest_ref, meta_ref, h2_ref, xs_hbm, zbuf, sem, zsem, *, tm):
    i = pl.program_id(0)
    n_sl = h2_ref.shape[0] // tm
    blk = BLOCK_ROWS * n_sl
    n_blocks = xs_hbm.shape[0] // blk
    used = meta_ref[META_USED_BLKS, 0]

    def slab(row, n_tokens=1):
        return pl.ds(pl.multiple_of(row * n_sl, n_sl), n_tokens * n_sl)

    def token_copy(j, k):
        t = i * tm + j
        return pltpu.make_async_copy(h2_ref.at[slab(j)], xs_hbm.at[slab(dest_ref[k, t])], sem.at[k])

    def scatter_tile():
        def one(j, _):
            for k in range(TOP_K):
                token_copy(j, k).start(priority=k)
            return 0

        lax.fori_loop(0, tm, one, 0, unroll=8)

    def wait_tile():
        for k in range(TOP_K):
            pltpu.make_async_copy(h2_ref, xs_hbm.at[slab(0, tm)], sem.at[k]).wait()

    @pl.when(i > 0)
    def _():
        scatter_tile()
        wait_tile()

    @pl.when(i == 0)
    def _():
        _scatter_first_step(meta_ref, xs_hbm, zbuf, zsem, slab, used, n_blocks, scatter_tile, wait_tile)


def _scatter_first_step(meta_ref, xs_hbm, zbuf, zsem, slab, used, n_blocks, scatter_tile, wait_tile):
    n_sl = zbuf.shape[0] // BLOCK_ROWS

    def zero_copy(row, n_tokens):
        return pltpu.make_async_copy(zbuf.at[pl.ds(0, n_tokens * n_sl)], xs_hbm.at[slab(row, n_tokens)], zsem)

    zbuf[...] = jnp.zeros_like(zbuf)

    def per_expert(e, issued):
        b0 = meta_ref[META_FIRST_BLK, e]
        nb = meta_ref[META_N_BLK, e]
        cnt = meta_ref[META_COUNT, e]
        pad = nb * BLOCK_ROWS - cnt
        row = b0 * BLOCK_ROWS + cnt
        out = []
        for bit, n_issued in zip(PAD_BITS, issued):
            size = 1 << bit
            take = pad & size

            @pl.when(take != 0)
            def _():
                zero_copy(row, size).start()

            row = row + take
            out.append(n_issued + take // size)
        return tuple(out)

    issued = lax.fori_loop(0, N_EXPERTS, per_expert, tuple(jnp.int32(0) for _ in PAD_BITS))
    lax.fori_loop(used, n_blocks, lambda b, _: (zero_copy(b * BLOCK_ROWS, BLOCK_ROWS).start(), 0)[1], 0)

    scatter_tile()
    wait_tile()

    for bit, n_issued in zip(PAD_BITS, issued):
        lax.fori_loop(0, n_issued, lambda i, _, n=1 << bit: (zero_copy(0, n).wait(), 0)[1], 0)
    lax.fori_loop(used, n_blocks, lambda b, _: (zero_copy(0, BLOCK_ROWS).wait(), 0)[1], 0)


def _scatter_rows(dest, meta, h2_slabs, rows, tm=512):
    s = dest.shape[1]
    n_sl = h2_slabs.shape[0] // s
    grid_spec = pltpu.PrefetchScalarGridSpec(
        num_scalar_prefetch=2,
        grid=(s // tm,),
        in_specs=[pl.BlockSpec((tm * n_sl, LANES), lambda i, dest_ref, meta_ref: (i, 0))],
        out_specs=pl.BlockSpec(memory_space=pl.ANY),
        scratch_shapes=[
            pltpu.VMEM((BLOCK_ROWS * n_sl, LANES), h2_slabs.dtype),
            pltpu.SemaphoreType.DMA((TOP_K,)),
            pltpu.SemaphoreType.DMA(()),
        ],
    )
    return pl.pallas_call(
        functools.partial(_scatter_body, tm=tm),
        grid_spec=grid_spec,
        out_shape=jax.ShapeDtypeStruct((rows * n_sl, LANES), h2_slabs.dtype),
        compiler_params=_cparams("arbitrary"),
        name="scatter_rows",
    )(dest, meta, h2_slabs)


def _row_gather_wait(src_hbm, dst, sem, n_rows):
    pltpu.make_async_copy(src_hbm.at[pl.ds(0, n_rows)], dst.at[pl.ds(0, n_rows)], sem).wait()


BIG_DMA_PRIORITY = 1


def _experts_body(meta_ref, xs_hbm, wg_hbm, wu_hbm, wd_hbm, ys_hbm,
                  xbuf, ybuf, zbuf, wgbuf, wubuf, wdbuf, gsem, ysem, zsem, wsem):
    used = meta_ref[META_USED_BLKS, 0]
    n_experts, d, _ = wg_hbm.shape
    n_sl = _slab_rows(d)
    blk = BLOCK_ROWS * n_sl
    n_blocks = ys_hbm.shape[0] // blk

    def xs_copy(b, sl):
        src_blk = jnp.minimum(b, n_blocks - 1)
        rows = pl.ds(pl.multiple_of(src_blk * blk, blk), blk)
        return pltpu.make_async_copy(xs_hbm.at[rows], xbuf.at[sl], gsem.at[sl])

    def gather_start(b, sl):
        xs_copy(b, sl).start()

    def gather_wait(sl):
        xs_copy(0, sl).wait()

    def ys_copy(b):
        sl = b % 2
        rows = pl.ds(pl.multiple_of(b * blk, blk), blk)
        return pltpu.make_async_copy(ybuf.at[sl], ys_hbm.at[rows], ysem.at[sl])

    def zero_copy(b):
        rows = pl.ds(pl.multiple_of(b * blk, blk), blk)
        return pltpu.make_async_copy(zbuf, ys_hbm.at[rows], zsem)

    def weight_copies(e, ws):
        return (pltpu.make_async_copy(wg_hbm.at[e], wgbuf.at[ws], wsem.at[ws, 0]),
                pltpu.make_async_copy(wu_hbm.at[e], wubuf.at[ws], wsem.at[ws, 1]),
                pltpu.make_async_copy(wd_hbm.at[e], wdbuf.at[ws], wsem.at[ws, 2]))

    for c in weight_copies(meta_ref[META_FIRST_USED, 0], 0):
        c.start(priority=BIG_DMA_PRIORITY)
    gather_start(0, 0)
    zbuf[...] = jnp.zeros_like(zbuf)
    lax.fori_loop(used, n_blocks, lambda b, _: (zero_copy(b).start(priority=BIG_DMA_PRIORITY), 0)[1], 0)

    def expert(e, _):
        nb = meta_ref[META_N_BLK, e]

        @pl.when(nb > 0)
        def _():
            b0 = meta_ref[META_FIRST_BLK, e]
            ws = meta_ref[META_ORDINAL, e] % 2
            nxt = meta_ref[META_NEXT_EXP, e]

            @pl.when(nxt >= 0)
            def _():
                for c in weight_copies(nxt, 1 - ws):
                    c.start(priority=BIG_DMA_PRIORITY)

            for c in weight_copies(e, ws):
                c.wait()

            def block(j, _):
                b = b0 + j
                sl = b % 2
                gather_start(b + 1, 1 - sl)
                gather_wait(sl)

                @pl.when(b >= 2)
                def _():
                    ys_copy(b - 2).wait()

                x = _load_slabs(xbuf.at[sl], BLOCK_ROWS, n_sl).astype(BF16)
                a = jnp.dot(x, wgbuf[ws].astype(BF16), preferred_element_type=F32)
                u = jnp.dot(x, wubuf[ws].astype(BF16), preferred_element_type=F32)
                act = (a * _sigmoid(a) * u).astype(BF16)
                y = jnp.dot(act, wdbuf[ws].astype(BF16), preferred_element_type=F32)
                _store_slabs(ybuf.at[sl], y)
                ys_copy(b).start(priority=BIG_DMA_PRIORITY)
                return 0

            lax.fori_loop(0, nb, block, 0)

        return 0

    lax.fori_loop(0, n_experts, expert, 0)

    gather_wait(used % 2)
    for back in (1, 2):
        @pl.when(used >= back)
        def _():
            ys_copy(used - back).wait()
    lax.fori_loop(used, n_blocks, lambda b, _: (zero_copy(b).wait(), 0)[1], 0)


def _experts(meta, xs_slabs, w_gate, w_up, w_down):
    n_e, d, de = w_gate.shape
    n_sl = _slab_rows(d)
    blk = BLOCK_ROWS * n_sl
    smem = pl.BlockSpec(memory_space=pltpu.SMEM)
    hbm = pl.BlockSpec(memory_space=pl.ANY)
    return pl.pallas_call(
        _experts_body,
        in_specs=[smem, hbm, hbm, hbm, hbm],
        out_specs=hbm,
        out_shape=jax.ShapeDtypeStruct(xs_slabs.shape, F32),
        scratch_shapes=[
            pltpu.VMEM((2, blk, LANES), F32),
            pltpu.VMEM((2, blk, LANES), F32),
            pltpu.VMEM((blk, LANES), F32),
            pltpu.VMEM((2, d, de), F32),
            pltpu.VMEM((2, d, de), F32),
            pltpu.VMEM((2, de, d), F32),
            pltpu.SemaphoreType.DMA((2,)),
            pltpu.SemaphoreType.DMA((2,)),
            pltpu.SemaphoreType.DMA(()),
            pltpu.SemaphoreType.DMA((2, 3)),
        ],
        compiler_params=pltpu.CompilerParams(vmem_limit_bytes=VMEM_LIMIT_BYTES),
        name="expert_mlp",
    )(meta, xs_slabs, w_gate, w_up, w_down)


def _combine_body(dest_ref, x2_ref, gcol_ref, fg_ref, ys_hbm, o_ref, ybuf, sem, *, tm):
    i = pl.program_id(0)
    n = pl.num_programs(0)
    s = n * tm
    slot = i % 2
    n_sl = _slab_rows(x2_ref.shape[1])

    def start(tile, sl):
        for k in range(TOP_K):
            for r in range(tm):
                row = dest_ref[k * s + tile * tm + r]
                src = ys_hbm.at[pl.ds(pl.multiple_of(row * n_sl, n_sl), n_sl)]
                pltpu.make_async_copy(src, ybuf.at[sl, k, pl.ds(r * n_sl, n_sl)],
                                      sem.at[sl, k]).start(priority=r % 2)

    def wait(sl):
        for k in range(TOP_K):
            _row_gather_wait(ys_hbm, ybuf.at[sl, k], sem.at[sl, k], tm * n_sl)

    @pl.when(i == 0)
    def _():
        start(0, 0)

    start(jnp.minimum(i + 1, n - 1), 1 - slot)
    wait(slot)
    g = gcol_ref[...]
    y0 = _load_slabs(ybuf.at[slot, 0], tm, n_sl)
    y1 = _load_slabs(ybuf.at[slot, 1], tm, n_sl)
    moe = g[:, 0:1] * y0 + g[:, 1:2] * y1
    x3 = x2_ref[...] + moe
    ms = jnp.mean(x3 * x3, axis=-1, keepdims=True)
    o_ref[...] = x3 * lax.rsqrt(ms + RMS_EPS) * fg_ref[...]

    @pl.when(i == n - 1)
    def _():
        wait(1 - slot)


def _combine(dest_flat, x2, gcol, final_g, ys, tm=128):
    s, d = x2.shape
    row = lambda i, dest_ref: (i, 0)
    const = lambda i, dest_ref: (0, 0)
    grid_spec = pltpu.PrefetchScalarGridSpec(
        num_scalar_prefetch=1,
        grid=(s // tm,),
        in_specs=[
            pl.BlockSpec((tm, d), row),
            pl.BlockSpec((tm, LANES), row),
            pl.BlockSpec((1, d), const),
            pl.BlockSpec(memory_space=pl.ANY),
        ],
        out_specs=pl.BlockSpec((tm, d), row),
        scratch_shapes=[
            pltpu.VMEM((2, TOP_K, tm * _slab_rows(d), LANES), F32),
            pltpu.SemaphoreType.DMA((2, TOP_K)),
        ],
    )
    return pl.pallas_call(
        functools.partial(_combine_body, tm=tm),
        grid_spec=grid_spec,
        out_shape=jax.ShapeDtypeStruct((s, d), F32),
        compiler_params=_cparams("arbitrary"),
        name="combine_norm",
    )(dest_flat, x2, gcol, final_g, ys)


def _layer(x, norm1_g, w_in, w_dw, b_dw, ln_g, ln_b, rel_bias, w_out, norm2_g,
           w_group, b_group, w_expert_router, b_expert_router, w_gate, w_up, w_down, out_g):
    s, d = x.shape
    dc = w_dw.shape[-1]
    da = (w_in.shape[1] - 2 * dc) // 3
    row2 = lambda a: a.reshape(1, -1)

    cv, cg, q, k, v = _in_proj(x, row2(norm1_g), w_in.astype(BF16), dc, da)
    conv_out = _conv_module(cv, cg, w_dw.reshape(CONV_WIDTH, dc), row2(b_dw), row2(ln_g), row2(ln_b))
    attn_out = _attention(q, k, v, _distance_rows(rel_bias))

    n_r = N_GROUPS + N_EXPERTS
    wr = jnp.pad(jnp.concatenate([w_group, w_expert_router], axis=1), ((0, 0), (0, LANES - n_r)))
    wr_hi = wr.astype(BF16)
    wr_lo = (wr - wr_hi.astype(F32)).astype(BF16)
    wr_cat = jnp.concatenate([wr_hi, wr_lo], axis=1)
    br_col = jnp.zeros((LANES, 1), F32).at[:n_r, 0].set(jnp.concatenate([b_group, b_expert_router]))
    x2, h2, eids, gcol = _out_proj_router(x, conv_out, attn_out, w_out.astype(BF16), row2(norm2_g),
                                          wr_cat, br_col)

    n_blocks = (s * TOP_K) // BLOCK_ROWS + N_EXPERTS
    assert n_blocks <= META_LANES
    dest, meta = _dispatch(eids)
    xs = _scatter_rows(dest, meta, h2, n_blocks * BLOCK_ROWS)
    ys = _experts(meta, xs, w_gate, w_up, w_down)
    return _combine(dest.reshape(-1), x2, gcol, row2(out_g), ys)


def kernel(x, norm1_g, w_in, w_dw, b_dw, ln_g, ln_b, rel_bias, w_out, norm2_g, w_group, b_group,
           w_expert_router, b_expert_router, w_gate, w_up, w_down, final_g):
    depth = norm1_g.shape[0]
    assert depth == 1, "the final norm is fused into the last layer's combine step"
    batch, s, d = x.shape
    assert batch == 1, "one sequence per call"
    out = _layer(x.reshape(s, d), norm1_g[0], w_in[0], w_dw[0], b_dw[0], ln_g[0], ln_b[0], rel_bias[0],
                 w_out[0], norm2_g[0], w_group[0], b_group[0], w_expert_router[0],
                 b_expert_router[0], w_gate[0], w_up[0], w_down[0], final_g)
    return out.reshape(batch, s, d)
```

```python
import functools

import jax
import jax.numpy as jnp
from jax import lax
from jax.experimental import pallas as pl
from jax.experimental.pallas import tpu as pltpu

F32 = jnp.float32
BF16 = jnp.bfloat16
I32 = jnp.int32

CHUNK = 64
HEAD_DIM = 64
LEFT_CHUNKS = 8
REL_CLIP = 128
CONV_WIDTH = 31
N_GROUPS = 8
EXPERTS_PER_GROUP = 8
N_EXPERTS = N_GROUPS * EXPERTS_PER_GROUP
TOP_K = 2
BLOCK_ROWS = 128
RMS_EPS = 1e-6
LN_EPS = 1e-5
NEG_INF = -1e30

VMEM_LIMIT_BYTES = 56 * 1024 * 1024
LANES = 128
SUBLANES = 8
PAIR = 2 * CHUNK
BAND_PAIR = (LEFT_CHUNKS + 2) * CHUNK
FRONT_PAD = LEFT_CHUNKS * CHUNK
CONV_HALO = 32


def _cparams(*sem):
    return pltpu.CompilerParams(dimension_semantics=sem, vmem_limit_bytes=VMEM_LIMIT_BYTES)


def _inproj_body(x_ref, g_ref, w_ref, cv_ref, cg_ref, q_ref, k_ref, v_ref):
    x = x_ref[...]
    ms = jnp.mean(x * x, axis=-1, keepdims=True)
    h = (x * lax.rsqrt(ms + RMS_EPS) * g_ref[...]).astype(BF16)
    dc = cv_ref.shape[1]
    da = q_ref.shape[1]

    def proj(lo, width):
        return jnp.dot(h, w_ref[:, lo:lo + width], preferred_element_type=F32)

    cv_ref[...] = proj(0, dc)
    cg_ref[...] = proj(dc, dc)
    q_ref[...] = (proj(2 * dc, da) * (HEAD_DIM ** -0.5)).astype(BF16)
    k_ref[...] = proj(2 * dc + da, da).astype(BF16)
    v_ref[...] = proj(2 * dc + 2 * da, da).astype(BF16)


def _in_proj(x, g, w_bf16, d_conv, d_att, tm=512):
    s, d = x.shape
    d_in = w_bf16.shape[1]
    row = lambda i: (i, 0)
    const = lambda i: (0, 0)
    return pl.pallas_call(
        _inproj_body,
        grid=(s // tm,),
        in_specs=[
            pl.BlockSpec((tm, d), row),
            pl.BlockSpec((1, d), const),
            pl.BlockSpec((d, d_in), const, pipeline_mode=pl.Buffered(1)),
        ],
        out_specs=[
            pl.BlockSpec((tm, d_conv), row),
            pl.BlockSpec((tm, d_conv), row),
            pl.BlockSpec((tm, d_att), row),
            pl.BlockSpec((tm, d_att), row),
            pl.BlockSpec((tm, d_att), row),
        ],
        out_shape=[
            jax.ShapeDtypeStruct((s, d_conv), F32),
            jax.ShapeDtypeStruct((s, d_conv), F32),
            jax.ShapeDtypeStruct((s, d_att), BF16),
            jax.ShapeDtypeStruct((s, d_att), BF16),
            jax.ShapeDtypeStruct((s, d_att), BF16),
        ],
        compiler_params=_cparams("arbitrary"),
        name="in_proj",
    )(x, g, w_bf16)


def _sigmoid(x):
    return 1.0 / (1.0 + jnp.exp(-x))


def _conv_body(cv_ref, cg_ref, cvh_ref, cgh_ref, w_ref, b_ref, lg_ref, lb_ref, o_ref,
               u_ref, us_ref, wb_ref, conv_ref, *, rows_per_chunk, groups_per_pass):
    i = pl.program_id(0)
    t, dc = cv_ref.shape

    @pl.when(i == 0)
    def _():
        for w in range(CONV_WIDTH):
            wb_ref[w] = jnp.broadcast_to(w_ref[w:w + 1, :], (SUBLANES, dc))

    halo = cvh_ref[...] * _sigmoid(cgh_ref[...])
    u_ref[0:CONV_HALO, :] = jnp.where(i > 0, halo, 0.0)
    u_ref[CONV_HALO:, :] = cv_ref[...] * _sigmoid(cg_ref[...])
    n_shift_rows = us_ref.shape[1]
    for sft in range(1, SUBLANES):
        us_ref[sft - 1] = u_ref[sft:sft + n_shift_rows, :]
    first_tap = CONV_HALO - (CONV_WIDTH - 1)

    def lane_tile(lt, _):
        lanes = pl.ds(pl.multiple_of(lt * LANES, LANES), LANES)
        wv = [wb_ref[w, :, lanes] for w in range(CONV_WIDTH)]
        bias = jnp.broadcast_to(b_ref[:, lanes], (SUBLANES, LANES))
        for g0 in range(0, t // SUBLANES, groups_per_pass):
            accs = [bias] * groups_per_pass
            for w in range(CONV_WIDTH):
                off = first_tap + w
                sft = off % SUBLANES
                for g in range(groups_per_pass):
                    lo = (g0 + g) * SUBLANES + off - sft
                    if sft == 0:
                        src = u_ref[lo:lo + SUBLANES, lanes]
                    else:
                        src = us_ref[sft - 1, lo:lo + SUBLANES, lanes]
                    accs[g] = accs[g] + wv[w] * src
            for g in range(groups_per_pass):
                lo = (g0 + g) * SUBLANES
                conv_ref[lo:lo + SUBLANES, lanes] = accs[g]
        return 0

    lax.fori_loop(0, dc // LANES, lane_tile, 0)

    r = rows_per_chunk
    for c in range(t // r):
        acc = conv_ref[c * r:(c + 1) * r, :]
        mu = jnp.mean(acc, axis=-1, keepdims=True)
        cen = acc - mu
        var = jnp.mean(cen * cen, axis=-1, keepdims=True)
        un = cen * lax.rsqrt(var + LN_EPS) * lg_ref[...] + lb_ref[...]
        o_ref[c * r:(c + 1) * r, :] = (un * _sigmoid(un)).astype(o_ref.dtype)


def _conv_module(cv, cg, w_dw, b_dw, ln_g, ln_b, t=256, rows_per_chunk=16, groups_per_pass=4):
    s, dc = cv.shape
    halo_blocks = t // CONV_HALO
    row = lambda i: (i, 0)
    prev = lambda i: (jnp.maximum(i * halo_blocks - 1, 0), 0)
    const = lambda i: (0, 0)
    return pl.pallas_call(
        functools.partial(_conv_body, rows_per_chunk=rows_per_chunk, groups_per_pass=groups_per_pass),
        grid=(s // t,),
        in_specs=[
            pl.BlockSpec((t, dc), row),
            pl.BlockSpec((t, dc), row),
            pl.BlockSpec((CONV_HALO, dc), prev),
            pl.BlockSpec((CONV_HALO, dc), prev),
            pl.BlockSpec((CONV_WIDTH, dc), const),
            pl.BlockSpec((1, dc), const),
            pl.BlockSpec((1, dc), const),
            pl.BlockSpec((1, dc), const),
        ],
        out_specs=pl.BlockSpec((t, dc), row),
        out_shape=jax.ShapeDtypeStruct((s, dc), BF16),
        scratch_shapes=[
            pltpu.VMEM((t + CONV_HALO, dc), F32),
            pltpu.VMEM((SUBLANES - 1, t + CONV_HALO - SUBLANES, dc), F32),
            pltpu.VMEM((CONV_WIDTH, SUBLANES, dc), F32),
            pltpu.VMEM((t, dc), F32),
        ],
        compiler_params=_cparams("arbitrary"),
        name="conv_module",
    )(cv, cg, cv, cg, w_dw, b_dw, ln_g, ln_b)


SOFTMAX_ROWS = 32


def _attn_pairs(q_ref, bands, bias_ref, s_ref, e_ref, o_ref):
    lane = lax.broadcasted_iota(I32, (PAIR, 2 * HEAD_DIM), 1)
    chains = [(u, h) for u in range(len(bands)) for h in range(2)]

    for c, (u, h) in enumerate(chains):
        kb, _, _ = bands[u]
        q2 = q_ref[u * PAIR:(u + 1) * PAIR, :]
        in_head = (lane >= h * HEAD_DIM) & (lane < (h + 1) * HEAD_DIM)
        qh = jnp.where(in_head, q2, jnp.zeros_like(q2))
        s_ref[c, :, :kb.shape[0]] = lax.dot_general(qh, kb, (((1,), (1,)), ((), ())),
                                                    preferred_element_type=F32)

    rdens = []
    for c, (u, h) in enumerate(chains):
        kb, _, col_lo = bands[u]
        n_keys = kb.shape[0]
        parts = []
        for r0 in range(0, PAIR, SOFTMAX_ROWS):
            rows = slice(r0, r0 + SOFTMAX_ROWS)
            t = s_ref[c, rows, :n_keys] + bias_ref[h, rows, col_lo:col_lo + n_keys]
            e = jnp.exp(t - jnp.max(t, axis=-1, keepdims=True))
            parts.append(1.0 / jnp.sum(e, axis=-1, keepdims=True))
            e_ref[c, rows, :n_keys] = e.astype(BF16)
        rdens.append(jnp.concatenate(parts, axis=0))

    for u in range(len(bands)):
        _, vb, _ = bands[u]
        n_keys = vb.shape[0]
        outs = []
        for h in range(2):
            c = 2 * u + h
            o = jnp.dot(e_ref[c, :, :n_keys], vb, preferred_element_type=F32)
            outs.append(o * rdens[c])
        o_ref[u * PAIR:(u + 1) * PAIR, :] = jnp.where(lane < HEAD_DIM, outs[0], outs[1]).astype(o_ref.dtype)


def _attn_body(q_ref, k_ref, v_ref, g_ref, o_ref, bias_ref, s_ref, e_ref, *, pairs_per_step):
    st = pl.program_id(1)
    npp = pairs_per_step

    @pl.when(st == 0)
    def _():
        sub = lax.broadcasted_iota(I32, (SUBLANES, BAND_PAIR), 0)
        kj = lax.broadcasted_iota(I32, (SUBLANES, BAND_PAIR), 1)
        for h in range(2):
            t8 = jnp.broadcast_to(g_ref[h], (SUBLANES, BAND_PAIR))
            shift = 1
            while shift < SUBLANES:
                t8 = jnp.where((sub & shift) != 0, pltpu.roll(t8, shift, 1), t8)
                shift *= 2
            for r0 in range(0, PAIR, SUBLANES):
                band_lo = (r0 // CHUNK) * CHUNK
                in_band = (kj >= band_lo) & (kj < band_lo + (LEFT_CHUNKS + 1) * CHUNK)
                rows = t8 if r0 == 0 else pltpu.roll(t8, r0, 1)
                bias_ref[h, r0:r0 + SUBLANES, :] = jnp.where(in_band, rows, NEG_INF)

    def run(first_step):
        bands = []
        for u in range(npp):
            if first_step:
                n_keys = min((u + 1) * PAIR, BAND_PAIR)
                lo = max((u + 1) * PAIR - BAND_PAIR, 0)
                bands.append((k_ref[lo:lo + n_keys, :], v_ref[lo:lo + n_keys, :], BAND_PAIR - n_keys))
            else:
                start = pl.multiple_of((st * npp + u) * PAIR - FRONT_PAD, PAIR)
                bands.append((k_ref[pl.ds(start, BAND_PAIR), :], v_ref[pl.ds(start, BAND_PAIR), :], 0))
        _attn_pairs(q_ref, bands, bias_ref, s_ref, e_ref, o_ref)

    @pl.when(st == 0)
    def _():
        run(True)

    @pl.when(st > 0)
    def _():
        run(False)


def _distance_rows(rel_table):
    t = rel_table.astype(F32)
    far = t[:, 2 * REL_CLIP:]
    n_far = LEFT_CHUNKS * CHUNK - REL_CLIP + 1
    near = jnp.flip(t[:, REL_CLIP - CHUNK + 1:2 * REL_CLIP], axis=1)
    tail = BAND_PAIR - n_far - near.shape[1]
    g = jnp.concatenate([jnp.repeat(far, n_far, axis=1), near, jnp.repeat(far, tail, axis=1)], axis=1)
    return g[:, None, :]


def _attention(q, k, v, g_rows, pairs_per_step=4):
    s, da = q.shape
    n_hp = da // (2 * HEAD_DIM)
    tq = pairs_per_step * PAIR
    assert pairs_per_step * PAIR >= FRONT_PAD, "first step must cover every clamped band"
    return pl.pallas_call(
        functools.partial(_attn_body, pairs_per_step=pairs_per_step),
        grid=(n_hp, s // tq),
        in_specs=[
            pl.BlockSpec((tq, 2 * HEAD_DIM), lambda hp, st: (st, hp)),
            pl.BlockSpec((s, 2 * HEAD_DIM), lambda hp, st: (0, hp)),
            pl.BlockSpec((s, 2 * HEAD_DIM), lambda hp, st: (0, hp)),
            pl.BlockSpec((2, 1, BAND_PAIR), lambda hp, st: (hp, 0, 0)),
        ],
        out_specs=pl.BlockSpec((tq, 2 * HEAD_DIM), lambda hp, st: (st, hp)),
        out_shape=jax.ShapeDtypeStruct((s, da), BF16),
        scratch_shapes=[
            pltpu.VMEM((2, PAIR, BAND_PAIR), F32),
            pltpu.VMEM((2 * pairs_per_step, PAIR, BAND_PAIR), F32),
            pltpu.VMEM((2 * pairs_per_step, PAIR, BAND_PAIR), BF16),
        ],
        compiler_params=_cparams("arbitrary", "arbitrary"),
        name="chunk_attention",
    )(q, k, v, g_rows)


def _slab_rows(d):
    return d // LANES


def _store_slabs(ref, val):
    t, d = val.shape
    n = _slab_rows(d)
    for c in range(n):
        ref[pl.ds(c, t, stride=n), :] = val[:, c * LANES:(c + 1) * LANES]


def _load_slabs(ref, t, n):
    return jnp.concatenate([ref[pl.ds(c, t, stride=n), :] for c in range(n)], axis=1)


def _first_index_of_max(vals, sub, n):
    vmax = jnp.max(vals, axis=0, keepdims=True)
    idx = jnp.min(jnp.where(vals == vmax, sub, n), axis=0, keepdims=True)
    return vmax, idx


def _outproj_body(x_ref, co_ref, ao_ref, w_ref, g2_ref, wr_ref, br_ref,
                  x2_ref, h2_ref, eid_ref, gcol_ref):
    dc = co_ref.shape[1]
    y = jnp.dot(co_ref[...], w_ref[0:dc, :], preferred_element_type=F32)
    y = y + jnp.dot(ao_ref[...], w_ref[dc:, :], preferred_element_type=F32)
    x2 = x_ref[...] + y
    ms = jnp.mean(x2 * x2, axis=-1, keepdims=True)
    h2 = x2 * lax.rsqrt(ms + RMS_EPS) * g2_ref[...]
    x2_ref[...] = x2
    _store_slabs(h2_ref, h2)

    hi = h2.astype(BF16)
    lo = (h2 - hi.astype(F32)).astype(BF16)
    pa = jnp.dot(hi, wr_ref[...], preferred_element_type=F32)
    pb = jnp.dot(lo, wr_ref[...], preferred_element_type=F32)
    logits = pa[:, :LANES] + pa[:, LANES:] + pb[:, :LANES]
    lt = logits.T + br_ref[...]
    tm = lt.shape[1]
    sub = lax.broadcasted_iota(I32, (N_GROUPS, tm), 0)
    g = lt[0:N_GROUPS, :]
    gmax, grp = _first_index_of_max(g, sub, N_GROUPS)
    p_grp = 1.0 / jnp.sum(jnp.exp(g - gmax), axis=0, keepdims=True)
    sel = jnp.zeros((EXPERTS_PER_GROUP, tm), F32)
    for gi in range(N_GROUPS):
        lo = N_GROUPS + gi * EXPERTS_PER_GROUP
        sel = jnp.where(grp == gi, lt[lo:lo + EXPERTS_PER_GROUP, :], sel)
    v1, i1 = _first_index_of_max(sel, sub, EXPERTS_PER_GROUP)
    rest = jnp.where(sub == i1, -jnp.inf, sel)
    v2, i2 = _first_index_of_max(rest, sub, EXPERTS_PER_GROUP)
    t = jnp.exp(v2 - v1)
    gate1 = p_grp / (1.0 + t)
    gate2 = p_grp * t / (1.0 + t)
    eid_ref[0:1, :] = grp * EXPERTS_PER_GROUP + i1
    eid_ref[1:2, :] = grp * EXPERTS_PER_GROUP + i2
    row = lax.broadcasted_iota(I32, (LANES, tm), 0)
    gl = jnp.where(row == 0, gate1, jnp.where(row == 1, gate2, 0.0))
    gcol_ref[...] = gl.T


def _out_proj_router(x, conv_out, attn_out, w_bf16, g2, wr_cat, br_col, tm=256):
    s, d = x.shape
    dc = conv_out.shape[1]
    da = attn_out.shape[1]
    row = lambda i: (i, 0)
    const = lambda i: (0, 0)
    return pl.pallas_call(
        _outproj_body,
        grid=(s // tm,),
        in_specs=[
            pl.BlockSpec((tm, d), row),
            pl.BlockSpec((tm, dc), row),
            pl.BlockSpec((tm, da), row),
            pl.BlockSpec((dc + da, d), const, pipeline_mode=pl.Buffered(1)),
            pl.BlockSpec((1, d), const),
            pl.BlockSpec((d, 2 * LANES), const),
            pl.BlockSpec((LANES, 1), const),
        ],
        out_specs=[
            pl.BlockSpec((tm, d), row),
            pl.BlockSpec((tm * _slab_rows(d), LANES), row),
            pl.BlockSpec((TOP_K, tm), lambda i: (0, i)),
            pl.BlockSpec((tm, LANES), row),
        ],
        out_shape=[
            jax.ShapeDtypeStruct((s, d), F32),
            jax.ShapeDtypeStruct((s * _slab_rows(d), LANES), F32),
            jax.ShapeDtypeStruct((TOP_K, s), I32),
            jax.ShapeDtypeStruct((s, LANES), F32),
        ],
        compiler_params=_cparams("arbitrary"),
        name="out_proj_router",
    )(x, conv_out, attn_out, w_bf16, g2, wr_cat, br_col)


META_LANES = 256
(META_FIRST_BLK, META_N_BLK, META_COUNT, META_NEXT_EXP, META_ORDINAL,
 META_USED_BLKS, META_FIRST_USED) = range(7)


def _dispatch_body(eid_ref, dest_ref, meta_ref, rank_ref):
    s = eid_ref.shape[1]
    w = META_LANES
    sub = lax.broadcasted_iota(I32, (N_EXPERTS, w), 0)
    lane = lax.broadcasted_iota(I32, (N_EXPERTS, w), 1)
    ri = lax.broadcasted_iota(I32, (w, w), 0)
    ci = lax.broadcasted_iota(I32, (w, w), 1)
    upper = (ri < ci).astype(BF16)

    def onehots(b):
        off = pl.multiple_of(b * w, w)
        e0 = eid_ref[0:1, pl.ds(off, w)]
        e1 = eid_ref[1:2, pl.ds(off, w)]
        return off, sub == e0, sub == e1

    def rank_block(b, carry):
        off, o0, o1 = onehots(b)
        both = jnp.where(o0 | o1, 1.0, 0.0)
        pre = jnp.dot(both.astype(BF16), upper, preferred_element_type=F32) + carry
        rank_ref[0:1, pl.ds(off, w)] = jnp.sum(jnp.where(o0, pre, 0.0), axis=0, keepdims=True)
        rank_ref[1:2, pl.ds(off, w)] = jnp.sum(jnp.where(o1, pre, 0.0), axis=0, keepdims=True)
        return carry + jnp.sum(both, axis=1, keepdims=True)

    counts = lax.fori_loop(0, s // w, rank_block, jnp.zeros((N_EXPERTS, w), F32))

    nblk = ((counts.astype(I32) + (BLOCK_ROWS - 1)) // BLOCK_ROWS).astype(F32)
    er = lax.broadcasted_iota(I32, (N_EXPERTS, N_EXPERTS), 0)
    ec = lax.broadcasted_iota(I32, (N_EXPERTS, N_EXPERTS), 1)
    lower_incl = (ec <= er).astype(BF16)
    pend_blk = jnp.dot(lower_incl, nblk.astype(BF16), preferred_element_type=F32)
    pstart = (pend_blk - nblk) * float(BLOCK_ROWS)

    def on_lanes(v):
        return jnp.sum(jnp.where(sub == lane, v, 0), axis=0, keepdims=True)

    nblk_i = nblk.astype(I32)
    first_blk = on_lanes((pend_blk - nblk).astype(I32))
    n_blk = on_lanes(nblk_i)
    count = on_lanes(counts.astype(I32))
    has_rows = nblk_i > 0
    next_exp = jnp.min(jnp.where(has_rows & (sub > lane), sub, N_EXPERTS), axis=0, keepdims=True)
    next_exp = jnp.where(next_exp < N_EXPERTS, next_exp, -1)
    ordinal = jnp.sum(jnp.where(has_rows & (sub < lane), 1, 0), axis=0, keepdims=True)
    first_used = jnp.min(jnp.where(has_rows, sub, N_EXPERTS - 1), axis=0, keepdims=True)
    used = pend_blk[N_EXPERTS - 1:N_EXPERTS, :].astype(I32)
    mrow = lax.broadcasted_iota(I32, (8, w), 0)
    meta = jnp.where(mrow == META_FIRST_BLK, first_blk, 0)
    meta = jnp.where(mrow == META_N_BLK, n_blk, meta)
    meta = jnp.where(mrow == META_COUNT, count, meta)
    meta = jnp.where(mrow == META_NEXT_EXP, next_exp, meta)
    meta = jnp.where(mrow == META_ORDINAL, ordinal, meta)
    meta = jnp.where(mrow == META_FIRST_USED, first_used, meta)
    meta_ref[...] = jnp.where(mrow == META_USED_BLKS, used, meta)

    def dest_block(b, _):
        off, o0, o1 = onehots(b)
        d0 = rank_ref[0:1, pl.ds(off, w)] + jnp.sum(jnp.where(o0, pstart, 0.0), axis=0, keepdims=True)
        d1 = rank_ref[1:2, pl.ds(off, w)] + jnp.sum(jnp.where(o1, pstart, 0.0), axis=0, keepdims=True)
        dest_ref[0:1, pl.ds(off, w)] = d0.astype(I32)
        dest_ref[1:2, pl.ds(off, w)] = d1.astype(I32)
        return 0

    lax.fori_loop(0, s // w, dest_block, 0)


def _dispatch(eids):
    s = eids.shape[1]
    return pl.pallas_call(
        _dispatch_body,
        out_shape=[
            jax.ShapeDtypeStruct((TOP_K, s), I32),
            jax.ShapeDtypeStruct((8, META_LANES), I32),
        ],
        scratch_shapes=[pltpu.VMEM((TOP_K, s), F32)],
        compiler_params=pltpu.CompilerParams(vmem_limit_bytes=VMEM_LIMIT_BYTES),
        name="dispatch",
    )(eids)


PAD_BITS = tuple(reversed(range(BLOCK_ROWS.bit_length() - 1)))


def _scatter_body(dest_ref, meta_ref, h2_ref, xs_hbm, zbuf, sem, zsem, *, tm):
    i = pl.program_id(0)
    n_sl = h2_ref.shape[0] // tm
    blk = BLOCK_ROWS * n_sl
    n_blocks = xs_hbm.shape[0] // blk
    used = meta_ref[META_USED_BLKS, 0]

    def slab(row, n_tokens=1):
        return pl.ds(pl.multiple_of(row * n_sl, n_sl), n_tokens * n_sl)

    def token_copy(j, k):
        t = i * tm + j
        return pltpu.make_async_copy(h2_ref.at[slab(j)], xs_hbm.at[slab(dest_ref[k, t])], sem.at[k])

    def scatter_tile():
        def one(j, _):
            for k in range(TOP_K):
                token_copy(j, k).start(priority=k)
            return 0

        lax.fori_loop(0, tm, one, 0, unroll=8)

    def wait_tile():
        for k in range(TOP_K):
            pltpu.make_async_copy(h2_ref, xs_hbm.at[slab(0, tm)], sem.at[k]).wait()

    @pl.when(i > 0)
    def _():
        scatter_tile()
        wait_tile()

    @pl.when(i == 0)
    def _():
        _scatter_first_step(meta_ref, xs_hbm, zbuf, zsem, slab, used, n_blocks, scatter_tile, wait_tile)


def _scatter_first_step(meta_ref, xs_hbm, zbuf, zsem, slab, used, n_blocks, scatter_tile, wait_tile):
    n_sl = zbuf.shape[0] // BLOCK_ROWS

    def zero_copy(row, n_tokens):
        return pltpu.make_async_copy(zbuf.at[pl.ds(0, n_tokens * n_sl)], xs_hbm.at[slab(row, n_tokens)], zsem)

    zbuf[...] = jnp.zeros_like(zbuf)

    def per_expert(e, issued):
        b0 = meta_ref[META_FIRST_BLK, e]
        nb = meta_ref[META_N_BLK, e]
        cnt = meta_ref[META_COUNT, e]
        pad = nb * BLOCK_ROWS - cnt
        row = b0 * BLOCK_ROWS + cnt
        out = []
        for bit, n_issued in zip(PAD_BITS, issued):
            size = 1 << bit
            take = pad & size

            @pl.when(take != 0)
            def _():
                zero_copy(row, size).start()

            row = row + take
            out.append(n_issued + take // size)
        return tuple(out)

    issued = lax.fori_loop(0, N_EXPERTS, per_expert, tuple(jnp.int32(0) for _ in PAD_BITS))
    lax.fori_loop(used, n_blocks, lambda b, _: (zero_copy(b * BLOCK_ROWS, BLOCK_ROWS).start(), 0)[1], 0)

    scatter_tile()
    wait_tile()

    for bit, n_issued in zip(PAD_BITS, issued):
        lax.fori_loop(0, n_issued, lambda i, _, n=1 << bit: (zero_copy(0, n).wait(), 0)[1], 0)
    lax.fori_loop(used, n_blocks, lambda b, _: (zero_copy(0, BLOCK_ROWS).wait(), 0)[1], 0)


def _scatter_rows(dest, meta, h2_slabs, rows, tm=512):
    s = dest.shape[1]
    n_sl = h2_slabs.shape[0] // s
    grid_spec = pltpu.PrefetchScalarGridSpec(
        num_scalar_prefetch=2,
        grid=(s // tm,),
        in_specs=[pl.BlockSpec((tm * n_sl, LANES), lambda i, dest_ref, meta_ref: (i, 0))],
        out_specs=pl.BlockSpec(memory_space=pl.ANY),
        scratch_shapes=[
            pltpu.VMEM((BLOCK_ROWS * n_sl, LANES), h2_slabs.dtype),
            pltpu.SemaphoreType.DMA((TOP_K,)),
            pltpu.SemaphoreType.DMA(()),
        ],
    )
    return pl.pallas_call(
        functools.partial(_scatter_body, tm=tm),
        grid_spec=grid_spec,
        out_shape=jax.ShapeDtypeStruct((rows * n_sl, LANES), h2_slabs.dtype),
        compiler_params=_cparams("arbitrary"),
        name="scatter_rows",
    )(dest, meta, h2_slabs)


def _row_gather_wait(src_hbm, dst, sem, n_rows):
    pltpu.make_async_copy(src_hbm.at[pl.ds(0, n_rows)], dst.at[pl.ds(0, n_rows)], sem).wait()


BIG_DMA_PRIORITY = 1


def _experts_body(meta_ref, xs_hbm, wg_hbm, wu_hbm, wd_hbm, ys_hbm,
                  xbuf, ybuf, zbuf, wgbuf, wubuf, wdbuf, gsem, ysem, zsem, wsem):
    used = meta_ref[META_USED_BLKS, 0]
    n_experts, d, _ = wg_hbm.shape
    n_sl = _slab_rows(d)
    blk = BLOCK_ROWS * n_sl
    n_blocks = ys_hbm.shape[0] // blk

    def xs_copy(b, sl):
        src_blk = jnp.minimum(b, n_blocks - 1)
        rows = pl.ds(pl.multiple_of(src_blk * blk, blk), blk)
        return pltpu.make_async_copy(xs_hbm.at[rows], xbuf.at[sl], gsem.at[sl])

    def gather_start(b, sl):
        xs_copy(b, sl).start()

    def gather_wait(sl):
        xs_copy(0, sl).wait()

    def ys_copy(b):
        sl = b % 2
        rows = pl.ds(pl.multiple_of(b * blk, blk), blk)
        return pltpu.make_async_copy(ybuf.at[sl], ys_hbm.at[rows], ysem.at[sl])

    def zero_copy(b):
        rows = pl.ds(pl.multiple_of(b * blk, blk), blk)
        return pltpu.make_async_copy(zbuf, ys_hbm.at[rows], zsem)

    def weight_copies(e, ws):
        return (pltpu.make_async_copy(wg_hbm.at[e], wgbuf.at[ws], wsem.at[ws, 0]),
                pltpu.make_async_copy(wu_hbm.at[e], wubuf.at[ws], wsem.at[ws, 1]),
                pltpu.make_async_copy(wd_hbm.at[e], wdbuf.at[ws], wsem.at[ws, 2]))

    for c in weight_copies(meta_ref[META_FIRST_USED, 0], 0):
        c.start(priority=BIG_DMA_PRIORITY)
    gather_start(0, 0)
    zbuf[...] = jnp.zeros_like(zbuf)
    lax.fori_loop(used, n_blocks, lambda b, _: (zero_copy(b).start(priority=BIG_DMA_PRIORITY), 0)[1], 0)

    def expert(e, _):
        nb = meta_ref[META_N_BLK, e]

        @pl.when(nb > 0)
        def _():
            b0 = meta_ref[META_FIRST_BLK, e]
            ws = meta_ref[META_ORDINAL, e] % 2
            nxt = meta_ref[META_NEXT_EXP, e]

            @pl.when(nxt >= 0)
            def _():
                for c in weight_copies(nxt, 1 - ws):
                    c.start(priority=BIG_DMA_PRIORITY)

            for c in weight_copies(e, ws):
                c.wait()

            def block(j, _):
                b = b0 + j
                sl = b % 2
                gather_start(b + 1, 1 - sl)
                gather_wait(sl)

                @pl.when(b >= 2)
                def _():
                    ys_copy(b - 2).wait()

                x = _load_slabs(xbuf.at[sl], BLOCK_ROWS, n_sl).astype(BF16)
                a = jnp.dot(x, wgbuf[ws].astype(BF16), preferred_element_type=F32)
                u = jnp.dot(x, wubuf[ws].astype(BF16), preferred_element_type=F32)
                act = (a * _sigmoid(a) * u).astype(BF16)
                y = jnp.dot(act, wdbuf[ws].astype(BF16), preferred_element_type=F32)
                _store_slabs(ybuf.at[sl], y)
                ys_copy(b).start(priority=BIG_DMA_PRIORITY)
                return 0

            lax.fori_loop(0, nb, block, 0)

        return 0

    lax.fori_loop(0, n_experts, expert, 0)

    gather_wait(used % 2)
    for back in (1, 2):
        @pl.when(used >= back)
        def _():
            ys_copy(used - back).wait()
    lax.fori_loop(used, n_blocks, lambda b, _: (zero_copy(b).wait(), 0)[1], 0)


def _experts(meta, xs_slabs, w_gate, w_up, w_down):
    n_e, d, de = w_gate.shape
    n_sl = _slab_rows(d)
    blk = BLOCK_ROWS * n_sl
    smem = pl.BlockSpec(memory_space=pltpu.SMEM)
    hbm = pl.BlockSpec(memory_space=pl.ANY)
    return pl.pallas_call(
        _experts_body,
        in_specs=[smem, hbm, hbm, hbm, hbm],
        out_specs=hbm,
        out_shape=jax.ShapeDtypeStruct(xs_slabs.shape, F32),
        scratch_shapes=[
            pltpu.VMEM((2, blk, LANES), F32),
            pltpu.VMEM((2, blk, LANES), F32),
            pltpu.VMEM((blk, LANES), F32),
            pltpu.VMEM((2, d, de), F32),
            pltpu.VMEM((2, d, de), F32),
            pltpu.VMEM((2, de, d), F32),
            pltpu.SemaphoreType.DMA((2,)),
            pltpu.SemaphoreType.DMA((2,)),
            pltpu.SemaphoreType.DMA(()),
            pltpu.SemaphoreType.DMA((2, 3)),
        ],
        compiler_params=pltpu.CompilerParams(vmem_limit_bytes=VMEM_LIMIT_BYTES),
        name="expert_mlp",
    )(meta, xs_slabs, w_gate, w_up, w_down)


def _combine_body(dest_ref, x2_ref, gcol_ref, fg_ref, ys_hbm, o_ref, ybuf, sem, *, tm):
    i = pl.program_id(0)
    n = pl.num_programs(0)
    s = n * tm
    slot = i % 2
    n_sl = _slab_rows(x2_ref.shape[1])

    def start(tile, sl):
        for k in range(TOP_K):
            for r in range(tm):
                row = dest_ref[k * s + tile * tm + r]
                src = ys_hbm.at[pl.ds(pl.multiple_of(row * n_sl, n_sl), n_sl)]
                pltpu.make_async_copy(src, ybuf.at[sl, k, pl.ds(r * n_sl, n_sl)],
                                      sem.at[sl, k]).start(priority=r % 2)

    def wait(sl):
        for k in range(TOP_K):
            _row_gather_wait(ys_hbm, ybuf.at[sl, k], sem.at[sl, k], tm * n_sl)

    @pl.when(i == 0)
    def _():
        start(0, 0)

    start(jnp.minimum(i + 1, n - 1), 1 - slot)
    wait(slot)
    g = gcol_ref[...]
    y0 = _load_slabs(ybuf.at[slot, 0], tm, n_sl)
    y1 = _load_slabs(ybuf.at[slot, 1], tm, n_sl)
    moe = g[:, 0:1] * y0 + g[:, 1:2] * y1
    x3 = x2_ref[...] + moe
    ms = jnp.mean(x3 * x3, axis=-1, keepdims=True)
    o_ref[...] = x3 * lax.rsqrt(ms + RMS_EPS) * fg_ref[...]

    @pl.when(i == n - 1)
    def _():
        wait(1 - slot)


def _combine(dest_flat, x2, gcol, final_g, ys, tm=128):
    s, d = x2.shape
    row = lambda i, dest_ref: (i, 0)
    const = lambda i, dest_ref: (0, 0)
    grid_spec = pltpu.PrefetchScalarGridSpec(
        num_scalar_prefetch=1,
        grid=(s // tm,),
        in_specs=[
            pl.BlockSpec((tm, d), row),
            pl.BlockSpec((tm, LANES), row),
            pl.BlockSpec((1, d), const),
            pl.BlockSpec(memory_space=pl.ANY),
        ],
        out_specs=pl.BlockSpec((tm, d), row),
        scratch_shapes=[
            pltpu.VMEM((2, TOP_K, tm * _slab_rows(d), LANES), F32),
            pltpu.SemaphoreType.DMA((2, TOP_K)),
        ],
    )
    return pl.pallas_call(
        functools.partial(_combine_body, tm=tm),
        grid_spec=grid_spec,
        out_shape=jax.ShapeDtypeStruct((s, d), F32),
        compiler_params=_cparams("arbitrary"),
        name="combine_norm",
    )(dest_flat, x2, gcol, final_g, ys)


def _layer(x, norm1_g, w_in, w_dw, b_dw, ln_g, ln_b, rel_bias, w_out, norm2_g,
           w_group, b_group, w_expert_router, b_expert_router, w_gate, w_up, w_down, out_g):
    s, d = x.shape
    dc = w_dw.shape[-1]
    da = (w_in.shape[1] - 2 * dc) // 3
    row2 = lambda a: a.reshape(1, -1)

    cv, cg, q, k, v = _in_proj(x, row2(norm1_g), w_in.astype(BF16), dc, da)
    conv_out = _conv_module(cv, cg, w_dw.reshape(CONV_WIDTH, dc), row2(b_dw), row2(ln_g), row2(ln_b))
    attn_out = _attention(q, k, v, _distance_rows(rel_bias))

    n_r = N_GROUPS + N_EXPERTS
    wr = jnp.pad(jnp.concatenate([w_group, w_expert_router], axis=1), ((0, 0), (0, LANES - n_r)))
    wr_hi = wr.astype(BF16)
    wr_lo = (wr - wr_hi.astype(F32)).astype(BF16)
    wr_cat = jnp.concatenate([wr_hi, wr_lo], axis=1)
    br_col = jnp.zeros((LANES, 1), F32).at[:n_r, 0].set(jnp.concatenate([b_group, b_expert_router]))
    x2, h2, eids, gcol = _out_proj_router(x, conv_out, attn_out, w_out.astype(BF16), row2(norm2_g),
                                          wr_cat, br_col)

    n_blocks = (s * TOP_K) // BLOCK_ROWS + N_EXPERTS
    assert n_blocks <= META_LANES
    dest, meta = _dispatch(eids)
    xs = _scatter_rows(dest, meta, h2, n_blocks * BLOCK_ROWS)
    ys = _experts(meta, xs, w_gate, w_up, w_down)
    return _combine(dest.reshape(-1), x2, gcol, row2(out_g), ys)


def kernel(x, norm1_g, w_in, w_dw, b_dw, ln_g, ln_b, rel_bias, w_out, norm2_g, w_group, b_group,
           w_expert_router, b_expert_router, w_gate, w_up, w_down, final_g):
    depth = norm1_g.shape[0]
    assert depth == 1, "the final norm is fused into the last layer's combine step"
    batch, s, d = x.shape
    assert batch == 1, "one sequence per call"
    out = _layer(x.reshape(s, d), norm1_g[0], w_in[0], w_dw[0], b_dw[0], ln_g[0], ln_b[0], rel_bias[0],
                 w_out[0], norm2_g[0], w_group[0], b_group[0], w_expert_router[0],
                 b_expert_router[0], w_gate[0], w_up[0], w_down[0], final_g)
    return out.reshape(batch, s, d)
```

```python
import functools

import jax
import jax.numpy as jnp
from jax import lax
from jax.experimental import pallas as pl
from jax.experimental.pallas import tpu as pltpu

F32 = jnp.float32
BF16 = jnp.bfloat16
I32 = jnp.int32

CHUNK = 64
HEAD_DIM = 64
LEFT_CHUNKS = 8
REL_CLIP = 128
CONV_WIDTH = 31
N_GROUPS = 8
EXPERTS_PER_GROUP = 8
N_EXPERTS = N_GROUPS * EXPERTS_PER_GROUP
TOP_K = 2
BLOCK_ROWS = 128
RMS_EPS = 1e-6
LN_EPS = 1e-5
NEG_INF = -1e30

VMEM_LIMIT_BYTES = 56 * 1024 * 1024
LANES = 128
SUBLANES = 8
PAIR = 2 * CHUNK
BAND_PAIR = (LEFT_CHUNKS + 2) * CHUNK
FRONT_PAD = LEFT_CHUNKS * CHUNK
CONV_HALO = 32


def _cparams(*sem):
    return pltpu.CompilerParams(dimension_semantics=sem, vmem_limit_bytes=VMEM_LIMIT_BYTES)


def _inproj_body(x_ref, g_ref, w_ref, cv_ref, cg_ref, q_ref, k_ref, v_ref):
    x = x_ref[...]
    ms = jnp.mean(x * x, axis=-1, keepdims=True)
    h = (x * lax.rsqrt(ms + RMS_EPS) * g_ref[...]).astype(BF16)
    dc = cv_ref.shape[1]
    da = q_ref.shape[1]

    def proj(lo, width):
        return jnp.dot(h, w_ref[:, lo:lo + width], preferred_element_type=F32)

    cv_ref[...] = proj(0, dc)
    cg_ref[...] = proj(dc, dc)
    q_ref[...] = (proj(2 * dc, da) * (HEAD_DIM ** -0.5)).astype(BF16)
    k_ref[...] = proj(2 * dc + da, da).astype(BF16)
    v_ref[...] = proj(2 * dc + 2 * da, da).astype(BF16)


def _in_proj(x, g, w_bf16, d_conv, d_att, tm=512):
    s, d = x.shape
    d_in = w_bf16.shape[1]
    row = lambda i: (i, 0)
    const = lambda i: (0, 0)
    return pl.pallas_call(
        _inproj_body,
        grid=(s // tm,),
        in_specs=[
            pl.BlockSpec((tm, d), row),
            pl.BlockSpec((1, d), const),
            pl.BlockSpec((d, d_in), const, pipeline_mode=pl.Buffered(1)),
        ],
        out_specs=[
            pl.BlockSpec((tm, d_conv), row),
            pl.BlockSpec((tm, d_conv), row),
            pl.BlockSpec((tm, d_att), row),
            pl.BlockSpec((tm, d_att), row),
            pl.BlockSpec((tm, d_att), row),
        ],
        out_shape=[
            jax.ShapeDtypeStruct((s, d_conv), F32),
            jax.ShapeDtypeStruct((s, d_conv), F32),
            jax.ShapeDtypeStruct((s, d_att), BF16),
            jax.ShapeDtypeStruct((s, d_att), BF16),
            jax.ShapeDtypeStruct((s, d_att), BF16),
        ],
        compiler_params=_cparams("arbitrary"),
        name="in_proj",
    )(x, g, w_bf16)


def _sigmoid(x):
    return 1.0 / (1.0 + jnp.exp(-x))


def _conv_body(cv_ref, cg_ref, cvh_ref, cgh_ref, w_ref, b_ref, lg_ref, lb_ref, o_ref,
               u_ref, us_ref, wb_ref, conv_ref, *, rows_per_chunk, groups_per_pass):
    i = pl.program_id(0)
    t, dc = cv_ref.shape

    @pl.when(i == 0)
    def _():
        for w in range(CONV_WIDTH):
            wb_ref[w] = jnp.broadcast_to(w_ref[w:w + 1, :], (SUBLANES, dc))

    halo = cvh_ref[...] * _sigmoid(cgh_ref[...])
    u_ref[0:CONV_HALO, :] = jnp.where(i > 0, halo, 0.0)
    u_ref[CONV_HALO:, :] = cv_ref[...] * _sigmoid(cg_ref[...])
    n_shift_rows = us_ref.shape[1]
    for sft in range(1, SUBLANES):
        us_ref[sft - 1] = u_ref[sft:sft + n_shift_rows, :]
    first_tap = CONV_HALO - (CONV_WIDTH - 1)

    def lane_tile(lt, _):
        lanes = pl.ds(pl.multiple_of(lt * LANES, LANES), LANES)
        wv = [wb_ref[w, :, lanes] for w in range(CONV_WIDTH)]
        bias = jnp.broadcast_to(b_ref[:, lanes], (SUBLANES, LANES))
        for g0 in range(0, t // SUBLANES, groups_per_pass):
            accs = [bias] * groups_per_pass
            for w in range(CONV_WIDTH):
                off = first_tap + w
                sft = off % SUBLANES
                for g in range(groups_per_pass):
                    lo = (g0 + g) * SUBLANES + off - sft
                    if sft == 0:
                        src = u_ref[lo:lo + SUBLANES, lanes]
                    else:
                        src = us_ref[sft - 1, lo:lo + SUBLANES, lanes]
                    accs[g] = accs[g] + wv[w] * src
            for g in range(groups_per_pass):
                lo = (g0 + g) * SUBLANES
                conv_ref[lo:lo + SUBLANES, lanes] = accs[g]
        return 0

    lax.fori_loop(0, dc // LANES, lane_tile, 0)

    r = rows_per_chunk
    for c in range(t // r):
        acc = conv_ref[c * r:(c + 1) * r, :]
        mu = jnp.mean(acc, axis=-1, keepdims=True)
        cen = acc - mu
        var = jnp.mean(cen * cen, axis=-1, keepdims=True)
        un = cen * lax.rsqrt(var + LN_EPS) * lg_ref[...] + lb_ref[...]
        o_ref[c * r:(c + 1) * r, :] = (un * _sigmoid(un)).astype(o_ref.dtype)


def _conv_module(cv, cg, w_dw, b_dw, ln_g, ln_b, t=256, rows_per_chunk=16, groups_per_pass=4):
    s, dc = cv.shape
    halo_blocks = t // CONV_HALO
    row = lambda i: (i, 0)
    prev = lambda i: (jnp.maximum(i * halo_blocks - 1, 0), 0)
    const = lambda i: (0, 0)
    return pl.pallas_call(
        functools.partial(_conv_body, rows_per_chunk=rows_per_chunk, groups_per_pass=groups_per_pass),
        grid=(s // t,),
        in_specs=[
            pl.BlockSpec((t, dc), row),
            pl.BlockSpec((t, dc), row),
            pl.BlockSpec((CONV_HALO, dc), prev),
            pl.BlockSpec((CONV_HALO, dc), prev),
            pl.BlockSpec((CONV_WIDTH, dc), const),
            pl.BlockSpec((1, dc), const),
            pl.BlockSpec((1, dc), const),
            pl.BlockSpec((1, dc), const),
        ],
        out_specs=pl.BlockSpec((t, dc), row),
        out_shape=jax.ShapeDtypeStruct((s, dc), BF16),
        scratch_shapes=[
            pltpu.VMEM((t + CONV_HALO, dc), F32),
            pltpu.VMEM((SUBLANES - 1, t + CONV_HALO - SUBLANES, dc), F32),
            pltpu.VMEM((CONV_WIDTH, SUBLANES, dc), F32),
            pltpu.VMEM((t, dc), F32),
        ],
        compiler_params=_cparams("arbitrary"),
        name="conv_module",
    )(cv, cg, cv, cg, w_dw, b_dw, ln_g, ln_b)


SOFTMAX_ROWS = 32


def _attn_pairs(q_ref, bands, bias_ref, s_ref, e_ref, o_ref):
    lane = lax.broadcasted_iota(I32, (PAIR, 2 * HEAD_DIM), 1)
    n_pairs = len(bands)
    rdens = {}

    def scores(u):
        kb, _, _ = bands[u]
        q2 = q_ref[u * PAIR:(u + 1) * PAIR, :]
        for h in range(2):
            in_head = (lane >= h * HEAD_DIM) & (lane < (h + 1) * HEAD_DIM)
            qh = jnp.where(in_head, q2, jnp.zeros_like(q2))
            s_ref[2 * u + h, :, :kb.shape[0]] = lax.dot_general(
                qh, kb, (((1,), (1,)), ((), ())), preferred_element_type=F32)

    def softmax(u):
        kb, _, col_lo = bands[u]
        n_keys = kb.shape[0]
        for h in range(2):
            c = 2 * u + h
            parts = []
            for r0 in range(0, PAIR, SOFTMAX_ROWS):
                rows = slice(r0, r0 + SOFTMAX_ROWS)
                t = s_ref[c, rows, :n_keys] + bias_ref[h, rows, col_lo:col_lo + n_keys]
                e = jnp.exp(t - jnp.max(t, axis=-1, keepdims=True))
                parts.append(1.0 / jnp.sum(e, axis=-1, keepdims=True))
                e_ref[c, rows, :n_keys] = e.astype(BF16)
            rdens[c] = jnp.concatenate(parts, axis=0)

    def values(u):
        _, vb, _ = bands[u]
        n_keys = vb.shape[0]
        outs = []
        for h in range(2):
            c = 2 * u + h
            o = jnp.dot(e_ref[c, :, :n_keys], vb, preferred_element_type=F32)
            outs.append(o * rdens[c])
        o_ref[u * PAIR:(u + 1) * PAIR, :] = jnp.where(lane < HEAD_DIM, outs[0], outs[1]).astype(o_ref.dtype)

    for step in range(n_pairs + 2):
        if step < n_pairs:
            scores(step)
        if 0 <= step - 1 < n_pairs:
            softmax(step - 1)
        if 0 <= step - 2 < n_pairs:
            values(step - 2)


def _attn_body(q_ref, k_ref, v_ref, g_ref, o_ref, bias_ref, s_ref, e_ref, *, pairs_per_step):
    st = pl.program_id(1)
    npp = pairs_per_step

    @pl.when(st == 0)
    def _():
        sub = lax.broadcasted_iota(I32, (SUBLANES, BAND_PAIR), 0)
        kj = lax.broadcasted_iota(I32, (SUBLANES, BAND_PAIR), 1)
        for h in range(2):
            t8 = jnp.broadcast_to(g_ref[h], (SUBLANES, BAND_PAIR))
            shift = 1
            while shift < SUBLANES:
                t8 = jnp.where((sub & shift) != 0, pltpu.roll(t8, shift, 1), t8)
                shift *= 2
            for r0 in range(0, PAIR, SUBLANES):
                band_lo = (r0 // CHUNK) * CHUNK
                in_band = (kj >= band_lo) & (kj < band_lo + (LEFT_CHUNKS + 1) * CHUNK)
                rows = t8 if r0 == 0 else pltpu.roll(t8, r0, 1)
                bias_ref[h, r0:r0 + SUBLANES, :] = jnp.where(in_band, rows, NEG_INF)

    def run(first_step):
        bands = []
        for u in range(npp):
            if first_step:
                n_keys = min((u + 1) * PAIR, BAND_PAIR)
                lo = max((u + 1) * PAIR - BAND_PAIR, 0)
                bands.append((k_ref[lo:lo + n_keys, :], v_ref[lo:lo + n_keys, :], BAND_PAIR - n_keys))
            else:
                start = pl.multiple_of((st * npp + u) * PAIR - FRONT_PAD, PAIR)
                bands.append((k_ref[pl.ds(start, BAND_PAIR), :], v_ref[pl.ds(start, BAND_PAIR), :], 0))
        _attn_pairs(q_ref, bands, bias_ref, s_ref, e_ref, o_ref)

    @pl.when(st == 0)
    def _():
        run(True)

    @pl.when(st > 0)
    def _():
        run(False)


def _distance_rows(rel_table):
    t = rel_table.astype(F32)
    far = t[:, 2 * REL_CLIP:]
    n_far = LEFT_CHUNKS * CHUNK - REL_CLIP + 1
    near = jnp.flip(t[:, REL_CLIP - CHUNK + 1:2 * REL_CLIP], axis=1)
    tail = BAND_PAIR - n_far - near.shape[1]
    g = jnp.concatenate([jnp.repeat(far, n_far, axis=1), near, jnp.repeat(far, tail, axis=1)], axis=1)
    return g[:, None, :]


def _attention(q, k, v, g_rows, pairs_per_step=8):
    s, da = q.shape
    n_hp = da // (2 * HEAD_DIM)
    tq = pairs_per_step * PAIR
    assert pairs_per_step * PAIR >= FRONT_PAD, "first step must cover every clamped band"
    return pl.pallas_call(
        functools.partial(_attn_body, pairs_per_step=pairs_per_step),
        grid=(n_hp, s // tq),
        in_specs=[
            pl.BlockSpec((tq, 2 * HEAD_DIM), lambda hp, st: (st, hp)),
            pl.BlockSpec((s, 2 * HEAD_DIM), lambda hp, st: (0, hp)),
            pl.BlockSpec((s, 2 * HEAD_DIM), lambda hp, st: (0, hp)),
            pl.BlockSpec((2, 1, BAND_PAIR), lambda hp, st: (hp, 0, 0)),
        ],
        out_specs=pl.BlockSpec((tq, 2 * HEAD_DIM), lambda hp, st: (st, hp)),
        out_shape=jax.ShapeDtypeStruct((s, da), BF16),
        scratch_shapes=[
            pltpu.VMEM((2, PAIR, BAND_PAIR), F32),
            pltpu.VMEM((2 * pairs_per_step, PAIR, BAND_PAIR), F32),
            pltpu.VMEM((2 * pairs_per_step, PAIR, BAND_PAIR), BF16),
        ],
        compiler_params=_cparams("arbitrary", "arbitrary"),
        name="chunk_attention",
    )(q, k, v, g_rows)


def _slab_rows(d):
    return d // LANES


def _store_slabs(ref, val):
    t, d = val.shape
    n = _slab_rows(d)
    for c in range(n):
        ref[pl.ds(c, t, stride=n), :] = val[:, c * LANES:(c + 1) * LANES]


def _load_slabs(ref, t, n):
    return jnp.concatenate([ref[pl.ds(c, t, stride=n), :] for c in range(n)], axis=1)


def _first_index_of_max(vals, sub, n):
    vmax = jnp.max(vals, axis=0, keepdims=True)
    idx = jnp.min(jnp.where(vals == vmax, sub, n), axis=0, keepdims=True)
    return vmax, idx


def _outproj_body(x_ref, co_ref, ao_ref, w_ref, g2_ref, wr_ref, br_ref,
                  x2_ref, h2_ref, eid_ref, gcol_ref):
    dc = co_ref.shape[1]
    y = jnp.dot(co_ref[...], w_ref[0:dc, :], preferred_element_type=F32)
    y = y + jnp.dot(ao_ref[...], w_ref[dc:, :], preferred_element_type=F32)
    x2 = x_ref[...] + y
    ms = jnp.mean(x2 * x2, axis=-1, keepdims=True)
    h2 = x2 * lax.rsqrt(ms + RMS_EPS) * g2_ref[...]
    x2_ref[...] = x2
    _store_slabs(h2_ref, h2)

    hi = h2.astype(BF16)
    lo = (h2 - hi.astype(F32)).astype(BF16)
    pa = jnp.dot(hi, wr_ref[...], preferred_element_type=F32)
    pb = jnp.dot(lo, wr_ref[...], preferred_element_type=F32)
    logits = pa[:, :LANES] + pa[:, LANES:] + pb[:, :LANES]
    lt = logits.T + br_ref[...]
    tm = lt.shape[1]
    sub = lax.broadcasted_iota(I32, (N_GROUPS, tm), 0)
    g = lt[0:N_GROUPS, :]
    gmax, grp = _first_index_of_max(g, sub, N_GROUPS)
    p_grp = 1.0 / jnp.sum(jnp.exp(g - gmax), axis=0, keepdims=True)
    sel = jnp.zeros((EXPERTS_PER_GROUP, tm), F32)
    for gi in range(N_GROUPS):
        lo = N_GROUPS + gi * EXPERTS_PER_GROUP
        sel = jnp.where(grp == gi, lt[lo:lo + EXPERTS_PER_GROUP, :], sel)
    v1, i1 = _first_index_of_max(sel, sub, EXPERTS_PER_GROUP)
    rest = jnp.where(sub == i1, -jnp.inf, sel)
    v2, i2 = _first_index_of_max(rest, sub, EXPERTS_PER_GROUP)
    t = jnp.exp(v2 - v1)
    gate1 = p_grp / (1.0 + t)
    gate2 = p_grp * t / (1.0 + t)
    eid_ref[0:1, :] = grp * EXPERTS_PER_GROUP + i1
    eid_ref[1:2, :] = grp * EXPERTS_PER_GROUP + i2
    row = lax.broadcasted_iota(I32, (LANES, tm), 0)
    gl = jnp.where(row == 0, gate1, jnp.where(row == 1, gate2, 0.0))
    gcol_ref[...] = gl.T


def _out_proj_router(x, conv_out, attn_out, w_bf16, g2, wr_cat, br_col, tm=256):
    s, d = x.shape
    dc = conv_out.shape[1]
    da = attn_out.shape[1]
    row = lambda i: (i, 0)
    const = lambda i: (0, 0)
    return pl.pallas_call(
        _outproj_body,
        grid=(s // tm,),
        in_specs=[
            pl.BlockSpec((tm, d), row),
            pl.BlockSpec((tm, dc), row),
            pl.BlockSpec((tm, da), row),
            pl.BlockSpec((dc + da, d), const, pipeline_mode=pl.Buffered(1)),
            pl.BlockSpec((1, d), const),
            pl.BlockSpec((d, 2 * LANES), const),
            pl.BlockSpec((LANES, 1), const),
        ],
        out_specs=[
            pl.BlockSpec((tm, d), row),
            pl.BlockSpec((tm * _slab_rows(d), LANES), row),
            pl.BlockSpec((TOP_K, tm), lambda i: (0, i)),
            pl.BlockSpec((tm, LANES), row),
        ],
        out_shape=[
            jax.ShapeDtypeStruct((s, d), F32),
            jax.ShapeDtypeStruct((s * _slab_rows(d), LANES), F32),
            jax.ShapeDtypeStruct((TOP_K, s), I32),
            jax.ShapeDtypeStruct((s, LANES), F32),
        ],
        compiler_params=_cparams("arbitrary"),
        name="out_proj_router",
    )(x, conv_out, attn_out, w_bf16, g2, wr_cat, br_col)


META_LANES = 256
(META_FIRST_BLK, META_N_BLK, META_COUNT, META_NEXT_EXP, META_ORDINAL,
 META_USED_BLKS, META_FIRST_USED) = range(7)


def _dispatch_body(eid_ref, dest_ref, meta_ref, rank_ref):
    s = eid_ref.shape[1]
    w = META_LANES
    sub = lax.broadcasted_iota(I32, (N_EXPERTS, w), 0)
    lane = lax.broadcasted_iota(I32, (N_EXPERTS, w), 1)
    ri = lax.broadcasted_iota(I32, (w, w), 0)
    ci = lax.broadcasted_iota(I32, (w, w), 1)
    upper = (ri < ci).astype(BF16)

    def onehots(b):
        off = pl.multiple_of(b * w, w)
        e0 = eid_ref[0:1, pl.ds(off, w)]
        e1 = eid_ref[1:2, pl.ds(off, w)]
        return off, sub == e0, sub == e1

    def rank_block(b, carry):
        off, o0, o1 = onehots(b)
        both = jnp.where(o0 | o1, 1.0, 0.0)
        pre = jnp.dot(both.astype(BF16), upper, preferred_element_type=F32) + carry
        rank_ref[0:1, pl.ds(off, w)] = jnp.sum(jnp.where(o0, pre, 0.0), axis=0, keepdims=True)
        rank_ref[1:2, pl.ds(off, w)] = jnp.sum(jnp.where(o1, pre, 0.0), axis=0, keepdims=True)
        return carry + jnp.sum(both, axis=1, keepdims=True)

    counts = lax.fori_loop(0, s // w, rank_block, jnp.zeros((N_EXPERTS, w), F32))

    nblk = ((counts.astype(I32) + (BLOCK_ROWS - 1)) // BLOCK_ROWS).astype(F32)
    er = lax.broadcasted_iota(I32, (N_EXPERTS, N_EXPERTS), 0)
    ec = lax.broadcasted_iota(I32, (N_EXPERTS, N_EXPERTS), 1)
    lower_incl = (ec <= er).astype(BF16)
    pend_blk = jnp.dot(lower_incl, nblk.astype(BF16), preferred_element_type=F32)
    pstart = (pend_blk - nblk) * float(BLOCK_ROWS)

    def on_lanes(v):
        return jnp.sum(jnp.where(sub == lane, v, 0), axis=0, keepdims=True)

    nblk_i = nblk.astype(I32)
    first_blk = on_lanes((pend_blk - nblk).astype(I32))
    n_blk = on_lanes(nblk_i)
    count = on_lanes(counts.astype(I32))
    has_rows = nblk_i > 0
    next_exp = jnp.min(jnp.where(has_rows & (sub > lane), sub, N_EXPERTS), axis=0, keepdims=True)
    next_exp = jnp.where(next_exp < N_EXPERTS, next_exp, -1)
    ordinal = jnp.sum(jnp.where(has_rows & (sub < lane), 1, 0), axis=0, keepdims=True)
    first_used = jnp.min(jnp.where(has_rows, sub, N_EXPERTS - 1), axis=0, keepdims=True)
    used = pend_blk[N_EXPERTS - 1:N_EXPERTS, :].astype(I32)
    mrow = lax.broadcasted_iota(I32, (8, w), 0)
    meta = jnp.where(mrow == META_FIRST_BLK, first_blk, 0)
    meta = jnp.where(mrow == META_N_BLK, n_blk, meta)
    meta = jnp.where(mrow == META_COUNT, count, meta)
    meta = jnp.where(mrow == META_NEXT_EXP, next_exp, meta)
    meta = jnp.where(mrow == META_ORDINAL, ordinal, meta)
    meta = jnp.where(mrow == META_FIRST_USED, first_used, meta)
    meta_ref[...] = jnp.where(mrow == META_USED_BLKS, used, meta)

    def dest_block(b, _):
        off, o0, o1 = onehots(b)
        d0 = rank_ref[0:1, pl.ds(off, w)] + jnp.sum(jnp.where(o0, pstart, 0.0), axis=0, keepdims=True)
        d1 = rank_ref[1:2, pl.ds(off, w)] + jnp.sum(jnp.where(o1, pstart, 0.0), axis=0, keepdims=True)
        dest_ref[0:1, pl.ds(off, w)] = d0.astype(I32)
        dest_ref[1:2, pl.ds(off, w)] = d1.astype(I32)
        return 0

    lax.fori_loop(0, s // w, dest_block, 0)


def _dispatch(eids):
    s = eids.shape[1]
    return pl.pallas_call(
        _dispatch_body,
        out_shape=[
            jax.ShapeDtypeStruct((TOP_K, s), I32),
            jax.ShapeDtypeStruct((8, META_LANES), I32),
        ],
        scratch_shapes=[pltpu.VMEM((TOP_K, s), F32)],
        compiler_params=pltpu.CompilerParams(vmem_limit_bytes=VMEM_LIMIT_BYTES),
        name="dispatch",
    )(eids)


PAD_BITS = tuple(reversed(range(BLOCK_ROWS.bit_length() - 1)))


def _scatter_body(dest_ref, meta_ref, h2_ref, xs_hbm, zbuf, sem, zsem, *, tm):
    i = pl.program_id(0)
    n_sl = h2_ref.shape[0] // tm
    blk = BLOCK_ROWS * n_sl
    n_blocks = xs_hbm.shape[0] // blk
    used = meta_ref[META_USED_BLKS, 0]

    def slab(row, n_tokens=1):
        return pl.ds(pl.multiple_of(row * n_sl, n_sl), n_tokens * n_sl)

    def token_copy(j, k):
        t = i * tm + j
        return pltpu.make_async_copy(h2_ref.at[slab(j)], xs_hbm.at[slab(dest_ref[k, t])], sem.at[k])

    def scatter_tile():
        def one(j, _):
            for k in range(TOP_K):
                token_copy(j, k).start(priority=k)
            return 0

        lax.fori_loop(0, tm, one, 0, unroll=8)

    def wait_tile():
        for k in range(TOP_K):
            pltpu.make_async_copy(h2_ref, xs_hbm.at[slab(0, tm)], sem.at[k]).wait()

    @pl.when(i > 0)
    def _():
        scatter_tile()
        wait_tile()

    @pl.when(i == 0)
    def _():
        _scatter_first_step(meta_ref, xs_hbm, zbuf, zsem, slab, used, n_blocks, scatter_tile, wait_tile)


def _scatter_first_step(meta_ref, xs_hbm, zbuf, zsem, slab, used, n_blocks, scatter_tile, wait_tile):
    n_sl = zbuf.shape[0] // BLOCK_ROWS

    def zero_copy(row, n_tokens):
        return pltpu.make_async_copy(zbuf.at[pl.ds(0, n_tokens * n_sl)], xs_hbm.at[slab(row, n_tokens)], zsem)

    zbuf[...] = jnp.zeros_like(zbuf)

    def per_expert(e, issued):
        b0 = meta_ref[META_FIRST_BLK, e]
        nb = meta_ref[META_N_BLK, e]
        cnt = meta_ref[META_COUNT, e]
        pad = nb * BLOCK_ROWS - cnt
        row = b0 * BLOCK_ROWS + cnt
        out = []
        for bit, n_issued in zip(PAD_BITS, issued):
            size = 1 << bit
            take = pad & size

            @pl.when(take != 0)
            def _():
                zero_copy(row, size).start()

            row = row + take
            out.append(n_issued + take // size)
        return tuple(out)

    issued = lax.fori_loop(0, N_EXPERTS, per_expert, tuple(jnp.int32(0) for _ in PAD_BITS))
    lax.fori_loop(used, n_blocks, lambda b, _: (zero_copy(b * BLOCK_ROWS, BLOCK_ROWS).start(), 0)[1], 0)

    scatter_tile()
    wait_tile()

    for bit, n_issued in zip(PAD_BITS, issued):
        lax.fori_loop(0, n_issued, lambda i, _, n=1 << bit: (zero_copy(0, n).wait(), 0)[1], 0)
    lax.fori_loop(used, n_blocks, lambda b, _: (zero_copy(0, BLOCK_ROWS).wait(), 0)[1], 0)


def _scatter_rows(dest, meta, h2_slabs, rows, tm=512):
    s = dest.shape[1]
    n_sl = h2_slabs.shape[0] // s
    grid_spec = pltpu.PrefetchScalarGridSpec(
        num_scalar_prefetch=2,
        grid=(s // tm,),
        in_specs=[pl.BlockSpec((tm * n_sl, LANES), lambda i, dest_ref, meta_ref: (i, 0))],
        out_specs=pl.BlockSpec(memory_space=pl.ANY),
        scratch_shapes=[
            pltpu.VMEM((BLOCK_ROWS * n_sl, LANES), h2_slabs.dtype),
            pltpu.SemaphoreType.DMA((TOP_K,)),
            pltpu.SemaphoreType.DMA(()),
        ],
    )
    return pl.pallas_call(
        functools.partial(_scatter_body, tm=tm),
        grid_spec=grid_spec,
        out_shape=jax.ShapeDtypeStruct((rows * n_sl, LANES), h2_slabs.dtype),
        compiler_params=_cparams("arbitrary"),
        name="scatter_rows",
    )(dest, meta, h2_slabs)


def _row_gather_wait(src_hbm, dst, sem, n_rows):
    pltpu.make_async_copy(src_hbm.at[pl.ds(0, n_rows)], dst.at[pl.ds(0, n_rows)], sem).wait()


BIG_DMA_PRIORITY = 1


def _experts_body(meta_ref, xs_hbm, wg_hbm, wu_hbm, wd_hbm, ys_hbm,
                  xbuf, ybuf, zbuf, wgbuf, wubuf, wdbuf, gsem, ysem, zsem, wsem):
    used = meta_ref[META_USED_BLKS, 0]
    n_experts, d, _ = wg_hbm.shape
    n_sl = _slab_rows(d)
    blk = BLOCK_ROWS * n_sl
    n_blocks = ys_hbm.shape[0] // blk

    def xs_copy(b, sl):
        src_blk = jnp.minimum(b, n_blocks - 1)
        rows = pl.ds(pl.multiple_of(src_blk * blk, blk), blk)
        return pltpu.make_async_copy(xs_hbm.at[rows], xbuf.at[sl], gsem.at[sl])

    def gather_start(b, sl):
        xs_copy(b, sl).start()

    def gather_wait(sl):
        xs_copy(0, sl).wait()

    def ys_copy(b):
        sl = b % 2
        rows = pl.ds(pl.multiple_of(b * blk, blk), blk)
        return pltpu.make_async_copy(ybuf.at[sl], ys_hbm.at[rows], ysem.at[sl])

    def zero_copy(b):
        rows = pl.ds(pl.multiple_of(b * blk, blk), blk)
        return pltpu.make_async_copy(zbuf, ys_hbm.at[rows], zsem)

    def weight_copies(e, ws):
        return (pltpu.make_async_copy(wg_hbm.at[e], wgbuf.at[ws], wsem.at[ws, 0]),
                pltpu.make_async_copy(wu_hbm.at[e], wubuf.at[ws], wsem.at[ws, 1]),
                pltpu.make_async_copy(wd_hbm.at[e], wdbuf.at[ws], wsem.at[ws, 2]))

    for c in weight_copies(meta_ref[META_FIRST_USED, 0], 0):
        c.start(priority=BIG_DMA_PRIORITY)
    gather_start(0, 0)
    zbuf[...] = jnp.zeros_like(zbuf)
    lax.fori_loop(used, n_blocks, lambda b, _: (zero_copy(b).start(priority=BIG_DMA_PRIORITY), 0)[1], 0)

    def expert(e, _):
        nb = meta_ref[META_N_BLK, e]

        @pl.when(nb > 0)
        def _():
            b0 = meta_ref[META_FIRST_BLK, e]
            ws = meta_ref[META_ORDINAL, e] % 2
            nxt = meta_ref[META_NEXT_EXP, e]

            @pl.when(nxt >= 0)
            def _():
                for c in weight_copies(nxt, 1 - ws):
                    c.start(priority=BIG_DMA_PRIORITY)

            for c in weight_copies(e, ws):
                c.wait()

            def block(j, _):
                b = b0 + j
                sl = b % 2
                gather_start(b + 1, 1 - sl)
                gather_wait(sl)

                @pl.when(b >= 2)
                def _():
                    ys_copy(b - 2).wait()

                x = _load_slabs(xbuf.at[sl], BLOCK_ROWS, n_sl).astype(BF16)
                a = jnp.dot(x, wgbuf[ws].astype(BF16), preferred_element_type=F32)
                u = jnp.dot(x, wubuf[ws].astype(BF16), preferred_element_type=F32)
                act = (a * _sigmoid(a) * u).astype(BF16)
                y = jnp.dot(act, wdbuf[ws].astype(BF16), preferred_element_type=F32)
                _store_slabs(ybuf.at[sl], y)
                ys_copy(b).start(priority=BIG_DMA_PRIORITY)
                return 0

            lax.fori_loop(0, nb, block, 0)

        return 0

    lax.fori_loop(0, n_experts, expert, 0)

    gather_wait(used % 2)
    for back in (1, 2):
        @pl.when(used >= back)
        def _():
            ys_copy(used - back).wait()
    lax.fori_loop(used, n_blocks, lambda b, _: (zero_copy(b).wait(), 0)[1], 0)


def _experts(meta, xs_slabs, w_gate, w_up, w_down):
    n_e, d, de = w_gate.shape
    n_sl = _slab_rows(d)
    blk = BLOCK_ROWS * n_sl
    smem = pl.BlockSpec(memory_space=pltpu.SMEM)
    hbm = pl.BlockSpec(memory_space=pl.ANY)
    return pl.pallas_call(
        _experts_body,
        in_specs=[smem, hbm, hbm, hbm, hbm],
        out_specs=hbm,
        out_shape=jax.ShapeDtypeStruct(xs_slabs.shape, F32),
        scratch_shapes=[
            pltpu.VMEM((2, blk, LANES), F32),
            pltpu.VMEM((2, blk, LANES), F32),
            pltpu.VMEM((blk, LANES), F32),
            pltpu.VMEM((2, d, de), F32),
            pltpu.VMEM((2, d, de), F32),
            pltpu.VMEM((2, de, d), F32),
            pltpu.SemaphoreType.DMA((2,)),
            pltpu.SemaphoreType.DMA((2,)),
            pltpu.SemaphoreType.DMA(()),
            pltpu.SemaphoreType.DMA((2, 3)),
        ],
        compiler_params=pltpu.CompilerParams(vmem_limit_bytes=VMEM_LIMIT_BYTES),
        name="expert_mlp",
    )(meta, xs_slabs, w_gate, w_up, w_down)


def _combine_body(dest_ref, x2_ref, gcol_ref, fg_ref, ys_hbm, o_ref, ybuf, sem, *, tm):
    i = pl.program_id(0)
    n = pl.num_programs(0)
    s = n * tm
    slot = i % 2
    n_sl = _slab_rows(x2_ref.shape[1])

    def start(tile, sl):
        for k in range(TOP_K):
            for r in range(tm):
                row = dest_ref[k * s + tile * tm + r]
                src = ys_hbm.at[pl.ds(pl.multiple_of(row * n_sl, n_sl), n_sl)]
                pltpu.make_async_copy(src, ybuf.at[sl, k, pl.ds(r * n_sl, n_sl)],
                                      sem.at[sl, k]).start(priority=r % 2)

    def wait(sl):
        for k in range(TOP_K):
            _row_gather_wait(ys_hbm, ybuf.at[sl, k], sem.at[sl, k], tm * n_sl)

    @pl.when(i == 0)
    def _():
        start(0, 0)

    start(jnp.minimum(i + 1, n - 1), 1 - slot)
    wait(slot)
    g = gcol_ref[...]
    y0 = _load_slabs(ybuf.at[slot, 0], tm, n_sl)
    y1 = _load_slabs(ybuf.at[slot, 1], tm, n_sl)
    moe = g[:, 0:1] * y0 + g[:, 1:2] * y1
    x3 = x2_ref[...] + moe
    ms = jnp.mean(x3 * x3, axis=-1, keepdims=True)
    o_ref[...] = x3 * lax.rsqrt(ms + RMS_EPS) * fg_ref[...]

    @pl.when(i == n - 1)
    def _():
        wait(1 - slot)


def _combine(dest_flat, x2, gcol, final_g, ys, tm=128):
    s, d = x2.shape
    row = lambda i, dest_ref: (i, 0)
    const = lambda i, dest_ref: (0, 0)
    grid_spec = pltpu.PrefetchScalarGridSpec(
        num_scalar_prefetch=1,
        grid=(s // tm,),
        in_specs=[
            pl.BlockSpec((tm, d), row),
            pl.BlockSpec((tm, LANES), row),
            pl.BlockSpec((1, d), const),
            pl.BlockSpec(memory_space=pl.ANY),
        ],
        out_specs=pl.BlockSpec((tm, d), row),
        scratch_shapes=[
            pltpu.VMEM((2, TOP_K, tm * _slab_rows(d), LANES), F32),
            pltpu.SemaphoreType.DMA((2, TOP_K)),
        ],
    )
    return pl.pallas_call(
        functools.partial(_combine_body, tm=tm),
        grid_spec=grid_spec,
        out_shape=jax.ShapeDtypeStruct((s, d), F32),
        compiler_params=_cparams("arbitrary"),
        name="combine_norm",
    )(dest_flat, x2, gcol, final_g, ys)


def _layer(x, norm1_g, w_in, w_dw, b_dw, ln_g, ln_b, rel_bias, w_out, norm2_g,
           w_group, b_group, w_expert_router, b_expert_router, w_gate, w_up, w_down, out_g):
    s, d = x.shape
    dc = w_dw.shape[-1]
    da = (w_in.shape[1] - 2 * dc) // 3
    row2 = lambda a: a.reshape(1, -1)

    cv, cg, q, k, v = _in_proj(x, row2(norm1_g), w_in.astype(BF16), dc, da)
    conv_out = _conv_module(cv, cg, w_dw.reshape(CONV_WIDTH, dc), row2(b_dw), row2(ln_g), row2(ln_b))
    attn_out = _attention(q, k, v, _distance_rows(rel_bias))

    n_r = N_GROUPS + N_EXPERTS
    wr = jnp.pad(jnp.concatenate([w_group, w_expert_router], axis=1), ((0, 0), (0, LANES - n_r)))
    wr_hi = wr.astype(BF16)
    wr_lo = (wr - wr_hi.astype(F32)).astype(BF16)
    wr_cat = jnp.concatenate([wr_hi, wr_lo], axis=1)
    br_col = jnp.zeros((LANES, 1), F32).at[:n_r, 0].set(jnp.concatenate([b_group, b_expert_router]))
    x2, h2, eids, gcol = _out_proj_router(x, conv_out, attn_out, w_out.astype(BF16), row2(norm2_g),
                                          wr_cat, br_col)

    n_blocks = (s * TOP_K) // BLOCK_ROWS + N_EXPERTS
    assert n_blocks <= META_LANES
    dest, meta = _dispatch(eids)
    xs = _scatter_rows(dest, meta, h2, n_blocks * BLOCK_ROWS)
    ys = _experts(meta, xs, w_gate, w_up, w_down)
    return _combine(dest.reshape(-1), x2, gcol, row2(out_g), ys)


def kernel(x, norm1_g, w_in, w_dw, b_dw, ln_g, ln_b, rel_bias, w_out, norm2_g, w_group, b_group,
           w_expert_router, b_expert_router, w_gate, w_up, w_down, final_g):
    depth = norm1_g.shape[0]
    assert depth == 1, "the final norm is fused into the last layer's combine step"
    batch, s, d = x.shape
    assert batch == 1, "one sequence per call"
    out = _layer(x.reshape(s, d), norm1_g[0], w_in[0], w_dw[0], b_dw[0], ln_g[0], ln_b[0], rel_bias[0],
                 w_out[0], norm2_g[0], w_group[0], b_group[0], w_expert_router[0],
                 b_expert_router[0], w_gate[0], w_up[0], w_down[0], final_g)
    return out.reshape(batch, s, d)
```

```python
import functools

import jax
import jax.numpy as jnp
from jax import lax
from jax.experimental import pallas as pl
from jax.experimental.pallas import tpu as pltpu

F32 = jnp.float32
BF16 = jnp.bfloat16
I32 = jnp.int32

CHUNK = 64
HEAD_DIM = 64
LEFT_CHUNKS = 8
REL_CLIP = 128
CONV_WIDTH = 31
N_GROUPS = 8
EXPERTS_PER_GROUP = 8
N_EXPERTS = N_GROUPS * EXPERTS_PER_GROUP
TOP_K = 2
BLOCK_ROWS = 128
RMS_EPS = 1e-6
LN_EPS = 1e-5
NEG_INF = -1e30

VMEM_LIMIT_BYTES = 56 * 1024 * 1024
LANES = 128
SUBLANES = 8
PAIR = 2 * CHUNK
BAND_PAIR = (LEFT_CHUNKS + 2) * CHUNK
FRONT_PAD = LEFT_CHUNKS * CHUNK
CONV_HALO = 32


def _cparams(*sem):
    return pltpu.CompilerParams(dimension_semantics=sem, vmem_limit_bytes=VMEM_LIMIT_BYTES)


def _inproj_body(x_ref, g_ref, w_ref, cv_ref, cg_ref, q_ref, k_ref, v_ref):
    x = x_ref[...]
    ms = jnp.mean(x * x, axis=-1, keepdims=True)
    h = (x * lax.rsqrt(ms + RMS_EPS) * g_ref[...]).astype(BF16)
    dc = cv_ref.shape[1]
    da = q_ref.shape[1]

    def proj(lo, width):
        return jnp.dot(h, w_ref[:, lo:lo + width], preferred_element_type=F32)

    cv_ref[...] = proj(0, dc)
    cg_ref[...] = proj(dc, dc)
    q_ref[...] = (proj(2 * dc, da) * (HEAD_DIM ** -0.5)).astype(BF16)
    k_ref[...] = proj(2 * dc + da, da).astype(BF16)
    v_ref[...] = proj(2 * dc + 2 * da, da).astype(BF16)


def _in_proj(x, g, w_bf16, d_conv, d_att, tm=512):
    s, d = x.shape
    d_in = w_bf16.shape[1]
    row = lambda i: (i, 0)
    const = lambda i: (0, 0)
    return pl.pallas_call(
        _inproj_body,
        grid=(s // tm,),
        in_specs=[
            pl.BlockSpec((tm, d), row),
            pl.BlockSpec((1, d), const),
            pl.BlockSpec((d, d_in), const, pipeline_mode=pl.Buffered(1)),
        ],
        out_specs=[
            pl.BlockSpec((tm, d_conv), row),
            pl.BlockSpec((tm, d_conv), row),
            pl.BlockSpec((tm, d_att), row),
            pl.BlockSpec((tm, d_att), row),
            pl.BlockSpec((tm, d_att), row),
        ],
        out_shape=[
            jax.ShapeDtypeStruct((s, d_conv), F32),
            jax.ShapeDtypeStruct((s, d_conv), F32),
            jax.ShapeDtypeStruct((s, d_att), BF16),
            jax.ShapeDtypeStruct((s, d_att), BF16),
            jax.ShapeDtypeStruct((s, d_att), BF16),
        ],
        compiler_params=_cparams("arbitrary"),
        name="in_proj",
    )(x, g, w_bf16)


def _sigmoid(x):
    return 1.0 / (1.0 + jnp.exp(-x))


def _conv_body(cv_ref, cg_ref, cvh_ref, cgh_ref, w_ref, b_ref, lg_ref, lb_ref, o_ref,
               u_ref, us_ref, wb_ref, conv_ref, *, rows_per_chunk, groups_per_pass):
    i = pl.program_id(0)
    t, dc = cv_ref.shape

    @pl.when(i == 0)
    def _():
        for w in range(CONV_WIDTH):
            wb_ref[w] = jnp.broadcast_to(w_ref[w:w + 1, :], (SUBLANES, dc))

    halo = cvh_ref[...] * _sigmoid(cgh_ref[...])
    u_ref[0:CONV_HALO, :] = jnp.where(i > 0, halo, 0.0)
    u_ref[CONV_HALO:, :] = cv_ref[...] * _sigmoid(cg_ref[...])
    n_shift_rows = us_ref.shape[1]
    for sft in range(1, SUBLANES):
        us_ref[sft - 1] = u_ref[sft:sft + n_shift_rows, :]
    first_tap = CONV_HALO - (CONV_WIDTH - 1)

    def lane_tile(lt, _):
        lanes = pl.ds(pl.multiple_of(lt * LANES, LANES), LANES)
        wv = [wb_ref[w, :, lanes] for w in range(CONV_WIDTH)]
        bias = jnp.broadcast_to(b_ref[:, lanes], (SUBLANES, LANES))
        for g0 in range(0, t // SUBLANES, groups_per_pass):
            accs = [bias] * groups_per_pass
            for w in range(CONV_WIDTH):
                off = first_tap + w
                sft = off % SUBLANES
                for g in range(groups_per_pass):
                    lo = (g0 + g) * SUBLANES + off - sft
                    if sft == 0:
                        src = u_ref[lo:lo + SUBLANES, lanes]
                    else:
                        src = us_ref[sft - 1, lo:lo + SUBLANES, lanes]
                    accs[g] = accs[g] + wv[w] * src
            for g in range(groups_per_pass):
                lo = (g0 + g) * SUBLANES
                conv_ref[lo:lo + SUBLANES, lanes] = accs[g]
        return 0

    lax.fori_loop(0, dc // LANES, lane_tile, 0)

    r = rows_per_chunk
    for c in range(t // r):
        acc = conv_ref[c * r:(c + 1) * r, :]
        mu = jnp.mean(acc, axis=-1, keepdims=True)
        cen = acc - mu
        var = jnp.mean(cen * cen, axis=-1, keepdims=True)
        un = cen * lax.rsqrt(var + LN_EPS) * lg_ref[...] + lb_ref[...]
        o_ref[c * r:(c + 1) * r, :] = (un * _sigmoid(un)).astype(o_ref.dtype)


def _conv_module(cv, cg, w_dw, b_dw, ln_g, ln_b, t=512, rows_per_chunk=16, groups_per_pass=4):
    s, dc = cv.shape
    halo_blocks = t // CONV_HALO
    row = lambda i: (i, 0)
    prev = lambda i: (jnp.maximum(i * halo_blocks - 1, 0), 0)
    const = lambda i: (0, 0)
    return pl.pallas_call(
        functools.partial(_conv_body, rows_per_chunk=rows_per_chunk, groups_per_pass=groups_per_pass),
        grid=(s // t,),
        in_specs=[
            pl.BlockSpec((t, dc), row),
            pl.BlockSpec((t, dc), row),
            pl.BlockSpec((CONV_HALO, dc), prev),
            pl.BlockSpec((CONV_HALO, dc), prev),
            pl.BlockSpec((CONV_WIDTH, dc), const),
            pl.BlockSpec((1, dc), const),
            pl.BlockSpec((1, dc), const),
            pl.BlockSpec((1, dc), const),
        ],
        out_specs=pl.BlockSpec((t, dc), row),
        out_shape=jax.ShapeDtypeStruct((s, dc), BF16),
        scratch_shapes=[
            pltpu.VMEM((t + CONV_HALO, dc), F32),
            pltpu.VMEM((SUBLANES - 1, t + CONV_HALO - SUBLANES, dc), F32),
            pltpu.VMEM((CONV_WIDTH, SUBLANES, dc), F32),
            pltpu.VMEM((t, dc), F32),
        ],
        compiler_params=_cparams("arbitrary"),
        name="conv_module",
    )(cv, cg, cv, cg, w_dw, b_dw, ln_g, ln_b)


SOFTMAX_ROWS = 32


def _attn_pairs(q_ref, bands, bias_ref, s_ref, e_ref, o_ref):
    lane = lax.broadcasted_iota(I32, (PAIR, 2 * HEAD_DIM), 1)
    n_pairs = len(bands)
    rdens = {}

    def scores(u):
        kb, _, _ = bands[u]
        q2 = q_ref[u * PAIR:(u + 1) * PAIR, :]
        for h in range(2):
            in_head = (lane >= h * HEAD_DIM) & (lane < (h + 1) * HEAD_DIM)
            qh = jnp.where(in_head, q2, jnp.zeros_like(q2))
            s_ref[2 * u + h, :, :kb.shape[0]] = lax.dot_general(
                qh, kb, (((1,), (1,)), ((), ())), preferred_element_type=F32)

    def softmax(u):
        kb, _, col_lo = bands[u]
        n_keys = kb.shape[0]
        for h in range(2):
            c = 2 * u + h
            parts = []
            for r0 in range(0, PAIR, SOFTMAX_ROWS):
                rows = slice(r0, r0 + SOFTMAX_ROWS)
                t = s_ref[c, rows, :n_keys] + bias_ref[h, rows, col_lo:col_lo + n_keys]
                e = jnp.exp(t - jnp.max(t, axis=-1, keepdims=True))
                parts.append(1.0 / jnp.sum(e, axis=-1, keepdims=True))
                e_ref[c, rows, :n_keys] = e.astype(BF16)
            rdens[c] = jnp.concatenate(parts, axis=0)

    def values(u):
        _, vb, _ = bands[u]
        n_keys = vb.shape[0]
        outs = []
        for h in range(2):
            c = 2 * u + h
            o = jnp.dot(e_ref[c, :, :n_keys], vb, preferred_element_type=F32)
            outs.append(o * rdens[c])
        o_ref[u * PAIR:(u + 1) * PAIR, :] = jnp.where(lane < HEAD_DIM, outs[0], outs[1]).astype(o_ref.dtype)

    for step in range(n_pairs + 2):
        if step < n_pairs:
            scores(step)
        if 0 <= step - 1 < n_pairs:
            softmax(step - 1)
        if 0 <= step - 2 < n_pairs:
            values(step - 2)


def _attn_body(q_ref, k_ref, v_ref, g_ref, o_ref, bias_ref, s_ref, e_ref, *, pairs_per_step):
    st = pl.program_id(1)
    npp = pairs_per_step

    @pl.when(st == 0)
    def _():
        sub = lax.broadcasted_iota(I32, (SUBLANES, BAND_PAIR), 0)
        kj = lax.broadcasted_iota(I32, (SUBLANES, BAND_PAIR), 1)
        for h in range(2):
            t8 = jnp.broadcast_to(g_ref[h], (SUBLANES, BAND_PAIR))
            shift = 1
            while shift < SUBLANES:
                t8 = jnp.where((sub & shift) != 0, pltpu.roll(t8, shift, 1), t8)
                shift *= 2
            for r0 in range(0, PAIR, SUBLANES):
                band_lo = (r0 // CHUNK) * CHUNK
                in_band = (kj >= band_lo) & (kj < band_lo + (LEFT_CHUNKS + 1) * CHUNK)
                rows = t8 if r0 == 0 else pltpu.roll(t8, r0, 1)
                bias_ref[h, r0:r0 + SUBLANES, :] = jnp.where(in_band, rows, NEG_INF)

    def run(first_step):
        bands = []
        for u in range(npp):
            if first_step:
                n_keys = min((u + 1) * PAIR, BAND_PAIR)
                lo = max((u + 1) * PAIR - BAND_PAIR, 0)
                bands.append((k_ref[lo:lo + n_keys, :], v_ref[lo:lo + n_keys, :], BAND_PAIR - n_keys))
            else:
                start = pl.multiple_of((st * npp + u) * PAIR - FRONT_PAD, PAIR)
                bands.append((k_ref[pl.ds(start, BAND_PAIR), :], v_ref[pl.ds(start, BAND_PAIR), :], 0))
        _attn_pairs(q_ref, bands, bias_ref, s_ref, e_ref, o_ref)

    @pl.when(st == 0)
    def _():
        run(True)

    @pl.when(st > 0)
    def _():
        run(False)


def _distance_rows(rel_table):
    t = rel_table.astype(F32)
    far = t[:, 2 * REL_CLIP:]
    n_far = LEFT_CHUNKS * CHUNK - REL_CLIP + 1
    near = jnp.flip(t[:, REL_CLIP - CHUNK + 1:2 * REL_CLIP], axis=1)
    tail = BAND_PAIR - n_far - near.shape[1]
    g = jnp.concatenate([jnp.repeat(far, n_far, axis=1), near, jnp.repeat(far, tail, axis=1)], axis=1)
    return g[:, None, :]


def _attention(q, k, v, g_rows, pairs_per_step=8):
    s, da = q.shape
    n_hp = da // (2 * HEAD_DIM)
    tq = pairs_per_step * PAIR
    assert pairs_per_step * PAIR >= FRONT_PAD, "first step must cover every clamped band"
    return pl.pallas_call(
        functools.partial(_attn_body, pairs_per_step=pairs_per_step),
        grid=(n_hp, s // tq),
        in_specs=[
            pl.BlockSpec((tq, 2 * HEAD_DIM), lambda hp, st: (st, hp)),
            pl.BlockSpec((s, 2 * HEAD_DIM), lambda hp, st: (0, hp)),
            pl.BlockSpec((s, 2 * HEAD_DIM), lambda hp, st: (0, hp)),
            pl.BlockSpec((2, 1, BAND_PAIR), lambda hp, st: (hp, 0, 0)),
        ],
        out_specs=pl.BlockSpec((tq, 2 * HEAD_DIM), lambda hp, st: (st, hp)),
        out_shape=jax.ShapeDtypeStruct((s, da), BF16),
        scratch_shapes=[
            pltpu.VMEM((2, PAIR, BAND_PAIR), F32),
            pltpu.VMEM((2 * pairs_per_step, PAIR, BAND_PAIR), F32),
            pltpu.VMEM((2 * pairs_per_step, PAIR, BAND_PAIR), BF16),
        ],
        compiler_params=_cparams("arbitrary", "arbitrary"),
        name="chunk_attention",
    )(q, k, v, g_rows)


def _slab_rows(d):
    return d // LANES


def _store_slabs(ref, val):
    t, d = val.shape
    n = _slab_rows(d)
    for c in range(n):
        ref[pl.ds(c, t, stride=n), :] = val[:, c * LANES:(c + 1) * LANES]


def _load_slabs(ref, t, n):
    return jnp.concatenate([ref[pl.ds(c, t, stride=n), :] for c in range(n)], axis=1)


def _first_index_of_max(vals, sub, n):
    vmax = jnp.max(vals, axis=0, keepdims=True)
    idx = jnp.min(jnp.where(vals == vmax, sub, n), axis=0, keepdims=True)
    return vmax, idx


def _outproj_body(x_ref, co_ref, ao_ref, w_ref, g2_ref, wr_ref, br_ref,
                  x2_ref, h2_ref, eid_ref, gcol_ref):
    dc = co_ref.shape[1]
    y = jnp.dot(co_ref[...], w_ref[0:dc, :], preferred_element_type=F32)
    y = y + jnp.dot(ao_ref[...], w_ref[dc:, :], preferred_element_type=F32)
    x2 = x_ref[...] + y
    ms = jnp.mean(x2 * x2, axis=-1, keepdims=True)
    h2 = x2 * lax.rsqrt(ms + RMS_EPS) * g2_ref[...]
    x2_ref[...] = x2
    _store_slabs(h2_ref, h2)

    hi = h2.astype(BF16)
    lo = (h2 - hi.astype(F32)).astype(BF16)
    pa = jnp.dot(hi, wr_ref[...], preferred_element_type=F32)
    pb = jnp.dot(lo, wr_ref[...], preferred_element_type=F32)
    logits = pa[:, :LANES] + pa[:, LANES:] + pb[:, :LANES]
    lt = logits.T + br_ref[...]
    tm = lt.shape[1]
    sub = lax.broadcasted_iota(I32, (N_GROUPS, tm), 0)
    g = lt[0:N_GROUPS, :]
    gmax, grp = _first_index_of_max(g, sub, N_GROUPS)
    p_grp = 1.0 / jnp.sum(jnp.exp(g - gmax), axis=0, keepdims=True)
    sel = jnp.zeros((EXPERTS_PER_GROUP, tm), F32)
    for gi in range(N_GROUPS):
        lo = N_GROUPS + gi * EXPERTS_PER_GROUP
        sel = jnp.where(grp == gi, lt[lo:lo + EXPERTS_PER_GROUP, :], sel)
    v1, i1 = _first_index_of_max(sel, sub, EXPERTS_PER_GROUP)
    rest = jnp.where(sub == i1, -jnp.inf, sel)
    v2, i2 = _first_index_of_max(rest, sub, EXPERTS_PER_GROUP)
    t = jnp.exp(v2 - v1)
    gate1 = p_grp / (1.0 + t)
    gate2 = p_grp * t / (1.0 + t)
    eid_ref[0:1, :] = grp * EXPERTS_PER_GROUP + i1
    eid_ref[1:2, :] = grp * EXPERTS_PER_GROUP + i2
    row = lax.broadcasted_iota(I32, (LANES, tm), 0)
    gl = jnp.where(row == 0, gate1, jnp.where(row == 1, gate2, 0.0))
    gcol_ref[...] = gl.T


def _out_proj_router(x, conv_out, attn_out, w_bf16, g2, wr_cat, br_col, tm=512):
    s, d = x.shape
    dc = conv_out.shape[1]
    da = attn_out.shape[1]
    row = lambda i: (i, 0)
    const = lambda i: (0, 0)
    return pl.pallas_call(
        _outproj_body,
        grid=(s // tm,),
        in_specs=[
            pl.BlockSpec((tm, d), row),
            pl.BlockSpec((tm, dc), row),
            pl.BlockSpec((tm, da), row),
            pl.BlockSpec((dc + da, d), const, pipeline_mode=pl.Buffered(1)),
            pl.BlockSpec((1, d), const),
            pl.BlockSpec((d, 2 * LANES), const),
            pl.BlockSpec((LANES, 1), const),
        ],
        out_specs=[
            pl.BlockSpec((tm, d), row),
            pl.BlockSpec((tm * _slab_rows(d), LANES), row),
            pl.BlockSpec((TOP_K, tm), lambda i: (0, i)),
            pl.BlockSpec((tm, LANES), row),
        ],
        out_shape=[
            jax.ShapeDtypeStruct((s, d), F32),
            jax.ShapeDtypeStruct((s * _slab_rows(d), LANES), F32),
            jax.ShapeDtypeStruct((TOP_K, s), I32),
            jax.ShapeDtypeStruct((s, LANES), F32),
        ],
        compiler_params=_cparams("arbitrary"),
        name="out_proj_router",
    )(x, conv_out, attn_out, w_bf16, g2, wr_cat, br_col)


META_LANES = 256
(META_FIRST_BLK, META_N_BLK, META_COUNT, META_NEXT_EXP, META_ORDINAL,
 META_USED_BLKS, META_FIRST_USED) = range(7)


def _dispatch_body(eid_ref, dest_ref, meta_ref, rank_ref):
    s = eid_ref.shape[1]
    w = META_LANES
    sub = lax.broadcasted_iota(I32, (N_EXPERTS, w), 0)
    lane = lax.broadcasted_iota(I32, (N_EXPERTS, w), 1)
    ri = lax.broadcasted_iota(I32, (w, w), 0)
    ci = lax.broadcasted_iota(I32, (w, w), 1)
    upper = (ri < ci).astype(BF16)

    def onehots(b):
        off = pl.multiple_of(b * w, w)
        e0 = eid_ref[0:1, pl.ds(off, w)]
        e1 = eid_ref[1:2, pl.ds(off, w)]
        return off, sub == e0, sub == e1

    def rank_block(b, carry):
        off, o0, o1 = onehots(b)
        both = jnp.where(o0 | o1, 1.0, 0.0)
        pre = jnp.dot(both.astype(BF16), upper, preferred_element_type=F32) + carry
        rank_ref[0:1, pl.ds(off, w)] = jnp.sum(jnp.where(o0, pre, 0.0), axis=0, keepdims=True)
        rank_ref[1:2, pl.ds(off, w)] = jnp.sum(jnp.where(o1, pre, 0.0), axis=0, keepdims=True)
        return carry + jnp.sum(both, axis=1, keepdims=True)

    counts = lax.fori_loop(0, s // w, rank_block, jnp.zeros((N_EXPERTS, w), F32))

    nblk = ((counts.astype(I32) + (BLOCK_ROWS - 1)) // BLOCK_ROWS).astype(F32)
    er = lax.broadcasted_iota(I32, (N_EXPERTS, N_EXPERTS), 0)
    ec = lax.broadcasted_iota(I32, (N_EXPERTS, N_EXPERTS), 1)
    lower_incl = (ec <= er).astype(BF16)
    pend_blk = jnp.dot(lower_incl, nblk.astype(BF16), preferred_element_type=F32)
    pstart = (pend_blk - nblk) * float(BLOCK_ROWS)

    def on_lanes(v):
        return jnp.sum(jnp.where(sub == lane, v, 0), axis=0, keepdims=True)

    nblk_i = nblk.astype(I32)
    first_blk = on_lanes((pend_blk - nblk).astype(I32))
    n_blk = on_lanes(nblk_i)
    count = on_lanes(counts.astype(I32))
    has_rows = nblk_i > 0
    next_exp = jnp.min(jnp.where(has_rows & (sub > lane), sub, N_EXPERTS), axis=0, keepdims=True)
    next_exp = jnp.where(next_exp < N_EXPERTS, next_exp, -1)
    ordinal = jnp.sum(jnp.where(has_rows & (sub < lane), 1, 0), axis=0, keepdims=True)
    first_used = jnp.min(jnp.where(has_rows, sub, N_EXPERTS - 1), axis=0, keepdims=True)
    used = pend_blk[N_EXPERTS - 1:N_EXPERTS, :].astype(I32)
    mrow = lax.broadcasted_iota(I32, (8, w), 0)
    meta = jnp.where(mrow == META_FIRST_BLK, first_blk, 0)
    meta = jnp.where(mrow == META_N_BLK, n_blk, meta)
    meta = jnp.where(mrow == META_COUNT, count, meta)
    meta = jnp.where(mrow == META_NEXT_EXP, next_exp, meta)
    meta = jnp.where(mrow == META_ORDINAL, ordinal, meta)
    meta = jnp.where(mrow == META_FIRST_USED, first_used, meta)
    meta_ref[...] = jnp.where(mrow == META_USED_BLKS, used, meta)

    def dest_block(b, _):
        off, o0, o1 = onehots(b)
        d0 = rank_ref[0:1, pl.ds(off, w)] + jnp.sum(jnp.where(o0, pstart, 0.0), axis=0, keepdims=True)
        d1 = rank_ref[1:2, pl.ds(off, w)] + jnp.sum(jnp.where(o1, pstart, 0.0), axis=0, keepdims=True)
        dest_ref[0:1, pl.ds(off, w)] = d0.astype(I32)
        dest_ref[1:2, pl.ds(off, w)] = d1.astype(I32)
        return 0

    lax.fori_loop(0, s // w, dest_block, 0)


def _dispatch(eids):
    s = eids.shape[1]
    return pl.pallas_call(
        _dispatch_body,
        out_shape=[
            jax.ShapeDtypeStruct((TOP_K, s), I32),
            jax.ShapeDtypeStruct((8, META_LANES), I32),
        ],
        scratch_shapes=[pltpu.VMEM((TOP_K, s), F32)],
        compiler_params=pltpu.CompilerParams(vmem_limit_bytes=VMEM_LIMIT_BYTES),
        name="dispatch",
    )(eids)


PAD_BITS = tuple(reversed(range(BLOCK_ROWS.bit_length() - 1)))


def _scatter_body(dest_ref, meta_ref, h2_ref, xs_hbm, zbuf, sem, zsem, *, tm):
    i = pl.program_id(0)
    n_sl = h2_ref.shape[0] // tm
    blk = BLOCK_ROWS * n_sl
    n_blocks = xs_hbm.shape[0] // blk
    used = meta_ref[META_USED_BLKS, 0]

    def slab(row, n_tokens=1):
        return pl.ds(pl.multiple_of(row * n_sl, n_sl), n_tokens * n_sl)

    def token_copy(j, k):
        t = i * tm + j
        return pltpu.make_async_copy(h2_ref.at[slab(j)], xs_hbm.at[slab(dest_ref[k, t])], sem.at[k])

    def scatter_tile():
        def one(j, _):
            for k in range(TOP_K):
                token_copy(j, k).start(priority=k)
            return 0

        lax.fori_loop(0, tm, one, 0, unroll=8)

    def wait_tile():
        for k in range(TOP_K):
            pltpu.make_async_copy(h2_ref, xs_hbm.at[slab(0, tm)], sem.at[k]).wait()

    @pl.when(i > 0)
    def _():
        scatter_tile()
        wait_tile()

    @pl.when(i == 0)
    def _():
        _scatter_first_step(meta_ref, xs_hbm, zbuf, zsem, slab, used, n_blocks, scatter_tile, wait_tile)


def _scatter_first_step(meta_ref, xs_hbm, zbuf, zsem, slab, used, n_blocks, scatter_tile, wait_tile):
    n_sl = zbuf.shape[0] // BLOCK_ROWS

    def zero_copy(row, n_tokens):
        return pltpu.make_async_copy(zbuf.at[pl.ds(0, n_tokens * n_sl)], xs_hbm.at[slab(row, n_tokens)], zsem)

    zbuf[...] = jnp.zeros_like(zbuf)

    def per_expert(e, issued):
        b0 = meta_ref[META_FIRST_BLK, e]
        nb = meta_ref[META_N_BLK, e]
        cnt = meta_ref[META_COUNT, e]
        pad = nb * BLOCK_ROWS - cnt
        row = b0 * BLOCK_ROWS + cnt
        out = []
        for bit, n_issued in zip(PAD_BITS, issued):
            size = 1 << bit
            take = pad & size

            @pl.when(take != 0)
            def _():
                zero_copy(row, size).start()

            row = row + take
            out.append(n_issued + take // size)
        return tuple(out)

    issued = lax.fori_loop(0, N_EXPERTS, per_expert, tuple(jnp.int32(0) for _ in PAD_BITS))
    lax.fori_loop(used, n_blocks, lambda b, _: (zero_copy(b * BLOCK_ROWS, BLOCK_ROWS).start(), 0)[1], 0)

    scatter_tile()
    wait_tile()

    for bit, n_issued in zip(PAD_BITS, issued):
        lax.fori_loop(0, n_issued, lambda i, _, n=1 << bit: (zero_copy(0, n).wait(), 0)[1], 0)
    lax.fori_loop(used, n_blocks, lambda b, _: (zero_copy(0, BLOCK_ROWS).wait(), 0)[1], 0)


def _scatter_rows(dest, meta, h2_slabs, rows, tm=512):
    s = dest.shape[1]
    n_sl = h2_slabs.shape[0] // s
    grid_spec = pltpu.PrefetchScalarGridSpec(
        num_scalar_prefetch=2,
        grid=(s // tm,),
        in_specs=[pl.BlockSpec((tm * n_sl, LANES), lambda i, dest_ref, meta_ref: (i, 0))],
        out_specs=pl.BlockSpec(memory_space=pl.ANY),
        scratch_shapes=[
            pltpu.VMEM((BLOCK_ROWS * n_sl, LANES), h2_slabs.dtype),
            pltpu.SemaphoreType.DMA((TOP_K,)),
            pltpu.SemaphoreType.DMA(()),
        ],
    )
    return pl.pallas_call(
        functools.partial(_scatter_body, tm=tm),
        grid_spec=grid_spec,
        out_shape=jax.ShapeDtypeStruct((rows * n_sl, LANES), h2_slabs.dtype),
        compiler_params=_cparams("arbitrary"),
        name="scatter_rows",
    )(dest, meta, h2_slabs)


def _row_gather_wait(src_hbm, dst, sem, n_rows):
    pltpu.make_async_copy(src_hbm.at[pl.ds(0, n_rows)], dst.at[pl.ds(0, n_rows)], sem).wait()


BIG_DMA_PRIORITY = 1


def _experts_body(meta_ref, xs_hbm, wg_hbm, wu_hbm, wd_hbm, ys_hbm,
                  xbuf, ybuf, zbuf, wgbuf, wubuf, wdbuf, gsem, ysem, zsem, wsem):
    used = meta_ref[META_USED_BLKS, 0]
    n_experts, d, _ = wg_hbm.shape
    n_sl = _slab_rows(d)
    blk = BLOCK_ROWS * n_sl
    n_blocks = ys_hbm.shape[0] // blk

    def xs_copy(b, sl):
        src_blk = jnp.minimum(b, n_blocks - 1)
        rows = pl.ds(pl.multiple_of(src_blk * blk, blk), blk)
        return pltpu.make_async_copy(xs_hbm.at[rows], xbuf.at[sl], gsem.at[sl])

    def gather_start(b, sl):
        xs_copy(b, sl).start()

    def gather_wait(sl):
        xs_copy(0, sl).wait()

    def ys_copy(b):
        sl = b % 2
        rows = pl.ds(pl.multiple_of(b * blk, blk), blk)
        return pltpu.make_async_copy(ybuf.at[sl], ys_hbm.at[rows], ysem.at[sl])

    def zero_copy(b):
        rows = pl.ds(pl.multiple_of(b * blk, blk), blk)
        return pltpu.make_async_copy(zbuf, ys_hbm.at[rows], zsem)

    def weight_copies(e, ws):
        return (pltpu.make_async_copy(wg_hbm.at[e], wgbuf.at[ws], wsem.at[ws, 0]),
                pltpu.make_async_copy(wu_hbm.at[e], wubuf.at[ws], wsem.at[ws, 1]),
                pltpu.make_async_copy(wd_hbm.at[e], wdbuf.at[ws], wsem.at[ws, 2]))

    for c in weight_copies(meta_ref[META_FIRST_USED, 0], 0):
        c.start(priority=BIG_DMA_PRIORITY)
    gather_start(0, 0)
    zbuf[...] = jnp.zeros_like(zbuf)
    lax.fori_loop(used, n_blocks, lambda b, _: (zero_copy(b).start(priority=BIG_DMA_PRIORITY), 0)[1], 0)

    def expert(e, _):
        nb = meta_ref[META_N_BLK, e]

        @pl.when(nb > 0)
        def _():
            b0 = meta_ref[META_FIRST_BLK, e]
            ws = meta_ref[META_ORDINAL, e] % 2
            nxt = meta_ref[META_NEXT_EXP, e]

            @pl.when(nxt >= 0)
            def _():
                for c in weight_copies(nxt, 1 - ws):
                    c.start(priority=BIG_DMA_PRIORITY)

            for c in weight_copies(e, ws):
                c.wait()

            def block(j, _):
                b = b0 + j
                sl = b % 2
                gather_start(b + 1, 1 - sl)
                gather_wait(sl)

                @pl.when(b >= 2)
                def _():
                    ys_copy(b - 2).wait()

                x = _load_slabs(xbuf.at[sl], BLOCK_ROWS, n_sl).astype(BF16)
                a = jnp.dot(x, wgbuf[ws].astype(BF16), preferred_element_type=F32)
                u = jnp.dot(x, wubuf[ws].astype(BF16), preferred_element_type=F32)
                act = (a * _sigmoid(a) * u).astype(BF16)
                y = jnp.dot(act, wdbuf[ws].astype(BF16), preferred_element_type=F32)
                _store_slabs(ybuf.at[sl], y)
                ys_copy(b).start(priority=BIG_DMA_PRIORITY)
                return 0

            lax.fori_loop(0, nb, block, 0)

        return 0

    lax.fori_loop(0, n_experts, expert, 0)

    gather_wait(used % 2)
    for back in (1, 2):
        @pl.when(used >= back)
        def _():
            ys_copy(used - back).wait()
    lax.fori_loop(used, n_blocks, lambda b, _: (zero_copy(b).wait(), 0)[1], 0)


def _experts(meta, xs_slabs, w_gate, w_up, w_down):
    n_e, d, de = w_gate.shape
    n_sl = _slab_rows(d)
    blk = BLOCK_ROWS * n_sl
    smem = pl.BlockSpec(memory_space=pltpu.SMEM)
    hbm = pl.BlockSpec(memory_space=pl.ANY)
    return pl.pallas_call(
        _experts_body,
        in_specs=[smem, hbm, hbm, hbm, hbm],
        out_specs=hbm,
        out_shape=jax.ShapeDtypeStruct(xs_slabs.shape, F32),
        scratch_shapes=[
            pltpu.VMEM((2, blk, LANES), F32),
            pltpu.VMEM((2, blk, LANES), F32),
            pltpu.VMEM((blk, LANES), F32),
            pltpu.VMEM((2, d, de), F32),
            pltpu.VMEM((2, d, de), F32),
            pltpu.VMEM((2, de, d), F32),
            pltpu.SemaphoreType.DMA((2,)),
            pltpu.SemaphoreType.DMA((2,)),
            pltpu.SemaphoreType.DMA(()),
            pltpu.SemaphoreType.DMA((2, 3)),
        ],
        compiler_params=pltpu.CompilerParams(vmem_limit_bytes=VMEM_LIMIT_BYTES),
        name="expert_mlp",
    )(meta, xs_slabs, w_gate, w_up, w_down)


def _combine_body(dest_ref, x2_ref, gcol_ref, fg_ref, ys_hbm, o_ref, ybuf, sem, *, tm):
    i = pl.program_id(0)
    n = pl.num_programs(0)
    s = n * tm
    slot = i % 2
    n_sl = _slab_rows(x2_ref.shape[1])

    def start(tile, sl):
        for k in range(TOP_K):
            for r in range(tm):
                row = dest_ref[k * s + tile * tm + r]
                src = ys_hbm.at[pl.ds(pl.multiple_of(row * n_sl, n_sl), n_sl)]
                pltpu.make_async_copy(src, ybuf.at[sl, k, pl.ds(r * n_sl, n_sl)],
                                      sem.at[sl, k]).start(priority=r % 2)

    def wait(sl):
        for k in range(TOP_K):
            _row_gather_wait(ys_hbm, ybuf.at[sl, k], sem.at[sl, k], tm * n_sl)

    @pl.when(i == 0)
    def _():
        start(0, 0)

    start(jnp.minimum(i + 1, n - 1), 1 - slot)
    wait(slot)
    g = gcol_ref[...]
    y0 = _load_slabs(ybuf.at[slot, 0], tm, n_sl)
    y1 = _load_slabs(ybuf.at[slot, 1], tm, n_sl)
    moe = g[:, 0:1] * y0 + g[:, 1:2] * y1
    x3 = x2_ref[...] + moe
    ms = jnp.mean(x3 * x3, axis=-1, keepdims=True)
    o_ref[...] = x3 * lax.rsqrt(ms + RMS_EPS) * fg_ref[...]

    @pl.when(i == n - 1)
    def _():
        wait(1 - slot)


def _combine(dest_flat, x2, gcol, final_g, ys, tm=128):
    s, d = x2.shape
    row = lambda i, dest_ref: (i, 0)
    const = lambda i, dest_ref: (0, 0)
    grid_spec = pltpu.PrefetchScalarGridSpec(
        num_scalar_prefetch=1,
        grid=(s // tm,),
        in_specs=[
            pl.BlockSpec((tm, d), row),
            pl.BlockSpec((tm, LANES), row),
            pl.BlockSpec((1, d), const),
            pl.BlockSpec(memory_space=pl.ANY),
        ],
        out_specs=pl.BlockSpec((tm, d), row),
        scratch_shapes=[
            pltpu.VMEM((2, TOP_K, tm * _slab_rows(d), LANES), F32),
            pltpu.SemaphoreType.DMA((2, TOP_K)),
        ],
    )
    return pl.pallas_call(
        functools.partial(_combine_body, tm=tm),
        grid_spec=grid_spec,
        out_shape=jax.ShapeDtypeStruct((s, d), F32),
        compiler_params=_cparams("arbitrary"),
        name="combine_norm",
    )(dest_flat, x2, gcol, final_g, ys)


def _layer(x, norm1_g, w_in, w_dw, b_dw, ln_g, ln_b, rel_bias, w_out, norm2_g,
           w_group, b_group, w_expert_router, b_expert_router, w_gate, w_up, w_down, out_g):
    s, d = x.shape
    dc = w_dw.shape[-1]
    da = (w_in.shape[1] - 2 * dc) // 3
    row2 = lambda a: a.reshape(1, -1)

    cv, cg, q, k, v = _in_proj(x, row2(norm1_g), w_in.astype(BF16), dc, da)
    conv_out = _conv_module(cv, cg, w_dw.reshape(CONV_WIDTH, dc), row2(b_dw), row2(ln_g), row2(ln_b))
    attn_out = _attention(q, k, v, _distance_rows(rel_bias))

    n_r = N_GROUPS + N_EXPERTS
    wr = jnp.pad(jnp.concatenate([w_group, w_expert_router], axis=1), ((0, 0), (0, LANES - n_r)))
    wr_hi = wr.astype(BF16)
    wr_lo = (wr - wr_hi.astype(F32)).astype(BF16)
    wr_cat = jnp.concatenate([wr_hi, wr_lo], axis=1)
    br_col = jnp.zeros((LANES, 1), F32).at[:n_r, 0].set(jnp.concatenate([b_group, b_expert_router]))
    x2, h2, eids, gcol = _out_proj_router(x, conv_out, attn_out, w_out.astype(BF16), row2(norm2_g),
                                          wr_cat, br_col)

    n_blocks = (s * TOP_K) // BLOCK_ROWS + N_EXPERTS
    assert n_blocks <= META_LANES
    dest, meta = _dispatch(eids)
    xs = _scatter_rows(dest, meta, h2, n_blocks * BLOCK_ROWS)
    ys = _experts(meta, xs, w_gate, w_up, w_down)
    return _combine(dest.reshape(-1), x2, gcol, row2(out_g), ys)


def kernel(x, norm1_g, w_in, w_dw, b_dw, ln_g, ln_b, rel_bias, w_out, norm2_g, w_group, b_group,
           w_expert_router, b_expert_router, w_gate, w_up, w_down, final_g):
    depth = norm1_g.shape[0]
    assert depth == 1, "the final norm is fused into the last layer's combine step"
    batch, s, d = x.shape
    assert batch == 1, "one sequence per call"
    out = _layer(x.reshape(s, d), norm1_g[0], w_in[0], w_dw[0], b_dw[0], ln_g[0], ln_b[0], rel_bias[0],
                 w_out[0], norm2_g[0], w_group[0], b_group[0], w_expert_router[0],
                 b_expert_router[0], w_gate[0], w_up[0], w_down[0], final_g)
    return out.reshape(batch, s, d)
```

```python
import functools

import jax
import jax.numpy as jnp
from jax import lax
from jax.experimental import pallas as pl
from jax.experimental.pallas import tpu as pltpu

F32 = jnp.float32
BF16 = jnp.bfloat16
I32 = jnp.int32

CHUNK = 64
HEAD_DIM = 64
LEFT_CHUNKS = 8
REL_CLIP = 128
CONV_WIDTH = 31
N_GROUPS = 8
EXPERTS_PER_GROUP = 8
N_EXPERTS = N_GROUPS * EXPERTS_PER_GROUP
TOP_K = 2
BLOCK_ROWS = 128
RMS_EPS = 1e-6
LN_EPS = 1e-5
NEG_INF = -1e30

VMEM_LIMIT_BYTES = 56 * 1024 * 1024
LANES = 128
SUBLANES = 8
PAIR = 2 * CHUNK
BAND_PAIR = (LEFT_CHUNKS + 2) * CHUNK
FRONT_PAD = LEFT_CHUNKS * CHUNK
CONV_HALO = 32


def _cparams(*sem):
    return pltpu.CompilerParams(dimension_semantics=sem, vmem_limit_bytes=VMEM_LIMIT_BYTES)


def _inproj_body(x_ref, g_ref, w_hbm, cv_ref, cg_ref, q_ref, k_ref, v_ref, w_ref, stage, wsem):
    @pl.when(pl.program_id(0) == 0)
    def _():
        n_chunks = w_ref.shape[1] // stage.shape[2]
        cw = stage.shape[2]

        def chunk_copy(c):
            return pltpu.make_async_copy(w_hbm.at[:, c * cw:(c + 1) * cw], stage.at[c % 2], wsem.at[c % 2])

        chunk_copy(0).start()
        for c in range(n_chunks):
            if c + 1 < n_chunks:
                chunk_copy(c + 1).start()
            chunk_copy(c).wait()
            w_ref[:, c * cw:(c + 1) * cw] = stage[c % 2].astype(BF16)

    x = x_ref[...]
    ms = jnp.mean(x * x, axis=-1, keepdims=True)
    h = (x * lax.rsqrt(ms + RMS_EPS) * g_ref[...]).astype(BF16)
    dc = cv_ref.shape[1]
    da = q_ref.shape[1]

    def proj(lo, width):
        return jnp.dot(h, w_ref[:, lo:lo + width], preferred_element_type=F32)

    cv_ref[...] = proj(0, dc)
    cg_ref[...] = proj(dc, dc)
    q_ref[...] = (proj(2 * dc, da) * (HEAD_DIM ** -0.5)).astype(BF16)
    k_ref[...] = proj(2 * dc + da, da).astype(BF16)
    v_ref[...] = proj(2 * dc + 2 * da, da).astype(BF16)


def _in_proj(x, g, w, d_conv, d_att, tm=256, w_chunk=512):
    s, d = x.shape
    d_in = w.shape[1]
    row = lambda i: (i, 0)
    const = lambda i: (0, 0)
    return pl.pallas_call(
        _inproj_body,
        grid=(s // tm,),
        in_specs=[
            pl.BlockSpec((tm, d), row),
            pl.BlockSpec((1, d), const),
            pl.BlockSpec(memory_space=pl.ANY),
        ],
        out_specs=[
            pl.BlockSpec((tm, d_conv), row),
            pl.BlockSpec((tm, d_conv), row),
            pl.BlockSpec((tm, d_att), row),
            pl.BlockSpec((tm, d_att), row),
            pl.BlockSpec((tm, d_att), row),
        ],
        out_shape=[
            jax.ShapeDtypeStruct((s, d_conv), F32),
            jax.ShapeDtypeStruct((s, d_conv), F32),
            jax.ShapeDtypeStruct((s, d_att), BF16),
            jax.ShapeDtypeStruct((s, d_att), BF16),
            jax.ShapeDtypeStruct((s, d_att), BF16),
        ],
        scratch_shapes=[
            pltpu.VMEM((d, d_in), BF16),
            pltpu.VMEM((2, d, w_chunk), F32),
            pltpu.SemaphoreType.DMA((2,)),
        ],
        compiler_params=_cparams("arbitrary"),
        name="in_proj",
    )(x, g, w)


def _sigmoid(x):
    return 1.0 / (1.0 + jnp.exp(-x))


def _conv_body(cv_ref, cg_ref, cvh_ref, cgh_ref, w_ref, b_ref, lg_ref, lb_ref, o_ref,
               u_ref, us_ref, wb_ref, conv_ref, *, rows_per_chunk, groups_per_pass):
    i = pl.program_id(0)
    t, dc = cv_ref.shape

    @pl.when(i == 0)
    def _():
        for w in range(CONV_WIDTH):
            wb_ref[w] = jnp.broadcast_to(w_ref[w:w + 1, :], (SUBLANES, dc))

    halo = cvh_ref[...] * _sigmoid(cgh_ref[...])
    u_ref[0:CONV_HALO, :] = jnp.where(i > 0, halo, 0.0)
    u_ref[CONV_HALO:, :] = cv_ref[...] * _sigmoid(cg_ref[...])
    n_shift_rows = us_ref.shape[1]
    for sft in range(1, SUBLANES):
        us_ref[sft - 1] = u_ref[sft:sft + n_shift_rows, :]
    first_tap = CONV_HALO - (CONV_WIDTH - 1)

    def lane_tile(lt, _):
        lanes = pl.ds(pl.multiple_of(lt * LANES, LANES), LANES)
        wv = [wb_ref[w, :, lanes] for w in range(CONV_WIDTH)]
        bias = jnp.broadcast_to(b_ref[:, lanes], (SUBLANES, LANES))
        for g0 in range(0, t // SUBLANES, groups_per_pass):
            accs = [bias] * groups_per_pass
            for w in range(CONV_WIDTH):
                off = first_tap + w
                sft = off % SUBLANES
                for g in range(groups_per_pass):
                    lo = (g0 + g) * SUBLANES + off - sft
                    if sft == 0:
                        src = u_ref[lo:lo + SUBLANES, lanes]
                    else:
                        src = us_ref[sft - 1, lo:lo + SUBLANES, lanes]
                    accs[g] = accs[g] + wv[w] * src
            for g in range(groups_per_pass):
                lo = (g0 + g) * SUBLANES
                conv_ref[lo:lo + SUBLANES, lanes] = accs[g]
        return 0

    lax.fori_loop(0, dc // LANES, lane_tile, 0)

    r = rows_per_chunk
    for c in range(t // r):
        acc = conv_ref[c * r:(c + 1) * r, :]
        mu = jnp.mean(acc, axis=-1, keepdims=True)
        cen = acc - mu
        var = jnp.mean(cen * cen, axis=-1, keepdims=True)
        un = cen * lax.rsqrt(var + LN_EPS) * lg_ref[...] + lb_ref[...]
        o_ref[c * r:(c + 1) * r, :] = (un * _sigmoid(un)).astype(o_ref.dtype)


def _conv_module(cv, cg, w_dw, b_dw, ln_g, ln_b, t=512, rows_per_chunk=16, groups_per_pass=4):
    s, dc = cv.shape
    halo_blocks = t // CONV_HALO
    row = lambda i: (i, 0)
    prev = lambda i: (jnp.maximum(i * halo_blocks - 1, 0), 0)
    const = lambda i: (0, 0)
    return pl.pallas_call(
        functools.partial(_conv_body, rows_per_chunk=rows_per_chunk, groups_per_pass=groups_per_pass),
        grid=(s // t,),
        in_specs=[
            pl.BlockSpec((t, dc), row),
            pl.BlockSpec((t, dc), row),
            pl.BlockSpec((CONV_HALO, dc), prev),
            pl.BlockSpec((CONV_HALO, dc), prev),
            pl.BlockSpec((CONV_WIDTH, dc), const),
            pl.BlockSpec((1, dc), const),
            pl.BlockSpec((1, dc), const),
            pl.BlockSpec((1, dc), const),
        ],
        out_specs=pl.BlockSpec((t, dc), row),
        out_shape=jax.ShapeDtypeStruct((s, dc), BF16),
        scratch_shapes=[
            pltpu.VMEM((t + CONV_HALO, dc), F32),
            pltpu.VMEM((SUBLANES - 1, t + CONV_HALO - SUBLANES, dc), F32),
            pltpu.VMEM((CONV_WIDTH, SUBLANES, dc), F32),
            pltpu.VMEM((t, dc), F32),
        ],
        compiler_params=_cparams("arbitrary"),
        name="conv_module",
    )(cv, cg, cv, cg, w_dw, b_dw, ln_g, ln_b)


SOFTMAX_ROWS = 32


def _attn_pairs(q_ref, bands, bias_ref, s_ref, e_ref, o_ref):
    lane = lax.broadcasted_iota(I32, (PAIR, 2 * HEAD_DIM), 1)
    n_pairs = len(bands)
    rdens = {}

    def scores(u):
        kb, _, _ = bands[u]
        q2 = q_ref[u * PAIR:(u + 1) * PAIR, :]
        for h in range(2):
            in_head = (lane >= h * HEAD_DIM) & (lane < (h + 1) * HEAD_DIM)
            qh = jnp.where(in_head, q2, jnp.zeros_like(q2))
            s_ref[2 * u + h, :, :kb.shape[0]] = lax.dot_general(
                qh, kb, (((1,), (1,)), ((), ())), preferred_element_type=F32)

    def softmax(u):
        kb, _, col_lo = bands[u]
        n_keys = kb.shape[0]
        for h in range(2):
            c = 2 * u + h
            parts = []
            for r0 in range(0, PAIR, SOFTMAX_ROWS):
                rows = slice(r0, r0 + SOFTMAX_ROWS)
                t = s_ref[c, rows, :n_keys] + bias_ref[h, rows, col_lo:col_lo + n_keys]
                e = jnp.exp(t - jnp.max(t, axis=-1, keepdims=True))
                parts.append(1.0 / jnp.sum(e, axis=-1, keepdims=True))
                e_ref[c, rows, :n_keys] = e.astype(BF16)
            rdens[c] = jnp.concatenate(parts, axis=0)

    def values(u):
        _, vb, _ = bands[u]
        n_keys = vb.shape[0]
        outs = []
        for h in range(2):
            c = 2 * u + h
            o = jnp.dot(e_ref[c, :, :n_keys], vb, preferred_element_type=F32)
            outs.append(o * rdens[c])
        o_ref[u * PAIR:(u + 1) * PAIR, :] = jnp.where(lane < HEAD_DIM, outs[0], outs[1]).astype(o_ref.dtype)

    for step in range(n_pairs + 2):
        if step < n_pairs:
            scores(step)
        if 0 <= step - 1 < n_pairs:
            softmax(step - 1)
        if 0 <= step - 2 < n_pairs:
            values(step - 2)


def _attn_body(q_ref, k_ref, v_ref, g_ref, o_ref, bias_ref, s_ref, e_ref, *, pairs_per_step):
    st = pl.program_id(1)
    npp = pairs_per_step

    @pl.when(st == 0)
    def _():
        sub = lax.broadcasted_iota(I32, (SUBLANES, BAND_PAIR), 0)
        kj = lax.broadcasted_iota(I32, (SUBLANES, BAND_PAIR), 1)
        for h in range(2):
            t8 = jnp.broadcast_to(g_ref[h], (SUBLANES, BAND_PAIR))
            shift = 1
            while shift < SUBLANES:
                t8 = jnp.where((sub & shift) != 0, pltpu.roll(t8, shift, 1), t8)
                shift *= 2
            for r0 in range(0, PAIR, SUBLANES):
                band_lo = (r0 // CHUNK) * CHUNK
                in_band = (kj >= band_lo) & (kj < band_lo + (LEFT_CHUNKS + 1) * CHUNK)
                rows = t8 if r0 == 0 else pltpu.roll(t8, r0, 1)
                bias_ref[h, r0:r0 + SUBLANES, :] = jnp.where(in_band, rows, NEG_INF)

    def run(first_step):
        bands = []
        for u in range(npp):
            if first_step:
                n_keys = min((u + 1) * PAIR, BAND_PAIR)
                lo = max((u + 1) * PAIR - BAND_PAIR, 0)
                bands.append((k_ref[lo:lo + n_keys, :], v_ref[lo:lo + n_keys, :], BAND_PAIR - n_keys))
            else:
                start = pl.multiple_of((st * npp + u) * PAIR - FRONT_PAD, PAIR)
                bands.append((k_ref[pl.ds(start, BAND_PAIR), :], v_ref[pl.ds(start, BAND_PAIR), :], 0))
        _attn_pairs(q_ref, bands, bias_ref, s_ref, e_ref, o_ref)

    @pl.when(st == 0)
    def _():
        run(True)

    @pl.when(st > 0)
    def _():
        run(False)


def _distance_rows(rel_table):
    t = rel_table.astype(F32)
    far = t[:, 2 * REL_CLIP:]
    n_far = LEFT_CHUNKS * CHUNK - REL_CLIP + 1
    near = jnp.flip(t[:, REL_CLIP - CHUNK + 1:2 * REL_CLIP], axis=1)
    tail = BAND_PAIR - n_far - near.shape[1]
    g = jnp.concatenate([jnp.repeat(far, n_far, axis=1), near, jnp.repeat(far, tail, axis=1)], axis=1)
    return g[:, None, :]


def _attention(q, k, v, g_rows, pairs_per_step=8):
    s, da = q.shape
    n_hp = da // (2 * HEAD_DIM)
    tq = pairs_per_step * PAIR
    assert pairs_per_step * PAIR >= FRONT_PAD, "first step must cover every clamped band"
    return pl.pallas_call(
        functools.partial(_attn_body, pairs_per_step=pairs_per_step),
        grid=(n_hp, s // tq),
        in_specs=[
            pl.BlockSpec((tq, 2 * HEAD_DIM), lambda hp, st: (st, hp)),
            pl.BlockSpec((s, 2 * HEAD_DIM), lambda hp, st: (0, hp)),
            pl.BlockSpec((s, 2 * HEAD_DIM), lambda hp, st: (0, hp)),
            pl.BlockSpec((2, 1, BAND_PAIR), lambda hp, st: (hp, 0, 0)),
        ],
        out_specs=pl.BlockSpec((tq, 2 * HEAD_DIM), lambda hp, st: (st, hp)),
        out_shape=jax.ShapeDtypeStruct((s, da), BF16),
        scratch_shapes=[
            pltpu.VMEM((2, PAIR, BAND_PAIR), F32),
            pltpu.VMEM((2 * pairs_per_step, PAIR, BAND_PAIR), F32),
            pltpu.VMEM((2 * pairs_per_step, PAIR, BAND_PAIR), BF16),
        ],
        compiler_params=_cparams("arbitrary", "arbitrary"),
        name="chunk_attention",
    )(q, k, v, g_rows)


def _slab_rows(d):
    return d // LANES


def _store_slabs(ref, val):
    t, d = val.shape
    n = _slab_rows(d)
    for c in range(n):
        ref[pl.ds(c, t, stride=n), :] = val[:, c * LANES:(c + 1) * LANES]


def _load_slabs(ref, t, n):
    return jnp.concatenate([ref[pl.ds(c, t, stride=n), :] for c in range(n)], axis=1)


def _first_index_of_max(vals, sub, n):
    vmax = jnp.max(vals, axis=0, keepdims=True)
    idx = jnp.min(jnp.where(vals == vmax, sub, n), axis=0, keepdims=True)
    return vmax, idx


def _outproj_body(x_ref, co_ref, ao_ref, w_ref, g2_ref, wr_ref, br_ref,
                  x2_ref, h2_ref, eid_ref, gcol_ref):
    dc = co_ref.shape[1]
    y = jnp.dot(co_ref[...], w_ref[0:dc, :], preferred_element_type=F32)
    y = y + jnp.dot(ao_ref[...], w_ref[dc:, :], preferred_element_type=F32)
    x2 = x_ref[...] + y
    ms = jnp.mean(x2 * x2, axis=-1, keepdims=True)
    h2 = x2 * lax.rsqrt(ms + RMS_EPS) * g2_ref[...]
    x2_ref[...] = x2
    _store_slabs(h2_ref, h2)

    hi = h2.astype(BF16)
    lo = (h2 - hi.astype(F32)).astype(BF16)
    pa = jnp.dot(hi, wr_ref[...], preferred_element_type=F32)
    pb = jnp.dot(lo, wr_ref[...], preferred_element_type=F32)
    logits = pa[:, :LANES] + pa[:, LANES:] + pb[:, :LANES]
    lt = logits.T + br_ref[...]
    tm = lt.shape[1]
    sub = lax.broadcasted_iota(I32, (N_GROUPS, tm), 0)
    g = lt[0:N_GROUPS, :]
    gmax, grp = _first_index_of_max(g, sub, N_GROUPS)
    p_grp = 1.0 / jnp.sum(jnp.exp(g - gmax), axis=0, keepdims=True)
    sel = jnp.zeros((EXPERTS_PER_GROUP, tm), F32)
    for gi in range(N_GROUPS):
        lo = N_GROUPS + gi * EXPERTS_PER_GROUP
        sel = jnp.where(grp == gi, lt[lo:lo + EXPERTS_PER_GROUP, :], sel)
    v1, i1 = _first_index_of_max(sel, sub, EXPERTS_PER_GROUP)
    rest = jnp.where(sub == i1, -jnp.inf, sel)
    v2, i2 = _first_index_of_max(rest, sub, EXPERTS_PER_GROUP)
    t = jnp.exp(v2 - v1)
    gate1 = p_grp / (1.0 + t)
    gate2 = p_grp * t / (1.0 + t)
    eid_ref[0:1, :] = grp * EXPERTS_PER_GROUP + i1
    eid_ref[1:2, :] = grp * EXPERTS_PER_GROUP + i2
    row = lax.broadcasted_iota(I32, (LANES, tm), 0)
    gl = jnp.where(row == 0, gate1, jnp.where(row == 1, gate2, 0.0))
    gcol_ref[...] = gl.T


def _out_proj_router(x, conv_out, attn_out, w_bf16, g2, wr_cat, br_col, tm=512):
    s, d = x.shape
    dc = conv_out.shape[1]
    da = attn_out.shape[1]
    row = lambda i: (i, 0)
    const = lambda i: (0, 0)
    return pl.pallas_call(
        _outproj_body,
        grid=(s // tm,),
        in_specs=[
            pl.BlockSpec((tm, d), row),
            pl.BlockSpec((tm, dc), row),
            pl.BlockSpec((tm, da), row),
            pl.BlockSpec((dc + da, d), const, pipeline_mode=pl.Buffered(1)),
            pl.BlockSpec((1, d), const),
            pl.BlockSpec((d, 2 * LANES), const),
            pl.BlockSpec((LANES, 1), const),
        ],
        out_specs=[
            pl.BlockSpec((tm, d), row),
            pl.BlockSpec((tm * _slab_rows(d), LANES), row),
            pl.BlockSpec((TOP_K, tm), lambda i: (0, i)),
            pl.BlockSpec((tm, LANES), row),
        ],
        out_shape=[
            jax.ShapeDtypeStruct((s, d), F32),
            jax.ShapeDtypeStruct((s * _slab_rows(d), LANES), F32),
            jax.ShapeDtypeStruct((TOP_K, s), I32),
            jax.ShapeDtypeStruct((s, LANES), F32),
        ],
        compiler_params=_cparams("arbitrary"),
        name="out_proj_router",
    )(x, conv_out, attn_out, w_bf16, g2, wr_cat, br_col)


META_LANES = 256
(META_FIRST_BLK, META_N_BLK, META_COUNT, META_NEXT_EXP, META_ORDINAL,
 META_USED_BLKS, META_FIRST_USED) = range(7)


def _dispatch_body(eid_ref, dest_ref, meta_ref, rank_ref):
    s = eid_ref.shape[1]
    w = META_LANES
    sub = lax.broadcasted_iota(I32, (N_EXPERTS, w), 0)
    lane = lax.broadcasted_iota(I32, (N_EXPERTS, w), 1)
    ri = lax.broadcasted_iota(I32, (w, w), 0)
    ci = lax.broadcasted_iota(I32, (w, w), 1)
    upper = (ri < ci).astype(BF16)

    def onehots(b):
        off = pl.multiple_of(b * w, w)
        e0 = eid_ref[0:1, pl.ds(off, w)]
        e1 = eid_ref[1:2, pl.ds(off, w)]
        return off, sub == e0, sub == e1

    def rank_block(b, carry):
        off, o0, o1 = onehots(b)
        both = jnp.where(o0 | o1, 1.0, 0.0)
        pre = jnp.dot(both.astype(BF16), upper, preferred_element_type=F32) + carry
        rank_ref[0:1, pl.ds(off, w)] = jnp.sum(jnp.where(o0, pre, 0.0), axis=0, keepdims=True)
        rank_ref[1:2, pl.ds(off, w)] = jnp.sum(jnp.where(o1, pre, 0.0), axis=0, keepdims=True)
        return carry + jnp.sum(both, axis=1, keepdims=True)

    counts = lax.fori_loop(0, s // w, rank_block, jnp.zeros((N_EXPERTS, w), F32))

    nblk = ((counts.astype(I32) + (BLOCK_ROWS - 1)) // BLOCK_ROWS).astype(F32)
    er = lax.broadcasted_iota(I32, (N_EXPERTS, N_EXPERTS), 0)
    ec = lax.broadcasted_iota(I32, (N_EXPERTS, N_EXPERTS), 1)
    lower_incl = (ec <= er).astype(BF16)
    pend_blk = jnp.dot(lower_incl, nblk.astype(BF16), preferred_element_type=F32)
    pstart = (pend_blk - nblk) * float(BLOCK_ROWS)

    def on_lanes(v):
        return jnp.sum(jnp.where(sub == lane, v, 0), axis=0, keepdims=True)

    nblk_i = nblk.astype(I32)
    first_blk = on_lanes((pend_blk - nblk).astype(I32))
    n_blk = on_lanes(nblk_i)
    count = on_lanes(counts.astype(I32))
    has_rows = nblk_i > 0
    next_exp = jnp.min(jnp.where(has_rows & (sub > lane), sub, N_EXPERTS), axis=0, keepdims=True)
    next_exp = jnp.where(next_exp < N_EXPERTS, next_exp, -1)
    ordinal = jnp.sum(jnp.where(has_rows & (sub < lane), 1, 0), axis=0, keepdims=True)
    first_used = jnp.min(jnp.where(has_rows, sub, N_EXPERTS - 1), axis=0, keepdims=True)
    used = pend_blk[N_EXPERTS - 1:N_EXPERTS, :].astype(I32)
    mrow = lax.broadcasted_iota(I32, (8, w), 0)
    meta = jnp.where(mrow == META_FIRST_BLK, first_blk, 0)
    meta = jnp.where(mrow == META_N_BLK, n_blk, meta)
    meta = jnp.where(mrow == META_COUNT, count, meta)
    meta = jnp.where(mrow == META_NEXT_EXP, next_exp, meta)
    meta = jnp.where(mrow == META_ORDINAL, ordinal, meta)
    meta = jnp.where(mrow == META_FIRST_USED, first_used, meta)
    meta_ref[...] = jnp.where(mrow == META_USED_BLKS, used, meta)

    def dest_block(b, _):
        off, o0, o1 = onehots(b)
        d0 = rank_ref[0:1, pl.ds(off, w)] + jnp.sum(jnp.where(o0, pstart, 0.0), axis=0, keepdims=True)
        d1 = rank_ref[1:2, pl.ds(off, w)] + jnp.sum(jnp.where(o1, pstart, 0.0), axis=0, keepdims=True)
        dest_ref[0:1, pl.ds(off, w)] = d0.astype(I32)
        dest_ref[1:2, pl.ds(off, w)] = d1.astype(I32)
        return 0

    lax.fori_loop(0, s // w, dest_block, 0)


def _dispatch(eids):
    s = eids.shape[1]
    return pl.pallas_call(
        _dispatch_body,
        out_shape=[
            jax.ShapeDtypeStruct((TOP_K, s), I32),
            jax.ShapeDtypeStruct((8, META_LANES), I32),
        ],
        scratch_shapes=[pltpu.VMEM((TOP_K, s), F32)],
        compiler_params=pltpu.CompilerParams(vmem_limit_bytes=VMEM_LIMIT_BYTES),
        name="dispatch",
    )(eids)


PAD_BITS = tuple(reversed(range(BLOCK_ROWS.bit_length() - 1)))


def _scatter_body(dest_ref, meta_ref, h2_ref, xs_hbm, zbuf, sem, zsem, *, tm):
    i = pl.program_id(0)
    n_sl = h2_ref.shape[0] // tm
    blk = BLOCK_ROWS * n_sl
    n_blocks = xs_hbm.shape[0] // blk
    used = meta_ref[META_USED_BLKS, 0]

    def slab(row, n_tokens=1):
        return pl.ds(pl.multiple_of(row * n_sl, n_sl), n_tokens * n_sl)

    def token_copy(j, k):
        t = i * tm + j
        return pltpu.make_async_copy(h2_ref.at[slab(j)], xs_hbm.at[slab(dest_ref[k, t])], sem.at[k])

    def scatter_tile():
        def one(j, _):
            for k in range(TOP_K):
                token_copy(j, k).start(priority=k)
            return 0

        lax.fori_loop(0, tm, one, 0, unroll=8)

    def wait_tile():
        for k in range(TOP_K):
            pltpu.make_async_copy(h2_ref, xs_hbm.at[slab(0, tm)], sem.at[k]).wait()

    @pl.when(i > 0)
    def _():
        scatter_tile()
        wait_tile()

    @pl.when(i == 0)
    def _():
        _scatter_first_step(meta_ref, xs_hbm, zbuf, zsem, slab, used, n_blocks, scatter_tile, wait_tile)


def _scatter_first_step(meta_ref, xs_hbm, zbuf, zsem, slab, used, n_blocks, scatter_tile, wait_tile):
    n_sl = zbuf.shape[0] // BLOCK_ROWS

    def zero_copy(row, n_tokens):
        return pltpu.make_async_copy(zbuf.at[pl.ds(0, n_tokens * n_sl)], xs_hbm.at[slab(row, n_tokens)], zsem)

    zbuf[...] = jnp.zeros_like(zbuf)

    def per_expert(e, issued):
        b0 = meta_ref[META_FIRST_BLK, e]
        nb = meta_ref[META_N_BLK, e]
        cnt = meta_ref[META_COUNT, e]
        pad = nb * BLOCK_ROWS - cnt
        row = b0 * BLOCK_ROWS + cnt
        out = []
        for bit, n_issued in zip(PAD_BITS, issued):
            size = 1 << bit
            take = pad & size

            @pl.when(take != 0)
            def _():
                zero_copy(row, size).start()

            row = row + take
            out.append(n_issued + take // size)
        return tuple(out)

    issued = lax.fori_loop(0, N_EXPERTS, per_expert, tuple(jnp.int32(0) for _ in PAD_BITS))
    lax.fori_loop(used, n_blocks, lambda b, _: (zero_copy(b * BLOCK_ROWS, BLOCK_ROWS).start(), 0)[1], 0)

    scatter_tile()
    wait_tile()

    for bit, n_issued in zip(PAD_BITS, issued):
        lax.fori_loop(0, n_issued, lambda i, _, n=1 << bit: (zero_copy(0, n).wait(), 0)[1], 0)
    lax.fori_loop(used, n_blocks, lambda b, _: (zero_copy(0, BLOCK_ROWS).wait(), 0)[1], 0)


def _scatter_rows(dest, meta, h2_slabs, rows, tm=512):
    s = dest.shape[1]
    n_sl = h2_slabs.shape[0] // s
    grid_spec = pltpu.PrefetchScalarGridSpec(
        num_scalar_prefetch=2,
        grid=(s // tm,),
        in_specs=[pl.BlockSpec((tm * n_sl, LANES), lambda i, dest_ref, meta_ref: (i, 0))],
        out_specs=pl.BlockSpec(memory_space=pl.ANY),
        scratch_shapes=[
            pltpu.VMEM((BLOCK_ROWS * n_sl, LANES), h2_slabs.dtype),
            pltpu.SemaphoreType.DMA((TOP_K,)),
            pltpu.SemaphoreType.DMA(()),
        ],
    )
    return pl.pallas_call(
        functools.partial(_scatter_body, tm=tm),
        grid_spec=grid_spec,
        out_shape=jax.ShapeDtypeStruct((rows * n_sl, LANES), h2_slabs.dtype),
        compiler_params=_cparams("arbitrary"),
        name="scatter_rows",
    )(dest, meta, h2_slabs)


def _row_gather_wait(src_hbm, dst, sem, n_rows):
    pltpu.make_async_copy(src_hbm.at[pl.ds(0, n_rows)], dst.at[pl.ds(0, n_rows)], sem).wait()


WEIGHT_QUEUES = (0, 1, 0)
ROW_BLOCK_QUEUE = 1


def _experts_body(meta_ref, xs_hbm, wg_hbm, wu_hbm, wd_hbm, ys_hbm,
                  xbuf, ybuf, zbuf, wgbuf, wubuf, wdbuf, gsem, ysem, zsem, wsem):
    used = meta_ref[META_USED_BLKS, 0]
    n_experts, d, _ = wg_hbm.shape
    n_sl = _slab_rows(d)
    blk = BLOCK_ROWS * n_sl
    n_blocks = ys_hbm.shape[0] // blk

    def xs_copy(b, sl):
        src_blk = jnp.minimum(b, n_blocks - 1)
        rows = pl.ds(pl.multiple_of(src_blk * blk, blk), blk)
        return pltpu.make_async_copy(xs_hbm.at[rows], xbuf.at[sl], gsem.at[sl])

    def gather_start(b, sl):
        xs_copy(b, sl).start(priority=ROW_BLOCK_QUEUE)

    def weights_start(e, ws):
        for c, queue in zip(weight_copies(e, ws), WEIGHT_QUEUES):
            c.start(priority=queue)

    def gather_wait(sl):
        xs_copy(0, sl).wait()

    def ys_copy(b):
        sl = b % 2
        rows = pl.ds(pl.multiple_of(b * blk, blk), blk)
        return pltpu.make_async_copy(ybuf.at[sl], ys_hbm.at[rows], ysem.at[sl])

    def zero_copy(b):
        rows = pl.ds(pl.multiple_of(b * blk, blk), blk)
        return pltpu.make_async_copy(zbuf, ys_hbm.at[rows], zsem)

    def weight_copies(e, ws):
        return (pltpu.make_async_copy(wg_hbm.at[e], wgbuf.at[ws], wsem.at[ws, 0]),
                pltpu.make_async_copy(wu_hbm.at[e], wubuf.at[ws], wsem.at[ws, 1]),
                pltpu.make_async_copy(wd_hbm.at[e], wdbuf.at[ws], wsem.at[ws, 2]))

    weights_start(meta_ref[META_FIRST_USED, 0], 0)
    gather_start(0, 0)
    zbuf[...] = jnp.zeros_like(zbuf)
    lax.fori_loop(used, n_blocks, lambda b, _: (zero_copy(b).start(priority=ROW_BLOCK_QUEUE), 0)[1], 0)

    def expert(e, _):
        nb = meta_ref[META_N_BLK, e]

        @pl.when(nb > 0)
        def _():
            b0 = meta_ref[META_FIRST_BLK, e]
            ws = meta_ref[META_ORDINAL, e] % 2
            nxt = meta_ref[META_NEXT_EXP, e]

            @pl.when(nxt >= 0)
            def _():
                weights_start(nxt, 1 - ws)

            for c in weight_copies(e, ws):
                c.wait()

            def block(j, _):
                b = b0 + j
                sl = b % 2
                gather_start(b + 1, 1 - sl)
                gather_wait(sl)

                @pl.when(b >= 2)
                def _():
                    ys_copy(b - 2).wait()

                x = _load_slabs(xbuf.at[sl], BLOCK_ROWS, n_sl).astype(BF16)
                a = jnp.dot(x, wgbuf[ws].astype(BF16), preferred_element_type=F32)
                u = jnp.dot(x, wubuf[ws].astype(BF16), preferred_element_type=F32)
                act = (a * _sigmoid(a) * u).astype(BF16)
                y = jnp.dot(act, wdbuf[ws].astype(BF16), preferred_element_type=F32)
                _store_slabs(ybuf.at[sl], y)
                ys_copy(b).start(priority=ROW_BLOCK_QUEUE)
                return 0

            lax.fori_loop(0, nb, block, 0)

        return 0

    lax.fori_loop(0, n_experts, expert, 0)

    gather_wait(used % 2)
    for back in (1, 2):
        @pl.when(used >= back)
        def _():
            ys_copy(used - back).wait()
    lax.fori_loop(used, n_blocks, lambda b, _: (zero_copy(b).wait(), 0)[1], 0)


def _experts(meta, xs_slabs, w_gate, w_up, w_down):
    n_e, d, de = w_gate.shape
    n_sl = _slab_rows(d)
    blk = BLOCK_ROWS * n_sl
    smem = pl.BlockSpec(memory_space=pltpu.SMEM)
    hbm = pl.BlockSpec(memory_space=pl.ANY)
    return pl.pallas_call(
        _experts_body,
        in_specs=[smem, hbm, hbm, hbm, hbm],
        out_specs=hbm,
        out_shape=jax.ShapeDtypeStruct(xs_slabs.shape, F32),
        scratch_shapes=[
            pltpu.VMEM((2, blk, LANES), F32),
            pltpu.VMEM((2, blk, LANES), F32),
            pltpu.VMEM((blk, LANES), F32),
            pltpu.VMEM((2, d, de), F32),
            pltpu.VMEM((2, d, de), F32),
            pltpu.VMEM((2, de, d), F32),
            pltpu.SemaphoreType.DMA((2,)),
            pltpu.SemaphoreType.DMA((2,)),
            pltpu.SemaphoreType.DMA(()),
            pltpu.SemaphoreType.DMA((2, 3)),
        ],
        compiler_params=pltpu.CompilerParams(vmem_limit_bytes=VMEM_LIMIT_BYTES),
        name="expert_mlp",
    )(meta, xs_slabs, w_gate, w_up, w_down)


def _combine_body(dest_ref, x2_ref, gcol_ref, fg_ref, ys_hbm, o_ref, ybuf, sem, *, tm):
    i = pl.program_id(0)
    n = pl.num_programs(0)
    s = n * tm
    slot = i % 2
    n_sl = _slab_rows(x2_ref.shape[1])

    def start(tile, sl):
        for k in range(TOP_K):
            for r in range(tm):
                row = dest_ref[k * s + tile * tm + r]
                src = ys_hbm.at[pl.ds(pl.multiple_of(row * n_sl, n_sl), n_sl)]
                pltpu.make_async_copy(src, ybuf.at[sl, k, pl.ds(r * n_sl, n_sl)],
                                      sem.at[sl, k]).start(priority=r % 2)

    def wait(sl):
        for k in range(TOP_K):
            _row_gather_wait(ys_hbm, ybuf.at[sl, k], sem.at[sl, k], tm * n_sl)

    @pl.when(i == 0)
    def _():
        start(0, 0)

    start(jnp.minimum(i + 1, n - 1), 1 - slot)
    wait(slot)
    g = gcol_ref[...]
    y0 = _load_slabs(ybuf.at[slot, 0], tm, n_sl)
    y1 = _load_slabs(ybuf.at[slot, 1], tm, n_sl)
    moe = g[:, 0:1] * y0 + g[:, 1:2] * y1
    x3 = x2_ref[...] + moe
    ms = jnp.mean(x3 * x3, axis=-1, keepdims=True)
    o_ref[...] = x3 * lax.rsqrt(ms + RMS_EPS) * fg_ref[...]

    @pl.when(i == n - 1)
    def _():
        wait(1 - slot)


def _combine(dest_flat, x2, gcol, final_g, ys, tm=128):
    s, d = x2.shape
    row = lambda i, dest_ref: (i, 0)
    const = lambda i, dest_ref: (0, 0)
    grid_spec = pltpu.PrefetchScalarGridSpec(
        num_scalar_prefetch=1,
        grid=(s // tm,),
        in_specs=[
            pl.BlockSpec((tm, d), row),
            pl.BlockSpec((tm, LANES), row),
            pl.BlockSpec((1, d), const),
            pl.BlockSpec(memory_space=pl.ANY),
        ],
        out_specs=pl.BlockSpec((tm, d), row),
        scratch_shapes=[
            pltpu.VMEM((2, TOP_K, tm * _slab_rows(d), LANES), F32),
            pltpu.SemaphoreType.DMA((2, TOP_K)),
        ],
    )
    return pl.pallas_call(
        functools.partial(_combine_body, tm=tm),
        grid_spec=grid_spec,
        out_shape=jax.ShapeDtypeStruct((s, d), F32),
        compiler_params=_cparams("arbitrary"),
        name="combine_norm",
    )(dest_flat, x2, gcol, final_g, ys)


def _layer(x, norm1_g, w_in, w_dw, b_dw, ln_g, ln_b, rel_bias, w_out, norm2_g,
           w_group, b_group, w_expert_router, b_expert_router, w_gate, w_up, w_down, out_g):
    s, d = x.shape
    dc = w_dw.shape[-1]
    da = (w_in.shape[1] - 2 * dc) // 3
    row2 = lambda a: a.reshape(1, -1)

    cv, cg, q, k, v = _in_proj(x, row2(norm1_g), w_in, dc, da)
    conv_out = _conv_module(cv, cg, w_dw.reshape(CONV_WIDTH, dc), row2(b_dw), row2(ln_g), row2(ln_b))
    attn_out = _attention(q, k, v, _distance_rows(rel_bias))

    n_r = N_GROUPS + N_EXPERTS
    wr = jnp.pad(jnp.concatenate([w_group, w_expert_router], axis=1), ((0, 0), (0, LANES - n_r)))
    wr_hi = wr.astype(BF16)
    wr_lo = (wr - wr_hi.astype(F32)).astype(BF16)
    wr_cat = jnp.concatenate([wr_hi, wr_lo], axis=1)
    br_col = jnp.zeros((LANES, 1), F32).at[:n_r, 0].set(jnp.concatenate([b_group, b_expert_router]))
    x2, h2, eids, gcol = _out_proj_router(x, conv_out, attn_out, w_out.astype(BF16), row2(norm2_g),
                                          wr_cat, br_col)

    n_blocks = (s * TOP_K) // BLOCK_ROWS + N_EXPERTS
    assert n_blocks <= META_LANES
    dest, meta = _dispatch(eids)
    xs = _scatter_rows(dest, meta, h2, n_blocks * BLOCK_ROWS)
    ys = _experts(meta, xs, w_gate, w_up, w_down)
    return _combine(dest.reshape(-1), x2, gcol, row2(out_g), ys)


def kernel(x, norm1_g, w_in, w_dw, b_dw, ln_g, ln_b, rel_bias, w_out, norm2_g, w_group, b_group,
           w_expert_router, b_expert_router, w_gate, w_up, w_down, final_g):
    depth = norm1_g.shape[0]
    assert depth == 1, "the final norm is fused into the last layer's combine step"
    batch, s, d = x.shape
    assert batch == 1, "one sequence per call"
    out = _layer(x.reshape(s, d), norm1_g[0], w_in[0], w_dw[0], b_dw[0], ln_g[0], ln_b[0], rel_bias[0],
                 w_out[0], norm2_g[0], w_group[0], b_group[0], w_expert_router[0],
                 b_expert_router[0], w_gate[0], w_up[0], w_down[0], final_g)
    return out.reshape(batch, s, d)
```

```python
import functools

import jax
import jax.numpy as jnp
from jax import lax
from jax.experimental import pallas as pl
from jax.experimental.pallas import tpu as pltpu

F32 = jnp.float32
BF16 = jnp.bfloat16
I32 = jnp.int32

CHUNK = 64
HEAD_DIM = 64
LEFT_CHUNKS = 8
REL_CLIP = 128
CONV_WIDTH = 31
N_GROUPS = 8
EXPERTS_PER_GROUP = 8
N_EXPERTS = N_GROUPS * EXPERTS_PER_GROUP
TOP_K = 2
BLOCK_ROWS = 128
RMS_EPS = 1e-6
LN_EPS = 1e-5
NEG_INF = -1e30

VMEM_LIMIT_BYTES = 56 * 1024 * 1024
LANES = 128
SUBLANES = 8
PAIR = 2 * CHUNK
BAND_PAIR = (LEFT_CHUNKS + 2) * CHUNK
FRONT_PAD = LEFT_CHUNKS * CHUNK
CONV_HALO = 32


def _cparams(*sem):
    return pltpu.CompilerParams(dimension_semantics=sem, vmem_limit_bytes=VMEM_LIMIT_BYTES)


def _load_weight_as_bf16(w_hbm, w_ref, stage, wsem):
    cw = stage.shape[2]
    n_chunks = w_ref.shape[1] // cw

    def chunk_copy(c):
        return pltpu.make_async_copy(w_hbm.at[:, c * cw:(c + 1) * cw], stage.at[c % 2], wsem.at[c % 2])

    chunk_copy(0).start()
    for c in range(n_chunks):
        if c + 1 < n_chunks:
            chunk_copy(c + 1).start()
        chunk_copy(c).wait()
        w_ref[:, c * cw:(c + 1) * cw] = stage[c % 2].astype(BF16)


def _weight_scratch(d, d_out, chunk):
    return [pltpu.VMEM((d, d_out), BF16), pltpu.VMEM((2, d, chunk), F32), pltpu.SemaphoreType.DMA((2,))]


def _inproj_body(x_ref, g_ref, w_hbm, cv_ref, cg_ref, q_ref, k_ref, v_ref, w_ref, stage, wsem):
    @pl.when(pl.program_id(0) == 0)
    def _():
        _load_weight_as_bf16(w_hbm, w_ref, stage, wsem)

    x = x_ref[...]
    ms = jnp.mean(x * x, axis=-1, keepdims=True)
    h = (x * lax.rsqrt(ms + RMS_EPS) * g_ref[...]).astype(BF16)
    dc = cv_ref.shape[1]
    da = q_ref.shape[1]

    def proj(lo, width):
        return jnp.dot(h, w_ref[:, lo:lo + width], preferred_element_type=F32)

    cv_ref[...] = proj(0, dc)
    cg_ref[...] = proj(dc, dc)
    q_ref[...] = (proj(2 * dc, da) * (HEAD_DIM ** -0.5)).astype(BF16)
    k_ref[...] = proj(2 * dc + da, da).astype(BF16)
    v_ref[...] = proj(2 * dc + 2 * da, da).astype(BF16)


def _in_proj(x, g, w, d_conv, d_att, tm=256, w_chunk=512):
    s, d = x.shape
    d_in = w.shape[1]
    row = lambda i: (i, 0)
    const = lambda i: (0, 0)
    return pl.pallas_call(
        _inproj_body,
        grid=(s // tm,),
        in_specs=[
            pl.BlockSpec((tm, d), row),
            pl.BlockSpec((1, d), const),
            pl.BlockSpec(memory_space=pl.ANY),
        ],
        out_specs=[
            pl.BlockSpec((tm, d_conv), row),
            pl.BlockSpec((tm, d_conv), row),
            pl.BlockSpec((tm, d_att), row),
            pl.BlockSpec((tm, d_att), row),
            pl.BlockSpec((tm, d_att), row),
        ],
        out_shape=[
            jax.ShapeDtypeStruct((s, d_conv), F32),
            jax.ShapeDtypeStruct((s, d_conv), F32),
            jax.ShapeDtypeStruct((s, d_att), BF16),
            jax.ShapeDtypeStruct((s, d_att), BF16),
            jax.ShapeDtypeStruct((s, d_att), BF16),
        ],
        scratch_shapes=_weight_scratch(d, d_in, w_chunk),
        compiler_params=_cparams("arbitrary"),
        name="in_proj",
    )(x, g, w)


def _sigmoid(x):
    return 1.0 / (1.0 + jnp.exp(-x))


def _conv_body(cv_ref, cg_ref, cvh_ref, cgh_ref, w_ref, b_ref, lg_ref, lb_ref, o_ref,
               u_ref, us_ref, wb_ref, conv_ref, *, rows_per_chunk, groups_per_pass):
    i = pl.program_id(0)
    t, dc = cv_ref.shape

    @pl.when(i == 0)
    def _():
        for w in range(CONV_WIDTH):
            wb_ref[w] = jnp.broadcast_to(w_ref[w:w + 1, :], (SUBLANES, dc))

    halo = cvh_ref[...] * _sigmoid(cgh_ref[...])
    u_ref[0:CONV_HALO, :] = jnp.where(i > 0, halo, 0.0)
    u_ref[CONV_HALO:, :] = cv_ref[...] * _sigmoid(cg_ref[...])
    n_shift_rows = us_ref.shape[1]
    for sft in range(1, SUBLANES):
        us_ref[sft - 1] = u_ref[sft:sft + n_shift_rows, :]
    first_tap = CONV_HALO - (CONV_WIDTH - 1)

    def lane_tile(lt, _):
        lanes = pl.ds(pl.multiple_of(lt * LANES, LANES), LANES)
        wv = [wb_ref[w, :, lanes] for w in range(CONV_WIDTH)]
        bias = jnp.broadcast_to(b_ref[:, lanes], (SUBLANES, LANES))
        for g0 in range(0, t // SUBLANES, groups_per_pass):
            accs = [bias] * groups_per_pass
            for w in range(CONV_WIDTH):
                off = first_tap + w
                sft = off % SUBLANES
                for g in range(groups_per_pass):
                    lo = (g0 + g) * SUBLANES + off - sft
                    if sft == 0:
                        src = u_ref[lo:lo + SUBLANES, lanes]
                    else:
                        src = us_ref[sft - 1, lo:lo + SUBLANES, lanes]
                    accs[g] = accs[g] + wv[w] * src
            for g in range(groups_per_pass):
                lo = (g0 + g) * SUBLANES
                conv_ref[lo:lo + SUBLANES, lanes] = accs[g]
        return 0

    lax.fori_loop(0, dc // LANES, lane_tile, 0)

    r = rows_per_chunk
    for c in range(t // r):
        acc = conv_ref[c * r:(c + 1) * r, :]
        mu = jnp.mean(acc, axis=-1, keepdims=True)
        cen = acc - mu
        var = jnp.mean(cen * cen, axis=-1, keepdims=True)
        un = cen * lax.rsqrt(var + LN_EPS) * lg_ref[...] + lb_ref[...]
        o_ref[c * r:(c + 1) * r, :] = (un * _sigmoid(un)).astype(o_ref.dtype)


def _conv_module(cv, cg, w_dw, b_dw, ln_g, ln_b, t=512, rows_per_chunk=16, groups_per_pass=4):
    s, dc = cv.shape
    halo_blocks = t // CONV_HALO
    row = lambda i: (i, 0)
    prev = lambda i: (jnp.maximum(i * halo_blocks - 1, 0), 0)
    const = lambda i: (0, 0)
    return pl.pallas_call(
        functools.partial(_conv_body, rows_per_chunk=rows_per_chunk, groups_per_pass=groups_per_pass),
        grid=(s // t,),
        in_specs=[
            pl.BlockSpec((t, dc), row),
            pl.BlockSpec((t, dc), row),
            pl.BlockSpec((CONV_HALO, dc), prev),
            pl.BlockSpec((CONV_HALO, dc), prev),
            pl.BlockSpec((CONV_WIDTH, dc), const),
            pl.BlockSpec((1, dc), const),
            pl.BlockSpec((1, dc), const),
            pl.BlockSpec((1, dc), const),
        ],
        out_specs=pl.BlockSpec((t, dc), row),
        out_shape=jax.ShapeDtypeStruct((s, dc), BF16),
        scratch_shapes=[
            pltpu.VMEM((t + CONV_HALO, dc), F32),
            pltpu.VMEM((SUBLANES - 1, t + CONV_HALO - SUBLANES, dc), F32),
            pltpu.VMEM((CONV_WIDTH, SUBLANES, dc), F32),
            pltpu.VMEM((t, dc), F32),
        ],
        compiler_params=_cparams("arbitrary"),
        name="conv_module",
    )(cv, cg, cv, cg, w_dw, b_dw, ln_g, ln_b)


SOFTMAX_ROWS = 32


def _attn_pairs(q_ref, bands, bias_ref, s_ref, e_ref, o_ref):
    lane = lax.broadcasted_iota(I32, (PAIR, 2 * HEAD_DIM), 1)
    n_pairs = len(bands)
    rdens = {}

    def scores(u):
        kb, _, _ = bands[u]
        q2 = q_ref[u * PAIR:(u + 1) * PAIR, :]
        for h in range(2):
            in_head = (lane >= h * HEAD_DIM) & (lane < (h + 1) * HEAD_DIM)
            qh = jnp.where(in_head, q2, jnp.zeros_like(q2))
            s_ref[2 * u + h, :, :kb.shape[0]] = lax.dot_general(
                qh, kb, (((1,), (1,)), ((), ())), preferred_element_type=F32)

    def softmax(u):
        kb, _, col_lo = bands[u]
        n_keys = kb.shape[0]
        for h in range(2):
            c = 2 * u + h
            parts = []
            for r0 in range(0, PAIR, SOFTMAX_ROWS):
                rows = slice(r0, r0 + SOFTMAX_ROWS)
                t = s_ref[c, rows, :n_keys] + bias_ref[h, rows, col_lo:col_lo + n_keys]
                e = jnp.exp(t - jnp.max(t, axis=-1, keepdims=True))
                parts.append(1.0 / jnp.sum(e, axis=-1, keepdims=True))
                e_ref[c, rows, :n_keys] = e.astype(BF16)
            rdens[c] = jnp.concatenate(parts, axis=0)

    def values(u):
        _, vb, _ = bands[u]
        n_keys = vb.shape[0]
        outs = []
        for h in range(2):
            c = 2 * u + h
            o = jnp.dot(e_ref[c, :, :n_keys], vb, preferred_element_type=F32)
            outs.append(o * rdens[c])
        o_ref[u * PAIR:(u + 1) * PAIR, :] = jnp.where(lane < HEAD_DIM, outs[0], outs[1]).astype(o_ref.dtype)

    for step in range(n_pairs + 2):
        if step < n_pairs:
            scores(step)
        if 0 <= step - 1 < n_pairs:
            softmax(step - 1)
        if 0 <= step - 2 < n_pairs:
            values(step - 2)


def _attn_body(q_ref, k_ref, v_ref, g_ref, o_ref, bias_ref, s_ref, e_ref, *, pairs_per_step):
    st = pl.program_id(1)
    npp = pairs_per_step

    @pl.when(st == 0)
    def _():
        sub = lax.broadcasted_iota(I32, (SUBLANES, BAND_PAIR), 0)
        kj = lax.broadcasted_iota(I32, (SUBLANES, BAND_PAIR), 1)
        for h in range(2):
            t8 = jnp.broadcast_to(g_ref[h], (SUBLANES, BAND_PAIR))
            shift = 1
            while shift < SUBLANES:
                t8 = jnp.where((sub & shift) != 0, pltpu.roll(t8, shift, 1), t8)
                shift *= 2
            for r0 in range(0, PAIR, SUBLANES):
                band_lo = (r0 // CHUNK) * CHUNK
                in_band = (kj >= band_lo) & (kj < band_lo + (LEFT_CHUNKS + 1) * CHUNK)
                rows = t8 if r0 == 0 else pltpu.roll(t8, r0, 1)
                bias_ref[h, r0:r0 + SUBLANES, :] = jnp.where(in_band, rows, NEG_INF)

    def run(first_step):
        bands = []
        for u in range(npp):
            if first_step:
                n_keys = min((u + 1) * PAIR, BAND_PAIR)
                lo = max((u + 1) * PAIR - BAND_PAIR, 0)
                bands.append((k_ref[lo:lo + n_keys, :], v_ref[lo:lo + n_keys, :], BAND_PAIR - n_keys))
            else:
                start = pl.multiple_of((st * npp + u) * PAIR - FRONT_PAD, PAIR)
                bands.append((k_ref[pl.ds(start, BAND_PAIR), :], v_ref[pl.ds(start, BAND_PAIR), :], 0))
        _attn_pairs(q_ref, bands, bias_ref, s_ref, e_ref, o_ref)

    @pl.when(st == 0)
    def _():
        run(True)

    @pl.when(st > 0)
    def _():
        run(False)


def _distance_rows(rel_table):
    t = rel_table.astype(F32)
    far = t[:, 2 * REL_CLIP:]
    n_far = LEFT_CHUNKS * CHUNK - REL_CLIP + 1
    near = jnp.flip(t[:, REL_CLIP - CHUNK + 1:2 * REL_CLIP], axis=1)
    tail = BAND_PAIR - n_far - near.shape[1]
    g = jnp.concatenate([jnp.repeat(far, n_far, axis=1), near, jnp.repeat(far, tail, axis=1)], axis=1)
    return g[:, None, :]


def _attention(q, k, v, g_rows, pairs_per_step=8):
    s, da = q.shape
    n_hp = da // (2 * HEAD_DIM)
    tq = pairs_per_step * PAIR
    assert pairs_per_step * PAIR >= FRONT_PAD, "first step must cover every clamped band"
    return pl.pallas_call(
        functools.partial(_attn_body, pairs_per_step=pairs_per_step),
        grid=(n_hp, s // tq),
        in_specs=[
            pl.BlockSpec((tq, 2 * HEAD_DIM), lambda hp, st: (st, hp)),
            pl.BlockSpec((s, 2 * HEAD_DIM), lambda hp, st: (0, hp)),
            pl.BlockSpec((s, 2 * HEAD_DIM), lambda hp, st: (0, hp)),
            pl.BlockSpec((2, 1, BAND_PAIR), lambda hp, st: (hp, 0, 0)),
        ],
        out_specs=pl.BlockSpec((tq, 2 * HEAD_DIM), lambda hp, st: (st, hp)),
        out_shape=jax.ShapeDtypeStruct((s, da), BF16),
        scratch_shapes=[
            pltpu.VMEM((2, PAIR, BAND_PAIR), F32),
            pltpu.VMEM((2 * pairs_per_step, PAIR, BAND_PAIR), F32),
            pltpu.VMEM((2 * pairs_per_step, PAIR, BAND_PAIR), BF16),
        ],
        compiler_params=_cparams("arbitrary", "arbitrary"),
        name="chunk_attention",
    )(q, k, v, g_rows)


def _slab_rows(d):
    return d // LANES


def _store_slabs(ref, val):
    t, d = val.shape
    n = _slab_rows(d)
    for c in range(n):
        ref[pl.ds(c, t, stride=n), :] = val[:, c * LANES:(c + 1) * LANES]


def _load_slabs(ref, t, n):
    return jnp.concatenate([ref[pl.ds(c, t, stride=n), :] for c in range(n)], axis=1)


def _first_index_of_max(vals, sub, n):
    vmax = jnp.max(vals, axis=0, keepdims=True)
    idx = jnp.min(jnp.where(vals == vmax, sub, n), axis=0, keepdims=True)
    return vmax, idx


def _outproj_body(x_ref, co_ref, ao_ref, w_hbm, g2_ref, wr_ref, br_ref,
                  x2_ref, h2_ref, eid_ref, gcol_ref, w_ref, stage, wsem):
    @pl.when(pl.program_id(0) == 0)
    def _():
        _load_weight_as_bf16(w_hbm, w_ref, stage, wsem)

    dc = co_ref.shape[1]
    y = jnp.dot(co_ref[...], w_ref[0:dc, :], preferred_element_type=F32)
    y = y + jnp.dot(ao_ref[...], w_ref[dc:, :], preferred_element_type=F32)
    x2 = x_ref[...] + y
    ms = jnp.mean(x2 * x2, axis=-1, keepdims=True)
    h2 = x2 * lax.rsqrt(ms + RMS_EPS) * g2_ref[...]
    x2_ref[...] = x2
    _store_slabs(h2_ref, h2)

    hi = h2.astype(BF16)
    lo = (h2 - hi.astype(F32)).astype(BF16)
    pa = jnp.dot(hi, wr_ref[...], preferred_element_type=F32)
    pb = jnp.dot(lo, wr_ref[...], preferred_element_type=F32)
    logits = pa[:, :LANES] + pa[:, LANES:] + pb[:, :LANES]
    lt = logits.T + br_ref[...]
    tm = lt.shape[1]
    sub = lax.broadcasted_iota(I32, (N_GROUPS, tm), 0)
    g = lt[0:N_GROUPS, :]
    gmax, grp = _first_index_of_max(g, sub, N_GROUPS)
    p_grp = 1.0 / jnp.sum(jnp.exp(g - gmax), axis=0, keepdims=True)
    sel = jnp.zeros((EXPERTS_PER_GROUP, tm), F32)
    for gi in range(N_GROUPS):
        lo = N_GROUPS + gi * EXPERTS_PER_GROUP
        sel = jnp.where(grp == gi, lt[lo:lo + EXPERTS_PER_GROUP, :], sel)
    v1, i1 = _first_index_of_max(sel, sub, EXPERTS_PER_GROUP)
    rest = jnp.where(sub == i1, -jnp.inf, sel)
    v2, i2 = _first_index_of_max(rest, sub, EXPERTS_PER_GROUP)
    t = jnp.exp(v2 - v1)
    gate1 = p_grp / (1.0 + t)
    gate2 = p_grp * t / (1.0 + t)
    eid_ref[0:1, :] = grp * EXPERTS_PER_GROUP + i1
    eid_ref[1:2, :] = grp * EXPERTS_PER_GROUP + i2
    row = lax.broadcasted_iota(I32, (LANES, tm), 0)
    gl = jnp.where(row == 0, gate1, jnp.where(row == 1, gate2, 0.0))
    gcol_ref[...] = gl.T


def _out_proj_router(x, conv_out, attn_out, w, g2, wr_cat, br_col, tm=512, w_chunk=512):
    s, d = x.shape
    dc = conv_out.shape[1]
    da = attn_out.shape[1]
    row = lambda i: (i, 0)
    const = lambda i: (0, 0)
    return pl.pallas_call(
        _outproj_body,
        grid=(s // tm,),
        in_specs=[
            pl.BlockSpec((tm, d), row),
            pl.BlockSpec((tm, dc), row),
            pl.BlockSpec((tm, da), row),
            pl.BlockSpec(memory_space=pl.ANY),
            pl.BlockSpec((1, d), const),
            pl.BlockSpec((d, 2 * LANES), const),
            pl.BlockSpec((LANES, 1), const),
        ],
        out_specs=[
            pl.BlockSpec((tm, d), row),
            pl.BlockSpec((tm * _slab_rows(d), LANES), row),
            pl.BlockSpec((TOP_K, tm), lambda i: (0, i)),
            pl.BlockSpec((tm, LANES), row),
        ],
        out_shape=[
            jax.ShapeDtypeStruct((s, d), F32),
            jax.ShapeDtypeStruct((s * _slab_rows(d), LANES), F32),
            jax.ShapeDtypeStruct((TOP_K, s), I32),
            jax.ShapeDtypeStruct((s, LANES), F32),
        ],
        scratch_shapes=_weight_scratch(dc + da, d, w_chunk),
        compiler_params=_cparams("arbitrary"),
        name="out_proj_router",
    )(x, conv_out, attn_out, w, g2, wr_cat, br_col)


META_LANES = 256
(META_FIRST_BLK, META_N_BLK, META_COUNT, META_NEXT_EXP, META_ORDINAL,
 META_USED_BLKS, META_FIRST_USED) = range(7)


def _dispatch_body(eid_ref, dest_ref, meta_ref, rank_ref):
    s = eid_ref.shape[1]
    w = META_LANES
    sub = lax.broadcasted_iota(I32, (N_EXPERTS, w), 0)
    lane = lax.broadcasted_iota(I32, (N_EXPERTS, w), 1)
    ri = lax.broadcasted_iota(I32, (w, w), 0)
    ci = lax.broadcasted_iota(I32, (w, w), 1)
    upper = (ri < ci).astype(BF16)

    def onehots(b):
        off = pl.multiple_of(b * w, w)
        e0 = eid_ref[0:1, pl.ds(off, w)]
        e1 = eid_ref[1:2, pl.ds(off, w)]
        return off, sub == e0, sub == e1

    def rank_block(b, carry):
        off, o0, o1 = onehots(b)
        both = jnp.where(o0 | o1, 1.0, 0.0)
        pre = jnp.dot(both.astype(BF16), upper, preferred_element_type=F32) + carry
        rank_ref[0:1, pl.ds(off, w)] = jnp.sum(jnp.where(o0, pre, 0.0), axis=0, keepdims=True)
        rank_ref[1:2, pl.ds(off, w)] = jnp.sum(jnp.where(o1, pre, 0.0), axis=0, keepdims=True)
        return carry + jnp.sum(both, axis=1, keepdims=True)

    counts = lax.fori_loop(0, s // w, rank_block, jnp.zeros((N_EXPERTS, w), F32))

    nblk = ((counts.astype(I32) + (BLOCK_ROWS - 1)) // BLOCK_ROWS).astype(F32)
    er = lax.broadcasted_iota(I32, (N_EXPERTS, N_EXPERTS), 0)
    ec = lax.broadcasted_iota(I32, (N_EXPERTS, N_EXPERTS), 1)
    lower_incl = (ec <= er).astype(BF16)
    pend_blk = jnp.dot(lower_incl, nblk.astype(BF16), preferred_element_type=F32)
    pstart = (pend_blk - nblk) * float(BLOCK_ROWS)

    def on_lanes(v):
        return jnp.sum(jnp.where(sub == lane, v, 0), axis=0, keepdims=True)

    nblk_i = nblk.astype(I32)
    first_blk = on_lanes((pend_blk - nblk).astype(I32))
    n_blk = on_lanes(nblk_i)
    count = on_lanes(counts.astype(I32))
    has_rows = nblk_i > 0
    next_exp = jnp.min(jnp.where(has_rows & (sub > lane), sub, N_EXPERTS), axis=0, keepdims=True)
    next_exp = jnp.where(next_exp < N_EXPERTS, next_exp, -1)
    ordinal = jnp.sum(jnp.where(has_rows & (sub < lane), 1, 0), axis=0, keepdims=True)
    first_used = jnp.min(jnp.where(has_rows, sub, N_EXPERTS - 1), axis=0, keepdims=True)
    used = pend_blk[N_EXPERTS - 1:N_EXPERTS, :].astype(I32)
    mrow = lax.broadcasted_iota(I32, (8, w), 0)
    meta = jnp.where(mrow == META_FIRST_BLK, first_blk, 0)
    meta = jnp.where(mrow == META_N_BLK, n_blk, meta)
    meta = jnp.where(mrow == META_COUNT, count, meta)
    meta = jnp.where(mrow == META_NEXT_EXP, next_exp, meta)
    meta = jnp.where(mrow == META_ORDINAL, ordinal, meta)
    meta = jnp.where(mrow == META_FIRST_USED, first_used, meta)
    meta_ref[...] = jnp.where(mrow == META_USED_BLKS, used, meta)

    def dest_block(b, _):
        off, o0, o1 = onehots(b)
        d0 = rank_ref[0:1, pl.ds(off, w)] + jnp.sum(jnp.where(o0, pstart, 0.0), axis=0, keepdims=True)
        d1 = rank_ref[1:2, pl.ds(off, w)] + jnp.sum(jnp.where(o1, pstart, 0.0), axis=0, keepdims=True)
        dest_ref[0:1, pl.ds(off, w)] = d0.astype(I32)
        dest_ref[1:2, pl.ds(off, w)] = d1.astype(I32)
        return 0

    lax.fori_loop(0, s // w, dest_block, 0)


def _dispatch(eids):
    s = eids.shape[1]
    return pl.pallas_call(
        _dispatch_body,
        out_shape=[
            jax.ShapeDtypeStruct((TOP_K, s), I32),
            jax.ShapeDtypeStruct((8, META_LANES), I32),
        ],
        scratch_shapes=[pltpu.VMEM((TOP_K, s), F32)],
        compiler_params=pltpu.CompilerParams(vmem_limit_bytes=VMEM_LIMIT_BYTES),
        name="dispatch",
    )(eids)


PAD_BITS = tuple(reversed(range(BLOCK_ROWS.bit_length() - 1)))


def _scatter_body(dest_ref, meta_ref, h2_ref, xs_hbm, zbuf, sem, zsem, *, tm):
    i = pl.program_id(0)
    n_sl = h2_ref.shape[0] // tm
    blk = BLOCK_ROWS * n_sl
    n_blocks = xs_hbm.shape[0] // blk
    used = meta_ref[META_USED_BLKS, 0]

    def slab(row, n_tokens=1):
        return pl.ds(pl.multiple_of(row * n_sl, n_sl), n_tokens * n_sl)

    def token_copy(j, k):
        t = i * tm + j
        return pltpu.make_async_copy(h2_ref.at[slab(j)], xs_hbm.at[slab(dest_ref[k, t])], sem.at[k])

    def scatter_tile():
        def one(j, _):
            for k in range(TOP_K):
                token_copy(j, k).start(priority=k)
            return 0

        lax.fori_loop(0, tm, one, 0, unroll=8)

    def wait_tile():
        for k in range(TOP_K):
            pltpu.make_async_copy(h2_ref, xs_hbm.at[slab(0, tm)], sem.at[k]).wait()

    @pl.when(i > 0)
    def _():
        scatter_tile()
        wait_tile()

    @pl.when(i == 0)
    def _():
        _scatter_first_step(meta_ref, xs_hbm, zbuf, zsem, slab, used, n_blocks, scatter_tile, wait_tile)


def _scatter_first_step(meta_ref, xs_hbm, zbuf, zsem, slab, used, n_blocks, scatter_tile, wait_tile):
    n_sl = zbuf.shape[0] // BLOCK_ROWS

    def zero_copy(row, n_tokens):
        return pltpu.make_async_copy(zbuf.at[pl.ds(0, n_tokens * n_sl)], xs_hbm.at[slab(row, n_tokens)], zsem)

    zbuf[...] = jnp.zeros_like(zbuf)

    def per_expert(e, issued):
        b0 = meta_ref[META_FIRST_BLK, e]
        nb = meta_ref[META_N_BLK, e]
        cnt = meta_ref[META_COUNT, e]
        pad = nb * BLOCK_ROWS - cnt
        row = b0 * BLOCK_ROWS + cnt
        out = []
        for bit, n_issued in zip(PAD_BITS, issued):
            size = 1 << bit
            take = pad & size

            @pl.when(take != 0)
            def _():
                zero_copy(row, size).start()

            row = row + take
            out.append(n_issued + take // size)
        return tuple(out)

    issued = lax.fori_loop(0, N_EXPERTS, per_expert, tuple(jnp.int32(0) for _ in PAD_BITS))
    lax.fori_loop(used, n_blocks, lambda b, _: (zero_copy(b * BLOCK_ROWS, BLOCK_ROWS).start(), 0)[1], 0)

    scatter_tile()
    wait_tile()

    for bit, n_issued in zip(PAD_BITS, issued):
        lax.fori_loop(0, n_issued, lambda i, _, n=1 << bit: (zero_copy(0, n).wait(), 0)[1], 0)
    lax.fori_loop(used, n_blocks, lambda b, _: (zero_copy(0, BLOCK_ROWS).wait(), 0)[1], 0)


def _scatter_rows(dest, meta, h2_slabs, rows, tm=512):
    s = dest.shape[1]
    n_sl = h2_slabs.shape[0] // s
    grid_spec = pltpu.PrefetchScalarGridSpec(
        num_scalar_prefetch=2,
        grid=(s // tm,),
        in_specs=[pl.BlockSpec((tm * n_sl, LANES), lambda i, dest_ref, meta_ref: (i, 0))],
        out_specs=pl.BlockSpec(memory_space=pl.ANY),
        scratch_shapes=[
            pltpu.VMEM((BLOCK_ROWS * n_sl, LANES), h2_slabs.dtype),
            pltpu.SemaphoreType.DMA((TOP_K,)),
            pltpu.SemaphoreType.DMA(()),
        ],
    )
    return pl.pallas_call(
        functools.partial(_scatter_body, tm=tm),
        grid_spec=grid_spec,
        out_shape=jax.ShapeDtypeStruct((rows * n_sl, LANES), h2_slabs.dtype),
        compiler_params=_cparams("arbitrary"),
        name="scatter_rows",
    )(dest, meta, h2_slabs)


def _row_gather_wait(src_hbm, dst, sem, n_rows):
    pltpu.make_async_copy(src_hbm.at[pl.ds(0, n_rows)], dst.at[pl.ds(0, n_rows)], sem).wait()


WEIGHT_QUEUES = (1, 1, 1)
ROW_BLOCK_QUEUE = 0


def _experts_body(meta_ref, xs_hbm, wg_hbm, wu_hbm, wd_hbm, ys_hbm,
                  xbuf, ybuf, zbuf, wgbuf, wubuf, wdbuf, gsem, ysem, zsem, wsem):
    used = meta_ref[META_USED_BLKS, 0]
    n_experts, d, _ = wg_hbm.shape
    n_sl = _slab_rows(d)
    blk = BLOCK_ROWS * n_sl
    n_blocks = ys_hbm.shape[0] // blk

    def xs_copy(b, sl):
        src_blk = jnp.minimum(b, n_blocks - 1)
        rows = pl.ds(pl.multiple_of(src_blk * blk, blk), blk)
        return pltpu.make_async_copy(xs_hbm.at[rows], xbuf.at[sl], gsem.at[sl])

    def gather_start(b, sl):
        xs_copy(b, sl).start(priority=ROW_BLOCK_QUEUE)

    def weights_start(e, ws):
        for c, queue in zip(weight_copies(e, ws), WEIGHT_QUEUES):
            c.start(priority=queue)

    def gather_wait(sl):
        xs_copy(0, sl).wait()

    def ys_copy(b):
        sl = b % 2
        rows = pl.ds(pl.multiple_of(b * blk, blk), blk)
        return pltpu.make_async_copy(ybuf.at[sl], ys_hbm.at[rows], ysem.at[sl])

    def zero_copy(b):
        rows = pl.ds(pl.multiple_of(b * blk, blk), blk)
        return pltpu.make_async_copy(zbuf, ys_hbm.at[rows], zsem)

    def weight_copies(e, ws):
        return (pltpu.make_async_copy(wg_hbm.at[e], wgbuf.at[ws], wsem.at[ws, 0]),
                pltpu.make_async_copy(wu_hbm.at[e], wubuf.at[ws], wsem.at[ws, 1]),
                pltpu.make_async_copy(wd_hbm.at[e], wdbuf.at[ws], wsem.at[ws, 2]))

    weights_start(meta_ref[META_FIRST_USED, 0], 0)
    gather_start(0, 0)
    zbuf[...] = jnp.zeros_like(zbuf)
    lax.fori_loop(used, n_blocks, lambda b, _: (zero_copy(b).start(priority=ROW_BLOCK_QUEUE), 0)[1], 0)

    def expert(e, _):
        nb = meta_ref[META_N_BLK, e]

        @pl.when(nb > 0)
        def _():
            b0 = meta_ref[META_FIRST_BLK, e]
            ws = meta_ref[META_ORDINAL, e] % 2
            nxt = meta_ref[META_NEXT_EXP, e]

            @pl.when(nxt >= 0)
            def _():
                weights_start(nxt, 1 - ws)

            for c in weight_copies(e, ws):
                c.wait()

            def block(j, _):
                b = b0 + j
                sl = b % 2
                gather_start(b + 1, 1 - sl)
                gather_wait(sl)

                @pl.when(b >= 2)
                def _():
                    ys_copy(b - 2).wait()

                x = _load_slabs(xbuf.at[sl], BLOCK_ROWS, n_sl).astype(BF16)
                a = jnp.dot(x, wgbuf[ws].astype(BF16), preferred_element_type=F32)
                u = jnp.dot(x, wubuf[ws].astype(BF16), preferred_element_type=F32)
                act = (a * _sigmoid(a) * u).astype(BF16)
                y = jnp.dot(act, wdbuf[ws].astype(BF16), preferred_element_type=F32)
                _store_slabs(ybuf.at[sl], y)
                ys_copy(b).start(priority=ROW_BLOCK_QUEUE)
                return 0

            lax.fori_loop(0, nb, block, 0)

        return 0

    lax.fori_loop(0, n_experts, expert, 0)

    gather_wait(used % 2)
    for back in (1, 2):
        @pl.when(used >= back)
        def _():
            ys_copy(used - back).wait()
    lax.fori_loop(used, n_blocks, lambda b, _: (zero_copy(b).wait(), 0)[1], 0)


def _experts(meta, xs_slabs, w_gate, w_up, w_down):
    n_e, d, de = w_gate.shape
    n_sl = _slab_rows(d)
    blk = BLOCK_ROWS * n_sl
    smem = pl.BlockSpec(memory_space=pltpu.SMEM)
    hbm = pl.BlockSpec(memory_space=pl.ANY)
    return pl.pallas_call(
        _experts_body,
        in_specs=[smem, hbm, hbm, hbm, hbm],
        out_specs=hbm,
        out_shape=jax.ShapeDtypeStruct(xs_slabs.shape, F32),
        scratch_shapes=[
            pltpu.VMEM((2, blk, LANES), F32),
            pltpu.VMEM((2, blk, LANES), F32),
            pltpu.VMEM((blk, LANES), F32),
            pltpu.VMEM((2, d, de), F32),
            pltpu.VMEM((2, d, de), F32),
            pltpu.VMEM((2, de, d), F32),
            pltpu.SemaphoreType.DMA((2,)),
            pltpu.SemaphoreType.DMA((2,)),
            pltpu.SemaphoreType.DMA(()),
            pltpu.SemaphoreType.DMA((2, 3)),
        ],
        compiler_params=pltpu.CompilerParams(vmem_limit_bytes=VMEM_LIMIT_BYTES),
        name="expert_mlp",
    )(meta, xs_slabs, w_gate, w_up, w_down)


def _combine_body(dest_ref, x2_ref, gcol_ref, fg_ref, ys_hbm, o_ref, ybuf, sem, *, tm):
    i = pl.program_id(0)
    n = pl.num_programs(0)
    s = n * tm
    slot = i % 2
    n_sl = _slab_rows(x2_ref.shape[1])

    def start(tile, sl):
        for k in range(TOP_K):
            for r in range(tm):
                row = dest_ref[k * s + tile * tm + r]
                src = ys_hbm.at[pl.ds(pl.multiple_of(row * n_sl, n_sl), n_sl)]
                pltpu.make_async_copy(src, ybuf.at[sl, k, pl.ds(r * n_sl, n_sl)],
                                      sem.at[sl, k]).start(priority=r % 2)

    def wait(sl):
        for k in range(TOP_K):
            _row_gather_wait(ys_hbm, ybuf.at[sl, k], sem.at[sl, k], tm * n_sl)

    @pl.when(i == 0)
    def _():
        start(0, 0)

    start(jnp.minimum(i + 1, n - 1), 1 - slot)
    wait(slot)
    g = gcol_ref[...]
    y0 = _load_slabs(ybuf.at[slot, 0], tm, n_sl)
    y1 = _load_slabs(ybuf.at[slot, 1], tm, n_sl)
    moe = g[:, 0:1] * y0 + g[:, 1:2] * y1
    x3 = x2_ref[...] + moe
    ms = jnp.mean(x3 * x3, axis=-1, keepdims=True)
    o_ref[...] = x3 * lax.rsqrt(ms + RMS_EPS) * fg_ref[...]

    @pl.when(i == n - 1)
    def _():
        wait(1 - slot)


def _combine(dest_flat, x2, gcol, final_g, ys, tm=128):
    s, d = x2.shape
    row = lambda i, dest_ref: (i, 0)
    const = lambda i, dest_ref: (0, 0)
    grid_spec = pltpu.PrefetchScalarGridSpec(
        num_scalar_prefetch=1,
        grid=(s // tm,),
        in_specs=[
            pl.BlockSpec((tm, d), row),
            pl.BlockSpec((tm, LANES), row),
            pl.BlockSpec((1, d), const),
            pl.BlockSpec(memory_space=pl.ANY),
        ],
        out_specs=pl.BlockSpec((tm, d), row),
        scratch_shapes=[
            pltpu.VMEM((2, TOP_K, tm * _slab_rows(d), LANES), F32),
            pltpu.SemaphoreType.DMA((2, TOP_K)),
        ],
    )
    return pl.pallas_call(
        functools.partial(_combine_body, tm=tm),
        grid_spec=grid_spec,
        out_shape=jax.ShapeDtypeStruct((s, d), F32),
        compiler_params=_cparams("arbitrary"),
        name="combine_norm",
    )(dest_flat, x2, gcol, final_g, ys)


def _layer(x, norm1_g, w_in, w_dw, b_dw, ln_g, ln_b, rel_bias, w_out, norm2_g,
           w_group, b_group, w_expert_router, b_expert_router, w_gate, w_up, w_down, out_g):
    s, d = x.shape
    dc = w_dw.shape[-1]
    da = (w_in.shape[1] - 2 * dc) // 3
    row2 = lambda a: a.reshape(1, -1)

    cv, cg, q, k, v = _in_proj(x, row2(norm1_g), w_in, dc, da)
    conv_out = _conv_module(cv, cg, w_dw.reshape(CONV_WIDTH, dc), row2(b_dw), row2(ln_g), row2(ln_b))
    attn_out = _attention(q, k, v, _distance_rows(rel_bias))

    n_r = N_GROUPS + N_EXPERTS
    wr = jnp.pad(jnp.concatenate([w_group, w_expert_router], axis=1), ((0, 0), (0, LANES - n_r)))
    wr_hi = wr.astype(BF16)
    wr_lo = (wr - wr_hi.astype(F32)).astype(BF16)
    wr_cat = jnp.concatenate([wr_hi, wr_lo], axis=1)
    br_col = jnp.zeros((LANES, 1), F32).at[:n_r, 0].set(jnp.concatenate([b_group, b_expert_router]))
    x2, h2, eids, gcol = _out_proj_router(x, conv_out, attn_out, w_out, row2(norm2_g),
                                          wr_cat, br_col)

    n_blocks = (s * TOP_K) // BLOCK_ROWS + N_EXPERTS
    assert n_blocks <= META_LANES
    dest, meta = _dispatch(eids)
    xs = _scatter_rows(dest, meta, h2, n_blocks * BLOCK_ROWS)
    ys = _experts(meta, xs, w_gate, w_up, w_down)
    return _combine(dest.reshape(-1), x2, gcol, row2(out_g), ys)


def kernel(x, norm1_g, w_in, w_dw, b_dw, ln_g, ln_b, rel_bias, w_out, norm2_g, w_group, b_group,
           w_expert_router, b_expert_router, w_gate, w_up, w_down, final_g):
    depth = norm1_g.shape[0]
    assert depth == 1, "the final norm is fused into the last layer's combine step"
    batch, s, d = x.shape
    assert batch == 1, "one sequence per call"
    out = _layer(x.reshape(s, d), norm1_g[0], w_in[0], w_dw[0], b_dw[0], ln_g[0], ln_b[0], rel_bias[0],
                 w_out[0], norm2_g[0], w_group[0], b_group[0], w_expert_router[0],
                 b_expert_router[0], w_gate[0], w_up[0], w_down[0], final_g)
    return out.reshape(batch, s, d)
```

```python
import functools

import jax
import jax.numpy as jnp
from jax import lax
from jax.experimental import pallas as pl
from jax.experimental.pallas import tpu as pltpu

F32 = jnp.float32
BF16 = jnp.bfloat16
I32 = jnp.int32

CHUNK = 64
HEAD_DIM = 64
LEFT_CHUNKS = 8
REL_CLIP = 128
CONV_WIDTH = 31
N_GROUPS = 8
EXPERTS_PER_GROUP = 8
N_EXPERTS = N_GROUPS * EXPERTS_PER_GROUP
TOP_K = 2
BLOCK_ROWS = 128
RMS_EPS = 1e-6
LN_EPS = 1e-5
NEG_INF = -1e30

VMEM_LIMIT_BYTES = 56 * 1024 * 1024
LANES = 128
SUBLANES = 8
PAIR = 2 * CHUNK
BAND_PAIR = (LEFT_CHUNKS + 2) * CHUNK
FRONT_PAD = LEFT_CHUNKS * CHUNK
CONV_HALO = 32


def _cparams(*sem):
    return pltpu.CompilerParams(dimension_semantics=sem, vmem_limit_bytes=VMEM_LIMIT_BYTES)


def _load_weight_as_bf16(w_hbm, w_ref, stage, wsem):
    cw = stage.shape[2]
    n_chunks = w_ref.shape[1] // cw

    def chunk_copy(c):
        return pltpu.make_async_copy(w_hbm.at[:, c * cw:(c + 1) * cw], stage.at[c % 2], wsem.at[c % 2])

    chunk_copy(0).start()
    for c in range(n_chunks):
        if c + 1 < n_chunks:
            chunk_copy(c + 1).start()
        chunk_copy(c).wait()
        w_ref[:, c * cw:(c + 1) * cw] = stage[c % 2].astype(BF16)


def _weight_scratch(d, d_out, chunk):
    return [pltpu.VMEM((d, d_out), BF16), pltpu.VMEM((2, d, chunk), F32), pltpu.SemaphoreType.DMA((2,))]


def _inproj_body(x_ref, g_ref, w_hbm, cv_ref, cg_ref, q_ref, k_ref, v_ref, w_ref, stage, wsem):
    @pl.when(pl.program_id(0) == 0)
    def _():
        _load_weight_as_bf16(w_hbm, w_ref, stage, wsem)

    x = x_ref[...]
    ms = jnp.mean(x * x, axis=-1, keepdims=True)
    h = (x * lax.rsqrt(ms + RMS_EPS) * g_ref[...]).astype(BF16)
    dc = cv_ref.shape[1]
    da = q_ref.shape[1]

    def proj(lo, width):
        return jnp.dot(h, w_ref[:, lo:lo + width], preferred_element_type=F32)

    cv_ref[...] = proj(0, dc)
    cg_ref[...] = proj(dc, dc)
    q_ref[...] = (proj(2 * dc, da) * (HEAD_DIM ** -0.5)).astype(BF16)
    k_ref[...] = proj(2 * dc + da, da).astype(BF16)
    v_ref[...] = proj(2 * dc + 2 * da, da).astype(BF16)


def _in_proj(x, g, w, d_conv, d_att, tm=256, w_chunk=512):
    s, d = x.shape
    d_in = w.shape[1]
    row = lambda i: (i, 0)
    const = lambda i: (0, 0)
    return pl.pallas_call(
        _inproj_body,
        grid=(s // tm,),
        in_specs=[
            pl.BlockSpec((tm, d), row),
            pl.BlockSpec((1, d), const),
            pl.BlockSpec(memory_space=pl.ANY),
        ],
        out_specs=[
            pl.BlockSpec((tm, d_conv), row),
            pl.BlockSpec((tm, d_conv), row),
            pl.BlockSpec((tm, d_att), row),
            pl.BlockSpec((tm, d_att), row),
            pl.BlockSpec((tm, d_att), row),
        ],
        out_shape=[
            jax.ShapeDtypeStruct((s, d_conv), F32),
            jax.ShapeDtypeStruct((s, d_conv), F32),
            jax.ShapeDtypeStruct((s, d_att), BF16),
            jax.ShapeDtypeStruct((s, d_att), BF16),
            jax.ShapeDtypeStruct((s, d_att), BF16),
        ],
        scratch_shapes=_weight_scratch(d, d_in, w_chunk),
        compiler_params=_cparams("arbitrary"),
        name="in_proj",
    )(x, g, w)


def _sigmoid(x):
    return 1.0 / (1.0 + jnp.exp(-x))


def _conv_body(cv_ref, cg_ref, cvh_ref, cgh_ref, w_ref, b_ref, lg_ref, lb_ref, o_ref,
               u_ref, us_ref, wb_ref, conv_ref, *, rows_per_chunk, groups_per_pass):
    i = pl.program_id(0)
    t, dc = cv_ref.shape

    @pl.when(i == 0)
    def _():
        for w in range(CONV_WIDTH):
            wb_ref[w] = jnp.broadcast_to(w_ref[w:w + 1, :], (SUBLANES, dc))

    halo = cvh_ref[...] * _sigmoid(cgh_ref[...])
    u_ref[0:CONV_HALO, :] = jnp.where(i > 0, halo, 0.0)
    u_ref[CONV_HALO:, :] = cv_ref[...] * _sigmoid(cg_ref[...])
    n_shift_rows = us_ref.shape[1]
    for sft in range(1, SUBLANES):
        us_ref[sft - 1] = u_ref[sft:sft + n_shift_rows, :]
    first_tap = CONV_HALO - (CONV_WIDTH - 1)

    def lane_tile(lt, _):
        lanes = pl.ds(pl.multiple_of(lt * LANES, LANES), LANES)
        wv = [wb_ref[w, :, lanes] for w in range(CONV_WIDTH)]
        bias = jnp.broadcast_to(b_ref[:, lanes], (SUBLANES, LANES))
        for g0 in range(0, t // SUBLANES, groups_per_pass):
            accs = [bias] * groups_per_pass
            for w in range(CONV_WIDTH):
                off = first_tap + w
                sft = off % SUBLANES
                for g in range(groups_per_pass):
                    lo = (g0 + g) * SUBLANES + off - sft
                    if sft == 0:
                        src = u_ref[lo:lo + SUBLANES, lanes]
                    else:
                        src = us_ref[sft - 1, lo:lo + SUBLANES, lanes]
                    accs[g] = accs[g] + wv[w] * src
            for g in range(groups_per_pass):
                lo = (g0 + g) * SUBLANES
                conv_ref[lo:lo + SUBLANES, lanes] = accs[g]
        return 0

    lax.fori_loop(0, dc // LANES, lane_tile, 0)

    r = rows_per_chunk
    for c in range(t // r):
        acc = conv_ref[c * r:(c + 1) * r, :]
        mu = jnp.mean(acc, axis=-1, keepdims=True)
        cen = acc - mu
        var = jnp.mean(cen * cen, axis=-1, keepdims=True)
        un = cen * lax.rsqrt(var + LN_EPS) * lg_ref[...] + lb_ref[...]
        o_ref[c * r:(c + 1) * r, :] = (un * _sigmoid(un)).astype(o_ref.dtype)


def _conv_module(cv, cg, w_dw, b_dw, ln_g, ln_b, t=512, rows_per_chunk=16, groups_per_pass=4):
    s, dc = cv.shape
    halo_blocks = t // CONV_HALO
    row = lambda i: (i, 0)
    prev = lambda i: (jnp.maximum(i * halo_blocks - 1, 0), 0)
    const = lambda i: (0, 0)
    return pl.pallas_call(
        functools.partial(_conv_body, rows_per_chunk=rows_per_chunk, groups_per_pass=groups_per_pass),
        grid=(s // t,),
        in_specs=[
            pl.BlockSpec((t, dc), row),
            pl.BlockSpec((t, dc), row),
            pl.BlockSpec((CONV_HALO, dc), prev),
            pl.BlockSpec((CONV_HALO, dc), prev),
            pl.BlockSpec((CONV_WIDTH, dc), const),
            pl.BlockSpec((1, dc), const),
            pl.BlockSpec((1, dc), const),
            pl.BlockSpec((1, dc), const),
        ],
        out_specs=pl.BlockSpec((t, dc), row),
        out_shape=jax.ShapeDtypeStruct((s, dc), BF16),
        scratch_shapes=[
            pltpu.VMEM((t + CONV_HALO, dc), F32),
            pltpu.VMEM((SUBLANES - 1, t + CONV_HALO - SUBLANES, dc), F32),
            pltpu.VMEM((CONV_WIDTH, SUBLANES, dc), F32),
            pltpu.VMEM((t, dc), F32),
        ],
        compiler_params=_cparams("arbitrary"),
        name="conv_module",
    )(cv, cg, cv, cg, w_dw, b_dw, ln_g, ln_b)


SOFTMAX_ROWS = 32


def _attn_pairs(q_ref, bands, bias_ref, s_ref, e_ref, o_ref):
    lane = lax.broadcasted_iota(I32, (PAIR, 2 * HEAD_DIM), 1)
    n_pairs = len(bands)
    rdens = {}

    def scores(u):
        kb, _, _ = bands[u]
        q2 = q_ref[u * PAIR:(u + 1) * PAIR, :]
        for h in range(2):
            in_head = (lane >= h * HEAD_DIM) & (lane < (h + 1) * HEAD_DIM)
            qh = jnp.where(in_head, q2, jnp.zeros_like(q2))
            s_ref[2 * u + h, :, :kb.shape[0]] = lax.dot_general(
                qh, kb, (((1,), (1,)), ((), ())), preferred_element_type=F32)

    def softmax(u):
        kb, _, col_lo = bands[u]
        n_keys = kb.shape[0]
        for h in range(2):
            c = 2 * u + h
            parts = []
            for r0 in range(0, PAIR, SOFTMAX_ROWS):
                rows = slice(r0, r0 + SOFTMAX_ROWS)
                t = s_ref[c, rows, :n_keys] + bias_ref[h, rows, col_lo:col_lo + n_keys]
                e = jnp.exp(t - jnp.max(t, axis=-1, keepdims=True))
                parts.append(1.0 / jnp.sum(e, axis=-1, keepdims=True))
                e_ref[c, rows, :n_keys] = e.astype(BF16)
            rdens[c] = jnp.concatenate(parts, axis=0)

    def values(u):
        _, vb, _ = bands[u]
        n_keys = vb.shape[0]
        outs = []
        for h in range(2):
            c = 2 * u + h
            o = jnp.dot(e_ref[c, :, :n_keys], vb, preferred_element_type=F32)
            outs.append(o * rdens[c])
        o_ref[u * PAIR:(u + 1) * PAIR, :] = jnp.where(lane < HEAD_DIM, outs[0], outs[1]).astype(o_ref.dtype)

    for step in range(n_pairs + 2):
        if step < n_pairs:
            scores(step)
        if 0 <= step - 1 < n_pairs:
            softmax(step - 1)
        if 0 <= step - 2 < n_pairs:
            values(step - 2)


def _attn_body(q_ref, k_ref, v_ref, g_ref, o_ref, bias_ref, s_ref, e_ref, *, pairs_per_step):
    st = pl.program_id(1)
    npp = pairs_per_step

    @pl.when(st == 0)
    def _():
        sub = lax.broadcasted_iota(I32, (SUBLANES, BAND_PAIR), 0)
        kj = lax.broadcasted_iota(I32, (SUBLANES, BAND_PAIR), 1)
        for h in range(2):
            t8 = jnp.broadcast_to(g_ref[h], (SUBLANES, BAND_PAIR))
            shift = 1
            while shift < SUBLANES:
                t8 = jnp.where((sub & shift) != 0, pltpu.roll(t8, shift, 1), t8)
                shift *= 2
            for r0 in range(0, PAIR, SUBLANES):
                band_lo = (r0 // CHUNK) * CHUNK
                in_band = (kj >= band_lo) & (kj < band_lo + (LEFT_CHUNKS + 1) * CHUNK)
                rows = t8 if r0 == 0 else pltpu.roll(t8, r0, 1)
                bias_ref[h, r0:r0 + SUBLANES, :] = jnp.where(in_band, rows, NEG_INF)

    def run(first_step):
        bands = []
        for u in range(npp):
            if first_step:
                n_keys = min((u + 1) * PAIR, BAND_PAIR)
                lo = max((u + 1) * PAIR - BAND_PAIR, 0)
                bands.append((k_ref[lo:lo + n_keys, :], v_ref[lo:lo + n_keys, :], BAND_PAIR - n_keys))
            else:
                start = pl.multiple_of((st * npp + u) * PAIR - FRONT_PAD, PAIR)
                bands.append((k_ref[pl.ds(start, BAND_PAIR), :], v_ref[pl.ds(start, BAND_PAIR), :], 0))
        _attn_pairs(q_ref, bands, bias_ref, s_ref, e_ref, o_ref)

    @pl.when(st == 0)
    def _():
        run(True)

    @pl.when(st > 0)
    def _():
        run(False)


def _distance_rows(rel_table):
    t = rel_table.astype(F32)
    far = t[:, 2 * REL_CLIP:]
    n_far = LEFT_CHUNKS * CHUNK - REL_CLIP + 1
    near = jnp.flip(t[:, REL_CLIP - CHUNK + 1:2 * REL_CLIP], axis=1)
    tail = BAND_PAIR - n_far - near.shape[1]
    g = jnp.concatenate([jnp.repeat(far, n_far, axis=1), near, jnp.repeat(far, tail, axis=1)], axis=1)
    return g[:, None, :]


def _attention(q, k, v, g_rows, pairs_per_step=8):
    s, da = q.shape
    n_hp = da // (2 * HEAD_DIM)
    tq = pairs_per_step * PAIR
    assert pairs_per_step * PAIR >= FRONT_PAD, "first step must cover every clamped band"
    return pl.pallas_call(
        functools.partial(_attn_body, pairs_per_step=pairs_per_step),
        grid=(n_hp, s // tq),
        in_specs=[
            pl.BlockSpec((tq, 2 * HEAD_DIM), lambda hp, st: (st, hp)),
            pl.BlockSpec((s, 2 * HEAD_DIM), lambda hp, st: (0, hp)),
            pl.BlockSpec((s, 2 * HEAD_DIM), lambda hp, st: (0, hp)),
            pl.BlockSpec((2, 1, BAND_PAIR), lambda hp, st: (hp, 0, 0)),
        ],
        out_specs=pl.BlockSpec((tq, 2 * HEAD_DIM), lambda hp, st: (st, hp)),
        out_shape=jax.ShapeDtypeStruct((s, da), BF16),
        scratch_shapes=[
            pltpu.VMEM((2, PAIR, BAND_PAIR), F32),
            pltpu.VMEM((2 * pairs_per_step, PAIR, BAND_PAIR), F32),
            pltpu.VMEM((2 * pairs_per_step, PAIR, BAND_PAIR), BF16),
        ],
        compiler_params=_cparams("arbitrary", "arbitrary"),
        name="chunk_attention",
    )(q, k, v, g_rows)


def _slab_rows(d):
    return d // LANES


def _store_slabs(ref, val):
    t, d = val.shape
    n = _slab_rows(d)
    for c in range(n):
        ref[pl.ds(c, t, stride=n), :] = val[:, c * LANES:(c + 1) * LANES]


def _load_slabs(ref, t, n):
    return jnp.concatenate([ref[pl.ds(c, t, stride=n), :] for c in range(n)], axis=1)


def _first_index_of_max(vals, sub, n):
    vmax = jnp.max(vals, axis=0, keepdims=True)
    idx = jnp.min(jnp.where(vals == vmax, sub, n), axis=0, keepdims=True)
    return vmax, idx


def _outproj_body(x_ref, co_ref, ao_ref, w_hbm, g2_ref, wr_ref, br_ref,
                  x2_ref, h2_ref, eid_ref, gcol_ref, w_ref, stage, wsem):
    @pl.when(pl.program_id(0) == 0)
    def _():
        _load_weight_as_bf16(w_hbm, w_ref, stage, wsem)

    dc = co_ref.shape[1]
    y = jnp.dot(co_ref[...], w_ref[0:dc, :], preferred_element_type=F32)
    y = y + jnp.dot(ao_ref[...], w_ref[dc:, :], preferred_element_type=F32)
    x2 = x_ref[...] + y
    ms = jnp.mean(x2 * x2, axis=-1, keepdims=True)
    h2 = x2 * lax.rsqrt(ms + RMS_EPS) * g2_ref[...]
    x2_ref[...] = x2
    _store_slabs(h2_ref, h2)

    hi = h2.astype(BF16)
    lo = (h2 - hi.astype(F32)).astype(BF16)
    pa = jnp.dot(hi, wr_ref[...], preferred_element_type=F32)
    pb = jnp.dot(lo, wr_ref[...], preferred_element_type=F32)
    logits = pa[:, :LANES] + pa[:, LANES:] + pb[:, :LANES]
    lt = logits.T + br_ref[...]
    tm = lt.shape[1]
    sub = lax.broadcasted_iota(I32, (N_GROUPS, tm), 0)
    g = lt[0:N_GROUPS, :]
    gmax, grp = _first_index_of_max(g, sub, N_GROUPS)
    p_grp = 1.0 / jnp.sum(jnp.exp(g - gmax), axis=0, keepdims=True)
    sel = jnp.zeros((EXPERTS_PER_GROUP, tm), F32)
    for gi in range(N_GROUPS):
        lo = N_GROUPS + gi * EXPERTS_PER_GROUP
        sel = jnp.where(grp == gi, lt[lo:lo + EXPERTS_PER_GROUP, :], sel)
    v1, i1 = _first_index_of_max(sel, sub, EXPERTS_PER_GROUP)
    rest = jnp.where(sub == i1, -jnp.inf, sel)
    v2, i2 = _first_index_of_max(rest, sub, EXPERTS_PER_GROUP)
    t = jnp.exp(v2 - v1)
    gate1 = p_grp / (1.0 + t)
    gate2 = p_grp * t / (1.0 + t)
    eid_ref[0:1, :] = grp * EXPERTS_PER_GROUP + i1
    eid_ref[1:2, :] = grp * EXPERTS_PER_GROUP + i2
    row = lax.broadcasted_iota(I32, (LANES, tm), 0)
    gl = jnp.where(row == 0, gate1, jnp.where(row == 1, gate2, 0.0))
    gcol_ref[...] = gl.T


def _out_proj_router(x, conv_out, attn_out, w, g2, wr_cat, br_col, tm=512, w_chunk=512):
    s, d = x.shape
    dc = conv_out.shape[1]
    da = attn_out.shape[1]
    row = lambda i: (i, 0)
    const = lambda i: (0, 0)
    return pl.pallas_call(
        _outproj_body,
        grid=(s // tm,),
        in_specs=[
            pl.BlockSpec((tm, d), row),
            pl.BlockSpec((tm, dc), row),
            pl.BlockSpec((tm, da), row),
            pl.BlockSpec(memory_space=pl.ANY),
            pl.BlockSpec((1, d), const),
            pl.BlockSpec((d, 2 * LANES), const),
            pl.BlockSpec((LANES, 1), const),
        ],
        out_specs=[
            pl.BlockSpec((tm, d), row),
            pl.BlockSpec((tm * _slab_rows(d), LANES), row),
            pl.BlockSpec((TOP_K, tm), lambda i: (0, i)),
            pl.BlockSpec((tm, LANES), row),
        ],
        out_shape=[
            jax.ShapeDtypeStruct((s, d), F32),
            jax.ShapeDtypeStruct((s * _slab_rows(d), LANES), F32),
            jax.ShapeDtypeStruct((TOP_K, s), I32),
            jax.ShapeDtypeStruct((s, LANES), F32),
        ],
        scratch_shapes=_weight_scratch(dc + da, d, w_chunk),
        compiler_params=_cparams("arbitrary"),
        name="out_proj_router",
    )(x, conv_out, attn_out, w, g2, wr_cat, br_col)


META_LANES = 256
(META_FIRST_BLK, META_N_BLK, META_COUNT, META_NEXT_EXP, META_ORDINAL,
 META_USED_BLKS, META_FIRST_USED) = range(7)


def _dispatch_body(eid_ref, dest_ref, meta_ref, rank_ref):
    s = eid_ref.shape[1]
    w = META_LANES
    sub = lax.broadcasted_iota(I32, (N_EXPERTS, w), 0)
    lane = lax.broadcasted_iota(I32, (N_EXPERTS, w), 1)
    ri = lax.broadcasted_iota(I32, (w, w), 0)
    ci = lax.broadcasted_iota(I32, (w, w), 1)
    upper = (ri < ci).astype(BF16)

    def onehots(b):
        off = pl.multiple_of(b * w, w)
        e0 = eid_ref[0:1, pl.ds(off, w)]
        e1 = eid_ref[1:2, pl.ds(off, w)]
        return off, sub == e0, sub == e1

    def rank_block(b, carry):
        off, o0, o1 = onehots(b)
        both = jnp.where(o0 | o1, 1.0, 0.0)
        pre = jnp.dot(both.astype(BF16), upper, preferred_element_type=F32) + carry
        rank_ref[0:1, pl.ds(off, w)] = jnp.sum(jnp.where(o0, pre, 0.0), axis=0, keepdims=True)
        rank_ref[1:2, pl.ds(off, w)] = jnp.sum(jnp.where(o1, pre, 0.0), axis=0, keepdims=True)
        return carry + jnp.sum(both, axis=1, keepdims=True)

    counts = lax.fori_loop(0, s // w, rank_block, jnp.zeros((N_EXPERTS, w), F32))

    nblk = ((counts.astype(I32) + (BLOCK_ROWS - 1)) // BLOCK_ROWS).astype(F32)
    er = lax.broadcasted_iota(I32, (N_EXPERTS, N_EXPERTS), 0)
    ec = lax.broadcasted_iota(I32, (N_EXPERTS, N_EXPERTS), 1)
    lower_incl = (ec <= er).astype(BF16)
    pend_blk = jnp.dot(lower_incl, nblk.astype(BF16), preferred_element_type=F32)
    pstart = (pend_blk - nblk) * float(BLOCK_ROWS)

    def on_lanes(v):
        return jnp.sum(jnp.where(sub == lane, v, 0), axis=0, keepdims=True)

    nblk_i = nblk.astype(I32)
    first_blk = on_lanes((pend_blk - nblk).astype(I32))
    n_blk = on_lanes(nblk_i)
    count = on_lanes(counts.astype(I32))
    has_rows = nblk_i > 0
    next_exp = jnp.min(jnp.where(has_rows & (sub > lane), sub, N_EXPERTS), axis=0, keepdims=True)
    next_exp = jnp.where(next_exp < N_EXPERTS, next_exp, -1)
    ordinal = jnp.sum(jnp.where(has_rows & (sub < lane), 1, 0), axis=0, keepdims=True)
    first_used = jnp.min(jnp.where(has_rows, sub, N_EXPERTS - 1), axis=0, keepdims=True)
    used = pend_blk[N_EXPERTS - 1:N_EXPERTS, :].astype(I32)
    mrow = lax.broadcasted_iota(I32, (8, w), 0)
    meta = jnp.where(mrow == META_FIRST_BLK, first_blk, 0)
    meta = jnp.where(mrow == META_N_BLK, n_blk, meta)
    meta = jnp.where(mrow == META_COUNT, count, meta)
    meta = jnp.where(mrow == META_NEXT_EXP, next_exp, meta)
    meta = jnp.where(mrow == META_ORDINAL, ordinal, meta)
    meta = jnp.where(mrow == META_FIRST_USED, first_used, meta)
    meta_ref[...] = jnp.where(mrow == META_USED_BLKS, used, meta)

    def dest_block(b, _):
        off, o0, o1 = onehots(b)
        d0 = rank_ref[0:1, pl.ds(off, w)] + jnp.sum(jnp.where(o0, pstart, 0.0), axis=0, keepdims=True)
        d1 = rank_ref[1:2, pl.ds(off, w)] + jnp.sum(jnp.where(o1, pstart, 0.0), axis=0, keepdims=True)
        dest_ref[0:1, pl.ds(off, w)] = d0.astype(I32)
        dest_ref[1:2, pl.ds(off, w)] = d1.astype(I32)
        return 0

    lax.fori_loop(0, s // w, dest_block, 0)


def _dispatch(eids):
    s = eids.shape[1]
    return pl.pallas_call(
        _dispatch_body,
        out_shape=[
            jax.ShapeDtypeStruct((TOP_K, s), I32),
            jax.ShapeDtypeStruct((8, META_LANES), I32),
        ],
        scratch_shapes=[pltpu.VMEM((TOP_K, s), F32)],
        compiler_params=pltpu.CompilerParams(vmem_limit_bytes=VMEM_LIMIT_BYTES),
        name="dispatch",
    )(eids)


PAD_BITS = tuple(reversed(range(BLOCK_ROWS.bit_length() - 1)))


def _scatter_body(dest_ref, meta_ref, h2_ref, xs_hbm, zbuf, sem, zsem, *, tm):
    i = pl.program_id(0)
    n_sl = h2_ref.shape[0] // tm
    blk = BLOCK_ROWS * n_sl
    n_blocks = xs_hbm.shape[0] // blk
    used = meta_ref[META_USED_BLKS, 0]

    def slab(row, n_tokens=1):
        return pl.ds(pl.multiple_of(row * n_sl, n_sl), n_tokens * n_sl)

    def token_copy(j, k):
        t = i * tm + j
        return pltpu.make_async_copy(h2_ref.at[slab(j)], xs_hbm.at[slab(dest_ref[k, t])], sem.at[k])

    def scatter_tile():
        def one(j, _):
            for k in range(TOP_K):
                token_copy(j, k).start(priority=k)
            return 0

        lax.fori_loop(0, tm, one, 0, unroll=8)

    def wait_tile():
        for k in range(TOP_K):
            pltpu.make_async_copy(h2_ref, xs_hbm.at[slab(0, tm)], sem.at[k]).wait()

    @pl.when(i > 0)
    def _():
        scatter_tile()
        wait_tile()

    @pl.when(i == 0)
    def _():
        _scatter_first_step(meta_ref, xs_hbm, zbuf, zsem, slab, used, n_blocks, scatter_tile, wait_tile)


def _scatter_first_step(meta_ref, xs_hbm, zbuf, zsem, slab, used, n_blocks, scatter_tile, wait_tile):
    n_sl = zbuf.shape[0] // BLOCK_ROWS

    def zero_copy(row, n_tokens):
        return pltpu.make_async_copy(zbuf.at[pl.ds(0, n_tokens * n_sl)], xs_hbm.at[slab(row, n_tokens)], zsem)

    zbuf[...] = jnp.zeros_like(zbuf)

    def per_expert(e, issued):
        b0 = meta_ref[META_FIRST_BLK, e]
        nb = meta_ref[META_N_BLK, e]
        cnt = meta_ref[META_COUNT, e]
        pad = nb * BLOCK_ROWS - cnt
        row = b0 * BLOCK_ROWS + cnt
        out = []
        for bit, n_issued in zip(PAD_BITS, issued):
            size = 1 << bit
            take = pad & size

            @pl.when(take != 0)
            def _():
                zero_copy(row, size).start()

            row = row + take
            out.append(n_issued + take // size)
        return tuple(out)

    issued = lax.fori_loop(0, N_EXPERTS, per_expert, tuple(jnp.int32(0) for _ in PAD_BITS))
    lax.fori_loop(used, n_blocks, lambda b, _: (zero_copy(b * BLOCK_ROWS, BLOCK_ROWS).start(), 0)[1], 0)

    scatter_tile()
    wait_tile()

    for bit, n_issued in zip(PAD_BITS, issued):
        lax.fori_loop(0, n_issued, lambda i, _, n=1 << bit: (zero_copy(0, n).wait(), 0)[1], 0)
    lax.fori_loop(used, n_blocks, lambda b, _: (zero_copy(0, BLOCK_ROWS).wait(), 0)[1], 0)


def _scatter_rows(dest, meta, h2_slabs, rows, tm=1024):
    s = dest.shape[1]
    n_sl = h2_slabs.shape[0] // s
    grid_spec = pltpu.PrefetchScalarGridSpec(
        num_scalar_prefetch=2,
        grid=(s // tm,),
        in_specs=[pl.BlockSpec((tm * n_sl, LANES), lambda i, dest_ref, meta_ref: (i, 0))],
        out_specs=pl.BlockSpec(memory_space=pl.ANY),
        scratch_shapes=[
            pltpu.VMEM((BLOCK_ROWS * n_sl, LANES), h2_slabs.dtype),
            pltpu.SemaphoreType.DMA((TOP_K,)),
            pltpu.SemaphoreType.DMA(()),
        ],
    )
    return pl.pallas_call(
        functools.partial(_scatter_body, tm=tm),
        grid_spec=grid_spec,
        out_shape=jax.ShapeDtypeStruct((rows * n_sl, LANES), h2_slabs.dtype),
        compiler_params=_cparams("arbitrary"),
        name="scatter_rows",
    )(dest, meta, h2_slabs)


def _row_gather_wait(src_hbm, dst, sem, n_rows):
    pltpu.make_async_copy(src_hbm.at[pl.ds(0, n_rows)], dst.at[pl.ds(0, n_rows)], sem).wait()


WEIGHT_QUEUES = (1, 1, 1)
ROW_BLOCK_QUEUE = 0


def _experts_body(meta_ref, xs_hbm, wg_hbm, wu_hbm, wd_hbm, ys_hbm,
                  xbuf, ybuf, zbuf, wgbuf, wubuf, wdbuf, gsem, ysem, zsem, wsem):
    used = meta_ref[META_USED_BLKS, 0]
    n_experts, d, _ = wg_hbm.shape
    n_sl = _slab_rows(d)
    blk = BLOCK_ROWS * n_sl
    n_blocks = ys_hbm.shape[0] // blk

    def xs_copy(b, sl):
        src_blk = jnp.minimum(b, n_blocks - 1)
        rows = pl.ds(pl.multiple_of(src_blk * blk, blk), blk)
        return pltpu.make_async_copy(xs_hbm.at[rows], xbuf.at[sl], gsem.at[sl])

    def gather_start(b, sl):
        xs_copy(b, sl).start(priority=ROW_BLOCK_QUEUE)

    def weights_start(e, ws):
        for c, queue in zip(weight_copies(e, ws), WEIGHT_QUEUES):
            c.start(priority=queue)

    def gather_wait(sl):
        xs_copy(0, sl).wait()

    def ys_copy(b):
        sl = b % 2
        rows = pl.ds(pl.multiple_of(b * blk, blk), blk)
        return pltpu.make_async_copy(ybuf.at[sl], ys_hbm.at[rows], ysem.at[sl])

    def zero_copy(b):
        rows = pl.ds(pl.multiple_of(b * blk, blk), blk)
        return pltpu.make_async_copy(zbuf, ys_hbm.at[rows], zsem)

    def weight_copies(e, ws):
        return (pltpu.make_async_copy(wg_hbm.at[e], wgbuf.at[ws], wsem.at[ws, 0]),
                pltpu.make_async_copy(wu_hbm.at[e], wubuf.at[ws], wsem.at[ws, 1]),
                pltpu.make_async_copy(wd_hbm.at[e], wdbuf.at[ws], wsem.at[ws, 2]))

    weights_start(meta_ref[META_FIRST_USED, 0], 0)
    gather_start(0, 0)
    zbuf[...] = jnp.zeros_like(zbuf)
    lax.fori_loop(used, n_blocks, lambda b, _: (zero_copy(b).start(priority=ROW_BLOCK_QUEUE), 0)[1], 0)

    def expert(e, _):
        nb = meta_ref[META_N_BLK, e]

        @pl.when(nb > 0)
        def _():
            b0 = meta_ref[META_FIRST_BLK, e]
            ws = meta_ref[META_ORDINAL, e] % 2
            nxt = meta_ref[META_NEXT_EXP, e]

            @pl.when(nxt >= 0)
            def _():
                weights_start(nxt, 1 - ws)

            for c in weight_copies(e, ws):
                c.wait()

            def block(j, _):
                b = b0 + j
                sl = b % 2
                gather_start(b + 1, 1 - sl)
                gather_wait(sl)

                @pl.when(b >= 2)
                def _():
                    ys_copy(b - 2).wait()

                x = _load_slabs(xbuf.at[sl], BLOCK_ROWS, n_sl).astype(BF16)
                a = jnp.dot(x, wgbuf[ws].astype(BF16), preferred_element_type=F32)
                u = jnp.dot(x, wubuf[ws].astype(BF16), preferred_element_type=F32)
                act = (a * _sigmoid(a) * u).astype(BF16)
                y = jnp.dot(act, wdbuf[ws].astype(BF16), preferred_element_type=F32)
                _store_slabs(ybuf.at[sl], y)
                ys_copy(b).start(priority=ROW_BLOCK_QUEUE)
                return 0

            lax.fori_loop(0, nb, block, 0)

        return 0

    lax.fori_loop(0, n_experts, expert, 0)

    gather_wait(used % 2)
    for back in (1, 2):
        @pl.when(used >= back)
        def _():
            ys_copy(used - back).wait()
    lax.fori_loop(used, n_blocks, lambda b, _: (zero_copy(b).wait(), 0)[1], 0)


def _experts(meta, xs_slabs, w_gate, w_up, w_down):
    n_e, d, de = w_gate.shape
    n_sl = _slab_rows(d)
    blk = BLOCK_ROWS * n_sl
    smem = pl.BlockSpec(memory_space=pltpu.SMEM)
    hbm = pl.BlockSpec(memory_space=pl.ANY)
    return pl.pallas_call(
        _experts_body,
        in_specs=[smem, hbm, hbm, hbm, hbm],
        out_specs=hbm,
        out_shape=jax.ShapeDtypeStruct(xs_slabs.shape, F32),
        scratch_shapes=[
            pltpu.VMEM((2, blk, LANES), F32),
            pltpu.VMEM((2, blk, LANES), F32),
            pltpu.VMEM((blk, LANES), F32),
            pltpu.VMEM((2, d, de), F32),
            pltpu.VMEM((2, d, de), F32),
            pltpu.VMEM((2, de, d), F32),
            pltpu.SemaphoreType.DMA((2,)),
            pltpu.SemaphoreType.DMA((2,)),
            pltpu.SemaphoreType.DMA(()),
            pltpu.SemaphoreType.DMA((2, 3)),
        ],
        compiler_params=pltpu.CompilerParams(vmem_limit_bytes=VMEM_LIMIT_BYTES),
        name="expert_mlp",
    )(meta, xs_slabs, w_gate, w_up, w_down)


def _combine_body(dest_ref, x2_ref, gcol_ref, fg_ref, ys_hbm, o_ref, ybuf, sem, *, tm):
    i = pl.program_id(0)
    n = pl.num_programs(0)
    s = n * tm
    slot = i % 2
    n_sl = _slab_rows(x2_ref.shape[1])

    def start(tile, sl):
        for k in range(TOP_K):
            for r in range(tm):
                row = dest_ref[k * s + tile * tm + r]
                src = ys_hbm.at[pl.ds(pl.multiple_of(row * n_sl, n_sl), n_sl)]
                pltpu.make_async_copy(src, ybuf.at[sl, k, pl.ds(r * n_sl, n_sl)],
                                      sem.at[sl, k]).start(priority=r % 2)

    def wait(sl):
        for k in range(TOP_K):
            _row_gather_wait(ys_hbm, ybuf.at[sl, k], sem.at[sl, k], tm * n_sl)

    @pl.when(i == 0)
    def _():
        start(0, 0)

    start(jnp.minimum(i + 1, n - 1), 1 - slot)
    wait(slot)
    g = gcol_ref[...]
    y0 = _load_slabs(ybuf.at[slot, 0], tm, n_sl)
    y1 = _load_slabs(ybuf.at[slot, 1], tm, n_sl)
    moe = g[:, 0:1] * y0 + g[:, 1:2] * y1
    x3 = x2_ref[...] + moe
    ms = jnp.mean(x3 * x3, axis=-1, keepdims=True)
    o_ref[...] = x3 * lax.rsqrt(ms + RMS_EPS) * fg_ref[...]

    @pl.when(i == n - 1)
    def _():
        wait(1 - slot)


def _combine(dest_flat, x2, gcol, final_g, ys, tm=256):
    s, d = x2.shape
    row = lambda i, dest_ref: (i, 0)
    const = lambda i, dest_ref: (0, 0)
    grid_spec = pltpu.PrefetchScalarGridSpec(
        num_scalar_prefetch=1,
        grid=(s // tm,),
        in_specs=[
            pl.BlockSpec((tm, d), row),
            pl.BlockSpec((tm, LANES), row),
            pl.BlockSpec((1, d), const),
            pl.BlockSpec(memory_space=pl.ANY),
        ],
        out_specs=pl.BlockSpec((tm, d), row),
        scratch_shapes=[
            pltpu.VMEM((2, TOP_K, tm * _slab_rows(d), LANES), F32),
            pltpu.SemaphoreType.DMA((2, TOP_K)),
        ],
    )
    return pl.pallas_call(
        functools.partial(_combine_body, tm=tm),
        grid_spec=grid_spec,
        out_shape=jax.ShapeDtypeStruct((s, d), F32),
        compiler_params=_cparams("arbitrary"),
        name="combine_norm",
    )(dest_flat, x2, gcol, final_g, ys)


def _layer(x, norm1_g, w_in, w_dw, b_dw, ln_g, ln_b, rel_bias, w_out, norm2_g,
           w_group, b_group, w_expert_router, b_expert_router, w_gate, w_up, w_down, out_g):
    s, d = x.shape
    dc = w_dw.shape[-1]
    da = (w_in.shape[1] - 2 * dc) // 3
    row2 = lambda a: a.reshape(1, -1)

    cv, cg, q, k, v = _in_proj(x, row2(norm1_g), w_in, dc, da)
    conv_out = _conv_module(cv, cg, w_dw.reshape(CONV_WIDTH, dc), row2(b_dw), row2(ln_g), row2(ln_b))
    attn_out = _attention(q, k, v, _distance_rows(rel_bias))

    n_r = N_GROUPS + N_EXPERTS
    wr = jnp.pad(jnp.concatenate([w_group, w_expert_router], axis=1), ((0, 0), (0, LANES - n_r)))
    wr_hi = wr.astype(BF16)
    wr_lo = (wr - wr_hi.astype(F32)).astype(BF16)
    wr_cat = jnp.concatenate([wr_hi, wr_lo], axis=1)
    br_col = jnp.zeros((LANES, 1), F32).at[:n_r, 0].set(jnp.concatenate([b_group, b_expert_router]))
    x2, h2, eids, gcol = _out_proj_router(x, conv_out, attn_out, w_out, row2(norm2_g),
                                          wr_cat, br_col)

    n_blocks = (s * TOP_K) // BLOCK_ROWS + N_EXPERTS
    assert n_blocks <= META_LANES
    dest, meta = _dispatch(eids)
    xs = _scatter_rows(dest, meta, h2, n_blocks * BLOCK_ROWS)
    ys = _experts(meta, xs, w_gate, w_up, w_down)
    return _combine(dest.reshape(-1), x2, gcol, row2(out_g), ys)


def kernel(x, norm1_g, w_in, w_dw, b_dw, ln_g, ln_b, rel_bias, w_out, norm2_g, w_group, b_group,
           w_expert_router, b_expert_router, w_gate, w_up, w_down, final_g):
    depth = norm1_g.shape[0]
    assert depth == 1, "the final norm is fused into the last layer's combine step"
    batch, s, d = x.shape
    assert batch == 1, "one sequence per call"
    out = _layer(x.reshape(s, d), norm1_g[0], w_in[0], w_dw[0], b_dw[0], ln_g[0], ln_b[0], rel_bias[0],
                 w_out[0], norm2_g[0], w_group[0], b_group[0], w_expert_router[0],
                 b_expert_router[0], w_gate[0], w_up[0], w_down[0], final_g)
    return out.reshape(batch, s, d)
```

```python
import functools

import jax
import jax.numpy as jnp
from jax import lax
from jax.experimental import pallas as pl
from jax.experimental.pallas import tpu as pltpu

F32 = jnp.float32
BF16 = jnp.bfloat16
I32 = jnp.int32

CHUNK = 64
HEAD_DIM = 64
LEFT_CHUNKS = 8
REL_CLIP = 128
CONV_WIDTH = 31
N_GROUPS = 8
EXPERTS_PER_GROUP = 8
N_EXPERTS = N_GROUPS * EXPERTS_PER_GROUP
TOP_K = 2
BLOCK_ROWS = 128
RMS_EPS = 1e-6
LN_EPS = 1e-5
NEG_INF = -1e30

VMEM_LIMIT_BYTES = 56 * 1024 * 1024
LANES = 128
SUBLANES = 8
PAIR = 2 * CHUNK
BAND_PAIR = (LEFT_CHUNKS + 2) * CHUNK
FRONT_PAD = LEFT_CHUNKS * CHUNK
CONV_HALO = 32


def _cparams(*sem):
    return pltpu.CompilerParams(dimension_semantics=sem, vmem_limit_bytes=VMEM_LIMIT_BYTES)


def _load_weight_as_bf16(w_hbm, w_ref, stage, wsem):
    cw = stage.shape[2]
    n_chunks = w_ref.shape[1] // cw

    def chunk_copy(c):
        return pltpu.make_async_copy(w_hbm.at[:, c * cw:(c + 1) * cw], stage.at[c % 2], wsem.at[c % 2])

    chunk_copy(0).start()
    for c in range(n_chunks):
        if c + 1 < n_chunks:
            chunk_copy(c + 1).start()
        chunk_copy(c).wait()
        w_ref[:, c * cw:(c + 1) * cw] = stage[c % 2].astype(BF16)


def _weight_scratch(d, d_out, chunk):
    return [pltpu.VMEM((d, d_out), BF16), pltpu.VMEM((2, d, chunk), F32), pltpu.SemaphoreType.DMA((2,))]


def _inproj_body(x_ref, g_ref, w_hbm, cv_ref, cg_ref, q_ref, k_ref, v_ref, w_ref, stage, wsem):
    @pl.when(pl.program_id(0) == 0)
    def _():
        _load_weight_as_bf16(w_hbm, w_ref, stage, wsem)

    x = x_ref[...]
    ms = jnp.mean(x * x, axis=-1, keepdims=True)
    h = (x * lax.rsqrt(ms + RMS_EPS) * g_ref[...]).astype(BF16)
    dc = cv_ref.shape[1]
    da = q_ref.shape[1]

    def proj(lo, width):
        return jnp.dot(h, w_ref[:, lo:lo + width], preferred_element_type=F32)

    cv_ref[...] = proj(0, dc)
    cg_ref[...] = proj(dc, dc)
    q_ref[...] = (proj(2 * dc, da) * (HEAD_DIM ** -0.5)).astype(BF16)
    k_ref[...] = proj(2 * dc + da, da).astype(BF16)
    v_ref[...] = proj(2 * dc + 2 * da, da).astype(BF16)


def _in_proj(x, g, w, d_conv, d_att, tm=512, w_chunk=256):
    s, d = x.shape
    assert s % tm == 0 and w.shape[1] % w_chunk == 0
    d_in = w.shape[1]
    row = lambda i: (i, 0)
    const = lambda i: (0, 0)
    return pl.pallas_call(
        _inproj_body,
        grid=(s // tm,),
        in_specs=[
            pl.BlockSpec((tm, d), row),
            pl.BlockSpec((1, d), const),
            pl.BlockSpec(memory_space=pl.ANY),
        ],
        out_specs=[
            pl.BlockSpec((tm, d_conv), row),
            pl.BlockSpec((tm, d_conv), row),
            pl.BlockSpec((tm, d_att), row),
            pl.BlockSpec((tm, d_att), row),
            pl.BlockSpec((tm, d_att), row),
        ],
        out_shape=[
            jax.ShapeDtypeStruct((s, d_conv), F32),
            jax.ShapeDtypeStruct((s, d_conv), F32),
            jax.ShapeDtypeStruct((s, d_att), BF16),
            jax.ShapeDtypeStruct((s, d_att), BF16),
            jax.ShapeDtypeStruct((s, d_att), BF16),
        ],
        scratch_shapes=_weight_scratch(d, d_in, w_chunk),
        compiler_params=_cparams("arbitrary"),
        name="in_proj",
    )(x, g, w)


def _sigmoid(x):
    return 1.0 / (1.0 + jnp.exp(-x))


def _conv_body(cv_ref, cg_ref, cvh_ref, cgh_ref, w_ref, b_ref, lg_ref, lb_ref, o_ref,
               u_ref, us_ref, wb_ref, conv_ref, *, rows_per_chunk, groups_per_pass):
    i = pl.program_id(0)
    t, dc = cv_ref.shape

    @pl.when(i == 0)
    def _():
        for w in range(CONV_WIDTH):
            wb_ref[w] = jnp.broadcast_to(w_ref[w:w + 1, :], (SUBLANES, dc))

    halo = cvh_ref[...] * _sigmoid(cgh_ref[...])
    u_ref[0:CONV_HALO, :] = jnp.where(i > 0, halo, 0.0)
    u_ref[CONV_HALO:, :] = cv_ref[...] * _sigmoid(cg_ref[...])
    n_shift_rows = us_ref.shape[1]
    for sft in range(1, SUBLANES):
        us_ref[sft - 1] = u_ref[sft:sft + n_shift_rows, :]
    first_tap = CONV_HALO - (CONV_WIDTH - 1)

    def lane_tile(lt, _):
        lanes = pl.ds(pl.multiple_of(lt * LANES, LANES), LANES)
        wv = [wb_ref[w, :, lanes] for w in range(CONV_WIDTH)]
        bias = jnp.broadcast_to(b_ref[:, lanes], (SUBLANES, LANES))
        for g0 in range(0, t // SUBLANES, groups_per_pass):
            accs = [bias] * groups_per_pass
            for w in range(CONV_WIDTH):
                off = first_tap + w
                sft = off % SUBLANES
                for g in range(groups_per_pass):
                    lo = (g0 + g) * SUBLANES + off - sft
                    if sft == 0:
                        src = u_ref[lo:lo + SUBLANES, lanes]
                    else:
                        src = us_ref[sft - 1, lo:lo + SUBLANES, lanes]
                    accs[g] = accs[g] + wv[w] * src
            for g in range(groups_per_pass):
                lo = (g0 + g) * SUBLANES
                conv_ref[lo:lo + SUBLANES, lanes] = accs[g]
        return 0

    lax.fori_loop(0, dc // LANES, lane_tile, 0)

    r = rows_per_chunk
    for c in range(t // r):
        acc = conv_ref[c * r:(c + 1) * r, :]
        mu = jnp.mean(acc, axis=-1, keepdims=True)
        cen = acc - mu
        var = jnp.mean(cen * cen, axis=-1, keepdims=True)
        un = cen * lax.rsqrt(var + LN_EPS) * lg_ref[...] + lb_ref[...]
        o_ref[c * r:(c + 1) * r, :] = (un * _sigmoid(un)).astype(o_ref.dtype)


def _conv_module(cv, cg, w_dw, b_dw, ln_g, ln_b, t=512, rows_per_chunk=16, groups_per_pass=4):
    s, dc = cv.shape
    assert s % t == 0
    halo_blocks = t // CONV_HALO
    row = lambda i: (i, 0)
    prev = lambda i: (jnp.maximum(i * halo_blocks - 1, 0), 0)
    const = lambda i: (0, 0)
    return pl.pallas_call(
        functools.partial(_conv_body, rows_per_chunk=rows_per_chunk, groups_per_pass=groups_per_pass),
        grid=(s // t,),
        in_specs=[
            pl.BlockSpec((t, dc), row),
            pl.BlockSpec((t, dc), row),
            pl.BlockSpec((CONV_HALO, dc), prev),
            pl.BlockSpec((CONV_HALO, dc), prev),
            pl.BlockSpec((CONV_WIDTH, dc), const),
            pl.BlockSpec((1, dc), const),
            pl.BlockSpec((1, dc), const),
            pl.BlockSpec((1, dc), const),
        ],
        out_specs=pl.BlockSpec((t, dc), row),
        out_shape=jax.ShapeDtypeStruct((s, dc), BF16),
        scratch_shapes=[
            pltpu.VMEM((t + CONV_HALO, dc), F32),
            pltpu.VMEM((SUBLANES - 1, t + CONV_HALO - SUBLANES, dc), F32),
            pltpu.VMEM((CONV_WIDTH, SUBLANES, dc), F32),
            pltpu.VMEM((t, dc), F32),
        ],
        compiler_params=_cparams("arbitrary"),
        name="conv_module",
    )(cv, cg, cv, cg, w_dw, b_dw, ln_g, ln_b)


SOFTMAX_ROWS = 32


def _attn_pairs(q_ref, bands, bias_ref, s_ref, e_ref, o_ref):
    lane = lax.broadcasted_iota(I32, (PAIR, 2 * HEAD_DIM), 1)
    n_pairs = len(bands)
    rdens = {}

    def scores(u):
        kb, _, _ = bands[u]
        q2 = q_ref[u * PAIR:(u + 1) * PAIR, :]
        for h in range(2):
            in_head = (lane >= h * HEAD_DIM) & (lane < (h + 1) * HEAD_DIM)
            qh = jnp.where(in_head, q2, jnp.zeros_like(q2))
            s_ref[2 * u + h, :, :kb.shape[0]] = lax.dot_general(
                qh, kb, (((1,), (1,)), ((), ())), preferred_element_type=F32)

    def softmax(u):
        kb, _, col_lo = bands[u]
        n_keys = kb.shape[0]
        for h in range(2):
            c = 2 * u + h
            parts = []
            for r0 in range(0, PAIR, SOFTMAX_ROWS):
                rows = slice(r0, r0 + SOFTMAX_ROWS)
                t = s_ref[c, rows, :n_keys] + bias_ref[h, rows, col_lo:col_lo + n_keys]
                e = jnp.exp(t - jnp.max(t, axis=-1, keepdims=True))
                parts.append(1.0 / jnp.sum(e, axis=-1, keepdims=True))
                e_ref[c, rows, :n_keys] = e.astype(BF16)
            rdens[c] = jnp.concatenate(parts, axis=0)

    def values(u):
        _, vb, _ = bands[u]
        n_keys = vb.shape[0]
        outs = []
        for h in range(2):
            c = 2 * u + h
            o = jnp.dot(e_ref[c, :, :n_keys], vb, preferred_element_type=F32)
            outs.append(o * rdens[c])
        o_ref[u * PAIR:(u + 1) * PAIR, :] = jnp.where(lane < HEAD_DIM, outs[0], outs[1]).astype(o_ref.dtype)

    for step in range(n_pairs + 2):
        if step < n_pairs:
            scores(step)
        if 0 <= step - 1 < n_pairs:
            softmax(step - 1)
        if 0 <= step - 2 < n_pairs:
            values(step - 2)


def _attn_body(q_ref, k_ref, v_ref, g_ref, o_ref, bias_ref, s_ref, e_ref, *, pairs_per_step):
    st = pl.program_id(1)
    npp = pairs_per_step

    @pl.when(st == 0)
    def _():
        sub = lax.broadcasted_iota(I32, (SUBLANES, BAND_PAIR), 0)
        kj = lax.broadcasted_iota(I32, (SUBLANES, BAND_PAIR), 1)
        for h in range(2):
            t8 = jnp.broadcast_to(g_ref[h], (SUBLANES, BAND_PAIR))
            shift = 1
            while shift < SUBLANES:
                t8 = jnp.where((sub & shift) != 0, pltpu.roll(t8, shift, 1), t8)
                shift *= 2
            for r0 in range(0, PAIR, SUBLANES):
                band_lo = (r0 // CHUNK) * CHUNK
                in_band = (kj >= band_lo) & (kj < band_lo + (LEFT_CHUNKS + 1) * CHUNK)
                rows = t8 if r0 == 0 else pltpu.roll(t8, r0, 1)
                bias_ref[h, r0:r0 + SUBLANES, :] = jnp.where(in_band, rows, NEG_INF)

    def run(first_step):
        bands = []
        for u in range(npp):
            if first_step:
                n_keys = min((u + 1) * PAIR, BAND_PAIR)
                lo = max((u + 1) * PAIR - BAND_PAIR, 0)
                bands.append((k_ref[lo:lo + n_keys, :], v_ref[lo:lo + n_keys, :], BAND_PAIR - n_keys))
            else:
                start = pl.multiple_of((st * npp + u) * PAIR - FRONT_PAD, PAIR)
                bands.append((k_ref[pl.ds(start, BAND_PAIR), :], v_ref[pl.ds(start, BAND_PAIR), :], 0))
        _attn_pairs(q_ref, bands, bias_ref, s_ref, e_ref, o_ref)

    @pl.when(st == 0)
    def _():
        run(True)

    @pl.when(st > 0)
    def _():
        run(False)


def _distance_rows(rel_table):
    t = rel_table.astype(F32)
    far = t[:, 2 * REL_CLIP:]
    n_far = LEFT_CHUNKS * CHUNK - REL_CLIP + 1
    near = jnp.flip(t[:, REL_CLIP - CHUNK + 1:2 * REL_CLIP], axis=1)
    tail = BAND_PAIR - n_far - near.shape[1]
    g = jnp.concatenate([jnp.repeat(far, n_far, axis=1), near, jnp.repeat(far, tail, axis=1)], axis=1)
    return g[:, None, :]


def _attention(q, k, v, g_rows, pairs_per_step=8):
    s, da = q.shape
    n_hp = da // (2 * HEAD_DIM)
    tq = pairs_per_step * PAIR
    assert tq >= FRONT_PAD, "first step must cover every clamped band"
    assert s % tq == 0
    return pl.pallas_call(
        functools.partial(_attn_body, pairs_per_step=pairs_per_step),
        grid=(n_hp, s // tq),
        in_specs=[
            pl.BlockSpec((tq, 2 * HEAD_DIM), lambda hp, st: (st, hp)),
            pl.BlockSpec((s, 2 * HEAD_DIM), lambda hp, st: (0, hp)),
            pl.BlockSpec((s, 2 * HEAD_DIM), lambda hp, st: (0, hp)),
            pl.BlockSpec((2, 1, BAND_PAIR), lambda hp, st: (hp, 0, 0)),
        ],
        out_specs=pl.BlockSpec((tq, 2 * HEAD_DIM), lambda hp, st: (st, hp)),
        out_shape=jax.ShapeDtypeStruct((s, da), BF16),
        scratch_shapes=[
            pltpu.VMEM((2, PAIR, BAND_PAIR), F32),
            pltpu.VMEM((2 * pairs_per_step, PAIR, BAND_PAIR), F32),
            pltpu.VMEM((2 * pairs_per_step, PAIR, BAND_PAIR), BF16),
        ],
        compiler_params=_cparams("arbitrary", "arbitrary"),
        name="chunk_attention",
    )(q, k, v, g_rows)


def _slab_rows(d):
    return d // LANES


def _store_slabs(ref, val):
    t, d = val.shape
    n = _slab_rows(d)
    for c in range(n):
        ref[pl.ds(c, t, stride=n), :] = val[:, c * LANES:(c + 1) * LANES]


def _load_slabs(ref, t, n):
    return jnp.concatenate([ref[pl.ds(c, t, stride=n), :] for c in range(n)], axis=1)


def _first_index_of_max(vals, sub, n):
    vmax = jnp.max(vals, axis=0, keepdims=True)
    idx = jnp.min(jnp.where(vals == vmax, sub, n), axis=0, keepdims=True)
    return vmax, idx


def _outproj_body(x_ref, co_ref, ao_ref, w_hbm, g2_ref, wr_ref, br_ref,
                  x2_ref, h2_ref, eid_ref, gcol_ref, w_ref, stage, wsem):
    @pl.when(pl.program_id(0) == 0)
    def _():
        _load_weight_as_bf16(w_hbm, w_ref, stage, wsem)

    dc = co_ref.shape[1]
    y = jnp.dot(co_ref[...], w_ref[0:dc, :], preferred_element_type=F32)
    y = y + jnp.dot(ao_ref[...], w_ref[dc:, :], preferred_element_type=F32)
    x2 = x_ref[...] + y
    ms = jnp.mean(x2 * x2, axis=-1, keepdims=True)
    h2 = x2 * lax.rsqrt(ms + RMS_EPS) * g2_ref[...]
    x2_ref[...] = x2
    _store_slabs(h2_ref, h2)

    hi = h2.astype(BF16)
    lo = (h2 - hi.astype(F32)).astype(BF16)
    pa = jnp.dot(hi, wr_ref[...], preferred_element_type=F32)
    pb = jnp.dot(lo, wr_ref[...], preferred_element_type=F32)
    logits = pa[:, :LANES] + pa[:, LANES:] + pb[:, :LANES]
    lt = logits.T + br_ref[...]
    tm = lt.shape[1]
    sub = lax.broadcasted_iota(I32, (N_GROUPS, tm), 0)
    g = lt[0:N_GROUPS, :]
    gmax, grp = _first_index_of_max(g, sub, N_GROUPS)
    p_grp = 1.0 / jnp.sum(jnp.exp(g - gmax), axis=0, keepdims=True)
    sel = jnp.zeros((EXPERTS_PER_GROUP, tm), F32)
    for gi in range(N_GROUPS):
        lo = N_GROUPS + gi * EXPERTS_PER_GROUP
        sel = jnp.where(grp == gi, lt[lo:lo + EXPERTS_PER_GROUP, :], sel)
    v1, i1 = _first_index_of_max(sel, sub, EXPERTS_PER_GROUP)
    rest = jnp.where(sub == i1, -jnp.inf, sel)
    v2, i2 = _first_index_of_max(rest, sub, EXPERTS_PER_GROUP)
    t = jnp.exp(v2 - v1)
    gate1 = p_grp / (1.0 + t)
    gate2 = p_grp * t / (1.0 + t)
    eid_ref[0:1, :] = grp * EXPERTS_PER_GROUP + i1
    eid_ref[1:2, :] = grp * EXPERTS_PER_GROUP + i2
    row = lax.broadcasted_iota(I32, (LANES, tm), 0)
    gl = jnp.where(row == 0, gate1, jnp.where(row == 1, gate2, 0.0))
    gcol_ref[...] = gl.T


def _out_proj_router(x, conv_out, attn_out, w, g2, wr_cat, br_col, tm=512, w_chunk=512):
    s, d = x.shape
    assert s % tm == 0
    dc = conv_out.shape[1]
    da = attn_out.shape[1]
    row = lambda i: (i, 0)
    const = lambda i: (0, 0)
    return pl.pallas_call(
        _outproj_body,
        grid=(s // tm,),
        in_specs=[
            pl.BlockSpec((tm, d), row),
            pl.BlockSpec((tm, dc), row),
            pl.BlockSpec((tm, da), row),
            pl.BlockSpec(memory_space=pl.ANY),
            pl.BlockSpec((1, d), const),
            pl.BlockSpec((d, 2 * LANES), const),
            pl.BlockSpec((LANES, 1), const),
        ],
        out_specs=[
            pl.BlockSpec((tm, d), row),
            pl.BlockSpec((tm * _slab_rows(d), LANES), row),
            pl.BlockSpec((TOP_K, tm), lambda i: (0, i)),
            pl.BlockSpec((tm, LANES), row),
        ],
        out_shape=[
            jax.ShapeDtypeStruct((s, d), F32),
            jax.ShapeDtypeStruct((s * _slab_rows(d), LANES), F32),
            jax.ShapeDtypeStruct((TOP_K, s), I32),
            jax.ShapeDtypeStruct((s, LANES), F32),
        ],
        scratch_shapes=_weight_scratch(dc + da, d, w_chunk),
        compiler_params=_cparams("arbitrary"),
        name="out_proj_router",
    )(x, conv_out, attn_out, w, g2, wr_cat, br_col)


META_LANES = 256
(META_FIRST_BLK, META_N_BLK, META_COUNT, META_NEXT_EXP, META_ORDINAL,
 META_USED_BLKS, META_FIRST_USED) = range(7)


def _dispatch_body(eid_ref, dest_ref, meta_ref, rank_ref):
    s = eid_ref.shape[1]
    w = META_LANES
    sub = lax.broadcasted_iota(I32, (N_EXPERTS, w), 0)
    lane = lax.broadcasted_iota(I32, (N_EXPERTS, w), 1)
    ri = lax.broadcasted_iota(I32, (w, w), 0)
    ci = lax.broadcasted_iota(I32, (w, w), 1)
    upper = (ri < ci).astype(BF16)

    def onehots(b):
        off = pl.multiple_of(b * w, w)
        e0 = eid_ref[0:1, pl.ds(off, w)]
        e1 = eid_ref[1:2, pl.ds(off, w)]
        return off, sub == e0, sub == e1

    def rank_block(b, carry):
        off, o0, o1 = onehots(b)
        both = jnp.where(o0 | o1, 1.0, 0.0)
        pre = jnp.dot(both.astype(BF16), upper, preferred_element_type=F32) + carry
        rank_ref[0:1, pl.ds(off, w)] = jnp.sum(jnp.where(o0, pre, 0.0), axis=0, keepdims=True)
        rank_ref[1:2, pl.ds(off, w)] = jnp.sum(jnp.where(o1, pre, 0.0), axis=0, keepdims=True)
        return carry + jnp.sum(both, axis=1, keepdims=True)

    counts = lax.fori_loop(0, s // w, rank_block, jnp.zeros((N_EXPERTS, w), F32))

    nblk = ((counts.astype(I32) + (BLOCK_ROWS - 1)) // BLOCK_ROWS).astype(F32)
    er = lax.broadcasted_iota(I32, (N_EXPERTS, N_EXPERTS), 0)
    ec = lax.broadcasted_iota(I32, (N_EXPERTS, N_EXPERTS), 1)
    lower_incl = (ec <= er).astype(BF16)
    pend_blk = jnp.dot(lower_incl, nblk.astype(BF16), preferred_element_type=F32)
    pstart = (pend_blk - nblk) * float(BLOCK_ROWS)

    def on_lanes(v):
        return jnp.sum(jnp.where(sub == lane, v, 0), axis=0, keepdims=True)

    nblk_i = nblk.astype(I32)
    first_blk = on_lanes((pend_blk - nblk).astype(I32))
    n_blk = on_lanes(nblk_i)
    count = on_lanes(counts.astype(I32))
    has_rows = nblk_i > 0
    next_exp = jnp.min(jnp.where(has_rows & (sub > lane), sub, N_EXPERTS), axis=0, keepdims=True)
    next_exp = jnp.where(next_exp < N_EXPERTS, next_exp, -1)
    ordinal = jnp.sum(jnp.where(has_rows & (sub < lane), 1, 0), axis=0, keepdims=True)
    first_used = jnp.min(jnp.where(has_rows, sub, N_EXPERTS - 1), axis=0, keepdims=True)
    used = pend_blk[N_EXPERTS - 1:N_EXPERTS, :].astype(I32)
    mrow = lax.broadcasted_iota(I32, (8, w), 0)
    meta = jnp.where(mrow == META_FIRST_BLK, first_blk, 0)
    meta = jnp.where(mrow == META_N_BLK, n_blk, meta)
    meta = jnp.where(mrow == META_COUNT, count, meta)
    meta = jnp.where(mrow == META_NEXT_EXP, next_exp, meta)
    meta = jnp.where(mrow == META_ORDINAL, ordinal, meta)
    meta = jnp.where(mrow == META_FIRST_USED, first_used, meta)
    meta_ref[...] = jnp.where(mrow == META_USED_BLKS, used, meta)

    def dest_block(b, _):
        off, o0, o1 = onehots(b)
        d0 = rank_ref[0:1, pl.ds(off, w)] + jnp.sum(jnp.where(o0, pstart, 0.0), axis=0, keepdims=True)
        d1 = rank_ref[1:2, pl.ds(off, w)] + jnp.sum(jnp.where(o1, pstart, 0.0), axis=0, keepdims=True)
        dest_ref[0:1, pl.ds(off, w)] = d0.astype(I32)
        dest_ref[1:2, pl.ds(off, w)] = d1.astype(I32)
        return 0

    lax.fori_loop(0, s // w, dest_block, 0)


def _dispatch(eids):
    s = eids.shape[1]
    return pl.pallas_call(
        _dispatch_body,
        out_shape=[
            jax.ShapeDtypeStruct((TOP_K, s), I32),
            jax.ShapeDtypeStruct((8, META_LANES), I32),
        ],
        scratch_shapes=[pltpu.VMEM((TOP_K, s), F32)],
        compiler_params=pltpu.CompilerParams(vmem_limit_bytes=VMEM_LIMIT_BYTES),
        name="dispatch",
    )(eids)


PAD_BITS = tuple(reversed(range(BLOCK_ROWS.bit_length() - 1)))


def _scatter_body(dest_ref, meta_ref, h2_ref, xs_hbm, zbuf, sem, zsem, *, tm):
    i = pl.program_id(0)
    n_sl = h2_ref.shape[0] // tm
    blk = BLOCK_ROWS * n_sl
    n_blocks = xs_hbm.shape[0] // blk
    used = meta_ref[META_USED_BLKS, 0]

    def slab(row, n_tokens=1):
        return pl.ds(pl.multiple_of(row * n_sl, n_sl), n_tokens * n_sl)

    def token_copy(j, k):
        t = i * tm + j
        return pltpu.make_async_copy(h2_ref.at[slab(j)], xs_hbm.at[slab(dest_ref[k, t])], sem.at[k])

    def scatter_tile():
        def one(j, _):
            for k in range(TOP_K):
                token_copy(j, k).start(priority=k)
            return 0

        lax.fori_loop(0, tm, one, 0, unroll=8)

    def wait_tile():
        for k in range(TOP_K):
            pltpu.make_async_copy(h2_ref, xs_hbm.at[slab(0, tm)], sem.at[k]).wait()

    @pl.when(i > 0)
    def _():
        scatter_tile()
        wait_tile()

    @pl.when(i == 0)
    def _():
        _scatter_first_step(meta_ref, xs_hbm, zbuf, zsem, slab, used, n_blocks, scatter_tile, wait_tile)


def _scatter_first_step(meta_ref, xs_hbm, zbuf, zsem, slab, used, n_blocks, scatter_tile, wait_tile):
    n_sl = zbuf.shape[0] // BLOCK_ROWS

    def zero_copy(row, n_tokens):
        return pltpu.make_async_copy(zbuf.at[pl.ds(0, n_tokens * n_sl)], xs_hbm.at[slab(row, n_tokens)], zsem)

    zbuf[...] = jnp.zeros_like(zbuf)

    def per_expert(e, issued):
        b0 = meta_ref[META_FIRST_BLK, e]
        nb = meta_ref[META_N_BLK, e]
        cnt = meta_ref[META_COUNT, e]
        pad = nb * BLOCK_ROWS - cnt
        row = b0 * BLOCK_ROWS + cnt
        out = []
        for bit, n_issued in zip(PAD_BITS, issued):
            size = 1 << bit
            take = pad & size

            @pl.when(take != 0)
            def _():
                zero_copy(row, size).start()

            row = row + take
            out.append(n_issued + take // size)
        return tuple(out)

    issued = lax.fori_loop(0, N_EXPERTS, per_expert, tuple(jnp.int32(0) for _ in PAD_BITS))
    lax.fori_loop(used, n_blocks, lambda b, _: (zero_copy(b * BLOCK_ROWS, BLOCK_ROWS).start(), 0)[1], 0)

    scatter_tile()
    wait_tile()

    for bit, n_issued in zip(PAD_BITS, issued):
        lax.fori_loop(0, n_issued, lambda i, _, n=1 << bit: (zero_copy(0, n).wait(), 0)[1], 0)
    lax.fori_loop(used, n_blocks, lambda b, _: (zero_copy(0, BLOCK_ROWS).wait(), 0)[1], 0)


def _scatter_rows(dest, meta, h2_slabs, rows, tm=2048):
    s = dest.shape[1]
    assert s % tm == 0
    n_sl = h2_slabs.shape[0] // s
    grid_spec = pltpu.PrefetchScalarGridSpec(
        num_scalar_prefetch=2,
        grid=(s // tm,),
        in_specs=[pl.BlockSpec((tm * n_sl, LANES), lambda i, dest_ref, meta_ref: (i, 0))],
        out_specs=pl.BlockSpec(memory_space=pl.ANY),
        scratch_shapes=[
            pltpu.VMEM((BLOCK_ROWS * n_sl, LANES), h2_slabs.dtype),
            pltpu.SemaphoreType.DMA((TOP_K,)),
            pltpu.SemaphoreType.DMA(()),
        ],
    )
    return pl.pallas_call(
        functools.partial(_scatter_body, tm=tm),
        grid_spec=grid_spec,
        out_shape=jax.ShapeDtypeStruct((rows * n_sl, LANES), h2_slabs.dtype),
        compiler_params=_cparams("arbitrary"),
        name="scatter_rows",
    )(dest, meta, h2_slabs)


def _row_gather_wait(src_hbm, dst, sem, n_rows):
    pltpu.make_async_copy(src_hbm.at[pl.ds(0, n_rows)], dst.at[pl.ds(0, n_rows)], sem).wait()


WEIGHT_QUEUES = (1, 1, 1)
ROW_BLOCK_QUEUE = 0


def _experts_body(meta_ref, xs_hbm, wg_hbm, wu_hbm, wd_hbm, ys_hbm,
                  xbuf, ybuf, wgbuf, wubuf, wdbuf, gsem, ysem, wsem):
    used = meta_ref[META_USED_BLKS, 0]
    n_experts, d, _ = wg_hbm.shape
    n_sl = _slab_rows(d)
    blk = BLOCK_ROWS * n_sl
    n_blocks = ys_hbm.shape[0] // blk

    def xs_copy(b, sl):
        src_blk = jnp.minimum(b, n_blocks - 1)
        rows = pl.ds(pl.multiple_of(src_blk * blk, blk), blk)
        return pltpu.make_async_copy(xs_hbm.at[rows], xbuf.at[sl], gsem.at[sl])

    def gather_start(b, sl):
        xs_copy(b, sl).start(priority=ROW_BLOCK_QUEUE)

    def weights_start(e, ws):
        for c, queue in zip(weight_copies(e, ws), WEIGHT_QUEUES):
            c.start(priority=queue)

    def gather_wait(sl):
        xs_copy(0, sl).wait()

    def ys_copy(b):
        sl = b % 2
        rows = pl.ds(pl.multiple_of(b * blk, blk), blk)
        return pltpu.make_async_copy(ybuf.at[sl], ys_hbm.at[rows], ysem.at[sl])

    def weight_copies(e, ws):
        return (pltpu.make_async_copy(wg_hbm.at[e], wgbuf.at[ws], wsem.at[ws, 0]),
                pltpu.make_async_copy(wu_hbm.at[e], wubuf.at[ws], wsem.at[ws, 1]),
                pltpu.make_async_copy(wd_hbm.at[e], wdbuf.at[ws], wsem.at[ws, 2]))

    weights_start(meta_ref[META_FIRST_USED, 0], 0)
    gather_start(0, 0)

    def expert(e, _):
        nb = meta_ref[META_N_BLK, e]

        @pl.when(nb > 0)
        def _():
            b0 = meta_ref[META_FIRST_BLK, e]
            ws = meta_ref[META_ORDINAL, e] % 2
            nxt = meta_ref[META_NEXT_EXP, e]

            @pl.when(nxt >= 0)
            def _():
                weights_start(nxt, 1 - ws)

            for c in weight_copies(e, ws):
                c.wait()

            def block(j, _):
                b = b0 + j
                sl = b % 2
                gather_start(b + 1, 1 - sl)
                gather_wait(sl)

                @pl.when(b >= 2)
                def _():
                    ys_copy(b - 2).wait()

                x = _load_slabs(xbuf.at[sl], BLOCK_ROWS, n_sl).astype(BF16)
                a = jnp.dot(x, wgbuf[ws].astype(BF16), preferred_element_type=F32)
                u = jnp.dot(x, wubuf[ws].astype(BF16), preferred_element_type=F32)
                act = (a * _sigmoid(a) * u).astype(BF16)
                y = jnp.dot(act, wdbuf[ws].astype(BF16), preferred_element_type=F32)
                _store_slabs(ybuf.at[sl], y)
                ys_copy(b).start(priority=ROW_BLOCK_QUEUE)
                return 0

            lax.fori_loop(0, nb, block, 0)

        return 0

    lax.fori_loop(0, n_experts, expert, 0)

    gather_wait(used % 2)
    for back in (1, 2):
        @pl.when(used >= back)
        def _():
            ys_copy(used - back).wait()


def _experts(meta, xs_slabs, w_gate, w_up, w_down):
    n_e, d, de = w_gate.shape
    n_sl = _slab_rows(d)
    blk = BLOCK_ROWS * n_sl
    smem = pl.BlockSpec(memory_space=pltpu.SMEM)
    hbm = pl.BlockSpec(memory_space=pl.ANY)
    return pl.pallas_call(
        _experts_body,
        in_specs=[smem, hbm, hbm, hbm, hbm],
        out_specs=hbm,
        out_shape=jax.ShapeDtypeStruct(xs_slabs.shape, F32),
        input_output_aliases={1: 0},
        scratch_shapes=[
            pltpu.VMEM((2, blk, LANES), F32),
            pltpu.VMEM((2, blk, LANES), F32),
            pltpu.VMEM((2, d, de), F32),
            pltpu.VMEM((2, d, de), F32),
            pltpu.VMEM((2, de, d), F32),
            pltpu.SemaphoreType.DMA((2,)),
            pltpu.SemaphoreType.DMA((2,)),
            pltpu.SemaphoreType.DMA((2, 3)),
        ],
        compiler_params=pltpu.CompilerParams(vmem_limit_bytes=VMEM_LIMIT_BYTES),
        name="expert_mlp",
    )(meta, xs_slabs, w_gate, w_up, w_down)


def _combine_body(dest_ref, x2_ref, gcol_ref, fg_ref, ys_hbm, o_ref, ybuf, sem, *, tm):
    i = pl.program_id(0)
    n = pl.num_programs(0)
    s = n * tm
    slot = i % 2
    n_sl = _slab_rows(x2_ref.shape[1])

    def start(tile, sl):
        for k in range(TOP_K):
            for r in range(tm):
                row = dest_ref[k * s + tile * tm + r]
                src = ys_hbm.at[pl.ds(pl.multiple_of(row * n_sl, n_sl), n_sl)]
                pltpu.make_async_copy(src, ybuf.at[sl, k, pl.ds(r * n_sl, n_sl)],
                                      sem.at[sl, k]).start(priority=r % 2)

    def wait(sl):
        for k in range(TOP_K):
            _row_gather_wait(ys_hbm, ybuf.at[sl, k], sem.at[sl, k], tm * n_sl)

    @pl.when(i == 0)
    def _():
        start(0, 0)

    start(jnp.minimum(i + 1, n - 1), 1 - slot)
    wait(slot)
    g = gcol_ref[...]
    y0 = _load_slabs(ybuf.at[slot, 0], tm, n_sl)
    y1 = _load_slabs(ybuf.at[slot, 1], tm, n_sl)
    moe = g[:, 0:1] * y0 + g[:, 1:2] * y1
    x3 = x2_ref[...] + moe
    ms = jnp.mean(x3 * x3, axis=-1, keepdims=True)
    o_ref[...] = x3 * lax.rsqrt(ms + RMS_EPS) * fg_ref[...]

    @pl.when(i == n - 1)
    def _():
        wait(1 - slot)


def _combine(dest_flat, x2, gcol, final_g, ys, tm=512):
    s, d = x2.shape
    assert s % tm == 0
    row = lambda i, dest_ref: (i, 0)
    const = lambda i, dest_ref: (0, 0)
    grid_spec = pltpu.PrefetchScalarGridSpec(
        num_scalar_prefetch=1,
        grid=(s // tm,),
        in_specs=[
            pl.BlockSpec((tm, d), row),
            pl.BlockSpec((tm, LANES), row),
            pl.BlockSpec((1, d), const),
            pl.BlockSpec(memory_space=pl.ANY),
        ],
        out_specs=pl.BlockSpec((tm, d), row),
        scratch_shapes=[
            pltpu.VMEM((2, TOP_K, tm * _slab_rows(d), LANES), F32),
            pltpu.SemaphoreType.DMA((2, TOP_K)),
        ],
    )
    return pl.pallas_call(
        functools.partial(_combine_body, tm=tm),
        grid_spec=grid_spec,
        out_shape=jax.ShapeDtypeStruct((s, d), F32),
        compiler_params=_cparams("arbitrary"),
        name="combine_norm",
    )(dest_flat, x2, gcol, final_g, ys)


def _layer(x, norm1_g, w_in, w_dw, b_dw, ln_g, ln_b, rel_bias, w_out, norm2_g,
           w_group, b_group, w_expert_router, b_expert_router, w_gate, w_up, w_down, out_g):
    s, d = x.shape
    dc = w_dw.shape[-1]
    da = (w_in.shape[1] - 2 * dc) // 3
    row2 = lambda a: a.reshape(1, -1)

    cv, cg, q, k, v = _in_proj(x, row2(norm1_g), w_in, dc, da)
    conv_out = _conv_module(cv, cg, w_dw.reshape(CONV_WIDTH, dc), row2(b_dw), row2(ln_g), row2(ln_b))
    attn_out = _attention(q, k, v, _distance_rows(rel_bias))

    n_r = N_GROUPS + N_EXPERTS
    wr = jnp.pad(jnp.concatenate([w_group, w_expert_router], axis=1), ((0, 0), (0, LANES - n_r)))
    wr_hi = wr.astype(BF16)
    wr_lo = (wr - wr_hi.astype(F32)).astype(BF16)
    wr_cat = jnp.concatenate([wr_hi, wr_lo], axis=1)
    br_col = jnp.zeros((LANES, 1), F32).at[:n_r, 0].set(jnp.concatenate([b_group, b_expert_router]))
    x2, h2, eids, gcol = _out_proj_router(x, conv_out, attn_out, w_out, row2(norm2_g),
                                          wr_cat, br_col)

    n_blocks = (s * TOP_K) // BLOCK_ROWS + N_EXPERTS
    assert n_blocks <= META_LANES
    dest, meta = _dispatch(eids)
    xs = _scatter_rows(dest, meta, h2, n_blocks * BLOCK_ROWS)
    ys = _experts(meta, xs, w_gate, w_up, w_down)
    return _combine(dest.reshape(-1), x2, gcol, row2(out_g), ys)


def kernel(x, norm1_g, w_in, w_dw, b_dw, ln_g, ln_b, rel_bias, w_out, norm2_g, w_group, b_group,
           w_expert_router, b_expert_router, w_gate, w_up, w_down, final_g):
    depth = norm1_g.shape[0]
    assert depth == 1, "the final norm is fused into the last layer's combine step"
    batch, s, d = x.shape
    assert batch == 1, "one sequence per call"
    out = _layer(x.reshape(s, d), norm1_g[0], w_in[0], w_dw[0], b_dw[0], ln_g[0], ln_b[0], rel_bias[0],
                 w_out[0], norm2_g[0], w_group[0], b_group[0], w_expert_router[0],
                 b_expert_router[0], w_gate[0], w_up[0], w_down[0], final_g)
    return out.reshape(batch, s, d)
```

```python
import functools

import jax
import jax.numpy as jnp
from jax import lax
from jax.experimental import pallas as pl
from jax.experimental.pallas import tpu as pltpu

F32 = jnp.float32
BF16 = jnp.bfloat16
I32 = jnp.int32

CHUNK = 64
HEAD_DIM = 64
LEFT_CHUNKS = 8
REL_CLIP = 128
CONV_WIDTH = 31
N_GROUPS = 8
EXPERTS_PER_GROUP = 8
N_EXPERTS = N_GROUPS * EXPERTS_PER_GROUP
TOP_K = 2
BLOCK_ROWS = 128
RMS_EPS = 1e-6
LN_EPS = 1e-5
NEG_INF = -1e30

VMEM_LIMIT_BYTES = 56 * 1024 * 1024
LANES = 128
SUBLANES = 8
PAIR = 2 * CHUNK
BAND_PAIR = (LEFT_CHUNKS + 2) * CHUNK
FRONT_PAD = LEFT_CHUNKS * CHUNK
CONV_HALO = 32


def _cparams(*sem):
    return pltpu.CompilerParams(dimension_semantics=sem, vmem_limit_bytes=VMEM_LIMIT_BYTES)


def _load_weight_as_bf16(w_hbm, w_ref, stage, wsem):
    cw = stage.shape[2]
    n_chunks = w_ref.shape[1] // cw

    def chunk_copy(c):
        return pltpu.make_async_copy(w_hbm.at[:, c * cw:(c + 1) * cw], stage.at[c % 2], wsem.at[c % 2])

    chunk_copy(0).start()
    for c in range(n_chunks):
        if c + 1 < n_chunks:
            chunk_copy(c + 1).start()
        chunk_copy(c).wait()
        w_ref[:, c * cw:(c + 1) * cw] = stage[c % 2].astype(BF16)


def _weight_scratch(d, d_out, chunk):
    return [pltpu.VMEM((d, d_out), BF16), pltpu.VMEM((2, d, chunk), F32), pltpu.SemaphoreType.DMA((2,))]


def _inproj_body(x_ref, g_ref, w_hbm, cv_ref, cg_ref, q_ref, k_ref, v_ref, w_ref, stage, wsem):
    @pl.when(pl.program_id(0) == 0)
    def _():
        _load_weight_as_bf16(w_hbm, w_ref, stage, wsem)

    x = x_ref[...]
    ms = jnp.mean(x * x, axis=-1, keepdims=True)
    h = (x * lax.rsqrt(ms + RMS_EPS) * g_ref[...]).astype(BF16)
    dc = cv_ref.shape[1]
    da = q_ref.shape[1]

    def proj(lo, width):
        return jnp.dot(h, w_ref[:, lo:lo + width], preferred_element_type=F32)

    cv_ref[...] = proj(0, dc)
    cg_ref[...] = proj(dc, dc)
    q_ref[...] = (proj(2 * dc, da) * (HEAD_DIM ** -0.5)).astype(BF16)
    k_ref[...] = proj(2 * dc + da, da).astype(BF16)
    v_ref[...] = proj(2 * dc + 2 * da, da).astype(BF16)


def _in_proj(x, g, w, d_conv, d_att, tm=256, w_chunk=512):
    s, d = x.shape
    assert s % tm == 0 and w.shape[1] % w_chunk == 0
    d_in = w.shape[1]
    row = lambda i: (i, 0)
    const = lambda i: (0, 0)
    return pl.pallas_call(
        _inproj_body,
        grid=(s // tm,),
        in_specs=[
            pl.BlockSpec((tm, d), row),
            pl.BlockSpec((1, d), const),
            pl.BlockSpec(memory_space=pl.ANY),
        ],
        out_specs=[
            pl.BlockSpec((tm, d_conv), row),
            pl.BlockSpec((tm, d_conv), row),
            pl.BlockSpec((tm, d_att), row),
            pl.BlockSpec((tm, d_att), row),
            pl.BlockSpec((tm, d_att), row),
        ],
        out_shape=[
            jax.ShapeDtypeStruct((s, d_conv), F32),
            jax.ShapeDtypeStruct((s, d_conv), F32),
            jax.ShapeDtypeStruct((s, d_att), BF16),
            jax.ShapeDtypeStruct((s, d_att), BF16),
            jax.ShapeDtypeStruct((s, d_att), BF16),
        ],
        scratch_shapes=_weight_scratch(d, d_in, w_chunk),
        compiler_params=_cparams("arbitrary"),
        name="in_proj",
    )(x, g, w)


def _sigmoid(x):
    return 1.0 / (1.0 + jnp.exp(-x))


def _conv_body(cv_ref, cg_ref, cvh_ref, cgh_ref, w_ref, b_ref, lg_ref, lb_ref, o_ref,
               u_ref, us_ref, wb_ref, conv_ref, *, rows_per_chunk, groups_per_pass):
    i = pl.program_id(0)
    t, dc = cv_ref.shape

    @pl.when(i == 0)
    def _():
        for w in range(CONV_WIDTH):
            wb_ref[w] = jnp.broadcast_to(w_ref[w:w + 1, :], (SUBLANES, dc))

    halo = cvh_ref[...] * _sigmoid(cgh_ref[...])
    u_ref[0:CONV_HALO, :] = jnp.where(i > 0, halo, 0.0)
    u_ref[CONV_HALO:, :] = cv_ref[...] * _sigmoid(cg_ref[...])
    n_shift_rows = us_ref.shape[1]
    for sft in range(1, SUBLANES):
        us_ref[sft - 1] = u_ref[sft:sft + n_shift_rows, :]
    first_tap = CONV_HALO - (CONV_WIDTH - 1)

    def lane_tile(lt, _):
        lanes = pl.ds(pl.multiple_of(lt * LANES, LANES), LANES)
        wv = [wb_ref[w, :, lanes] for w in range(CONV_WIDTH)]
        bias = jnp.broadcast_to(b_ref[:, lanes], (SUBLANES, LANES))
        for g0 in range(0, t // SUBLANES, groups_per_pass):
            accs = [bias] * groups_per_pass
            for w in range(CONV_WIDTH):
                off = first_tap + w
                sft = off % SUBLANES
                for g in range(groups_per_pass):
                    lo = (g0 + g) * SUBLANES + off - sft
                    if sft == 0:
                        src = u_ref[lo:lo + SUBLANES, lanes]
                    else:
                        src = us_ref[sft - 1, lo:lo + SUBLANES, lanes]
                    accs[g] = accs[g] + wv[w] * src
            for g in range(groups_per_pass):
                lo = (g0 + g) * SUBLANES
                conv_ref[lo:lo + SUBLANES, lanes] = accs[g]
        return 0

    lax.fori_loop(0, dc // LANES, lane_tile, 0)

    r = rows_per_chunk
    for c in range(t // r):
        acc = conv_ref[c * r:(c + 1) * r, :]
        mu = jnp.mean(acc, axis=-1, keepdims=True)
        cen = acc - mu
        var = jnp.mean(cen * cen, axis=-1, keepdims=True)
        un = cen * lax.rsqrt(var + LN_EPS) * lg_ref[...] + lb_ref[...]
        o_ref[c * r:(c + 1) * r, :] = (un * _sigmoid(un)).astype(o_ref.dtype)


def _conv_module(cv, cg, w_dw, b_dw, ln_g, ln_b, t=512, rows_per_chunk=16, groups_per_pass=4):
    s, dc = cv.shape
    assert s % t == 0
    halo_blocks = t // CONV_HALO
    row = lambda i: (i, 0)
    prev = lambda i: (jnp.maximum(i * halo_blocks - 1, 0), 0)
    const = lambda i: (0, 0)
    return pl.pallas_call(
        functools.partial(_conv_body, rows_per_chunk=rows_per_chunk, groups_per_pass=groups_per_pass),
        grid=(s // t,),
        in_specs=[
            pl.BlockSpec((t, dc), row),
            pl.BlockSpec((t, dc), row),
            pl.BlockSpec((CONV_HALO, dc), prev),
            pl.BlockSpec((CONV_HALO, dc), prev),
            pl.BlockSpec((CONV_WIDTH, dc), const),
            pl.BlockSpec((1, dc), const),
            pl.BlockSpec((1, dc), const),
            pl.BlockSpec((1, dc), const),
        ],
        out_specs=pl.BlockSpec((t, dc), row),
        out_shape=jax.ShapeDtypeStruct((s, dc), BF16),
        scratch_shapes=[
            pltpu.VMEM((t + CONV_HALO, dc), F32),
            pltpu.VMEM((SUBLANES - 1, t + CONV_HALO - SUBLANES, dc), F32),
            pltpu.VMEM((CONV_WIDTH, SUBLANES, dc), F32),
            pltpu.VMEM((t, dc), F32),
        ],
        compiler_params=_cparams("arbitrary"),
        name="conv_module",
    )(cv, cg, cv, cg, w_dw, b_dw, ln_g, ln_b)


SOFTMAX_ROWS = 32


def _attn_pairs(q_ref, bands, bias_ref, s_ref, e_ref, o_ref):
    lane = lax.broadcasted_iota(I32, (PAIR, 2 * HEAD_DIM), 1)
    n_pairs = len(bands)
    rdens = {}

    def scores(u):
        kb, _, _ = bands[u]
        q2 = q_ref[u * PAIR:(u + 1) * PAIR, :]
        for h in range(2):
            in_head = (lane >= h * HEAD_DIM) & (lane < (h + 1) * HEAD_DIM)
            qh = jnp.where(in_head, q2, jnp.zeros_like(q2))
            s_ref[2 * u + h, :, :kb.shape[0]] = lax.dot_general(
                qh, kb, (((1,), (1,)), ((), ())), preferred_element_type=F32)

    def softmax(u):
        kb, _, col_lo = bands[u]
        n_keys = kb.shape[0]
        for h in range(2):
            c = 2 * u + h
            parts = []
            for r0 in range(0, PAIR, SOFTMAX_ROWS):
                rows = slice(r0, r0 + SOFTMAX_ROWS)
                t = s_ref[c, rows, :n_keys] + bias_ref[h, rows, col_lo:col_lo + n_keys]
                e = jnp.exp(t - jnp.max(t, axis=-1, keepdims=True))
                parts.append(1.0 / jnp.sum(e, axis=-1, keepdims=True))
                e_ref[c, rows, :n_keys] = e.astype(BF16)
            rdens[c] = jnp.concatenate(parts, axis=0)

    def values(u):
        _, vb, _ = bands[u]
        n_keys = vb.shape[0]
        outs = []
        for h in range(2):
            c = 2 * u + h
            o = jnp.dot(e_ref[c, :, :n_keys], vb, preferred_element_type=F32)
            outs.append(o * rdens[c])
        o_ref[u * PAIR:(u + 1) * PAIR, :] = jnp.where(lane < HEAD_DIM, outs[0], outs[1]).astype(o_ref.dtype)

    for step in range(n_pairs + 2):
        if step < n_pairs:
            scores(step)
        if 0 <= step - 1 < n_pairs:
            softmax(step - 1)
        if 0 <= step - 2 < n_pairs:
            values(step - 2)


def _attn_body(q_ref, k_ref, v_ref, g_ref, o_ref, bias_ref, s_ref, e_ref, *, pairs_per_step):
    st = pl.program_id(1)
    npp = pairs_per_step

    @pl.when(st == 0)
    def _():
        sub = lax.broadcasted_iota(I32, (SUBLANES, BAND_PAIR), 0)
        kj = lax.broadcasted_iota(I32, (SUBLANES, BAND_PAIR), 1)
        for h in range(2):
            t8 = jnp.broadcast_to(g_ref[h], (SUBLANES, BAND_PAIR))
            shift = 1
            while shift < SUBLANES:
                t8 = jnp.where((sub & shift) != 0, pltpu.roll(t8, shift, 1), t8)
                shift *= 2
            for r0 in range(0, PAIR, SUBLANES):
                band_lo = (r0 // CHUNK) * CHUNK
                in_band = (kj >= band_lo) & (kj < band_lo + (LEFT_CHUNKS + 1) * CHUNK)
                rows = t8 if r0 == 0 else pltpu.roll(t8, r0, 1)
                bias_ref[h, r0:r0 + SUBLANES, :] = jnp.where(in_band, rows, NEG_INF)

    def run(first_step):
        bands = []
        for u in range(npp):
            if first_step:
                n_keys = min((u + 1) * PAIR, BAND_PAIR)
                lo = max((u + 1) * PAIR - BAND_PAIR, 0)
                bands.append((k_ref[lo:lo + n_keys, :], v_ref[lo:lo + n_keys, :], BAND_PAIR - n_keys))
            else:
                start = pl.multiple_of((st * npp + u) * PAIR - FRONT_PAD, PAIR)
                bands.append((k_ref[pl.ds(start, BAND_PAIR), :], v_ref[pl.ds(start, BAND_PAIR), :], 0))
        _attn_pairs(q_ref, bands, bias_ref, s_ref, e_ref, o_ref)

    @pl.when(st == 0)
    def _():
        run(True)

    @pl.when(st > 0)
    def _():
        run(False)


def _distance_rows(rel_table):
    t = rel_table.astype(F32)
    far = t[:, 2 * REL_CLIP:]
    n_far = LEFT_CHUNKS * CHUNK - REL_CLIP + 1
    near = jnp.flip(t[:, REL_CLIP - CHUNK + 1:2 * REL_CLIP], axis=1)
    tail = BAND_PAIR - n_far - near.shape[1]
    g = jnp.concatenate([jnp.repeat(far, n_far, axis=1), near, jnp.repeat(far, tail, axis=1)], axis=1)
    return g[:, None, :]


def _attention(q, k, v, g_rows, pairs_per_step=8):
    s, da = q.shape
    n_hp = da // (2 * HEAD_DIM)
    tq = pairs_per_step * PAIR
    assert tq >= FRONT_PAD, "first step must cover every clamped band"
    assert s % tq == 0
    return pl.pallas_call(
        functools.partial(_attn_body, pairs_per_step=pairs_per_step),
        grid=(n_hp, s // tq),
        in_specs=[
            pl.BlockSpec((tq, 2 * HEAD_DIM), lambda hp, st: (st, hp)),
            pl.BlockSpec((s, 2 * HEAD_DIM), lambda hp, st: (0, hp)),
            pl.BlockSpec((s, 2 * HEAD_DIM), lambda hp, st: (0, hp)),
            pl.BlockSpec((2, 1, BAND_PAIR), lambda hp, st: (hp, 0, 0)),
        ],
        out_specs=pl.BlockSpec((tq, 2 * HEAD_DIM), lambda hp, st: (st, hp)),
        out_shape=jax.ShapeDtypeStruct((s, da), BF16),
        scratch_shapes=[
            pltpu.VMEM((2, PAIR, BAND_PAIR), F32),
            pltpu.VMEM((2 * pairs_per_step, PAIR, BAND_PAIR), F32),
            pltpu.VMEM((2 * pairs_per_step, PAIR, BAND_PAIR), BF16),
        ],
        compiler_params=_cparams("arbitrary", "arbitrary"),
        name="chunk_attention",
    )(q, k, v, g_rows)


def _slab_rows(d):
    return d // LANES


def _store_slabs(ref, val):
    t, d = val.shape
    n = _slab_rows(d)
    for c in range(n):
        ref[pl.ds(c, t, stride=n), :] = val[:, c * LANES:(c + 1) * LANES]


def _load_slabs(ref, t, n):
    return jnp.concatenate([ref[pl.ds(c, t, stride=n), :] for c in range(n)], axis=1)


def _first_index_of_max(vals, sub, n):
    vmax = jnp.max(vals, axis=0, keepdims=True)
    idx = jnp.min(jnp.where(vals == vmax, sub, n), axis=0, keepdims=True)
    return vmax, idx


def _outproj_body(x_ref, co_ref, ao_ref, w_hbm, g2_ref, wr_ref, br_ref,
                  x2_ref, h2_ref, eid_ref, gcol_ref, w_ref, stage, wsem):
    @pl.when(pl.program_id(0) == 0)
    def _():
        _load_weight_as_bf16(w_hbm, w_ref, stage, wsem)

    dc = co_ref.shape[1]
    y = jnp.dot(co_ref[...], w_ref[0:dc, :], preferred_element_type=F32)
    y = y + jnp.dot(ao_ref[...], w_ref[dc:, :], preferred_element_type=F32)
    x2 = x_ref[...] + y
    ms = jnp.mean(x2 * x2, axis=-1, keepdims=True)
    h2 = x2 * lax.rsqrt(ms + RMS_EPS) * g2_ref[...]
    x2_ref[...] = x2
    _store_slabs(h2_ref, h2)

    hi = h2.astype(BF16)
    lo = (h2 - hi.astype(F32)).astype(BF16)
    pa = jnp.dot(hi, wr_ref[...], preferred_element_type=F32)
    pb = jnp.dot(lo, wr_ref[...], preferred_element_type=F32)
    logits = pa[:, :LANES] + pa[:, LANES:] + pb[:, :LANES]
    lt = logits.T + br_ref[...]
    tm = lt.shape[1]
    sub = lax.broadcasted_iota(I32, (N_GROUPS, tm), 0)
    g = lt[0:N_GROUPS, :]
    gmax, grp = _first_index_of_max(g, sub, N_GROUPS)
    p_grp = 1.0 / jnp.sum(jnp.exp(g - gmax), axis=0, keepdims=True)
    sel = jnp.zeros((EXPERTS_PER_GROUP, tm), F32)
    for gi in range(N_GROUPS):
        lo = N_GROUPS + gi * EXPERTS_PER_GROUP
        sel = jnp.where(grp == gi, lt[lo:lo + EXPERTS_PER_GROUP, :], sel)
    v1, i1 = _first_index_of_max(sel, sub, EXPERTS_PER_GROUP)
    rest = jnp.where(sub == i1, -jnp.inf, sel)
    v2, i2 = _first_index_of_max(rest, sub, EXPERTS_PER_GROUP)
    t = jnp.exp(v2 - v1)
    gate1 = p_grp / (1.0 + t)
    gate2 = p_grp * t / (1.0 + t)
    eid_ref[0:1, :] = grp * EXPERTS_PER_GROUP + i1
    eid_ref[1:2, :] = grp * EXPERTS_PER_GROUP + i2
    row = lax.broadcasted_iota(I32, (LANES, tm), 0)
    gl = jnp.where(row == 0, gate1, jnp.where(row == 1, gate2, 0.0))
    gcol_ref[...] = gl.T


def _out_proj_router(x, conv_out, attn_out, w, g2, wr_cat, br_col, tm=512, w_chunk=512):
    s, d = x.shape
    assert s % tm == 0
    dc = conv_out.shape[1]
    da = attn_out.shape[1]
    row = lambda i: (i, 0)
    const = lambda i: (0, 0)
    return pl.pallas_call(
        _outproj_body,
        grid=(s // tm,),
        in_specs=[
            pl.BlockSpec((tm, d), row),
            pl.BlockSpec((tm, dc), row),
            pl.BlockSpec((tm, da), row),
            pl.BlockSpec(memory_space=pl.ANY),
            pl.BlockSpec((1, d), const),
            pl.BlockSpec((d, 2 * LANES), const),
            pl.BlockSpec((LANES, 1), const),
        ],
        out_specs=[
            pl.BlockSpec((tm, d), row),
            pl.BlockSpec((tm * _slab_rows(d), LANES), row),
            pl.BlockSpec((TOP_K, tm), lambda i: (0, i)),
            pl.BlockSpec((tm, LANES), row),
        ],
        out_shape=[
            jax.ShapeDtypeStruct((s, d), F32),
            jax.ShapeDtypeStruct((s * _slab_rows(d), LANES), F32),
            jax.ShapeDtypeStruct((TOP_K, s), I32),
            jax.ShapeDtypeStruct((s, LANES), F32),
        ],
        scratch_shapes=_weight_scratch(dc + da, d, w_chunk),
        compiler_params=_cparams("arbitrary"),
        name="out_proj_router",
    )(x, conv_out, attn_out, w, g2, wr_cat, br_col)


META_LANES = 256
(META_FIRST_BLK, META_N_BLK, META_COUNT, META_NEXT_EXP, META_ORDINAL,
 META_USED_BLKS, META_FIRST_USED) = range(7)


def _dispatch_body(eid_ref, dest_ref, meta_ref, rank_ref):
    s = eid_ref.shape[1]
    w = META_LANES
    sub = lax.broadcasted_iota(I32, (N_EXPERTS, w), 0)
    lane = lax.broadcasted_iota(I32, (N_EXPERTS, w), 1)
    ri = lax.broadcasted_iota(I32, (w, w), 0)
    ci = lax.broadcasted_iota(I32, (w, w), 1)
    upper = (ri < ci).astype(BF16)

    def onehots(b):
        off = pl.multiple_of(b * w, w)
        e0 = eid_ref[0:1, pl.ds(off, w)]
        e1 = eid_ref[1:2, pl.ds(off, w)]
        return off, sub == e0, sub == e1

    def rank_block(b, carry):
        off, o0, o1 = onehots(b)
        both = jnp.where(o0 | o1, 1.0, 0.0)
        pre = jnp.dot(both.astype(BF16), upper, preferred_element_type=F32) + carry
        rank_ref[0:1, pl.ds(off, w)] = jnp.sum(jnp.where(o0, pre, 0.0), axis=0, keepdims=True)
        rank_ref[1:2, pl.ds(off, w)] = jnp.sum(jnp.where(o1, pre, 0.0), axis=0, keepdims=True)
        return carry + jnp.sum(both, axis=1, keepdims=True)

    counts = lax.fori_loop(0, s // w, rank_block, jnp.zeros((N_EXPERTS, w), F32))

    nblk = ((counts.astype(I32) + (BLOCK_ROWS - 1)) // BLOCK_ROWS).astype(F32)
    er = lax.broadcasted_iota(I32, (N_EXPERTS, N_EXPERTS), 0)
    ec = lax.broadcasted_iota(I32, (N_EXPERTS, N_EXPERTS), 1)
    lower_incl = (ec <= er).astype(BF16)
    pend_blk = jnp.dot(lower_incl, nblk.astype(BF16), preferred_element_type=F32)
    pstart = (pend_blk - nblk) * float(BLOCK_ROWS)

    def on_lanes(v):
        return jnp.sum(jnp.where(sub == lane, v, 0), axis=0, keepdims=True)

    nblk_i = nblk.astype(I32)
    first_blk = on_lanes((pend_blk - nblk).astype(I32))
    n_blk = on_lanes(nblk_i)
    count = on_lanes(counts.astype(I32))
    has_rows = nblk_i > 0
    next_exp = jnp.min(jnp.where(has_rows & (sub > lane), sub, N_EXPERTS), axis=0, keepdims=True)
    next_exp = jnp.where(next_exp < N_EXPERTS, next_exp, -1)
    ordinal = jnp.sum(jnp.where(has_rows & (sub < lane), 1, 0), axis=0, keepdims=True)
    first_used = jnp.min(jnp.where(has_rows, sub, N_EXPERTS - 1), axis=0, keepdims=True)
    used = pend_blk[N_EXPERTS - 1:N_EXPERTS, :].astype(I32)
    mrow = lax.broadcasted_iota(I32, (8, w), 0)
    meta = jnp.where(mrow == META_FIRST_BLK, first_blk, 0)
    meta = jnp.where(mrow == META_N_BLK, n_blk, meta)
    meta = jnp.where(mrow == META_COUNT, count, meta)
    meta = jnp.where(mrow == META_NEXT_EXP, next_exp, meta)
    meta = jnp.where(mrow == META_ORDINAL, ordinal, meta)
    meta = jnp.where(mrow == META_FIRST_USED, first_used, meta)
    meta_ref[...] = jnp.where(mrow == META_USED_BLKS, used, meta)

    def dest_block(b, _):
        off, o0, o1 = onehots(b)
        d0 = rank_ref[0:1, pl.ds(off, w)] + jnp.sum(jnp.where(o0, pstart, 0.0), axis=0, keepdims=True)
        d1 = rank_ref[1:2, pl.ds(off, w)] + jnp.sum(jnp.where(o1, pstart, 0.0), axis=0, keepdims=True)
        dest_ref[0:1, pl.ds(off, w)] = d0.astype(I32)
        dest_ref[1:2, pl.ds(off, w)] = d1.astype(I32)
        return 0

    lax.fori_loop(0, s // w, dest_block, 0)


def _dispatch(eids):
    s = eids.shape[1]
    return pl.pallas_call(
        _dispatch_body,
        out_shape=[
            jax.ShapeDtypeStruct((TOP_K, s), I32),
            jax.ShapeDtypeStruct((8, META_LANES), I32),
        ],
        scratch_shapes=[pltpu.VMEM((TOP_K, s), F32)],
        compiler_params=pltpu.CompilerParams(vmem_limit_bytes=VMEM_LIMIT_BYTES),
        name="dispatch",
    )(eids)


PAD_BITS = tuple(reversed(range(BLOCK_ROWS.bit_length() - 1)))


def _scatter_body(dest_ref, meta_ref, h2_ref, xs_hbm, zbuf, sem, zsem, *, tm):
    i = pl.program_id(0)
    n_sl = h2_ref.shape[0] // tm
    blk = BLOCK_ROWS * n_sl
    n_blocks = xs_hbm.shape[0] // blk
    used = meta_ref[META_USED_BLKS, 0]

    def slab(row, n_tokens=1):
        return pl.ds(pl.multiple_of(row * n_sl, n_sl), n_tokens * n_sl)

    def token_copy(j, k):
        t = i * tm + j
        return pltpu.make_async_copy(h2_ref.at[slab(j)], xs_hbm.at[slab(dest_ref[k, t])], sem.at[k])

    def scatter_tile():
        def one(j, _):
            for k in range(TOP_K):
                token_copy(j, k).start(priority=k)
            return 0

        lax.fori_loop(0, tm, one, 0, unroll=8)

    def wait_tile():
        for k in range(TOP_K):
            pltpu.make_async_copy(h2_ref, xs_hbm.at[slab(0, tm)], sem.at[k]).wait()

    @pl.when(i > 0)
    def _():
        scatter_tile()
        wait_tile()

    @pl.when(i == 0)
    def _():
        _scatter_first_step(meta_ref, xs_hbm, zbuf, zsem, slab, used, n_blocks, scatter_tile, wait_tile)


def _scatter_first_step(meta_ref, xs_hbm, zbuf, zsem, slab, used, n_blocks, scatter_tile, wait_tile):
    n_sl = zbuf.shape[0] // BLOCK_ROWS

    def zero_copy(row, n_tokens):
        return pltpu.make_async_copy(zbuf.at[pl.ds(0, n_tokens * n_sl)], xs_hbm.at[slab(row, n_tokens)], zsem)

    zbuf[...] = jnp.zeros_like(zbuf)

    def per_expert(e, issued):
        b0 = meta_ref[META_FIRST_BLK, e]
        nb = meta_ref[META_N_BLK, e]
        cnt = meta_ref[META_COUNT, e]
        pad = nb * BLOCK_ROWS - cnt
        row = b0 * BLOCK_ROWS + cnt
        out = []
        for bit, n_issued in zip(PAD_BITS, issued):
            size = 1 << bit
            take = pad & size

            @pl.when(take != 0)
            def _():
                zero_copy(row, size).start()

            row = row + take
            out.append(n_issued + take // size)
        return tuple(out)

    issued = lax.fori_loop(0, N_EXPERTS, per_expert, tuple(jnp.int32(0) for _ in PAD_BITS))
    lax.fori_loop(used, n_blocks, lambda b, _: (zero_copy(b * BLOCK_ROWS, BLOCK_ROWS).start(), 0)[1], 0)

    scatter_tile()
    wait_tile()

    for bit, n_issued in zip(PAD_BITS, issued):
        lax.fori_loop(0, n_issued, lambda i, _, n=1 << bit: (zero_copy(0, n).wait(), 0)[1], 0)
    lax.fori_loop(used, n_blocks, lambda b, _: (zero_copy(0, BLOCK_ROWS).wait(), 0)[1], 0)


def _scatter_rows(dest, meta, h2_slabs, rows, tm=1024):
    s = dest.shape[1]
    assert s % tm == 0
    n_sl = h2_slabs.shape[0] // s
    grid_spec = pltpu.PrefetchScalarGridSpec(
        num_scalar_prefetch=2,
        grid=(s // tm,),
        in_specs=[pl.BlockSpec((tm * n_sl, LANES), lambda i, dest_ref, meta_ref: (i, 0))],
        out_specs=pl.BlockSpec(memory_space=pl.ANY),
        scratch_shapes=[
            pltpu.VMEM((BLOCK_ROWS * n_sl, LANES), h2_slabs.dtype),
            pltpu.SemaphoreType.DMA((TOP_K,)),
            pltpu.SemaphoreType.DMA(()),
        ],
    )
    return pl.pallas_call(
        functools.partial(_scatter_body, tm=tm),
        grid_spec=grid_spec,
        out_shape=jax.ShapeDtypeStruct((rows * n_sl, LANES), h2_slabs.dtype),
        compiler_params=_cparams("arbitrary"),
        name="scatter_rows",
    )(dest, meta, h2_slabs)


def _row_gather_wait(src_hbm, dst, sem, n_rows):
    pltpu.make_async_copy(src_hbm.at[pl.ds(0, n_rows)], dst.at[pl.ds(0, n_rows)], sem).wait()


WEIGHT_QUEUES = (1, 1, 1)
ROW_BLOCK_QUEUE = 0


def _experts_body(meta_ref, xs_hbm, wg_hbm, wu_hbm, wd_hbm, ys_hbm,
                  xbuf, ybuf, wgbuf, wubuf, wdbuf, gsem, ysem, wsem):
    used = meta_ref[META_USED_BLKS, 0]
    n_experts, d, _ = wg_hbm.shape
    n_sl = _slab_rows(d)
    blk = BLOCK_ROWS * n_sl
    n_blocks = ys_hbm.shape[0] // blk

    def xs_copy(b, sl):
        src_blk = jnp.minimum(b, n_blocks - 1)
        rows = pl.ds(pl.multiple_of(src_blk * blk, blk), blk)
        return pltpu.make_async_copy(xs_hbm.at[rows], xbuf.at[sl], gsem.at[sl])

    def gather_start(b, sl):
        xs_copy(b, sl).start(priority=ROW_BLOCK_QUEUE)

    def weights_start(e, ws):
        for c, queue in zip(weight_copies(e, ws), WEIGHT_QUEUES):
            c.start(priority=queue)

    def gather_wait(sl):
        xs_copy(0, sl).wait()

    def ys_copy(b):
        sl = b % 2
        rows = pl.ds(pl.multiple_of(b * blk, blk), blk)
        return pltpu.make_async_copy(ybuf.at[sl], ys_hbm.at[rows], ysem.at[sl])

    def weight_copies(e, ws):
        return (pltpu.make_async_copy(wg_hbm.at[e], wgbuf.at[ws], wsem.at[ws, 0]),
                pltpu.make_async_copy(wu_hbm.at[e], wubuf.at[ws], wsem.at[ws, 1]),
                pltpu.make_async_copy(wd_hbm.at[e], wdbuf.at[ws], wsem.at[ws, 2]))

    weights_start(meta_ref[META_FIRST_USED, 0], 0)
    gather_start(0, 0)

    def expert(e, _):
        nb = meta_ref[META_N_BLK, e]

        @pl.when(nb > 0)
        def _():
            b0 = meta_ref[META_FIRST_BLK, e]
            ws = meta_ref[META_ORDINAL, e] % 2
            nxt = meta_ref[META_NEXT_EXP, e]

            @pl.when(nxt >= 0)
            def _():
                weights_start(nxt, 1 - ws)

            for c in weight_copies(e, ws):
                c.wait()

            def block(j, _):
                b = b0 + j
                sl = b % 2
                gather_start(b + 1, 1 - sl)
                gather_wait(sl)

                @pl.when(b >= 2)
                def _():
                    ys_copy(b - 2).wait()

                x = _load_slabs(xbuf.at[sl], BLOCK_ROWS, n_sl).astype(BF16)
                a = jnp.dot(x, wgbuf[ws].astype(BF16), preferred_element_type=F32)
                u = jnp.dot(x, wubuf[ws].astype(BF16), preferred_element_type=F32)
                act = (a * _sigmoid(a) * u).astype(BF16)
                y = jnp.dot(act, wdbuf[ws].astype(BF16), preferred_element_type=F32)
                _store_slabs(ybuf.at[sl], y)
                ys_copy(b).start(priority=ROW_BLOCK_QUEUE)
                return 0

            lax.fori_loop(0, nb, block, 0)

        return 0

    lax.fori_loop(0, n_experts, expert, 0)

    gather_wait(used % 2)
    for back in (1, 2):
        @pl.when(used >= back)
        def _():
            ys_copy(used - back).wait()


def _experts(meta, xs_slabs, w_gate, w_up, w_down):
    n_e, d, de = w_gate.shape
    n_sl = _slab_rows(d)
    blk = BLOCK_ROWS * n_sl
    smem = pl.BlockSpec(memory_space=pltpu.SMEM)
    hbm = pl.BlockSpec(memory_space=pl.ANY)
    return pl.pallas_call(
        _experts_body,
        in_specs=[smem, hbm, hbm, hbm, hbm],
        out_specs=hbm,
        out_shape=jax.ShapeDtypeStruct(xs_slabs.shape, F32),
        input_output_aliases={1: 0},
        scratch_shapes=[
            pltpu.VMEM((2, blk, LANES), F32),
            pltpu.VMEM((2, blk, LANES), F32),
            pltpu.VMEM((2, d, de), F32),
            pltpu.VMEM((2, d, de), F32),
            pltpu.VMEM((2, de, d), F32),
            pltpu.SemaphoreType.DMA((2,)),
            pltpu.SemaphoreType.DMA((2,)),
            pltpu.SemaphoreType.DMA((2, 3)),
        ],
        compiler_params=pltpu.CompilerParams(vmem_limit_bytes=VMEM_LIMIT_BYTES),
        name="expert_mlp",
    )(meta, xs_slabs, w_gate, w_up, w_down)


def _combine_body(dest_ref, x2_ref, gcol_ref, fg_ref, ys_hbm, o_ref, ybuf, sem, *, tm):
    i = pl.program_id(0)
    n = pl.num_programs(0)
    s = n * tm
    slot = i % 2
    n_sl = _slab_rows(x2_ref.shape[1])

    def start(tile, sl):
        for k in range(TOP_K):
            for r in range(tm):
                row = dest_ref[k * s + tile * tm + r]
                src = ys_hbm.at[pl.ds(pl.multiple_of(row * n_sl, n_sl), n_sl)]
                pltpu.make_async_copy(src, ybuf.at[sl, k, pl.ds(r * n_sl, n_sl)],
                                      sem.at[sl, k]).start(priority=r % 2)

    def wait(sl):
        for k in range(TOP_K):
            _row_gather_wait(ys_hbm, ybuf.at[sl, k], sem.at[sl, k], tm * n_sl)

    @pl.when(i == 0)
    def _():
        start(0, 0)

    start(jnp.minimum(i + 1, n - 1), 1 - slot)
    wait(slot)
    g = gcol_ref[...]
    y0 = _load_slabs(ybuf.at[slot, 0], tm, n_sl)
    y1 = _load_slabs(ybuf.at[slot, 1], tm, n_sl)
    moe = g[:, 0:1] * y0 + g[:, 1:2] * y1
    x3 = x2_ref[...] + moe
    ms = jnp.mean(x3 * x3, axis=-1, keepdims=True)
    o_ref[...] = x3 * lax.rsqrt(ms + RMS_EPS) * fg_ref[...]

    @pl.when(i == n - 1)
    def _():
        wait(1 - slot)


def _combine(dest_flat, x2, gcol, final_g, ys, tm=256):
    s, d = x2.shape
    assert s % tm == 0
    row = lambda i, dest_ref: (i, 0)
    const = lambda i, dest_ref: (0, 0)
    grid_spec = pltpu.PrefetchScalarGridSpec(
        num_scalar_prefetch=1,
        grid=(s // tm,),
        in_specs=[
            pl.BlockSpec((tm, d), row),
            pl.BlockSpec((tm, LANES), row),
            pl.BlockSpec((1, d), const),
            pl.BlockSpec(memory_space=pl.ANY),
        ],
        out_specs=pl.BlockSpec((tm, d), row),
        scratch_shapes=[
            pltpu.VMEM((2, TOP_K, tm * _slab_rows(d), LANES), F32),
            pltpu.SemaphoreType.DMA((2, TOP_K)),
        ],
    )
    return pl.pallas_call(
        functools.partial(_combine_body, tm=tm),
        grid_spec=grid_spec,
        out_shape=jax.ShapeDtypeStruct((s, d), F32),
        compiler_params=_cparams("arbitrary"),
        name="combine_norm",
    )(dest_flat, x2, gcol, final_g, ys)


def _layer(x, norm1_g, w_in, w_dw, b_dw, ln_g, ln_b, rel_bias, w_out, norm2_g,
           w_group, b_group, w_expert_router, b_expert_router, w_gate, w_up, w_down, out_g):
    s, d = x.shape
    dc = w_dw.shape[-1]
    da = (w_in.shape[1] - 2 * dc) // 3
    row2 = lambda a: a.reshape(1, -1)

    cv, cg, q, k, v = _in_proj(x, row2(norm1_g), w_in, dc, da)
    conv_out = _conv_module(cv, cg, w_dw.reshape(CONV_WIDTH, dc), row2(b_dw), row2(ln_g), row2(ln_b))
    attn_out = _attention(q, k, v, _distance_rows(rel_bias))

    n_r = N_GROUPS + N_EXPERTS
    wr = jnp.pad(jnp.concatenate([w_group, w_expert_router], axis=1), ((0, 0), (0, LANES - n_r)))
    wr_hi = wr.astype(BF16)
    wr_lo = (wr - wr_hi.astype(F32)).astype(BF16)
    wr_cat = jnp.concatenate([wr_hi, wr_lo], axis=1)
    br_col = jnp.zeros((LANES, 1), F32).at[:n_r, 0].set(jnp.concatenate([b_group, b_expert_router]))
    x2, h2, eids, gcol = _out_proj_router(x, conv_out, attn_out, w_out, row2(norm2_g),
                                          wr_cat, br_col)

    n_blocks = (s * TOP_K) // BLOCK_ROWS + N_EXPERTS
    assert n_blocks <= META_LANES
    dest, meta = _dispatch(eids)
    xs = _scatter_rows(dest, meta, h2, n_blocks * BLOCK_ROWS)
    ys = _experts(meta, xs, w_gate, w_up, w_down)
    return _combine(dest.reshape(-1), x2, gcol, row2(out_g), ys)


def kernel(x, norm1_g, w_in, w_dw, b_dw, ln_g, ln_b, rel_bias, w_out, norm2_g, w_group, b_group,
           w_expert_router, b_expert_router, w_gate, w_up, w_down, final_g):
    depth = norm1_g.shape[0]
    assert depth == 1, "the final norm is fused into the last layer's combine step"
    batch, s, d = x.shape
    assert batch == 1, "one sequence per call"
    out = _layer(x.reshape(s, d), norm1_g[0], w_in[0], w_dw[0], b_dw[0], ln_g[0], ln_b[0], rel_bias[0],
                 w_out[0], norm2_g[0], w_group[0], b_group[0], w_expert_router[0],
                 b_expert_router[0], w_gate[0], w_up[0], w_down[0], final_g)
    return out.reshape(batch, s, d)
```

```python
import functools

import jax
import jax.numpy as jnp
from jax import lax
from jax.experimental import pallas as pl
from jax.experimental.pallas import tpu as pltpu

F32 = jnp.float32
BF16 = jnp.bfloat16
I32 = jnp.int32

CHUNK = 64
HEAD_DIM = 64
LEFT_CHUNKS = 8
REL_CLIP = 128
CONV_WIDTH = 31
N_GROUPS = 8
EXPERTS_PER_GROUP = 8
N_EXPERTS = N_GROUPS * EXPERTS_PER_GROUP
TOP_K = 2
BLOCK_ROWS = 128
RMS_EPS = 1e-6
LN_EPS = 1e-5
NEG_INF = -1e30

VMEM_LIMIT_BYTES = 56 * 1024 * 1024
LANES = 128
SUBLANES = 8
PAIR = 2 * CHUNK
BAND_PAIR = (LEFT_CHUNKS + 2) * CHUNK
FRONT_PAD = LEFT_CHUNKS * CHUNK
CONV_HALO = 32


def _cparams(*sem):
    return pltpu.CompilerParams(dimension_semantics=sem, vmem_limit_bytes=VMEM_LIMIT_BYTES)


def _load_weight_as_bf16(w_hbm, w_ref, stage, wsem):
    cw = stage.shape[2]
    n_chunks = w_ref.shape[1] // cw

    def chunk_copy(c):
        return pltpu.make_async_copy(w_hbm.at[:, c * cw:(c + 1) * cw], stage.at[c % 2], wsem.at[c % 2])

    chunk_copy(0).start()
    for c in range(n_chunks):
        if c + 1 < n_chunks:
            chunk_copy(c + 1).start()
        chunk_copy(c).wait()
        w_ref[:, c * cw:(c + 1) * cw] = stage[c % 2].astype(BF16)


def _weight_scratch(d, d_out, chunk):
    return [pltpu.VMEM((d, d_out), BF16), pltpu.VMEM((2, d, chunk), F32), pltpu.SemaphoreType.DMA((2,))]


def _inproj_body(x_ref, g_ref, w_hbm, cv_ref, cg_ref, q_ref, k_ref, v_ref, w_ref, stage, wsem):
    @pl.when(pl.program_id(0) == 0)
    def _():
        _load_weight_as_bf16(w_hbm, w_ref, stage, wsem)

    x = x_ref[...]
    ms = jnp.mean(x * x, axis=-1, keepdims=True)
    h = (x * lax.rsqrt(ms + RMS_EPS) * g_ref[...]).astype(BF16)
    dc = cv_ref.shape[1]
    da = q_ref.shape[1]

    def proj(lo, width):
        return jnp.dot(h, w_ref[:, lo:lo + width], preferred_element_type=F32)

    cv_ref[...] = proj(0, dc)
    cg_ref[...] = proj(dc, dc)
    q_ref[...] = (proj(2 * dc, da) * (HEAD_DIM ** -0.5)).astype(BF16)
    k_ref[...] = proj(2 * dc + da, da).astype(BF16)
    v_ref[...] = proj(2 * dc + 2 * da, da).astype(BF16)


def _in_proj(x, g, w, d_conv, d_att, tm=256, w_chunk=512):
    s, d = x.shape
    assert s % tm == 0 and w.shape[1] % w_chunk == 0
    d_in = w.shape[1]
    row = lambda i: (i, 0)
    const = lambda i: (0, 0)
    return pl.pallas_call(
        _inproj_body,
        grid=(s // tm,),
        in_specs=[
            pl.BlockSpec((tm, d), row),
            pl.BlockSpec((1, d), const),
            pl.BlockSpec(memory_space=pl.ANY),
        ],
        out_specs=[
            pl.BlockSpec((tm, d_conv), row),
            pl.BlockSpec((tm, d_conv), row),
            pl.BlockSpec((tm, d_att), row),
            pl.BlockSpec((tm, d_att), row),
            pl.BlockSpec((tm, d_att), row),
        ],
        out_shape=[
            jax.ShapeDtypeStruct((s, d_conv), F32),
            jax.ShapeDtypeStruct((s, d_conv), F32),
            jax.ShapeDtypeStruct((s, d_att), BF16),
            jax.ShapeDtypeStruct((s, d_att), BF16),
            jax.ShapeDtypeStruct((s, d_att), BF16),
        ],
        scratch_shapes=_weight_scratch(d, d_in, w_chunk),
        compiler_params=_cparams("arbitrary"),
        name="in_proj",
    )(x, g, w)


def _sigmoid(x):
    return 1.0 / (1.0 + jnp.exp(-x))


def _conv_body(cv_ref, cg_ref, cvh_ref, cgh_ref, w_ref, b_ref, lg_ref, lb_ref, o_ref,
               u_ref, us_ref, wb_ref, conv_ref, *, rows_per_chunk, groups_per_pass):
    i = pl.program_id(0)
    t, dc = cv_ref.shape

    @pl.when(i == 0)
    def _():
        for w in range(CONV_WIDTH):
            wb_ref[w] = jnp.broadcast_to(w_ref[w:w + 1, :], (SUBLANES, dc))

    halo = cvh_ref[...] * _sigmoid(cgh_ref[...])
    u_ref[0:CONV_HALO, :] = jnp.where(i > 0, halo, 0.0)
    u_ref[CONV_HALO:, :] = cv_ref[...] * _sigmoid(cg_ref[...])
    n_shift_rows = us_ref.shape[1]
    for sft in range(1, SUBLANES):
        us_ref[sft - 1] = u_ref[sft:sft + n_shift_rows, :]
    first_tap = CONV_HALO - (CONV_WIDTH - 1)

    def lane_tile(lt, _):
        lanes = pl.ds(pl.multiple_of(lt * LANES, LANES), LANES)
        wv = [wb_ref[w, :, lanes] for w in range(CONV_WIDTH)]
        bias = jnp.broadcast_to(b_ref[:, lanes], (SUBLANES, LANES))
        for g0 in range(0, t // SUBLANES, groups_per_pass):
            accs = [bias] * groups_per_pass
            for w in range(CONV_WIDTH):
                off = first_tap + w
                sft = off % SUBLANES
                for g in range(groups_per_pass):
                    lo = (g0 + g) * SUBLANES + off - sft
                    if sft == 0:
                        src = u_ref[lo:lo + SUBLANES, lanes]
                    else:
                        src = us_ref[sft - 1, lo:lo + SUBLANES, lanes]
                    accs[g] = accs[g] + wv[w] * src
            for g in range(groups_per_pass):
                lo = (g0 + g) * SUBLANES
                conv_ref[lo:lo + SUBLANES, lanes] = accs[g]
        return 0

    lax.fori_loop(0, dc // LANES, lane_tile, 0)

    r = rows_per_chunk
    for c in range(t // r):
        acc = conv_ref[c * r:(c + 1) * r, :]
        mu = jnp.mean(acc, axis=-1, keepdims=True)
        cen = acc - mu
        var = jnp.mean(cen * cen, axis=-1, keepdims=True)
        un = cen * lax.rsqrt(var + LN_EPS) * lg_ref[...] + lb_ref[...]
        o_ref[c * r:(c + 1) * r, :] = (un * _sigmoid(un)).astype(o_ref.dtype)


def _conv_module(cv, cg, w_dw, b_dw, ln_g, ln_b, t=512, rows_per_chunk=16, groups_per_pass=4):
    s, dc = cv.shape
    assert s % t == 0
    halo_blocks = t // CONV_HALO
    row = lambda i: (i, 0)
    prev = lambda i: (jnp.maximum(i * halo_blocks - 1, 0), 0)
    const = lambda i: (0, 0)
    return pl.pallas_call(
        functools.partial(_conv_body, rows_per_chunk=rows_per_chunk, groups_per_pass=groups_per_pass),
        grid=(s // t,),
        in_specs=[
            pl.BlockSpec((t, dc), row),
            pl.BlockSpec((t, dc), row),
            pl.BlockSpec((CONV_HALO, dc), prev),
            pl.BlockSpec((CONV_HALO, dc), prev),
            pl.BlockSpec((CONV_WIDTH, dc), const),
            pl.BlockSpec((1, dc), const),
            pl.BlockSpec((1, dc), const),
            pl.BlockSpec((1, dc), const),
        ],
        out_specs=pl.BlockSpec((t, dc), row),
        out_shape=jax.ShapeDtypeStruct((s, dc), BF16),
        scratch_shapes=[
            pltpu.VMEM((t + CONV_HALO, dc), F32),
            pltpu.VMEM((SUBLANES - 1, t + CONV_HALO - SUBLANES, dc), F32),
            pltpu.VMEM((CONV_WIDTH, SUBLANES, dc), F32),
            pltpu.VMEM((t, dc), F32),
        ],
        compiler_params=_cparams("arbitrary"),
        name="conv_module",
    )(cv, cg, cv, cg, w_dw, b_dw, ln_g, ln_b)


SOFTMAX_ROWS = 64


def _attn_pairs(q_ref, bands, bias_ref, s_ref, e_ref, o_ref):
    lane = lax.broadcasted_iota(I32, (PAIR, 2 * HEAD_DIM), 1)
    n_pairs = len(bands)
    rdens = {}

    def scores(u):
        kb, _, _ = bands[u]
        q2 = q_ref[u * PAIR:(u + 1) * PAIR, :]
        for h in range(2):
            in_head = (lane >= h * HEAD_DIM) & (lane < (h + 1) * HEAD_DIM)
            qh = jnp.where(in_head, q2, jnp.zeros_like(q2))
            s_ref[2 * u + h, :, :kb.shape[0]] = lax.dot_general(
                qh, kb, (((1,), (1,)), ((), ())), preferred_element_type=F32)

    def softmax(u):
        kb, _, col_lo = bands[u]
        n_keys = kb.shape[0]
        for h in range(2):
            c = 2 * u + h
            parts = []
            for r0 in range(0, PAIR, SOFTMAX_ROWS):
                rows = slice(r0, r0 + SOFTMAX_ROWS)
                t = s_ref[c, rows, :n_keys] + bias_ref[h, rows, col_lo:col_lo + n_keys]
                e = jnp.exp(t - jnp.max(t, axis=-1, keepdims=True))
                parts.append(1.0 / jnp.sum(e, axis=-1, keepdims=True))
                e_ref[c, rows, :n_keys] = e.astype(BF16)
            rdens[c] = jnp.concatenate(parts, axis=0)

    def values(u):
        _, vb, _ = bands[u]
        n_keys = vb.shape[0]
        outs = []
        for h in range(2):
            c = 2 * u + h
            o = jnp.dot(e_ref[c, :, :n_keys], vb, preferred_element_type=F32)
            outs.append(o * rdens[c])
        o_ref[u * PAIR:(u + 1) * PAIR, :] = jnp.where(lane < HEAD_DIM, outs[0], outs[1]).astype(o_ref.dtype)

    for step in range(n_pairs + 2):
        if step < n_pairs:
            scores(step)
        if 0 <= step - 1 < n_pairs:
            softmax(step - 1)
        if 0 <= step - 2 < n_pairs:
            values(step - 2)


def _attn_body(q_ref, k_ref, v_ref, g_ref, o_ref, bias_ref, s_ref, e_ref, *, pairs_per_step):
    st = pl.program_id(1)
    npp = pairs_per_step

    @pl.when(st == 0)
    def _():
        sub = lax.broadcasted_iota(I32, (SUBLANES, BAND_PAIR), 0)
        kj = lax.broadcasted_iota(I32, (SUBLANES, BAND_PAIR), 1)
        for h in range(2):
            t8 = jnp.broadcast_to(g_ref[h], (SUBLANES, BAND_PAIR))
            shift = 1
            while shift < SUBLANES:
                t8 = jnp.where((sub & shift) != 0, pltpu.roll(t8, shift, 1), t8)
                shift *= 2
            for r0 in range(0, PAIR, SUBLANES):
                band_lo = (r0 // CHUNK) * CHUNK
                in_band = (kj >= band_lo) & (kj < band_lo + (LEFT_CHUNKS + 1) * CHUNK)
                rows = t8 if r0 == 0 else pltpu.roll(t8, r0, 1)
                bias_ref[h, r0:r0 + SUBLANES, :] = jnp.where(in_band, rows, NEG_INF)

    def run(first_step):
        bands = []
        for u in range(npp):
            if first_step:
                n_keys = min((u + 1) * PAIR, BAND_PAIR)
                lo = max((u + 1) * PAIR - BAND_PAIR, 0)
                bands.append((k_ref[lo:lo + n_keys, :], v_ref[lo:lo + n_keys, :], BAND_PAIR - n_keys))
            else:
                start = pl.multiple_of((st * npp + u) * PAIR - FRONT_PAD, PAIR)
                bands.append((k_ref[pl.ds(start, BAND_PAIR), :], v_ref[pl.ds(start, BAND_PAIR), :], 0))
        _attn_pairs(q_ref, bands, bias_ref, s_ref, e_ref, o_ref)

    @pl.when(st == 0)
    def _():
        run(True)

    @pl.when(st > 0)
    def _():
        run(False)


def _distance_rows(rel_table):
    t = rel_table.astype(F32)
    far = t[:, 2 * REL_CLIP:]
    n_far = LEFT_CHUNKS * CHUNK - REL_CLIP + 1
    near = jnp.flip(t[:, REL_CLIP - CHUNK + 1:2 * REL_CLIP], axis=1)
    tail = BAND_PAIR - n_far - near.shape[1]
    g = jnp.concatenate([jnp.repeat(far, n_far, axis=1), near, jnp.repeat(far, tail, axis=1)], axis=1)
    return g[:, None, :]


def _attention(q, k, v, g_rows, pairs_per_step=8):
    s, da = q.shape
    n_hp = da // (2 * HEAD_DIM)
    tq = pairs_per_step * PAIR
    assert tq >= FRONT_PAD, "first step must cover every clamped band"
    assert s % tq == 0
    return pl.pallas_call(
        functools.partial(_attn_body, pairs_per_step=pairs_per_step),
        grid=(n_hp, s // tq),
        in_specs=[
            pl.BlockSpec((tq, 2 * HEAD_DIM), lambda hp, st: (st, hp)),
            pl.BlockSpec((s, 2 * HEAD_DIM), lambda hp, st: (0, hp)),
            pl.BlockSpec((s, 2 * HEAD_DIM), lambda hp, st: (0, hp)),
            pl.BlockSpec((2, 1, BAND_PAIR), lambda hp, st: (hp, 0, 0)),
        ],
        out_specs=pl.BlockSpec((tq, 2 * HEAD_DIM), lambda hp, st: (st, hp)),
        out_shape=jax.ShapeDtypeStruct((s, da), BF16),
        scratch_shapes=[
            pltpu.VMEM((2, PAIR, BAND_PAIR), F32),
            pltpu.VMEM((2 * pairs_per_step, PAIR, BAND_PAIR), F32),
            pltpu.VMEM((2 * pairs_per_step, PAIR, BAND_PAIR), BF16),
        ],
        compiler_params=_cparams("arbitrary", "arbitrary"),
        name="chunk_attention",
    )(q, k, v, g_rows)


def _slab_rows(d):
    return d // LANES


def _store_slabs(ref, val):
    t, d = val.shape
    n = _slab_rows(d)
    for c in range(n):
        ref[pl.ds(c, t, stride=n), :] = val[:, c * LANES:(c + 1) * LANES]


def _load_slabs(ref, t, n):
    return jnp.concatenate([ref[pl.ds(c, t, stride=n), :] for c in range(n)], axis=1)


def _first_index_of_max(vals, sub, n):
    vmax = jnp.max(vals, axis=0, keepdims=True)
    idx = jnp.min(jnp.where(vals == vmax, sub, n), axis=0, keepdims=True)
    return vmax, idx


def _outproj_body(x_ref, co_ref, ao_ref, w_hbm, g2_ref, wr_ref, br_ref,
                  x2_ref, h2_ref, eid_ref, gcol_ref, w_ref, stage, wsem):
    @pl.when(pl.program_id(0) == 0)
    def _():
        _load_weight_as_bf16(w_hbm, w_ref, stage, wsem)

    dc = co_ref.shape[1]
    y = jnp.dot(co_ref[...], w_ref[0:dc, :], preferred_element_type=F32)
    y = y + jnp.dot(ao_ref[...], w_ref[dc:, :], preferred_element_type=F32)
    x2 = x_ref[...] + y
    ms = jnp.mean(x2 * x2, axis=-1, keepdims=True)
    h2 = x2 * lax.rsqrt(ms + RMS_EPS) * g2_ref[...]
    x2_ref[...] = x2
    _store_slabs(h2_ref, h2)

    hi = h2.astype(BF16)
    lo = (h2 - hi.astype(F32)).astype(BF16)
    pa = jnp.dot(hi, wr_ref[...], preferred_element_type=F32)
    pb = jnp.dot(lo, wr_ref[...], preferred_element_type=F32)
    logits = pa[:, :LANES] + pa[:, LANES:] + pb[:, :LANES]
    lt = logits.T + br_ref[...]
    tm = lt.shape[1]
    sub = lax.broadcasted_iota(I32, (N_GROUPS, tm), 0)
    g = lt[0:N_GROUPS, :]
    gmax, grp = _first_index_of_max(g, sub, N_GROUPS)
    p_grp = 1.0 / jnp.sum(jnp.exp(g - gmax), axis=0, keepdims=True)
    sel = jnp.zeros((EXPERTS_PER_GROUP, tm), F32)
    for gi in range(N_GROUPS):
        lo = N_GROUPS + gi * EXPERTS_PER_GROUP
        sel = jnp.where(grp == gi, lt[lo:lo + EXPERTS_PER_GROUP, :], sel)
    v1, i1 = _first_index_of_max(sel, sub, EXPERTS_PER_GROUP)
    rest = jnp.where(sub == i1, -jnp.inf, sel)
    v2, i2 = _first_index_of_max(rest, sub, EXPERTS_PER_GROUP)
    t = jnp.exp(v2 - v1)
    gate1 = p_grp / (1.0 + t)
    gate2 = p_grp * t / (1.0 + t)
    eid_ref[0:1, :] = grp * EXPERTS_PER_GROUP + i1
    eid_ref[1:2, :] = grp * EXPERTS_PER_GROUP + i2
    row = lax.broadcasted_iota(I32, (LANES, tm), 0)
    gl = jnp.where(row == 0, gate1, jnp.where(row == 1, gate2, 0.0))
    gcol_ref[...] = gl.T


def _out_proj_router(x, conv_out, attn_out, w, g2, wr_cat, br_col, tm=512, w_chunk=512):
    s, d = x.shape
    assert s % tm == 0
    dc = conv_out.shape[1]
    da = attn_out.shape[1]
    row = lambda i: (i, 0)
    const = lambda i: (0, 0)
    return pl.pallas_call(
        _outproj_body,
        grid=(s // tm,),
        in_specs=[
            pl.BlockSpec((tm, d), row),
            pl.BlockSpec((tm, dc), row),
            pl.BlockSpec((tm, da), row),
            pl.BlockSpec(memory_space=pl.ANY),
            pl.BlockSpec((1, d), const),
            pl.BlockSpec((d, 2 * LANES), const),
            pl.BlockSpec((LANES, 1), const),
        ],
        out_specs=[
            pl.BlockSpec((tm, d), row),
            pl.BlockSpec((tm * _slab_rows(d), LANES), row),
            pl.BlockSpec((TOP_K, tm), lambda i: (0, i)),
            pl.BlockSpec((tm, LANES), row),
        ],
        out_shape=[
            jax.ShapeDtypeStruct((s, d), F32),
            jax.ShapeDtypeStruct((s * _slab_rows(d), LANES), F32),
            jax.ShapeDtypeStruct((TOP_K, s), I32),
            jax.ShapeDtypeStruct((s, LANES), F32),
        ],
        scratch_shapes=_weight_scratch(dc + da, d, w_chunk),
        compiler_params=_cparams("arbitrary"),
        name="out_proj_router",
    )(x, conv_out, attn_out, w, g2, wr_cat, br_col)


META_LANES = 256
(META_FIRST_BLK, META_N_BLK, META_COUNT, META_NEXT_EXP, META_ORDINAL,
 META_USED_BLKS, META_FIRST_USED) = range(7)


def _dispatch_body(eid_ref, dest_ref, meta_ref, rank_ref):
    s = eid_ref.shape[1]
    w = META_LANES
    sub = lax.broadcasted_iota(I32, (N_EXPERTS, w), 0)
    lane = lax.broadcasted_iota(I32, (N_EXPERTS, w), 1)
    ri = lax.broadcasted_iota(I32, (w, w), 0)
    ci = lax.broadcasted_iota(I32, (w, w), 1)
    upper = (ri < ci).astype(BF16)

    def onehots(b):
        off = pl.multiple_of(b * w, w)
        e0 = eid_ref[0:1, pl.ds(off, w)]
        e1 = eid_ref[1:2, pl.ds(off, w)]
        return off, sub == e0, sub == e1

    def rank_block(b, carry):
        off, o0, o1 = onehots(b)
        both = jnp.where(o0 | o1, 1.0, 0.0)
        pre = jnp.dot(both.astype(BF16), upper, preferred_element_type=F32) + carry
        rank_ref[0:1, pl.ds(off, w)] = jnp.sum(jnp.where(o0, pre, 0.0), axis=0, keepdims=True)
        rank_ref[1:2, pl.ds(off, w)] = jnp.sum(jnp.where(o1, pre, 0.0), axis=0, keepdims=True)
        return carry + jnp.sum(both, axis=1, keepdims=True)

    counts = lax.fori_loop(0, s // w, rank_block, jnp.zeros((N_EXPERTS, w), F32))

    nblk = ((counts.astype(I32) + (BLOCK_ROWS - 1)) // BLOCK_ROWS).astype(F32)
    er = lax.broadcasted_iota(I32, (N_EXPERTS, N_EXPERTS), 0)
    ec = lax.broadcasted_iota(I32, (N_EXPERTS, N_EXPERTS), 1)
    lower_incl = (ec <= er).astype(BF16)
    pend_blk = jnp.dot(lower_incl, nblk.astype(BF16), preferred_element_type=F32)
    pstart = (pend_blk - nblk) * float(BLOCK_ROWS)

    def on_lanes(v):
        return jnp.sum(jnp.where(sub == lane, v, 0), axis=0, keepdims=True)

    nblk_i = nblk.astype(I32)
    first_blk = on_lanes((pend_blk - nblk).astype(I32))
    n_blk = on_lanes(nblk_i)
    count = on_lanes(counts.astype(I32))
    has_rows = nblk_i > 0
    next_exp = jnp.min(jnp.where(has_rows & (sub > lane), sub, N_EXPERTS), axis=0, keepdims=True)
    next_exp = jnp.where(next_exp < N_EXPERTS, next_exp, -1)
    ordinal = jnp.sum(jnp.where(has_rows & (sub < lane), 1, 0), axis=0, keepdims=True)
    first_used = jnp.min(jnp.where(has_rows, sub, N_EXPERTS - 1), axis=0, keepdims=True)
    used = pend_blk[N_EXPERTS - 1:N_EXPERTS, :].astype(I32)
    mrow = lax.broadcasted_iota(I32, (8, w), 0)
    meta = jnp.where(mrow == META_FIRST_BLK, first_blk, 0)
    meta = jnp.where(mrow == META_N_BLK, n_blk, meta)
    meta = jnp.where(mrow == META_COUNT, count, meta)
    meta = jnp.where(mrow == META_NEXT_EXP, next_exp, meta)
    meta = jnp.where(mrow == META_ORDINAL, ordinal, meta)
    meta = jnp.where(mrow == META_FIRST_USED, first_used, meta)
    meta_ref[...] = jnp.where(mrow == META_USED_BLKS, used, meta)

    def dest_block(b, _):
        off, o0, o1 = onehots(b)
        d0 = rank_ref[0:1, pl.ds(off, w)] + jnp.sum(jnp.where(o0, pstart, 0.0), axis=0, keepdims=True)
        d1 = rank_ref[1:2, pl.ds(off, w)] + jnp.sum(jnp.where(o1, pstart, 0.0), axis=0, keepdims=True)
        dest_ref[0:1, pl.ds(off, w)] = d0.astype(I32)
        dest_ref[1:2, pl.ds(off, w)] = d1.astype(I32)
        return 0

    lax.fori_loop(0, s // w, dest_block, 0)


def _dispatch(eids):
    s = eids.shape[1]
    return pl.pallas_call(
        _dispatch_body,
        out_shape=[
            jax.ShapeDtypeStruct((TOP_K, s), I32),
            jax.ShapeDtypeStruct((8, META_LANES), I32),
        ],
        scratch_shapes=[pltpu.VMEM((TOP_K, s), F32)],
        compiler_params=pltpu.CompilerParams(vmem_limit_bytes=VMEM_LIMIT_BYTES),
        name="dispatch",
    )(eids)


PAD_BITS = tuple(reversed(range(BLOCK_ROWS.bit_length() - 1)))


def _scatter_body(dest_ref, meta_ref, h2_ref, xs_hbm, zbuf, sem, zsem, *, tm):
    i = pl.program_id(0)
    n_sl = h2_ref.shape[0] // tm
    blk = BLOCK_ROWS * n_sl
    n_blocks = xs_hbm.shape[0] // blk
    used = meta_ref[META_USED_BLKS, 0]

    def slab(row, n_tokens=1):
        return pl.ds(pl.multiple_of(row * n_sl, n_sl), n_tokens * n_sl)

    def token_copy(j, k):
        t = i * tm + j
        return pltpu.make_async_copy(h2_ref.at[slab(j)], xs_hbm.at[slab(dest_ref[k, t])], sem.at[k])

    def scatter_tile():
        def one(j, _):
            for k in range(TOP_K):
                token_copy(j, k).start(priority=k)
            return 0

        lax.fori_loop(0, tm, one, 0, unroll=8)

    def wait_tile():
        for k in range(TOP_K):
            pltpu.make_async_copy(h2_ref, xs_hbm.at[slab(0, tm)], sem.at[k]).wait()

    @pl.when(i > 0)
    def _():
        scatter_tile()
        wait_tile()

    @pl.when(i == 0)
    def _():
        _scatter_first_step(meta_ref, xs_hbm, zbuf, zsem, slab, used, n_blocks, scatter_tile, wait_tile)


def _scatter_first_step(meta_ref, xs_hbm, zbuf, zsem, slab, used, n_blocks, scatter_tile, wait_tile):
    n_sl = zbuf.shape[0] // BLOCK_ROWS

    def zero_copy(row, n_tokens):
        return pltpu.make_async_copy(zbuf.at[pl.ds(0, n_tokens * n_sl)], xs_hbm.at[slab(row, n_tokens)], zsem)

    zbuf[...] = jnp.zeros_like(zbuf)

    def per_expert(e, issued):
        b0 = meta_ref[META_FIRST_BLK, e]
        nb = meta_ref[META_N_BLK, e]
        cnt = meta_ref[META_COUNT, e]
        pad = nb * BLOCK_ROWS - cnt
        row = b0 * BLOCK_ROWS + cnt
        out = []
        for bit, n_issued in zip(PAD_BITS, issued):
            size = 1 << bit
            take = pad & size

            @pl.when(take != 0)
            def _():
                zero_copy(row, size).start()

            row = row + take
            out.append(n_issued + take // size)
        return tuple(out)

    issued = lax.fori_loop(0, N_EXPERTS, per_expert, tuple(jnp.int32(0) for _ in PAD_BITS))
    lax.fori_loop(used, n_blocks, lambda b, _: (zero_copy(b * BLOCK_ROWS, BLOCK_ROWS).start(), 0)[1], 0)

    scatter_tile()
    wait_tile()

    for bit, n_issued in zip(PAD_BITS, issued):
        lax.fori_loop(0, n_issued, lambda i, _, n=1 << bit: (zero_copy(0, n).wait(), 0)[1], 0)
    lax.fori_loop(used, n_blocks, lambda b, _: (zero_copy(0, BLOCK_ROWS).wait(), 0)[1], 0)


def _scatter_rows(dest, meta, h2_slabs, rows, tm=1024):
    s = dest.shape[1]
    assert s % tm == 0
    n_sl = h2_slabs.shape[0] // s
    grid_spec = pltpu.PrefetchScalarGridSpec(
        num_scalar_prefetch=2,
        grid=(s // tm,),
        in_specs=[pl.BlockSpec((tm * n_sl, LANES), lambda i, dest_ref, meta_ref: (i, 0))],
        out_specs=pl.BlockSpec(memory_space=pl.ANY),
        scratch_shapes=[
            pltpu.VMEM((BLOCK_ROWS * n_sl, LANES), h2_slabs.dtype),
            pltpu.SemaphoreType.DMA((TOP_K,)),
            pltpu.SemaphoreType.DMA(()),
        ],
    )
    return pl.pallas_call(
        functools.partial(_scatter_body, tm=tm),
        grid_spec=grid_spec,
        out_shape=jax.ShapeDtypeStruct((rows * n_sl, LANES), h2_slabs.dtype),
        compiler_params=_cparams("arbitrary"),
        name="scatter_rows",
    )(dest, meta, h2_slabs)


def _row_gather_wait(src_hbm, dst, sem, n_rows):
    pltpu.make_async_copy(src_hbm.at[pl.ds(0, n_rows)], dst.at[pl.ds(0, n_rows)], sem).wait()


WEIGHT_QUEUES = (1, 1, 1)
ROW_BLOCK_QUEUE = 0


def _experts_body(meta_ref, xs_hbm, wg_hbm, wu_hbm, wd_hbm, ys_hbm,
                  xbuf, ybuf, wgbuf, wubuf, wdbuf, gsem, ysem, wsem):
    used = meta_ref[META_USED_BLKS, 0]
    n_experts, d, _ = wg_hbm.shape
    n_sl = _slab_rows(d)
    blk = BLOCK_ROWS * n_sl
    n_blocks = ys_hbm.shape[0] // blk

    def xs_copy(b, sl):
        src_blk = jnp.minimum(b, n_blocks - 1)
        rows = pl.ds(pl.multiple_of(src_blk * blk, blk), blk)
        return pltpu.make_async_copy(xs_hbm.at[rows], xbuf.at[sl], gsem.at[sl])

    def gather_start(b, sl):
        xs_copy(b, sl).start(priority=ROW_BLOCK_QUEUE)

    def weights_start(e, ws):
        for c, queue in zip(weight_copies(e, ws), WEIGHT_QUEUES):
            c.start(priority=queue)

    def gather_wait(sl):
        xs_copy(0, sl).wait()

    def ys_copy(b):
        sl = b % 2
        rows = pl.ds(pl.multiple_of(b * blk, blk), blk)
        return pltpu.make_async_copy(ybuf.at[sl], ys_hbm.at[rows], ysem.at[sl])

    def weight_copies(e, ws):
        return (pltpu.make_async_copy(wg_hbm.at[e], wgbuf.at[ws], wsem.at[ws, 0]),
                pltpu.make_async_copy(wu_hbm.at[e], wubuf.at[ws], wsem.at[ws, 1]),
                pltpu.make_async_copy(wd_hbm.at[e], wdbuf.at[ws], wsem.at[ws, 2]))

    weights_start(meta_ref[META_FIRST_USED, 0], 0)
    gather_start(0, 0)

    def expert(e, _):
        nb = meta_ref[META_N_BLK, e]

        @pl.when(nb > 0)
        def _():
            b0 = meta_ref[META_FIRST_BLK, e]
            ws = meta_ref[META_ORDINAL, e] % 2
            nxt = meta_ref[META_NEXT_EXP, e]

            @pl.when(nxt >= 0)
            def _():
                weights_start(nxt, 1 - ws)

            for c in weight_copies(e, ws):
                c.wait()

            def block(j, _):
                b = b0 + j
                sl = b % 2
                gather_start(b + 1, 1 - sl)
                gather_wait(sl)

                @pl.when(b >= 2)
                def _():
                    ys_copy(b - 2).wait()

                x = _load_slabs(xbuf.at[sl], BLOCK_ROWS, n_sl).astype(BF16)
                a = jnp.dot(x, wgbuf[ws].astype(BF16), preferred_element_type=F32)
                u = jnp.dot(x, wubuf[ws].astype(BF16), preferred_element_type=F32)
                act = (a * _sigmoid(a) * u).astype(BF16)
                y = jnp.dot(act, wdbuf[ws].astype(BF16), preferred_element_type=F32)
                _store_slabs(ybuf.at[sl], y)
                ys_copy(b).start(priority=ROW_BLOCK_QUEUE)
                return 0

            lax.fori_loop(0, nb, block, 0)

        return 0

    lax.fori_loop(0, n_experts, expert, 0)

    gather_wait(used % 2)
    for back in (1, 2):
        @pl.when(used >= back)
        def _():
            ys_copy(used - back).wait()


def _experts(meta, xs_slabs, w_gate, w_up, w_down):
    n_e, d, de = w_gate.shape
    n_sl = _slab_rows(d)
    blk = BLOCK_ROWS * n_sl
    smem = pl.BlockSpec(memory_space=pltpu.SMEM)
    hbm = pl.BlockSpec(memory_space=pl.ANY)
    return pl.pallas_call(
        _experts_body,
        in_specs=[smem, hbm, hbm, hbm, hbm],
        out_specs=hbm,
        out_shape=jax.ShapeDtypeStruct(xs_slabs.shape, F32),
        input_output_aliases={1: 0},
        scratch_shapes=[
            pltpu.VMEM((2, blk, LANES), F32),
            pltpu.VMEM((2, blk, LANES), F32),
            pltpu.VMEM((2, d, de), F32),
            pltpu.VMEM((2, d, de), F32),
            pltpu.VMEM((2, de, d), F32),
            pltpu.SemaphoreType.DMA((2,)),
            pltpu.SemaphoreType.DMA((2,)),
            pltpu.SemaphoreType.DMA((2, 3)),
        ],
        compiler_params=pltpu.CompilerParams(vmem_limit_bytes=VMEM_LIMIT_BYTES),
        name="expert_mlp",
    )(meta, xs_slabs, w_gate, w_up, w_down)


def _combine_body(dest_ref, x2_ref, gcol_ref, fg_ref, ys_hbm, o_ref, ybuf, sem, *, tm):
    i = pl.program_id(0)
    n = pl.num_programs(0)
    s = n * tm
    slot = i % 2
    n_sl = _slab_rows(x2_ref.shape[1])

    def start(tile, sl):
        for k in range(TOP_K):
            for r in range(tm):
                row = dest_ref[k * s + tile * tm + r]
                src = ys_hbm.at[pl.ds(pl.multiple_of(row * n_sl, n_sl), n_sl)]
                pltpu.make_async_copy(src, ybuf.at[sl, k, pl.ds(r * n_sl, n_sl)],
                                      sem.at[sl, k]).start(priority=r % 2)

    def wait(sl):
        for k in range(TOP_K):
            _row_gather_wait(ys_hbm, ybuf.at[sl, k], sem.at[sl, k], tm * n_sl)

    @pl.when(i == 0)
    def _():
        start(0, 0)

    start(jnp.minimum(i + 1, n - 1), 1 - slot)
    wait(slot)
    g = gcol_ref[...]
    y0 = _load_slabs(ybuf.at[slot, 0], tm, n_sl)
    y1 = _load_slabs(ybuf.at[slot, 1], tm, n_sl)
    moe = g[:, 0:1] * y0 + g[:, 1:2] * y1
    x3 = x2_ref[...] + moe
    ms = jnp.mean(x3 * x3, axis=-1, keepdims=True)
    o_ref[...] = x3 * lax.rsqrt(ms + RMS_EPS) * fg_ref[...]

    @pl.when(i == n - 1)
    def _():
        wait(1 - slot)


def _combine(dest_flat, x2, gcol, final_g, ys, tm=256):
    s, d = x2.shape
    assert s % tm == 0
    row = lambda i, dest_ref: (i, 0)
    const = lambda i, dest_ref: (0, 0)
    grid_spec = pltpu.PrefetchScalarGridSpec(
        num_scalar_prefetch=1,
        grid=(s // tm,),
        in_specs=[
            pl.BlockSpec((tm, d), row),
            pl.BlockSpec((tm, LANES), row),
            pl.BlockSpec((1, d), const),
            pl.BlockSpec(memory_space=pl.ANY),
        ],
        out_specs=pl.BlockSpec((tm, d), row),
        scratch_shapes=[
            pltpu.VMEM((2, TOP_K, tm * _slab_rows(d), LANES), F32),
            pltpu.SemaphoreType.DMA((2, TOP_K)),
        ],
    )
    return pl.pallas_call(
        functools.partial(_combine_body, tm=tm),
        grid_spec=grid_spec,
        out_shape=jax.ShapeDtypeStruct((s, d), F32),
        compiler_params=_cparams("arbitrary"),
        name="combine_norm",
    )(dest_flat, x2, gcol, final_g, ys)


def _layer(x, norm1_g, w_in, w_dw, b_dw, ln_g, ln_b, rel_bias, w_out, norm2_g,
           w_group, b_group, w_expert_router, b_expert_router, w_gate, w_up, w_down, out_g):
    s, d = x.shape
    dc = w_dw.shape[-1]
    da = (w_in.shape[1] - 2 * dc) // 3
    row2 = lambda a: a.reshape(1, -1)

    cv, cg, q, k, v = _in_proj(x, row2(norm1_g), w_in, dc, da)
    conv_out = _conv_module(cv, cg, w_dw.reshape(CONV_WIDTH, dc), row2(b_dw), row2(ln_g), row2(ln_b))
    attn_out = _attention(q, k, v, _distance_rows(rel_bias))

    n_r = N_GROUPS + N_EXPERTS
    wr = jnp.pad(jnp.concatenate([w_group, w_expert_router], axis=1), ((0, 0), (0, LANES - n_r)))
    wr_hi = wr.astype(BF16)
    wr_lo = (wr - wr_hi.astype(F32)).astype(BF16)
    wr_cat = jnp.concatenate([wr_hi, wr_lo], axis=1)
    br_col = jnp.zeros((LANES, 1), F32).at[:n_r, 0].set(jnp.concatenate([b_group, b_expert_router]))
    x2, h2, eids, gcol = _out_proj_router(x, conv_out, attn_out, w_out, row2(norm2_g),
                                          wr_cat, br_col)

    n_blocks = (s * TOP_K) // BLOCK_ROWS + N_EXPERTS
    assert n_blocks <= META_LANES
    dest, meta = _dispatch(eids)
    xs = _scatter_rows(dest, meta, h2, n_blocks * BLOCK_ROWS)
    ys = _experts(meta, xs, w_gate, w_up, w_down)
    return _combine(dest.reshape(-1), x2, gcol, row2(out_g), ys)


def kernel(x, norm1_g, w_in, w_dw, b_dw, ln_g, ln_b, rel_bias, w_out, norm2_g, w_group, b_group,
           w_expert_router, b_expert_router, w_gate, w_up, w_down, final_g):
    depth = norm1_g.shape[0]
    assert depth == 1, "the final norm is fused into the last layer's combine step"
    batch, s, d = x.shape
    assert batch == 1, "one sequence per call"
    out = _layer(x.reshape(s, d), norm1_g[0], w_in[0], w_dw[0], b_dw[0], ln_g[0], ln_b[0], rel_bias[0],
                 w_out[0], norm2_g[0], w_group[0], b_group[0], w_expert_router[0],
                 b_expert_router[0], w_gate[0], w_up[0], w_down[0], final_g)
    return out.reshape(batch, s, d)
```

```python
import functools

import jax
import jax.numpy as jnp
from jax import lax
from jax.experimental import pallas as pl
from jax.experimental.pallas import tpu as pltpu

F32 = jnp.float32
BF16 = jnp.bfloat16
I32 = jnp.int32

CHUNK = 64
HEAD_DIM = 64
LEFT_CHUNKS = 8
REL_CLIP = 128
CONV_WIDTH = 31
N_GROUPS = 8
EXPERTS_PER_GROUP = 8
N_EXPERTS = N_GROUPS * EXPERTS_PER_GROUP
TOP_K = 2
BLOCK_ROWS = 128
RMS_EPS = 1e-6
LN_EPS = 1e-5
NEG_INF = -1e30

VMEM_LIMIT_BYTES = 56 * 1024 * 1024
LANES = 128
SUBLANES = 8
PAIR = 2 * CHUNK
BAND_PAIR = (LEFT_CHUNKS + 2) * CHUNK
FRONT_PAD = LEFT_CHUNKS * CHUNK
CONV_HALO = 32


def _cparams(*sem):
    return pltpu.CompilerParams(dimension_semantics=sem, vmem_limit_bytes=VMEM_LIMIT_BYTES)


def _load_weight_as_bf16(w_hbm, w_ref, stage, wsem):
    cw = stage.shape[2]
    n_chunks = w_ref.shape[1] // cw

    def chunk_copy(c):
        return pltpu.make_async_copy(w_hbm.at[:, c * cw:(c + 1) * cw], stage.at[c % 2], wsem.at[c % 2])

    chunk_copy(0).start()
    for c in range(n_chunks):
        if c + 1 < n_chunks:
            chunk_copy(c + 1).start()
        chunk_copy(c).wait()
        w_ref[:, c * cw:(c + 1) * cw] = stage[c % 2].astype(BF16)


def _weight_scratch(d, d_out, chunk):
    return [pltpu.VMEM((d, d_out), BF16), pltpu.VMEM((2, d, chunk), F32), pltpu.SemaphoreType.DMA((2,))]


def _inproj_body(x_ref, g_ref, w_hbm, cv_ref, cg_ref, q_ref, k_ref, v_ref, w_ref, stage, wsem):
    @pl.when(pl.program_id(0) == 0)
    def _():
        _load_weight_as_bf16(w_hbm, w_ref, stage, wsem)

    x = x_ref[...]
    ms = jnp.mean(x * x, axis=-1, keepdims=True)
    h = (x * lax.rsqrt(ms + RMS_EPS) * g_ref[...]).astype(BF16)
    dc = cv_ref.shape[1]
    da = q_ref.shape[1]

    def proj(lo, width):
        return jnp.dot(h, w_ref[:, lo:lo + width], preferred_element_type=F32)

    cv_ref[...] = proj(0, dc)
    cg_ref[...] = proj(dc, dc)
    q_ref[...] = (proj(2 * dc, da) * (HEAD_DIM ** -0.5)).astype(BF16)
    k_ref[...] = proj(2 * dc + da, da).astype(BF16)
    v_ref[...] = proj(2 * dc + 2 * da, da).astype(BF16)


def _in_proj(x, g, w, d_conv, d_att, tm=256, w_chunk=512):
    s, d = x.shape
    assert s % tm == 0 and w.shape[1] % w_chunk == 0
    d_in = w.shape[1]
    row = lambda i: (i, 0)
    const = lambda i: (0, 0)
    return pl.pallas_call(
        _inproj_body,
        grid=(s // tm,),
        in_specs=[
            pl.BlockSpec((tm, d), row),
            pl.BlockSpec((1, d), const),
            pl.BlockSpec(memory_space=pl.ANY),
        ],
        out_specs=[
            pl.BlockSpec((tm, d_conv), row),
            pl.BlockSpec((tm, d_conv), row),
            pl.BlockSpec((tm, d_att), row),
            pl.BlockSpec((tm, d_att), row),
            pl.BlockSpec((tm, d_att), row),
        ],
        out_shape=[
            jax.ShapeDtypeStruct((s, d_conv), F32),
            jax.ShapeDtypeStruct((s, d_conv), F32),
            jax.ShapeDtypeStruct((s, d_att), BF16),
            jax.ShapeDtypeStruct((s, d_att), BF16),
            jax.ShapeDtypeStruct((s, d_att), BF16),
        ],
        scratch_shapes=_weight_scratch(d, d_in, w_chunk),
        compiler_params=_cparams("arbitrary"),
        name="in_proj",
    )(x, g, w)


def _sigmoid(x):
    return 1.0 / (1.0 + jnp.exp(-x))


def _conv_body(cv_ref, cg_ref, cvh_ref, cgh_ref, w_ref, b_ref, lg_ref, lb_ref, o_ref,
               u_ref, us_ref, wb_ref, conv_ref, *, rows_per_chunk, groups_per_pass):
    i = pl.program_id(0)
    t, dc = cv_ref.shape

    @pl.when(i == 0)
    def _():
        for w in range(CONV_WIDTH):
            wb_ref[w] = jnp.broadcast_to(w_ref[w:w + 1, :], (SUBLANES, dc))

    halo = cvh_ref[...] * _sigmoid(cgh_ref[...])
    u_ref[0:CONV_HALO, :] = jnp.where(i > 0, halo, 0.0)
    u_ref[CONV_HALO:, :] = cv_ref[...] * _sigmoid(cg_ref[...])
    n_shift_rows = us_ref.shape[1]
    for sft in range(1, SUBLANES):
        us_ref[sft - 1] = u_ref[sft:sft + n_shift_rows, :]
    first_tap = CONV_HALO - (CONV_WIDTH - 1)

    def lane_tile(lt, _):
        lanes = pl.ds(pl.multiple_of(lt * LANES, LANES), LANES)
        wv = [wb_ref[w, :, lanes] for w in range(CONV_WIDTH)]
        bias = jnp.broadcast_to(b_ref[:, lanes], (SUBLANES, LANES))
        for g0 in range(0, t // SUBLANES, groups_per_pass):
            accs = [bias] * groups_per_pass
            for w in range(CONV_WIDTH):
                off = first_tap + w
                sft = off % SUBLANES
                for g in range(groups_per_pass):
                    lo = (g0 + g) * SUBLANES + off - sft
                    if sft == 0:
                        src = u_ref[lo:lo + SUBLANES, lanes]
                    else:
                        src = us_ref[sft - 1, lo:lo + SUBLANES, lanes]
                    accs[g] = accs[g] + wv[w] * src
            for g in range(groups_per_pass):
                lo = (g0 + g) * SUBLANES
                conv_ref[lo:lo + SUBLANES, lanes] = accs[g]
        return 0

    lax.fori_loop(0, dc // LANES, lane_tile, 0)

    r = rows_per_chunk
    for c in range(t // r):
        acc = conv_ref[c * r:(c + 1) * r, :]
        mu = jnp.mean(acc, axis=-1, keepdims=True)
        cen = acc - mu
        var = jnp.mean(cen * cen, axis=-1, keepdims=True)
        un = cen * lax.rsqrt(var + LN_EPS) * lg_ref[...] + lb_ref[...]
        o_ref[c * r:(c + 1) * r, :] = (un * _sigmoid(un)).astype(o_ref.dtype)


def _conv_module(cv, cg, w_dw, b_dw, ln_g, ln_b, t=512, rows_per_chunk=16, groups_per_pass=4):
    s, dc = cv.shape
    assert s % t == 0
    halo_blocks = t // CONV_HALO
    row = lambda i: (i, 0)
    prev = lambda i: (jnp.maximum(i * halo_blocks - 1, 0), 0)
    const = lambda i: (0, 0)
    return pl.pallas_call(
        functools.partial(_conv_body, rows_per_chunk=rows_per_chunk, groups_per_pass=groups_per_pass),
        grid=(s // t,),
        in_specs=[
            pl.BlockSpec((t, dc), row),
            pl.BlockSpec((t, dc), row),
            pl.BlockSpec((CONV_HALO, dc), prev),
            pl.BlockSpec((CONV_HALO, dc), prev),
            pl.BlockSpec((CONV_WIDTH, dc), const),
            pl.BlockSpec((1, dc), const),
            pl.BlockSpec((1, dc), const),
            pl.BlockSpec((1, dc), const),
        ],
        out_specs=pl.BlockSpec((t, dc), row),
        out_shape=jax.ShapeDtypeStruct((s, dc), BF16),
        scratch_shapes=[
            pltpu.VMEM((t + CONV_HALO, dc), F32),
            pltpu.VMEM((SUBLANES - 1, t + CONV_HALO - SUBLANES, dc), F32),
            pltpu.VMEM((CONV_WIDTH, SUBLANES, dc), F32),
            pltpu.VMEM((t, dc), F32),
        ],
        compiler_params=_cparams("arbitrary"),
        name="conv_module",
    )(cv, cg, cv, cg, w_dw, b_dw, ln_g, ln_b)


SOFTMAX_ROWS = 64


def _attn_pairs(q_ref, bands, bias_ref, s_ref, e_ref, o_ref):
    lane = lax.broadcasted_iota(I32, (PAIR, 2 * HEAD_DIM), 1)
    n_pairs = len(bands)
    rdens = {}

    def scores(u):
        kb, _, _ = bands[u]
        q2 = q_ref[u * PAIR:(u + 1) * PAIR, :]
        for h in range(2):
            in_head = (lane >= h * HEAD_DIM) & (lane < (h + 1) * HEAD_DIM)
            qh = jnp.where(in_head, q2, jnp.zeros_like(q2))
            s_ref[2 * u + h, :, :kb.shape[0]] = lax.dot_general(
                qh, kb, (((1,), (1,)), ((), ())), preferred_element_type=F32)

    def softmax(u):
        kb, _, col_lo = bands[u]
        n_keys = kb.shape[0]
        for h in range(2):
            c = 2 * u + h
            parts = []
            for r0 in range(0, PAIR, SOFTMAX_ROWS):
                rows = slice(r0, r0 + SOFTMAX_ROWS)
                t = s_ref[c, rows, :n_keys] + bias_ref[h, rows, col_lo:col_lo + n_keys]
                e = jnp.exp(t - jnp.max(t, axis=-1, keepdims=True))
                parts.append(1.0 / jnp.sum(e, axis=-1, keepdims=True))
                e_ref[c, rows, :n_keys] = e.astype(BF16)
            rdens[c] = jnp.concatenate(parts, axis=0)

    def values(u):
        _, vb, _ = bands[u]
        n_keys = vb.shape[0]
        outs = []
        for h in range(2):
            c = 2 * u + h
            o = jnp.dot(e_ref[c, :, :n_keys], vb, preferred_element_type=F32)
            outs.append(o * rdens[c])
        o_ref[u * PAIR:(u + 1) * PAIR, :] = jnp.where(lane < HEAD_DIM, outs[0], outs[1]).astype(o_ref.dtype)

    for step in range(n_pairs + 2):
        if step < n_pairs:
            scores(step)
        if 0 <= step - 1 < n_pairs:
            softmax(step - 1)
        if 0 <= step - 2 < n_pairs:
            values(step - 2)


def _attn_body(q_ref, k_ref, v_ref, g_ref, o_ref, bias_ref, s_ref, e_ref, *, pairs_per_step):
    st = pl.program_id(1)
    npp = pairs_per_step

    @pl.when(st == 0)
    def _():
        sub = lax.broadcasted_iota(I32, (SUBLANES, BAND_PAIR), 0)
        kj = lax.broadcasted_iota(I32, (SUBLANES, BAND_PAIR), 1)
        for h in range(2):
            t8 = jnp.broadcast_to(g_ref[h], (SUBLANES, BAND_PAIR))
            shift = 1
            while shift < SUBLANES:
                t8 = jnp.where((sub & shift) != 0, pltpu.roll(t8, shift, 1), t8)
                shift *= 2
            for r0 in range(0, PAIR, SUBLANES):
                band_lo = (r0 // CHUNK) * CHUNK
                in_band = (kj >= band_lo) & (kj < band_lo + (LEFT_CHUNKS + 1) * CHUNK)
                rows = t8 if r0 == 0 else pltpu.roll(t8, r0, 1)
                bias_ref[h, r0:r0 + SUBLANES, :] = jnp.where(in_band, rows, NEG_INF)

    def run(first_step):
        bands = []
        for u in range(npp):
            if first_step:
                n_keys = min((u + 1) * PAIR, BAND_PAIR)
                lo = max((u + 1) * PAIR - BAND_PAIR, 0)
                bands.append((k_ref[lo:lo + n_keys, :], v_ref[lo:lo + n_keys, :], BAND_PAIR - n_keys))
            else:
                start = pl.multiple_of((st * npp + u) * PAIR - FRONT_PAD, PAIR)
                bands.append((k_ref[pl.ds(start, BAND_PAIR), :], v_ref[pl.ds(start, BAND_PAIR), :], 0))
        _attn_pairs(q_ref, bands, bias_ref, s_ref, e_ref, o_ref)

    @pl.when(st == 0)
    def _():
        run(True)

    @pl.when(st > 0)
    def _():
        run(False)


def _distance_rows(rel_table):
    t = rel_table.astype(F32)
    far = t[:, 2 * REL_CLIP:]
    n_far = LEFT_CHUNKS * CHUNK - REL_CLIP + 1
    near = jnp.flip(t[:, REL_CLIP - CHUNK + 1:2 * REL_CLIP], axis=1)
    tail = BAND_PAIR - n_far - near.shape[1]
    g = jnp.concatenate([jnp.repeat(far, n_far, axis=1), near, jnp.repeat(far, tail, axis=1)], axis=1)
    return g[:, None, :]


def _attention(q, k, v, g_rows, pairs_per_step=16):
    s, da = q.shape
    n_hp = da // (2 * HEAD_DIM)
    tq = pairs_per_step * PAIR
    assert tq >= FRONT_PAD, "first step must cover every clamped band"
    assert s % tq == 0
    return pl.pallas_call(
        functools.partial(_attn_body, pairs_per_step=pairs_per_step),
        grid=(n_hp, s // tq),
        in_specs=[
            pl.BlockSpec((tq, 2 * HEAD_DIM), lambda hp, st: (st, hp)),
            pl.BlockSpec((s, 2 * HEAD_DIM), lambda hp, st: (0, hp)),
            pl.BlockSpec((s, 2 * HEAD_DIM), lambda hp, st: (0, hp)),
            pl.BlockSpec((2, 1, BAND_PAIR), lambda hp, st: (hp, 0, 0)),
        ],
        out_specs=pl.BlockSpec((tq, 2 * HEAD_DIM), lambda hp, st: (st, hp)),
        out_shape=jax.ShapeDtypeStruct((s, da), BF16),
        scratch_shapes=[
            pltpu.VMEM((2, PAIR, BAND_PAIR), F32),
            pltpu.VMEM((2 * pairs_per_step, PAIR, BAND_PAIR), F32),
            pltpu.VMEM((2 * pairs_per_step, PAIR, BAND_PAIR), BF16),
        ],
        compiler_params=_cparams("arbitrary", "arbitrary"),
        name="chunk_attention",
    )(q, k, v, g_rows)


def _slab_rows(d):
    return d // LANES


def _store_slabs(ref, val):
    t, d = val.shape
    n = _slab_rows(d)
    for c in range(n):
        ref[pl.ds(c, t, stride=n), :] = val[:, c * LANES:(c + 1) * LANES]


def _load_slabs(ref, t, n):
    return jnp.concatenate([ref[pl.ds(c, t, stride=n), :] for c in range(n)], axis=1)


def _first_index_of_max(vals, sub, n):
    vmax = jnp.max(vals, axis=0, keepdims=True)
    idx = jnp.min(jnp.where(vals == vmax, sub, n), axis=0, keepdims=True)
    return vmax, idx


def _outproj_body(x_ref, co_ref, ao_ref, w_hbm, g2_ref, wr_ref, br_ref,
                  x2_ref, h2_ref, eid_ref, gcol_ref, w_ref, stage, wsem):
    @pl.when(pl.program_id(0) == 0)
    def _():
        _load_weight_as_bf16(w_hbm, w_ref, stage, wsem)

    dc = co_ref.shape[1]
    y = jnp.dot(co_ref[...], w_ref[0:dc, :], preferred_element_type=F32)
    y = y + jnp.dot(ao_ref[...], w_ref[dc:, :], preferred_element_type=F32)
    x2 = x_ref[...] + y
    ms = jnp.mean(x2 * x2, axis=-1, keepdims=True)
    h2 = x2 * lax.rsqrt(ms + RMS_EPS) * g2_ref[...]
    x2_ref[...] = x2
    _store_slabs(h2_ref, h2)

    hi = h2.astype(BF16)
    lo = (h2 - hi.astype(F32)).astype(BF16)
    pa = jnp.dot(hi, wr_ref[...], preferred_element_type=F32)
    pb = jnp.dot(lo, wr_ref[...], preferred_element_type=F32)
    logits = pa[:, :LANES] + pa[:, LANES:] + pb[:, :LANES]
    lt = logits.T + br_ref[...]
    tm = lt.shape[1]
    sub = lax.broadcasted_iota(I32, (N_GROUPS, tm), 0)
    g = lt[0:N_GROUPS, :]
    gmax, grp = _first_index_of_max(g, sub, N_GROUPS)
    p_grp = 1.0 / jnp.sum(jnp.exp(g - gmax), axis=0, keepdims=True)
    sel = jnp.zeros((EXPERTS_PER_GROUP, tm), F32)
    for gi in range(N_GROUPS):
        lo = N_GROUPS + gi * EXPERTS_PER_GROUP
        sel = jnp.where(grp == gi, lt[lo:lo + EXPERTS_PER_GROUP, :], sel)
    v1, i1 = _first_index_of_max(sel, sub, EXPERTS_PER_GROUP)
    rest = jnp.where(sub == i1, -jnp.inf, sel)
    v2, i2 = _first_index_of_max(rest, sub, EXPERTS_PER_GROUP)
    t = jnp.exp(v2 - v1)
    gate1 = p_grp / (1.0 + t)
    gate2 = p_grp * t / (1.0 + t)
    eid_ref[0:1, :] = grp * EXPERTS_PER_GROUP + i1
    eid_ref[1:2, :] = grp * EXPERTS_PER_GROUP + i2
    row = lax.broadcasted_iota(I32, (LANES, tm), 0)
    gl = jnp.where(row == 0, gate1, jnp.where(row == 1, gate2, 0.0))
    gcol_ref[...] = gl.T


def _out_proj_router(x, conv_out, attn_out, w, g2, wr_cat, br_col, tm=512, w_chunk=512):
    s, d = x.shape
    assert s % tm == 0
    dc = conv_out.shape[1]
    da = attn_out.shape[1]
    row = lambda i: (i, 0)
    const = lambda i: (0, 0)
    return pl.pallas_call(
        _outproj_body,
        grid=(s // tm,),
        in_specs=[
            pl.BlockSpec((tm, d), row),
            pl.BlockSpec((tm, dc), row),
            pl.BlockSpec((tm, da), row),
            pl.BlockSpec(memory_space=pl.ANY),
            pl.BlockSpec((1, d), const),
            pl.BlockSpec((d, 2 * LANES), const),
            pl.BlockSpec((LANES, 1), const),
        ],
        out_specs=[
            pl.BlockSpec((tm, d), row),
            pl.BlockSpec((tm * _slab_rows(d), LANES), row),
            pl.BlockSpec((TOP_K, tm), lambda i: (0, i)),
            pl.BlockSpec((tm, LANES), row),
        ],
        out_shape=[
            jax.ShapeDtypeStruct((s, d), F32),
            jax.ShapeDtypeStruct((s * _slab_rows(d), LANES), F32),
            jax.ShapeDtypeStruct((TOP_K, s), I32),
            jax.ShapeDtypeStruct((s, LANES), F32),
        ],
        scratch_shapes=_weight_scratch(dc + da, d, w_chunk),
        compiler_params=_cparams("arbitrary"),
        name="out_proj_router",
    )(x, conv_out, attn_out, w, g2, wr_cat, br_col)


META_LANES = 256
(META_FIRST_BLK, META_N_BLK, META_COUNT, META_NEXT_EXP, META_ORDINAL,
 META_USED_BLKS, META_FIRST_USED) = range(7)


def _dispatch_body(eid_ref, dest_ref, meta_ref, rank_ref):
    s = eid_ref.shape[1]
    w = META_LANES
    sub = lax.broadcasted_iota(I32, (N_EXPERTS, w), 0)
    lane = lax.broadcasted_iota(I32, (N_EXPERTS, w), 1)
    ri = lax.broadcasted_iota(I32, (w, w), 0)
    ci = lax.broadcasted_iota(I32, (w, w), 1)
    upper = (ri < ci).astype(BF16)

    def onehots(b):
        off = pl.multiple_of(b * w, w)
        e0 = eid_ref[0:1, pl.ds(off, w)]
        e1 = eid_ref[1:2, pl.ds(off, w)]
        return off, sub == e0, sub == e1

    def rank_block(b, carry):
        off, o0, o1 = onehots(b)
        both = jnp.where(o0 | o1, 1.0, 0.0)
        pre = jnp.dot(both.astype(BF16), upper, preferred_element_type=F32) + carry
        rank_ref[0:1, pl.ds(off, w)] = jnp.sum(jnp.where(o0, pre, 0.0), axis=0, keepdims=True)
        rank_ref[1:2, pl.ds(off, w)] = jnp.sum(jnp.where(o1, pre, 0.0), axis=0, keepdims=True)
        return carry + jnp.sum(both, axis=1, keepdims=True)

    counts = lax.fori_loop(0, s // w, rank_block, jnp.zeros((N_EXPERTS, w), F32))

    nblk = ((counts.astype(I32) + (BLOCK_ROWS - 1)) // BLOCK_ROWS).astype(F32)
    er = lax.broadcasted_iota(I32, (N_EXPERTS, N_EXPERTS), 0)
    ec = lax.broadcasted_iota(I32, (N_EXPERTS, N_EXPERTS), 1)
    lower_incl = (ec <= er).astype(BF16)
    pend_blk = jnp.dot(lower_incl, nblk.astype(BF16), preferred_element_type=F32)
    pstart = (pend_blk - nblk) * float(BLOCK_ROWS)

    def on_lanes(v):
        return jnp.sum(jnp.where(sub == lane, v, 0), axis=0, keepdims=True)

    nblk_i = nblk.astype(I32)
    first_blk = on_lanes((pend_blk - nblk).astype(I32))
    n_blk = on_lanes(nblk_i)
    count = on_lanes(counts.astype(I32))
    has_rows = nblk_i > 0
    next_exp = jnp.min(jnp.where(has_rows & (sub > lane), sub, N_EXPERTS), axis=0, keepdims=True)
    next_exp = jnp.where(next_exp < N_EXPERTS, next_exp, -1)
    ordinal = jnp.sum(jnp.where(has_rows & (sub < lane), 1, 0), axis=0, keepdims=True)
    first_used = jnp.min(jnp.where(has_rows, sub, N_EXPERTS - 1), axis=0, keepdims=True)
    used = pend_blk[N_EXPERTS - 1:N_EXPERTS, :].astype(I32)
    mrow = lax.broadcasted_iota(I32, (8, w), 0)
    meta = jnp.where(mrow == META_FIRST_BLK, first_blk, 0)
    meta = jnp.where(mrow == META_N_BLK, n_blk, meta)
    meta = jnp.where(mrow == META_COUNT, count, meta)
    meta = jnp.where(mrow == META_NEXT_EXP, next_exp, meta)
    meta = jnp.where(mrow == META_ORDINAL, ordinal, meta)
    meta = jnp.where(mrow == META_FIRST_USED, first_used, meta)
    meta_ref[...] = jnp.where(mrow == META_USED_BLKS, used, meta)

    def dest_block(b, _):
        off, o0, o1 = onehots(b)
        d0 = rank_ref[0:1, pl.ds(off, w)] + jnp.sum(jnp.where(o0, pstart, 0.0), axis=0, keepdims=True)
        d1 = rank_ref[1:2, pl.ds(off, w)] + jnp.sum(jnp.where(o1, pstart, 0.0), axis=0, keepdims=True)
        dest_ref[0:1, pl.ds(off, w)] = d0.astype(I32)
        dest_ref[1:2, pl.ds(off, w)] = d1.astype(I32)
        return 0

    lax.fori_loop(0, s // w, dest_block, 0)


def _dispatch(eids):
    s = eids.shape[1]
    return pl.pallas_call(
        _dispatch_body,
        out_shape=[
            jax.ShapeDtypeStruct((TOP_K, s), I32),
            jax.ShapeDtypeStruct((8, META_LANES), I32),
        ],
        scratch_shapes=[pltpu.VMEM((TOP_K, s), F32)],
        compiler_params=pltpu.CompilerParams(vmem_limit_bytes=VMEM_LIMIT_BYTES),
        name="dispatch",
    )(eids)


PAD_BITS = tuple(reversed(range(BLOCK_ROWS.bit_length() - 1)))


def _scatter_body(dest_ref, meta_ref, h2_ref, xs_hbm, zbuf, sem, zsem, *, tm):
    i = pl.program_id(0)
    n_sl = h2_ref.shape[0] // tm
    blk = BLOCK_ROWS * n_sl
    n_blocks = xs_hbm.shape[0] // blk
    used = meta_ref[META_USED_BLKS, 0]

    def slab(row, n_tokens=1):
        return pl.ds(pl.multiple_of(row * n_sl, n_sl), n_tokens * n_sl)

    def token_copy(j, k):
        t = i * tm + j
        return pltpu.make_async_copy(h2_ref.at[slab(j)], xs_hbm.at[slab(dest_ref[k, t])], sem.at[k])

    def scatter_tile():
        def one(j, _):
            for k in range(TOP_K):
                token_copy(j, k).start(priority=k)
            return 0

        lax.fori_loop(0, tm, one, 0, unroll=8)

    def wait_tile():
        for k in range(TOP_K):
            pltpu.make_async_copy(h2_ref, xs_hbm.at[slab(0, tm)], sem.at[k]).wait()

    @pl.when(i > 0)
    def _():
        scatter_tile()
        wait_tile()

    @pl.when(i == 0)
    def _():
        _scatter_first_step(meta_ref, xs_hbm, zbuf, zsem, slab, used, n_blocks, scatter_tile, wait_tile)


def _scatter_first_step(meta_ref, xs_hbm, zbuf, zsem, slab, used, n_blocks, scatter_tile, wait_tile):
    n_sl = zbuf.shape[0] // BLOCK_ROWS

    def zero_copy(row, n_tokens):
        return pltpu.make_async_copy(zbuf.at[pl.ds(0, n_tokens * n_sl)], xs_hbm.at[slab(row, n_tokens)], zsem)

    zbuf[...] = jnp.zeros_like(zbuf)

    def per_expert(e, issued):
        b0 = meta_ref[META_FIRST_BLK, e]
        nb = meta_ref[META_N_BLK, e]
        cnt = meta_ref[META_COUNT, e]
        pad = nb * BLOCK_ROWS - cnt
        row = b0 * BLOCK_ROWS + cnt
        out = []
        for bit, n_issued in zip(PAD_BITS, issued):
            size = 1 << bit
            take = pad & size

            @pl.when(take != 0)
            def _():
                zero_copy(row, size).start()

            row = row + take
            out.append(n_issued + take // size)
        return tuple(out)

    issued = lax.fori_loop(0, N_EXPERTS, per_expert, tuple(jnp.int32(0) for _ in PAD_BITS))
    lax.fori_loop(used, n_blocks, lambda b, _: (zero_copy(b * BLOCK_ROWS, BLOCK_ROWS).start(), 0)[1], 0)

    scatter_tile()
    wait_tile()

    for bit, n_issued in zip(PAD_BITS, issued):
        lax.fori_loop(0, n_issued, lambda i, _, n=1 << bit: (zero_copy(0, n).wait(), 0)[1], 0)
    lax.fori_loop(used, n_blocks, lambda b, _: (zero_copy(0, BLOCK_ROWS).wait(), 0)[1], 0)


def _scatter_rows(dest, meta, h2_slabs, rows, tm=1024):
    s = dest.shape[1]
    assert s % tm == 0
    n_sl = h2_slabs.shape[0] // s
    grid_spec = pltpu.PrefetchScalarGridSpec(
        num_scalar_prefetch=2,
        grid=(s // tm,),
        in_specs=[pl.BlockSpec((tm * n_sl, LANES), lambda i, dest_ref, meta_ref: (i, 0))],
        out_specs=pl.BlockSpec(memory_space=pl.ANY),
        scratch_shapes=[
            pltpu.VMEM((BLOCK_ROWS * n_sl, LANES), h2_slabs.dtype),
            pltpu.SemaphoreType.DMA((TOP_K,)),
            pltpu.SemaphoreType.DMA(()),
        ],
    )
    return pl.pallas_call(
        functools.partial(_scatter_body, tm=tm),
        grid_spec=grid_spec,
        out_shape=jax.ShapeDtypeStruct((rows * n_sl, LANES), h2_slabs.dtype),
        compiler_params=_cparams("arbitrary"),
        name="scatter_rows",
    )(dest, meta, h2_slabs)


def _row_gather_wait(src_hbm, dst, sem, n_rows):
    pltpu.make_async_copy(src_hbm.at[pl.ds(0, n_rows)], dst.at[pl.ds(0, n_rows)], sem).wait()


WEIGHT_QUEUES = (1, 1, 1)
ROW_BLOCK_QUEUE = 0


def _experts_body(meta_ref, xs_hbm, wg_hbm, wu_hbm, wd_hbm, ys_hbm,
                  xbuf, ybuf, wgbuf, wubuf, wdbuf, gsem, ysem, wsem):
    used = meta_ref[META_USED_BLKS, 0]
    n_experts, d, _ = wg_hbm.shape
    n_sl = _slab_rows(d)
    blk = BLOCK_ROWS * n_sl
    n_blocks = ys_hbm.shape[0] // blk

    def xs_copy(b, sl):
        src_blk = jnp.minimum(b, n_blocks - 1)
        rows = pl.ds(pl.multiple_of(src_blk * blk, blk), blk)
        return pltpu.make_async_copy(xs_hbm.at[rows], xbuf.at[sl], gsem.at[sl])

    def gather_start(b, sl):
        xs_copy(b, sl).start(priority=ROW_BLOCK_QUEUE)

    def weights_start(e, ws):
        for c, queue in zip(weight_copies(e, ws), WEIGHT_QUEUES):
            c.start(priority=queue)

    def gather_wait(sl):
        xs_copy(0, sl).wait()

    def ys_copy(b):
        sl = b % 2
        rows = pl.ds(pl.multiple_of(b * blk, blk), blk)
        return pltpu.make_async_copy(ybuf.at[sl], ys_hbm.at[rows], ysem.at[sl])

    def weight_copies(e, ws):
        return (pltpu.make_async_copy(wg_hbm.at[e], wgbuf.at[ws], wsem.at[ws, 0]),
                pltpu.make_async_copy(wu_hbm.at[e], wubuf.at[ws], wsem.at[ws, 1]),
                pltpu.make_async_copy(wd_hbm.at[e], wdbuf.at[ws], wsem.at[ws, 2]))

    weights_start(meta_ref[META_FIRST_USED, 0], 0)
    gather_start(0, 0)

    def expert(e, _):
        nb = meta_ref[META_N_BLK, e]

        @pl.when(nb > 0)
        def _():
            b0 = meta_ref[META_FIRST_BLK, e]
            ws = meta_ref[META_ORDINAL, e] % 2
            nxt = meta_ref[META_NEXT_EXP, e]

            @pl.when(nxt >= 0)
            def _():
                weights_start(nxt, 1 - ws)

            for c in weight_copies(e, ws):
                c.wait()

            def block(j, _):
                b = b0 + j
                sl = b % 2
                gather_start(b + 1, 1 - sl)
                gather_wait(sl)

                @pl.when(b >= 2)
                def _():
                    ys_copy(b - 2).wait()

                x = _load_slabs(xbuf.at[sl], BLOCK_ROWS, n_sl).astype(BF16)
                a = jnp.dot(x, wgbuf[ws].astype(BF16), preferred_element_type=F32)
                u = jnp.dot(x, wubuf[ws].astype(BF16), preferred_element_type=F32)
                act = (a * _sigmoid(a) * u).astype(BF16)
                y = jnp.dot(act, wdbuf[ws].astype(BF16), preferred_element_type=F32)
                _store_slabs(ybuf.at[sl], y)
                ys_copy(b).start(priority=ROW_BLOCK_QUEUE)
                return 0

            lax.fori_loop(0, nb, block, 0)

        return 0

    lax.fori_loop(0, n_experts, expert, 0)

    gather_wait(used % 2)
    for back in (1, 2):
        @pl.when(used >= back)
        def _():
            ys_copy(used - back).wait()


def _experts(meta, xs_slabs, w_gate, w_up, w_down):
    n_e, d, de = w_gate.shape
    n_sl = _slab_rows(d)
    blk = BLOCK_ROWS * n_sl
    smem = pl.BlockSpec(memory_space=pltpu.SMEM)
    hbm = pl.BlockSpec(memory_space=pl.ANY)
    return pl.pallas_call(
        _experts_body,
        in_specs=[smem, hbm, hbm, hbm, hbm],
        out_specs=hbm,
        out_shape=jax.ShapeDtypeStruct(xs_slabs.shape, F32),
        input_output_aliases={1: 0},
        scratch_shapes=[
            pltpu.VMEM((2, blk, LANES), F32),
            pltpu.VMEM((2, blk, LANES), F32),
            pltpu.VMEM((2, d, de), F32),
            pltpu.VMEM((2, d, de), F32),
            pltpu.VMEM((2, de, d), F32),
            pltpu.SemaphoreType.DMA((2,)),
            pltpu.SemaphoreType.DMA((2,)),
            pltpu.SemaphoreType.DMA((2, 3)),
        ],
        compiler_params=pltpu.CompilerParams(vmem_limit_bytes=VMEM_LIMIT_BYTES),
        name="expert_mlp",
    )(meta, xs_slabs, w_gate, w_up, w_down)


def _combine_body(dest_ref, x2_ref, gcol_ref, fg_ref, ys_hbm, o_ref, ybuf, sem, *, tm):
    i = pl.program_id(0)
    n = pl.num_programs(0)
    s = n * tm
    slot = i % 2
    n_sl = _slab_rows(x2_ref.shape[1])

    def start(tile, sl):
        for k in range(TOP_K):
            for r in range(tm):
                row = dest_ref[k * s + tile * tm + r]
                src = ys_hbm.at[pl.ds(pl.multiple_of(row * n_sl, n_sl), n_sl)]
                pltpu.make_async_copy(src, ybuf.at[sl, k, pl.ds(r * n_sl, n_sl)],
                                      sem.at[sl, k]).start(priority=r % 2)

    def wait(sl):
        for k in range(TOP_K):
            _row_gather_wait(ys_hbm, ybuf.at[sl, k], sem.at[sl, k], tm * n_sl)

    @pl.when(i == 0)
    def _():
        start(0, 0)

    start(jnp.minimum(i + 1, n - 1), 1 - slot)
    wait(slot)
    g = gcol_ref[...]
    y0 = _load_slabs(ybuf.at[slot, 0], tm, n_sl)
    y1 = _load_slabs(ybuf.at[slot, 1], tm, n_sl)
    moe = g[:, 0:1] * y0 + g[:, 1:2] * y1
    x3 = x2_ref[...] + moe
    ms = jnp.mean(x3 * x3, axis=-1, keepdims=True)
    o_ref[...] = x3 * lax.rsqrt(ms + RMS_EPS) * fg_ref[...]

    @pl.when(i == n - 1)
    def _():
        wait(1 - slot)


def _combine(dest_flat, x2, gcol, final_g, ys, tm=256):
    s, d = x2.shape
    assert s % tm == 0
    row = lambda i, dest_ref: (i, 0)
    const = lambda i, dest_ref: (0, 0)
    grid_spec = pltpu.PrefetchScalarGridSpec(
        num_scalar_prefetch=1,
        grid=(s // tm,),
        in_specs=[
            pl.BlockSpec((tm, d), row),
            pl.BlockSpec((tm, LANES), row),
            pl.BlockSpec((1, d), const),
            pl.BlockSpec(memory_space=pl.ANY),
        ],
        out_specs=pl.BlockSpec((tm, d), row),
        scratch_shapes=[
            pltpu.VMEM((2, TOP_K, tm * _slab_rows(d), LANES), F32),
            pltpu.SemaphoreType.DMA((2, TOP_K)),
        ],
    )
    return pl.pallas_call(
        functools.partial(_combine_body, tm=tm),
        grid_spec=grid_spec,
        out_shape=jax.ShapeDtypeStruct((s, d), F32),
        compiler_params=_cparams("arbitrary"),
        name="combine_norm",
    )(dest_flat, x2, gcol, final_g, ys)


def _layer(x, norm1_g, w_in, w_dw, b_dw, ln_g, ln_b, rel_bias, w_out, norm2_g,
           w_group, b_group, w_expert_router, b_expert_router, w_gate, w_up, w_down, out_g):
    s, d = x.shape
    dc = w_dw.shape[-1]
    da = (w_in.shape[1] - 2 * dc) // 3
    row2 = lambda a: a.reshape(1, -1)

    cv, cg, q, k, v = _in_proj(x, row2(norm1_g), w_in, dc, da)
    conv_out = _conv_module(cv, cg, w_dw.reshape(CONV_WIDTH, dc), row2(b_dw), row2(ln_g), row2(ln_b))
    attn_out = _attention(q, k, v, _distance_rows(rel_bias))

    n_r = N_GROUPS + N_EXPERTS
    wr = jnp.pad(jnp.concatenate([w_group, w_expert_router], axis=1), ((0, 0), (0, LANES - n_r)))
    wr_hi = wr.astype(BF16)
    wr_lo = (wr - wr_hi.astype(F32)).astype(BF16)
    wr_cat = jnp.concatenate([wr_hi, wr_lo], axis=1)
    br_col = jnp.zeros((LANES, 1), F32).at[:n_r, 0].set(jnp.concatenate([b_group, b_expert_router]))
    x2, h2, eids, gcol = _out_proj_router(x, conv_out, attn_out, w_out, row2(norm2_g),
                                          wr_cat, br_col)

    n_blocks = (s * TOP_K) // BLOCK_ROWS + N_EXPERTS
    assert n_blocks <= META_LANES
    dest, meta = _dispatch(eids)
    xs = _scatter_rows(dest, meta, h2, n_blocks * BLOCK_ROWS)
    ys = _experts(meta, xs, w_gate, w_up, w_down)
    return _combine(dest.reshape(-1), x2, gcol, row2(out_g), ys)


def kernel(x, norm1_g, w_in, w_dw, b_dw, ln_g, ln_b, rel_bias, w_out, norm2_g, w_group, b_group,
           w_expert_router, b_expert_router, w_gate, w_up, w_down, final_g):
    depth = norm1_g.shape[0]
    assert depth == 1, "the final norm is fused into the last layer's combine step"
    batch, s, d = x.shape
    assert batch == 1, "one sequence per call"
    out = _layer(x.reshape(s, d), norm1_g[0], w_in[0], w_dw[0], b_dw[0], ln_g[0], ln_b[0], rel_bias[0],
                 w_out[0], norm2_g[0], w_group[0], b_group[0], w_expert_router[0],
                 b_expert_router[0], w_gate[0], w_up[0], w_down[0], final_g)
    return out.reshape(batch, s, d)
```
